```python
import jax, jax.numpy as jnp
from jax import lax
import numpy as np

D_MODEL = 4096
BATCH = 16
SEQ = 2048
DEPTH = 2

N_MIXERS = 2
POOL_WINDOWS = (2, 4, 8, 16)
N_POOL_GROUPS = len(POOL_WINDOWS)
POOL_GROUP = D_MODEL // N_POOL_GROUPS
GLA_HEADS = 4
GLA_KEY_DIM = D_MODEL // 2
GLA_VAL_DIM = D_MODEL
GLA_HEAD_K = GLA_KEY_DIM // GLA_HEADS
GLA_HEAD_V = GLA_VAL_DIM // GLA_HEADS
GATE_RANK = 16
GATE_TAU = 16.0
CHUNK = 64
D_FF = 4 * D_MODEL
EPS = 1e-6
N_POOL_LAYERS = (DEPTH + 1) // 2
N_GLA_LAYERS = DEPTH // 2
PROJ_WIDTH = 2 * GLA_KEY_DIM + 2 * GLA_VAL_DIM + 2 * GATE_RANK

kernel_name = "hybrid_pool_gla_encoder"


def rmsnorm(x, g):
    xf = x.astype(jnp.float32)
    y = xf * lax.rsqrt(jnp.mean(xf * xf, axis=-1, keepdims=True) + EPS)
    return (y * g.astype(jnp.float32)).astype(x.dtype)


def pool_mixer(h, w_group, scale):
    B, S, D = h.shape
    hf = h.astype(jnp.float32)
    csum = jnp.concatenate([jnp.zeros((B, 1, D), jnp.float32), jnp.cumsum(hf, axis=1)], axis=1)
    pos = jnp.arange(S)
    diffs = []
    for g, w in enumerate(POOL_WINDOWS):
        lo = jnp.clip(pos - w // 2, 0, S)
        hi = jnp.clip(pos + w // 2, 0, S)
        sl = slice(g * POOL_GROUP, (g + 1) * POOL_GROUP)
        cs = csum[..., sl]
        count = (hi - lo).astype(jnp.float32)[None, :, None]
        mean = (cs[:, hi] - cs[:, lo]) / count
        diffs.append(mean - hf[..., sl])
    d = jnp.stack(diffs, axis=2).astype(h.dtype)
    y = jnp.einsum('bsgc,gcd->bsgd', d, w_group).reshape(B, S, D)
    return y * scale


def gla_chunked(q, k, v, log_a):
    B, S, H, dk = q.shape
    dv = v.shape[-1]
    n_chunks = S // CHUNK

    def to_chunks(t):
        return t.astype(jnp.float32).reshape(B, n_chunks, CHUNK, H, -1).transpose(1, 0, 3, 2, 4)

    qc, kc, vc, gc = to_chunks(q), to_chunks(k), to_chunks(v), to_chunks(log_a)
    b = jnp.cumsum(gc, axis=3)
    b_last = b[..., CHUNK - 1:, :]
    b_mid = b[..., CHUNK // 2 - 1:CHUNK // 2, :]
    scores = jnp.einsum('nbhik,nbhjk->nbhij', qc * jnp.exp(b - b_mid), kc * jnp.exp(b_mid - b))
    mask = jnp.tril(jnp.ones((CHUNK, CHUNK), dtype=bool))
    scores = jnp.where(mask, scores, 0.0)
    o_intra = jnp.einsum('nbhij,nbhjv->nbhiv', scores, vc)
    q_inter = qc * jnp.exp(b)
    k_state = kc * jnp.exp(b_last - b)
    a_chunk = jnp.exp(b_last[..., 0, :])

    def step(state, inp):
        qi, ki, vi, ai = inp
        o = jnp.einsum('bhik,bhkv->bhiv', qi, state)
        state = ai[..., None] * state + jnp.einsum('bhjk,bhjv->bhkv', ki, vi)
        return state, o

    state0 = jnp.zeros((B, H, dk, dv), jnp.float32)
    _, o_inter = lax.scan(step, state0, (q_inter, k_state, vc, a_chunk))
    o = o_intra + o_inter
    return o.transpose(1, 0, 3, 2, 4).reshape(B, S, H, dv)


def gla_mixer(h, w_in, w_up_f, b_up_f, w_up_b, b_up_b, g_norm, w_out):
    B, S, D = h.shape
    p = h @ w_in
    c1 = GLA_KEY_DIM
    c2 = 2 * GLA_KEY_DIM
    c3 = c2 + GLA_VAL_DIM
    c4 = c3 + GLA_VAL_DIM
    c5 = c4 + GATE_RANK
    q, k, v, gate, r_f, r_b = jnp.split(p, [c1, c2, c3, c4, c5], axis=-1)
    q = q.reshape(B, S, GLA_HEADS, GLA_HEAD_K) * (GLA_HEAD_K ** -0.5)
    k = k.reshape(B, S, GLA_HEADS, GLA_HEAD_K)
    v = v.reshape(B, S, GLA_HEADS, GLA_HEAD_V)
    log_a_f = jax.nn.log_sigmoid((r_f @ w_up_f + b_up_f).astype(jnp.float32)) / GATE_TAU
    log_a_b = jax.nn.log_sigmoid((r_b @ w_up_b + b_up_b).astype(jnp.float32)) / GATE_TAU
    log_a_f = log_a_f.reshape(B, S, GLA_HEADS, GLA_HEAD_K)
    log_a_b = log_a_b.reshape(B, S, GLA_HEADS, GLA_HEAD_K)
    o_fwd = gla_chunked(q, k, v, log_a_f)
    flip = lambda t: jnp.flip(t, axis=1)
    o_bwd = flip(gla_chunked(flip(q), flip(k), flip(v), flip(log_a_b)))
    o = o_fwd + o_bwd
    o = o * lax.rsqrt(jnp.mean(o * o, axis=-1, keepdims=True) + EPS) * g_norm.astype(jnp.float32)
    o = o.reshape(B, S, GLA_VAL_DIM) * jax.nn.silu(gate.astype(jnp.float32))
    return o.astype(h.dtype) @ w_out


def relu2_mlp(h, w1, w2):
    a = jax.nn.relu(h @ w1)
    return (a * a) @ w2


def _fwd_setup_inputs(seed: int = 0) -> dict:
    key = jax.random.key(seed)
    ks = jax.random.split(key, 16)
    f32 = jnp.float32
    nrm = lambda k, shape, s: (jax.random.normal(k, shape, f32) * s).astype(f32)
    return {
        "x": nrm(ks[0], (BATCH, SEQ, D_MODEL), 1.0),
        "norm_mix": 1.0 + nrm(ks[1], (DEPTH, D_MODEL), 0.02),
        "norm_mlp": 1.0 + nrm(ks[2], (DEPTH, D_MODEL), 0.02),
        "norm_final": 1.0 + nrm(ks[3], (D_MODEL,), 0.02),
        "pool_w": nrm(ks[4], (N_POOL_LAYERS, N_POOL_GROUPS, POOL_GROUP, POOL_GROUP), POOL_GROUP ** -0.5),
        "pool_scale": 1.0 + nrm(ks[5], (N_POOL_LAYERS, D_MODEL), 0.02),
        "gla_w_in": nrm(ks[6], (N_GLA_LAYERS, D_MODEL, PROJ_WIDTH), D_MODEL ** -0.5),
        "gla_w_up_f": nrm(ks[7], (N_GLA_LAYERS, GATE_RANK, GLA_KEY_DIM), GATE_RANK ** -0.5),
        "gla_b_up_f": nrm(ks[8], (N_GLA_LAYERS, GLA_KEY_DIM), 0.1),
        "gla_w_up_b": nrm(ks[9], (N_GLA_LAYERS, GATE_RANK, GLA_KEY_DIM), GATE_RANK ** -0.5),
        "gla_b_up_b": nrm(ks[10], (N_GLA_LAYERS, GLA_KEY_DIM), 0.1),
        "gla_g_norm": 1.0 + nrm(ks[11], (N_GLA_LAYERS, GLA_HEAD_V), 0.02),
        "gla_w_out": nrm(ks[12], (N_GLA_LAYERS, GLA_VAL_DIM, D_MODEL), GLA_VAL_DIM ** -0.5),
        "mlp_w_in": nrm(ks[13], (DEPTH, D_MODEL, D_FF), D_MODEL ** -0.5),
        "mlp_w_out": nrm(ks[14], (DEPTH, D_FF, D_MODEL), D_FF ** -0.5),
    }


def _fwd_reference(x, norm_mix, norm_mlp, norm_final, pool_w, pool_scale, gla_w_in, gla_w_up_f,
              gla_b_up_f, gla_w_up_b, gla_b_up_b, gla_g_norm, gla_w_out, mlp_w_in, mlp_w_out):
    h = x
    for layer in range(DEPTH):
        j = layer // N_MIXERS
        hn = rmsnorm(h, norm_mix[layer])
        if layer % N_MIXERS == 0:
            h = h + pool_mixer(hn, pool_w[j], pool_scale[j])
        else:
            h = h + gla_mixer(hn, gla_w_in[j], gla_w_up_f[j], gla_b_up_f[j], gla_w_up_b[j],
                              gla_b_up_b[j], gla_g_norm[j], gla_w_out[j])
        hn = rmsnorm(h, norm_mlp[layer])
        h = h + relu2_mlp(hn, mlp_w_in[layer], mlp_w_out[layer])
    return rmsnorm(h, norm_final)


import jax as _jax
import jax.numpy as _jnp

TWIN_FORMAT = 'train_step'
FWD_PARAMS = ['x', 'norm_mix', 'norm_mlp', 'norm_final', 'pool_w', 'pool_scale', 'gla_w_in', 'gla_w_up_f', 'gla_b_up_f', 'gla_w_up_b', 'gla_b_up_b', 'gla_g_norm', 'gla_w_out', 'mlp_w_in', 'mlp_w_out']
TWIN_WEIGHTS = ['norm_mix', 'norm_mlp', 'norm_final', 'pool_w', 'pool_scale', 'gla_w_in', 'gla_w_up_f', 'gla_b_up_f', 'gla_w_up_b', 'gla_b_up_b', 'gla_g_norm', 'gla_w_out', 'mlp_w_in', 'mlp_w_out']
TWIN_DIFF_INPUT = 'x'
TWIN_INPUTS = ['x', 'norm_mix', 'norm_mlp', 'norm_final', 'pool_w', 'pool_scale', 'gla_w_in', 'gla_w_up_f', 'gla_b_up_f', 'gla_w_up_b', 'gla_b_up_b', 'gla_g_norm', 'gla_w_out', 'mlp_w_in', 'mlp_w_out', 'loss_target', 'm_norm_mix', 'm_norm_mlp', 'm_norm_final', 'm_pool_w', 'm_pool_scale', 'm_gla_w_in', 'm_gla_w_up_f', 'm_gla_b_up_f', 'm_gla_w_up_b', 'm_gla_b_up_b', 'm_gla_g_norm', 'm_gla_w_out', 'm_mlp_w_in', 'm_mlp_w_out', 'v_norm_mix', 'v_norm_mlp', 'v_norm_final', 'v_pool_w', 'v_pool_scale', 'v_gla_w_in', 'v_gla_w_up_f', 'v_gla_b_up_f', 'v_gla_w_up_b', 'v_gla_b_up_b', 'v_gla_g_norm', 'v_gla_w_out', 'v_mlp_w_in', 'v_mlp_w_out']
TWIN_OUTPUTS = ['loss', 'grad_x', 'grad_norm_mix', 'grad_norm_mlp', 'grad_norm_final', 'grad_pool_w', 'grad_pool_scale', 'grad_gla_w_in', 'grad_gla_w_up_f', 'grad_gla_b_up_f', 'grad_gla_w_up_b', 'grad_gla_b_up_b', 'grad_gla_g_norm', 'grad_gla_w_out', 'grad_mlp_w_in', 'grad_mlp_w_out', 'delta_norm_mix', 'delta_norm_mlp', 'delta_norm_final', 'delta_pool_w', 'delta_pool_scale', 'delta_gla_w_in', 'delta_gla_w_up_f', 'delta_gla_b_up_f', 'delta_gla_w_up_b', 'delta_gla_b_up_b', 'delta_gla_g_norm', 'delta_gla_w_out', 'delta_mlp_w_in', 'delta_mlp_w_out', 'new_m_norm_mix', 'new_m_norm_mlp', 'new_m_norm_final', 'new_m_pool_w', 'new_m_pool_scale', 'new_m_gla_w_in', 'new_m_gla_w_up_f', 'new_m_gla_b_up_f', 'new_m_gla_w_up_b', 'new_m_gla_b_up_b', 'new_m_gla_g_norm', 'new_m_gla_w_out', 'new_m_mlp_w_in', 'new_m_mlp_w_out', 'new_v_norm_mix', 'new_v_norm_mlp', 'new_v_norm_final', 'new_v_pool_w', 'new_v_pool_scale', 'new_v_gla_w_in', 'new_v_gla_w_up_f', 'new_v_gla_b_up_f', 'new_v_gla_w_up_b', 'new_v_gla_b_up_b', 'new_v_gla_g_norm', 'new_v_gla_w_out', 'new_v_mlp_w_in', 'new_v_mlp_w_out']
TWIN_LEAF_KINDS = {'loss': 'loss', 'grad_x': 'grad_x', 'grad_norm_mix': 'grad_w', 'grad_norm_mlp': 'grad_w', 'grad_norm_final': 'grad_w', 'grad_pool_w': 'grad_w', 'grad_pool_scale': 'grad_w', 'grad_gla_w_in': 'grad_w', 'grad_gla_w_up_f': 'grad_w', 'grad_gla_b_up_f': 'grad_w', 'grad_gla_w_up_b': 'grad_w', 'grad_gla_b_up_b': 'grad_w', 'grad_gla_g_norm': 'grad_w', 'grad_gla_w_out': 'grad_w', 'grad_mlp_w_in': 'grad_w', 'grad_mlp_w_out': 'grad_w', 'delta_norm_mix': 'delta_w', 'delta_norm_mlp': 'delta_w', 'delta_norm_final': 'delta_w', 'delta_pool_w': 'delta_w', 'delta_pool_scale': 'delta_w', 'delta_gla_w_in': 'delta_w', 'delta_gla_w_up_f': 'delta_w', 'delta_gla_b_up_f': 'delta_w', 'delta_gla_w_up_b': 'delta_w', 'delta_gla_b_up_b': 'delta_w', 'delta_gla_g_norm': 'delta_w', 'delta_gla_w_out': 'delta_w', 'delta_mlp_w_in': 'delta_w', 'delta_mlp_w_out': 'delta_w', 'new_m_norm_mix': 'new_m', 'new_m_norm_mlp': 'new_m', 'new_m_norm_final': 'new_m', 'new_m_pool_w': 'new_m', 'new_m_pool_scale': 'new_m', 'new_m_gla_w_in': 'new_m', 'new_m_gla_w_up_f': 'new_m', 'new_m_gla_b_up_f': 'new_m', 'new_m_gla_w_up_b': 'new_m', 'new_m_gla_b_up_b': 'new_m', 'new_m_gla_g_norm': 'new_m', 'new_m_gla_w_out': 'new_m', 'new_m_mlp_w_in': 'new_m', 'new_m_mlp_w_out': 'new_m', 'new_v_norm_mix': 'new_v', 'new_v_norm_mlp': 'new_v', 'new_v_norm_final': 'new_v', 'new_v_pool_w': 'new_v', 'new_v_pool_scale': 'new_v', 'new_v_gla_w_in': 'new_v', 'new_v_gla_w_up_f': 'new_v', 'new_v_gla_b_up_f': 'new_v', 'new_v_gla_w_up_b': 'new_v', 'new_v_gla_b_up_b': 'new_v', 'new_v_gla_g_norm': 'new_v', 'new_v_gla_w_out': 'new_v', 'new_v_mlp_w_in': 'new_v', 'new_v_mlp_w_out': 'new_v'}


def _forward(args):
    return _fwd_reference(*[args[k] for k in FWD_PARAMS])


def _output_shape():
    def fwd():
        inp = _fwd_setup_inputs(0)
        return _fwd_reference(*[inp[k] for k in FWD_PARAMS])
    out = _jax.eval_shape(fwd)
    return out.shape, out.dtype

N_MICROBATCH = 1
ADAM_LR = 0.001
ADAM_B1 = 0.9
ADAM_B2 = 0.999
ADAM_EPS = 1e-08
ADAM_WD = 0.01
ADAM_STEP = 10
PER_EXAMPLE_BATCH_AXIS = {'x': 0, 'loss_target': 0}
SHARED_INPUTS = []
_WEIGHT_DTYPES = {'norm_mix': _jnp.float32, 'norm_mlp': _jnp.float32, 'norm_final': _jnp.float32, 'pool_w': _jnp.float32, 'pool_scale': _jnp.float32, 'gla_w_in': _jnp.float32, 'gla_w_up_f': _jnp.float32, 'gla_b_up_f': _jnp.float32, 'gla_w_up_b': _jnp.float32, 'gla_b_up_b': _jnp.float32, 'gla_g_norm': _jnp.float32, 'gla_w_out': _jnp.float32, 'mlp_w_in': _jnp.float32, 'mlp_w_out': _jnp.float32}
MOMENT_SCALE = {'norm_mix': 3.396860e-02, 'norm_mlp': 3.550500e-02, 'norm_final': 8.108377e+00, 'pool_w': 3.715245e-02, 'pool_scale': 5.093796e-02, 'gla_w_in': 1.755299e-02, 'gla_w_up_f': 1.641412e-03, 'gla_b_up_f': 7.003930e-03, 'gla_w_up_b': 2.473689e-03, 'gla_b_up_b': 7.050598e-03, 'gla_g_norm': 2.899145e-02, 'gla_w_out': 1.463437e-02, 'mlp_w_in': 1.764391e-02, 'mlp_w_out': 3.544094e-02}


def _to_microbatches(a, axis):
    t = _jnp.moveaxis(a, axis, 0)
    t = t.reshape((N_MICROBATCH, t.shape[0] // N_MICROBATCH) + t.shape[1:])
    return _jnp.moveaxis(t, 1, axis + 1)


def setup_inputs(seed: int = 0) -> dict:
    inp = _fwd_setup_inputs(seed)
    key = _jax.random.fold_in(_jax.random.key(seed), 7919)
    shape, _ = _output_shape()
    out = dict(inp)
    out["loss_target"] = _jax.random.normal(_jax.random.fold_in(key, 0), shape, _jnp.float32)
    for i, name in enumerate(TWIN_WEIGHTS):
        w = inp[name].astype(_jnp.float32)
        if MOMENT_SCALE is None:
            s = _jnp.sqrt(_jnp.mean(_jnp.square(w)) + 1e-30)
        else:
            s = MOMENT_SCALE[name]
        km, kv = _jax.random.split(_jax.random.fold_in(key, i + 1))
        out[name] = w
        out["m_" + name] = s * _jax.random.normal(km, w.shape, _jnp.float32)
        out["v_" + name] = (s * s) * _jax.random.uniform(kv, w.shape, _jnp.float32, 0.5, 1.5)
    if N_MICROBATCH > 1:
        for name, axis in PER_EXAMPLE_BATCH_AXIS.items():
            out[name] = _to_microbatches(out[name], axis)
    return {'x': out['x'], 'norm_mix': out['norm_mix'], 'norm_mlp': out['norm_mlp'], 'norm_final': out['norm_final'], 'pool_w': out['pool_w'], 'pool_scale': out['pool_scale'], 'gla_w_in': out['gla_w_in'], 'gla_w_up_f': out['gla_w_up_f'], 'gla_b_up_f': out['gla_b_up_f'], 'gla_w_up_b': out['gla_w_up_b'], 'gla_b_up_b': out['gla_b_up_b'], 'gla_g_norm': out['gla_g_norm'], 'gla_w_out': out['gla_w_out'], 'mlp_w_in': out['mlp_w_in'], 'mlp_w_out': out['mlp_w_out'], 'loss_target': out['loss_target'], 'm_norm_mix': out['m_norm_mix'], 'm_norm_mlp': out['m_norm_mlp'], 'm_norm_final': out['m_norm_final'], 'm_pool_w': out['m_pool_w'], 'm_pool_scale': out['m_pool_scale'], 'm_gla_w_in': out['m_gla_w_in'], 'm_gla_w_up_f': out['m_gla_w_up_f'], 'm_gla_b_up_f': out['m_gla_b_up_f'], 'm_gla_w_up_b': out['m_gla_w_up_b'], 'm_gla_b_up_b': out['m_gla_b_up_b'], 'm_gla_g_norm': out['m_gla_g_norm'], 'm_gla_w_out': out['m_gla_w_out'], 'm_mlp_w_in': out['m_mlp_w_in'], 'm_mlp_w_out': out['m_mlp_w_out'], 'v_norm_mix': out['v_norm_mix'], 'v_norm_mlp': out['v_norm_mlp'], 'v_norm_final': out['v_norm_final'], 'v_pool_w': out['v_pool_w'], 'v_pool_scale': out['v_pool_scale'], 'v_gla_w_in': out['v_gla_w_in'], 'v_gla_w_up_f': out['v_gla_w_up_f'], 'v_gla_b_up_f': out['v_gla_b_up_f'], 'v_gla_w_up_b': out['v_gla_w_up_b'], 'v_gla_b_up_b': out['v_gla_b_up_b'], 'v_gla_g_norm': out['v_gla_g_norm'], 'v_gla_w_out': out['v_gla_w_out'], 'v_mlp_w_in': out['v_mlp_w_in'], 'v_mlp_w_out': out['v_mlp_w_out']}


def _loss(weights, diff, rest, loss_target):
    with _jax.named_scope("forward"):
        args = {**rest, TWIN_DIFF_INPUT: diff, **{k: w.astype(_WEIGHT_DTYPES[k]) for k, w in weights.items()}}
        y = _forward(args)
    with _jax.named_scope("loss_head"):
        err = _jnp.square(y.astype(_jnp.float32) - loss_target)
        return 0.5 * _jnp.sum(_jnp.mean(err, axis=-1)) if err.ndim else 0.5 * err


def _adamw(w, g, m, v):
    m = ADAM_B1 * m + (1.0 - ADAM_B1) * g
    v = ADAM_B2 * v + (1.0 - ADAM_B2) * _jnp.square(g)
    m_hat = m / (1.0 - ADAM_B1 ** ADAM_STEP)
    v_hat = v / (1.0 - ADAM_B2 ** ADAM_STEP)
    delta = -ADAM_LR * (m_hat / (_jnp.sqrt(v_hat) + ADAM_EPS) + ADAM_WD * w)
    return delta, m, v


def reference(x, norm_mix, norm_mlp, norm_final, pool_w, pool_scale, gla_w_in, gla_w_up_f, gla_b_up_f, gla_w_up_b, gla_b_up_b, gla_g_norm, gla_w_out, mlp_w_in, mlp_w_out, loss_target, m_norm_mix, m_norm_mlp, m_norm_final, m_pool_w, m_pool_scale, m_gla_w_in, m_gla_w_up_f, m_gla_b_up_f, m_gla_w_up_b, m_gla_b_up_b, m_gla_g_norm, m_gla_w_out, m_mlp_w_in, m_mlp_w_out, v_norm_mix, v_norm_mlp, v_norm_final, v_pool_w, v_pool_scale, v_gla_w_in, v_gla_w_up_f, v_gla_b_up_f, v_gla_w_up_b, v_gla_b_up_b, v_gla_g_norm, v_gla_w_out, v_mlp_w_in, v_mlp_w_out):
    given = dict(x=x, norm_mix=norm_mix, norm_mlp=norm_mlp, norm_final=norm_final, pool_w=pool_w, pool_scale=pool_scale, gla_w_in=gla_w_in, gla_w_up_f=gla_w_up_f, gla_b_up_f=gla_b_up_f, gla_w_up_b=gla_w_up_b, gla_b_up_b=gla_b_up_b, gla_g_norm=gla_g_norm, gla_w_out=gla_w_out, mlp_w_in=mlp_w_in, mlp_w_out=mlp_w_out, loss_target=loss_target, m_norm_mix=m_norm_mix, m_norm_mlp=m_norm_mlp, m_norm_final=m_norm_final, m_pool_w=m_pool_w, m_pool_scale=m_pool_scale, m_gla_w_in=m_gla_w_in, m_gla_w_up_f=m_gla_w_up_f, m_gla_b_up_f=m_gla_b_up_f, m_gla_w_up_b=m_gla_w_up_b, m_gla_b_up_b=m_gla_b_up_b, m_gla_g_norm=m_gla_g_norm, m_gla_w_out=m_gla_w_out, m_mlp_w_in=m_mlp_w_in, m_mlp_w_out=m_mlp_w_out, v_norm_mix=v_norm_mix, v_norm_mlp=v_norm_mlp, v_norm_final=v_norm_final, v_pool_w=v_pool_w, v_pool_scale=v_pool_scale, v_gla_w_in=v_gla_w_in, v_gla_w_up_f=v_gla_w_up_f, v_gla_b_up_f=v_gla_b_up_f, v_gla_w_up_b=v_gla_w_up_b, v_gla_b_up_b=v_gla_b_up_b, v_gla_g_norm=v_gla_g_norm, v_gla_w_out=v_gla_w_out, v_mlp_w_in=v_mlp_w_in, v_mlp_w_out=v_mlp_w_out)
    weights = {n: given[n] for n in TWIN_WEIGHTS}
    shared = {n: given[n] for n in SHARED_INPUTS}
    per_example = {n: given[n] for n in ['x']}
    grad_fn = _jax.value_and_grad(_loss, argnums=(0, 1))

    def one_microbatch(ex, loss_target):
        ex = dict(ex)
        diff = ex.pop(TWIN_DIFF_INPUT)
        return grad_fn(weights, diff, {**shared, **ex}, loss_target)

    if N_MICROBATCH == 1:
        loss, (grad_w, grad_x) = one_microbatch(per_example, given["loss_target"])
    else:
        def body(carry, xs):
            loss_sum, grad_sum = carry
            l_k, (gw_k, gx_k) = one_microbatch(xs[0], xs[1])
            with _jax.named_scope("update"):
                return (loss_sum + l_k, _jax.tree.map(_jnp.add, grad_sum, gw_k)), gx_k

        init = (_jnp.zeros((), _jnp.float32), _jax.tree.map(_jnp.zeros_like, weights))
        (loss, grad_w), grad_x = _jax.lax.scan(body, init, (per_example, given["loss_target"]))
    with _jax.named_scope("update"):
        delta_w, new_m, new_v = {}, {}, {}
        for n in TWIN_WEIGHTS:
            delta_w[n], new_m[n], new_v[n] = _adamw(weights[n], grad_w[n], given["m_" + n], given["v_" + n])
    return (loss, grad_x, *[grad_w[n] for n in TWIN_WEIGHTS], *[delta_w[n] for n in TWIN_WEIGHTS],
            *[new_m[n] for n in TWIN_WEIGHTS], *[new_v[n] for n in TWIN_WEIGHTS])
```

```python
import functools
import math

import jax
import jax.numpy as jnp
from jax import lax
from jax.experimental import pallas as pl
from jax.experimental.pallas import tpu as pltpu

F32 = jnp.float32
BF16 = jnp.bfloat16

N_HEADS = 4
N_GROUPS = 4
POOL_HALF = (1, 2, 4, 8)
GATE_RANK = 16
GATE_TAU = 16.0
CHUNK = 64
EPS = 1e-6
N_SHARDS = 4
R_PAD = 128
ONES_COL = 2 * GATE_RANK

ADAM_LR = 0.001
ADAM_B1 = 0.9
ADAM_B2 = 0.999
ADAM_EPS = 1e-08
ADAM_WD = 0.01
ADAM_STEP = 10

_NN = (((1,), (0,)), ((), ()))
_NT = (((1,), (1,)), ((), ()))
_TN = (((0,), (0,)), ((), ()))

VMEM_LIMIT = 56 * 1024 * 1024
MESH = pl.DeviceIdType.MESH
ANY = pl.BlockSpec(memory_space=pl.ANY)


def _tile(dim, pref):
    return pref if dim % pref == 0 else dim


def _params(*sem):
    return pltpu.CompilerParams(dimension_semantics=sem, vmem_limit_bytes=VMEM_LIMIT)


def _sds(shape, dtype):
    return jax.ShapeDtypeStruct(shape, dtype)


def _matmul(a, b, *, dims, grid, a_spec, b_spec, acc_shape, out_shapes, out_specs, epilogue,
            extras=(), extra_specs=(), name):
    nk = grid[2]
    n_extra = len(extras)
    n_out = len(out_shapes)

    def body(a_ref, b_ref, *rest):
        extra_refs = rest[:n_extra]
        out_refs = rest[n_extra:n_extra + n_out]
        acc_ref = rest[n_extra + n_out]
        kk = pl.program_id(2)
        part = lax.dot_general(a_ref[...], b_ref[...], dims, preferred_element_type=F32)

        if nk > 1:
            @pl.when(kk == 0)
            def _():
                acc_ref[...] = part

            @pl.when(kk > 0)
            def _():
                acc_ref[...] += part

        @pl.when(kk == nk - 1)
        def _():
            outs = epilogue(part if nk == 1 else acc_ref[...], *[r[...] for r in extra_refs])
            for o_ref, o in zip(out_refs, outs):
                o_ref[...] = o.astype(o_ref.dtype)

    return pl.pallas_call(
        body,
        grid=grid,
        in_specs=[a_spec, b_spec, *extra_specs],
        out_specs=list(out_specs),
        out_shape=list(out_shapes),
        scratch_shapes=[pltpu.VMEM(acc_shape, F32)],
        compiler_params=_params("parallel", "parallel", "arbitrary"),
        name=name,
    )(a, b, *extras)


def _mm(a, b, kind, *, out_dtypes, epilogue=None, extras=(), name, tm=1024, tn=1024, tk=512):
    if kind == "nn":
        (m, k), n = a.shape, b.shape[1]
    elif kind == "nt":
        (m, k), n = a.shape, b.shape[0]
    else:
        (k, m), n = a.shape, b.shape[1]
    tm, tn, tk = _tile(m, tm), _tile(n, tn), _tile(k, tk)
    if kind == "nn":
        a_spec = pl.BlockSpec((tm, tk), lambda i, j, kk: (i, kk))
        b_spec = pl.BlockSpec((tk, tn), lambda i, j, kk: (kk, j))
        dims = _NN
    elif kind == "nt":
        a_spec = pl.BlockSpec((tm, tk), lambda i, j, kk: (i, kk))
        b_spec = pl.BlockSpec((tn, tk), lambda i, j, kk: (j, kk))
        dims = _NT
    else:
        a_spec = pl.BlockSpec((tk, tm), lambda i, j, kk: (kk, i))
        b_spec = pl.BlockSpec((tk, tn), lambda i, j, kk: (kk, j))
        dims = _TN
    o_spec = pl.BlockSpec((tm, tn), lambda i, j, kk: (i, j))
    if epilogue is None:
        epilogue = lambda acc, *e: tuple(acc for _ in out_dtypes)
    return _matmul(
        a, b, dims=dims, grid=(m // tm, n // tn, k // tk), a_spec=a_spec, b_spec=b_spec, acc_shape=(tm, tn),
        out_shapes=[_sds((m, n), d) for d in out_dtypes], out_specs=[o_spec for _ in out_dtypes],
        epilogue=epilogue, extras=extras, extra_specs=[o_spec for _ in extras], name=name)


def _rmsnorm_fwd(h, g, out_dtype, name):
    t, d = h.shape
    tr = _tile(t, 256)

    def body(h_ref, g_ref, o_ref):
        x = h_ref[...]
        r = lax.rsqrt(jnp.mean(x * x, axis=-1, keepdims=True) + EPS)
        o_ref[...] = (x * r * g_ref[...]).astype(o_ref.dtype)

    return pl.pallas_call(
        body, grid=(t // tr,),
        in_specs=[pl.BlockSpec((tr, d), lambda i: (i, 0)), pl.BlockSpec((1, d), lambda i: (0, 0))],
        out_specs=pl.BlockSpec((tr, d), lambda i: (i, 0)),
        out_shape=_sds((t, d), out_dtype), compiler_params=_params("parallel"), name=name)(h, g)


def _rmsnorm_bwd(h, g, dy, resid, name):
    t, d = h.shape
    tr = _tile(t, 128)

    def body(h_ref, g_ref, dy_ref, res_ref, dh_ref, dhb_ref, dg_ref):
        x = h_ref[...]
        r = lax.rsqrt(jnp.mean(x * x, axis=-1, keepdims=True) + EPS)
        xn = x * r
        dyv = dy_ref[...]
        gdy = dyv * g_ref[...]
        dh = res_ref[...] + r * (gdy - xn * jnp.mean(gdy * xn, axis=-1, keepdims=True))
        dh_ref[...] = dh
        dhb_ref[...] = dh.astype(BF16)
        part = jnp.sum(dyv * xn, axis=0, keepdims=True)

        @pl.when(pl.program_id(0) == 0)
        def _():
            dg_ref[...] = part

        @pl.when(pl.program_id(0) > 0)
        def _():
            dg_ref[...] += part

    row = pl.BlockSpec((tr, d), lambda i: (i, 0))
    vec = pl.BlockSpec((1, d), lambda i: (0, 0))
    return pl.pallas_call(
        body, grid=(t // tr,), in_specs=[row, vec, row, row], out_specs=[row, row, vec],
        out_shape=[_sds((t, d), F32), _sds((t, d), BF16), _sds((1, d), F32)],
        compiler_params=_params("arbitrary"), name=name)(h, g, dy, resid)


def _final_bwd(h, g, tgt):
    t, d = h.shape
    tr = _tile(t, 128)

    def body(h_ref, g_ref, t_ref, loss_ref, dh_ref, dhb_ref, dg_ref):
        x = h_ref[...]
        r = lax.rsqrt(jnp.mean(x * x, axis=-1, keepdims=True) + EPS)
        xn = x * r
        gv = g_ref[...]
        e = xn * gv - t_ref[...]
        lpart = jnp.full((1, 128), 0.5 * jnp.sum(jnp.mean(e * e, axis=-1, keepdims=True)), F32)
        dyv = e * (1.0 / d)
        gdy = dyv * gv
        dh = r * (gdy - xn * jnp.mean(gdy * xn, axis=-1, keepdims=True))
        dh_ref[...] = dh
        dhb_ref[...] = dh.astype(BF16)
        part = jnp.sum(dyv * xn, axis=0, keepdims=True)

        @pl.when(pl.program_id(0) == 0)
        def _():
            dg_ref[...] = part
            loss_ref[...] = lpart

        @pl.when(pl.program_id(0) > 0)
        def _():
            dg_ref[...] += part
            loss_ref[...] += lpart

    row = pl.BlockSpec((tr, d), lambda i: (i, 0))
    vec = pl.BlockSpec((1, d), lambda i: (0, 0))
    return pl.pallas_call(
        body, grid=(t // tr,), in_specs=[row, vec, row],
        out_specs=[pl.BlockSpec((1, 128), lambda i: (0, 0)), row, row, vec],
        out_shape=[_sds((1, 128), F32), _sds((t, d), F32), _sds((t, d), BF16), _sds((1, d), F32)],
        compiler_params=_params("arbitrary"), name="final_loss_bwd")(h, g, tgt)


def _shift_rows(x, s, row):
    n = x.shape[0]
    y = pltpu.roll(x, (-s) % n, 0)
    return jnp.where((row + s >= 0) & (row + s < n), y, 0.0)


def _span_sum(x, start, length, row):
    if start >= 0:
        y, step = _shift_rows(x, start, row) if start else x, 1
    else:
        last = start + length - 1
        assert last <= 0
        y, step = _shift_rows(x, last, row) if last else x, -1
    n = 1
    while n < length:
        y = y + _shift_rows(y, step * n, row)
        n *= 2
    return y


def _pool_apply(x, seq, transpose, name):
    t, d = x.shape
    dg = d // N_GROUPS
    tc = _tile(dg, 256)
    nblk = dg // tc

    def body(x_ref, o_ref):
        grp = pl.program_id(1)
        row = lax.broadcasted_iota(jnp.int32, (seq, tc), 0)
        for gi, hw in enumerate(POOL_HALF):
            @pl.when(grp == gi)
            def _(hw=hw):
                xv = x_ref[...]
                cnt = (jnp.minimum(row + hw, seq) - jnp.maximum(row - hw, 0)).astype(F32)
                if not transpose:
                    w = _span_sum(xv, 0, hw, row) + _span_sum(xv, -hw, hw, row)
                    o_ref[...] = (w / cnt - xv).astype(o_ref.dtype)
                else:
                    u = xv / cnt
                    w = _span_sum(u, 1, hw, row) + _span_sum(u, -(hw - 1), hw, row)
                    o_ref[...] = (w - xv).astype(o_ref.dtype)

    spec = pl.BlockSpec((seq, tc), lambda b, g, j: (b, g * nblk + j))
    return pl.pallas_call(
        body, grid=(t // seq, N_GROUPS, nblk), in_specs=[spec], out_specs=spec,
        out_shape=_sds((t, d), F32 if transpose else BF16),
        compiler_params=_params("parallel", "parallel", "parallel"), name=name)(x)


def _pool_mm_fwd(dm, wp, x, scale):
    t, d = dm.shape
    dg = d // N_GROUPS
    rs = dg // N_SHARDS
    tm = _tile(t, 1024)
    o_spec = pl.BlockSpec((tm, dg), lambda i, j, kk: (i, j))
    return _matmul(
        dm, wp, dims=_NN, grid=(t // tm, N_GROUPS, N_SHARDS),
        a_spec=pl.BlockSpec((tm, rs), lambda i, j, kk: (i, j * N_SHARDS + kk)),
        b_spec=pl.BlockSpec((None, None, rs, dg), lambda i, j, kk: (kk, j, 0, 0)),
        acc_shape=(tm, dg), out_shapes=[_sds((t, d), F32), _sds((t, d), F32)], out_specs=[o_spec, o_spec],
        epilogue=lambda acc, xv, sc: (xv + acc * sc, acc),
        extras=(x, scale), extra_specs=[o_spec, pl.BlockSpec((1, dg), lambda i, j, kk: (0, j))], name="pool_mm_fwd")


def _pool_scale_bwd(dh, ypre, scale):
    t, d = dh.shape
    tr = _tile(t, 256)

    def body(dh_ref, y_ref, s_ref, o_ref, ds_ref):
        g = dh_ref[...]
        o_ref[...] = (g * s_ref[...]).astype(BF16)
        part = jnp.sum(g * y_ref[...], axis=0, keepdims=True)

        @pl.when(pl.program_id(0) == 0)
        def _():
            ds_ref[...] = part

        @pl.when(pl.program_id(0) > 0)
        def _():
            ds_ref[...] += part

    row = pl.BlockSpec((tr, d), lambda i: (i, 0))
    vec = pl.BlockSpec((1, d), lambda i: (0, 0))
    return pl.pallas_call(
        body, grid=(t // tr,), in_specs=[row, row, vec], out_specs=[row, vec],
        out_shape=[_sds((t, d), BF16), _sds((1, d), F32)], compiler_params=_params("arbitrary"),
        name="pool_scale_bwd")(dh, ypre, scale)


def _pool_mm_bwd_x(dys, wp):
    t, d = dys.shape
    dg = d // N_GROUPS
    rs = dg // N_SHARDS
    tm = _tile(t, 1024)
    return _matmul(
        dys, wp, dims=_NT, grid=(t // tm, N_GROUPS * N_SHARDS, 1),
        a_spec=pl.BlockSpec((tm, dg), lambda i, j, kk: (i, j // N_SHARDS)),
        b_spec=pl.BlockSpec((None, None, rs, dg), lambda i, j, kk: (j % N_SHARDS, j // N_SHARDS, 0, 0)),
        acc_shape=(tm, rs), out_shapes=[_sds((t, d), F32)],
        out_specs=[pl.BlockSpec((tm, rs), lambda i, j, kk: (i, j))],
        epilogue=lambda acc: (acc,), name="pool_mm_bwd_x")[0]


def _pool_mm_bwd_w(dm, dys):
    t, d = dm.shape
    dg = d // N_GROUPS
    rs = dg // N_SHARDS
    tk = _tile(t, 512)
    return _matmul(
        dm, dys, dims=_TN, grid=(N_GROUPS * N_SHARDS, 1, t // tk),
        a_spec=pl.BlockSpec((tk, rs), lambda i, j, kk: (kk, i)),
        b_spec=pl.BlockSpec((tk, dg), lambda i, j, kk: (kk, i // N_SHARDS)),
        acc_shape=(rs, dg), out_shapes=[_sds((N_SHARDS, N_GROUPS, rs, dg), BF16)],
        out_specs=[pl.BlockSpec((None, None, rs, dg), lambda i, j, kk: (i % N_SHARDS, i // N_SHARDS, 0, 0))],
        epilogue=lambda acc: (acc,), name="pool_mm_bwd_w")[0]


def _mlp_fwd(h, hn, w1g, w2):
    t, d = hn.shape
    f4 = w1g.shape[2]
    dff = N_SHARDS * f4
    tm, tn, tk = _tile(t, 1024), _tile(f4, 1024), _tile(d, 512)
    nb = f4 // tn
    o_spec = pl.BlockSpec((tm, tn), lambda i, j, kk: (i, j))

    def act(acc):
        r = jnp.maximum(acc, 0.0)
        return r, r * r

    r, u = _matmul(
        hn, w1g, dims=_NN, grid=(t // tm, dff // tn, d // tk),
        a_spec=pl.BlockSpec((tm, tk), lambda i, j, kk: (i, kk)),
        b_spec=pl.BlockSpec((None, tk, tn), lambda i, j, kk: (j // nb, kk, j % nb)),
        acc_shape=(tm, tn), out_shapes=[_sds((t, dff), BF16), _sds((t, dff), BF16)], out_specs=[o_spec, o_spec],
        epilogue=act, name="mlp_up")
    (out,) = _mm(u, w2, "nn", out_dtypes=[F32], epilogue=lambda acc, hv: (hv + acc,), extras=(h,), name="mlp_down")
    return out, r, u


def _mlp_bwd(dhb, hn, r, u, w1g, w2):
    t, d = hn.shape
    f4 = w1g.shape[2]
    dff = N_SHARDS * f4
    (da,) = _mm(dhb, w2, "nt", out_dtypes=[BF16], epilogue=lambda acc, rv: (acc * (2.0 * rv.astype(F32)),),
                extras=(r,), name="mlp_bwd_da")
    (dw2,) = _mm(u, dhb, "tn", out_dtypes=[BF16], name="mlp_bwd_dw2")
    tm, tn, tk = _tile(t, 1024), _tile(d, 1024), _tile(f4, 512)
    nbk = f4 // tk
    (dhn,) = _matmul(
        da, w1g, dims=_NT, grid=(t // tm, d // tn, dff // tk),
        a_spec=pl.BlockSpec((tm, tk), lambda i, j, kk: (i, kk)),
        b_spec=pl.BlockSpec((None, tn, tk), lambda i, j, kk: (kk // nbk, j, kk % nbk)),
        acc_shape=(tm, tn), out_shapes=[_sds((t, d), F32)],
        out_specs=[pl.BlockSpec((tm, tn), lambda i, j, kk: (i, j))], epilogue=lambda acc: (acc,), name="mlp_bwd_dhn")
    tm, tn, tk = _tile(d, 1024), _tile(f4, 1024), _tile(t, 512)
    nb = f4 // tn
    (dw1,) = _matmul(
        hn, da, dims=_TN, grid=(d // tm, dff // tn, t // tk),
        a_spec=pl.BlockSpec((tk, tm), lambda i, j, kk: (kk, i)),
        b_spec=pl.BlockSpec((tk, tn), lambda i, j, kk: (kk, j)),
        acc_shape=(tm, tn), out_shapes=[_sds((N_SHARDS, d, f4), BF16)],
        out_specs=[pl.BlockSpec((None, tm, tn), lambda i, j, kk: (j // nb, i, j % nb))],
        epilogue=lambda acc: (acc,), name="mlp_bwd_dw1")
    return dhn, dw1, dw2


def _split3(x):
    a = x.astype(BF16)
    r1 = x - a.astype(F32)
    b = r1.astype(BF16)
    c = (r1 - b.astype(F32)).astype(BF16)
    return a, b, c


def _dot(a, b, dims):
    return lax.dot_general(a.astype(BF16), b.astype(BF16), dims, preferred_element_type=F32)


def _chunk_terms(q, k, g, rev, scale):
    c = q.shape[0]
    ri = lax.broadcasted_iota(jnp.int32, (c, c), 0)
    ci = lax.broadcasted_iota(jnp.int32, (c, c), 1)
    seen = (ci >= ri) if rev else (ci <= ri)
    tri = seen.astype(BF16)
    g1, g2, g3 = _split3(g)
    b = (lax.dot_general(tri, g1, _NN, preferred_element_type=F32)
         + lax.dot_general(tri, g2, _NN, preferred_element_type=F32)
         + lax.dot_general(tri, g3, _NN, preferred_element_type=F32))
    mid = c // 2 if rev else c // 2 - 1
    last = 0 if rev else c - 1
    rows = lax.broadcasted_iota(jnp.int32, b.shape, 0)
    b_mid = jnp.sum(jnp.where(rows == mid, b, 0.0), axis=0, keepdims=True)
    b_last = jnp.sum(jnp.where(rows == last, b, 0.0), axis=0, keepdims=True)
    qs = q * scale
    e1 = jnp.exp(b - b_mid)
    e2 = jnp.exp(b_mid - b)
    eb = jnp.exp(b)
    el = jnp.exp(b_last - b)
    return dict(seen=seen, tri=tri, mid=mid, last=last, e1=e1, e2=e2, eb=eb, el=el, a=jnp.exp(b_last),
                qe=qs * e1, ke=k * e2, qi=qs * eb, ks=k * el)


def _gla_fwd(pm, la, seq):
    t = pm.shape[0]
    d = pm.shape[1] // 3
    dk, dv = d // 2 // N_HEADS, d // N_HEADS
    nb, nc = t // seq, seq // CHUNK
    scale = dk ** -0.5
    kq, kk_, kv = 0, N_HEADS, (d // dv)

    def body(qf, kf, vf, gf, qb, kb, vb, gb, of_ref, ob_ref, stf_ref, stb_ref, sf, sb):
        n = pl.program_id(1)

        @pl.when(n == 0)
        def _():
            sf[...] = jnp.zeros_like(sf)
            sb[...] = jnp.zeros_like(sb)

        for (q_ref, k_ref, v_ref, g_ref, o_ref, st_ref, s_ref, rev) in (
                (qf, kf, vf, gf, of_ref, stf_ref, sf, False), (qb, kb, vb, gb, ob_ref, stb_ref, sb, True)):
            tm = _chunk_terms(q_ref[...], k_ref[...], g_ref[...], rev, scale)
            v = v_ref[...]
            st = s_ref[...]
            stb = st.astype(BF16)
            st_ref[...] = stb
            sc = jnp.where(tm["seen"], _dot(tm["qe"], tm["ke"], _NT), 0.0)
            o = _dot(sc, v, _NN) + lax.dot_general(tm["qi"].astype(BF16), stb, _NT, preferred_element_type=F32)
            o_ref[...] = o
            s_ref[...] = st * tm["a"] + _dot(v, tm["ks"], _TN)

    def row(bh, n, rev):
        return (bh // N_HEADS) * nc + (nc - 1 - n if rev else n)

    def specs(rev):
        return [
            pl.BlockSpec((CHUNK, dk), lambda bh, n: (row(bh, n, rev), kq + bh % N_HEADS)),
            pl.BlockSpec((CHUNK, dk), lambda bh, n: (row(bh, n, rev), kk_ + bh % N_HEADS)),
            pl.BlockSpec((CHUNK, dv), lambda bh, n: (row(bh, n, rev), kv + bh % N_HEADS)),
            pl.BlockSpec((CHUNK, dk), lambda bh, n: (row(bh, n, rev), (N_HEADS if rev else 0) + bh % N_HEADS)),
        ]

    def o_spec(rev):
        return pl.BlockSpec((CHUNK, dv), lambda bh, n: (row(bh, n, rev), bh % N_HEADS))

    def st_spec(rev):
        return pl.BlockSpec((None, None, dv, dk), lambda bh, n: (bh, nc - 1 - n if rev else n, 0, 0))

    sf_, sb_ = specs(False), specs(True)
    return pl.pallas_call(
        body, grid=(nb * N_HEADS, nc),
        in_specs=[*sf_, *sb_],
        out_specs=[o_spec(False), o_spec(True), st_spec(False), st_spec(True)],
        out_shape=[_sds((t, d), F32), _sds((t, d), F32),
                   _sds((nb * N_HEADS, nc, dv, dk), BF16), _sds((nb * N_HEADS, nc, dv, dk), BF16)],
        scratch_shapes=[pltpu.VMEM((dv, dk), F32), pltpu.VMEM((dv, dk), F32)],
        compiler_params=_params("parallel", "arbitrary"), name="gla_scan_fwd",
    )(pm, pm, pm, la, pm, pm, pm, la)


def _gla_bwd(pm, la, do, st_f, st_b, seq):
    t = pm.shape[0]
    d = pm.shape[1] // 3
    dk, dv = d // 2 // N_HEADS, d // N_HEADS
    kd = dk * N_HEADS
    nb, nc = t // seq, seq // CHUNK
    scale = dk ** -0.5
    kq, kk_, kv = 0, N_HEADS, (d // dv)

    def body(qf, kf, vf, gf, dof, stf, qb, kb, vb, gb, dob, stb_,
             dqf, dkf, dvf, dgf, dqb, dkb, dvb, dgb, dsf, dsb):
        n = pl.program_id(1)

        @pl.when(n == 0)
        def _():
            dsf[...] = jnp.zeros_like(dsf)
            dsb[...] = jnp.zeros_like(dsb)

        for (q_ref, k_ref, v_ref, g_ref, do_ref, st_ref, dq_ref, dk_ref, dv_ref, dg_ref, ds_ref, rev) in (
                (qf, kf, vf, gf, dof, stf, dqf, dkf, dvf, dgf, dsf, False),
                (qb, kb, vb, gb, dob, stb_, dqb, dkb, dvb, dgb, dsb, True)):
            tm = _chunk_terms(q_ref[...], k_ref[...], g_ref[...], rev, scale)
            v = v_ref[...]
            dov = do_ref[...]
            st = st_ref[...]
            ds = ds_ref[...]
            dsb16 = ds.astype(BF16)
            sc = jnp.where(tm["seen"], _dot(tm["qe"], tm["ke"], _NT), 0.0)
            dsc = jnp.where(tm["seen"], _dot(dov, v, _NT), 0.0)
            dv_ref[...] = _dot(sc, dov, _TN) + lax.dot_general(tm["ks"].astype(BF16), dsb16, _NT,
                                                               preferred_element_type=F32)
            dqe = _dot(dsc, tm["ke"], _NN)
            dke = _dot(dsc, tm["qe"], _TN)
            dqi = lax.dot_general(dov.astype(BF16), st, _NN, preferred_element_type=F32)
            dks = lax.dot_general(v.astype(BF16), dsb16, _NN, preferred_element_type=F32)
            da = jnp.sum(ds * st.astype(F32), axis=0, keepdims=True)
            dq_ref[...] = (dqe * tm["e1"] + dqi * tm["eb"]) * scale
            dk_ref[...] = dke * tm["e2"] + dks * tm["el"]
            t_q, t_k, t_s = dqe * tm["qe"], dke * tm["ke"], dks * tm["ks"]
            db = t_q - t_k + dqi * tm["qi"] - t_s
            mid_row = jnp.sum(t_k - t_q, axis=0, keepdims=True)
            last_row = jnp.sum(t_s, axis=0, keepdims=True) + da * tm["a"]
            ridx = lax.broadcasted_iota(jnp.int32, db.shape, 0)
            db = db + jnp.where(ridx == tm["mid"], mid_row, 0.0) + jnp.where(ridx == tm["last"], last_row, 0.0)
            d1, d2, d3 = _split3(db)
            dg_ref[...] = (lax.dot_general(tm["tri"], d1, _TN, preferred_element_type=F32)
                           + lax.dot_general(tm["tri"], d2, _TN, preferred_element_type=F32)
                           + lax.dot_general(tm["tri"], d3, _TN, preferred_element_type=F32))
            ds_ref[...] = ds * tm["a"] + _dot(dov, tm["qi"], _TN)

    def row(bh, n, rev):
        return (bh // N_HEADS) * nc + (n if rev else nc - 1 - n)

    def specs(rev):
        return [
            pl.BlockSpec((CHUNK, dk), lambda bh, n: (row(bh, n, rev), kq + bh % N_HEADS)),
            pl.BlockSpec((CHUNK, dk), lambda bh, n: (row(bh, n, rev), kk_ + bh % N_HEADS)),
            pl.BlockSpec((CHUNK, dv), lambda bh, n: (row(bh, n, rev), kv + bh % N_HEADS)),
            pl.BlockSpec((CHUNK, dk), lambda bh, n: (row(bh, n, rev), (N_HEADS if rev else 0) + bh % N_HEADS)),
            pl.BlockSpec((CHUNK, dv), lambda bh, n: (row(bh, n, rev), bh % N_HEADS)),
            pl.BlockSpec((None, None, dv, dk), lambda bh, n: (bh, n if rev else nc - 1 - n, 0, 0)),
        ]

    def outs(rev):
        return [
            pl.BlockSpec((CHUNK, dk), lambda bh, n: (row(bh, n, rev), bh % N_HEADS)),
            pl.BlockSpec((CHUNK, dk), lambda bh, n: (row(bh, n, rev), bh % N_HEADS)),
            pl.BlockSpec((CHUNK, dv), lambda bh, n: (row(bh, n, rev), bh % N_HEADS)),
            pl.BlockSpec((CHUNK, dk), lambda bh, n: (row(bh, n, rev), bh % N_HEADS)),
        ]

    of_, ob_ = outs(False), outs(True)
    res = pl.pallas_call(
        body, grid=(nb * N_HEADS, nc),
        in_specs=[*specs(False), *specs(True)],
        out_specs=[*of_, *ob_],
        out_shape=[_sds((t, kd), F32), _sds((t, kd), F32), _sds((t, d), F32), _sds((t, kd), F32),
                   _sds((t, kd), F32), _sds((t, kd), F32), _sds((t, d), F32), _sds((t, kd), F32)],
        scratch_shapes=[pltpu.VMEM((dv, dk), F32), pltpu.VMEM((dv, dk), F32)],
        compiler_params=_params("parallel", "arbitrary"), name="gla_scan_bwd",
    )(pm, pm, pm, la, do, st_f, pm, pm, pm, la, do, st_b)
    return res


def _sigmoid(x):
    return 1.0 / (1.0 + jnp.exp(-x))


def _gla_post_fwd(o_f, o_b, pm, gn):
    t, d = o_f.shape
    dv = d // N_HEADS
    tr = _tile(t, 512)
    gate_blk = 2 * d // dv

    def body(of_ref, ob_ref, gt_ref, gn_ref, out_ref):
        o = of_ref[...] + ob_ref[...]
        n = o * lax.rsqrt(jnp.mean(o * o, axis=-1, keepdims=True) + EPS) * gn_ref[...]
        gt = gt_ref[...]
        out_ref[...] = (n * (gt * _sigmoid(gt))).astype(BF16)

    blk = pl.BlockSpec((tr, dv), lambda i, h: (i, h))
    return pl.pallas_call(
        body, grid=(t // tr, N_HEADS),
        in_specs=[blk, blk, pl.BlockSpec((tr, dv), lambda i, h: (i, gate_blk + h)),
                  pl.BlockSpec((1, dv), lambda i, h: (0, 0))],
        out_specs=blk, out_shape=_sds((t, d), BF16), compiler_params=_params("parallel", "parallel"),
        name="gla_post_fwd")(o_f, o_b, pm, gn)


def _gla_post_bwd(dog, o_f, o_b, pm, gn):
    t, d = o_f.shape
    dv = d // N_HEADS
    tr = _tile(t, 256)
    gate_blk = 2 * d // dv

    def body(dog_ref, of_ref, ob_ref, gt_ref, gn_ref, do_ref, dgt_ref, dgn_ref):
        o = of_ref[...] + ob_ref[...]
        rr = lax.rsqrt(jnp.mean(o * o, axis=-1, keepdims=True) + EPS)
        on = o * rr
        gnv = gn_ref[...]
        gt = gt_ref[...]
        sg = _sigmoid(gt)
        sl = gt * sg
        dg_out = dog_ref[...]
        dn = dg_out * sl
        dgt_ref[...] = dg_out * (on * gnv) * (sg * (1.0 + gt * (1.0 - sg)))
        gdn = dn * gnv
        do_ref[...] = rr * (gdn - on * jnp.mean(gdn * on, axis=-1, keepdims=True))
        part = jnp.sum(dn * on, axis=0, keepdims=True)
        first = (pl.program_id(0) == 0) & (pl.program_id(1) == 0)

        @pl.when(first)
        def _():
            dgn_ref[...] = part

        @pl.when(jnp.logical_not(first))
        def _():
            dgn_ref[...] += part

    blk = pl.BlockSpec((tr, dv), lambda i, h: (i, h))
    vec = pl.BlockSpec((1, dv), lambda i, h: (0, 0))
    return pl.pallas_call(
        body, grid=(t // tr, N_HEADS),
        in_specs=[blk, blk, blk, pl.BlockSpec((tr, dv), lambda i, h: (i, gate_blk + h)), vec],
        out_specs=[blk, blk, vec], out_shape=[_sds((t, d), F32), _sds((t, d), F32), _sds((1, dv), F32)],
        compiler_params=_params("arbitrary", "arbitrary"), name="gla_post_bwd")(dog, o_f, o_b, pm, gn)


def _gla_dp(dq_f, dq_b, dk_f, dk_b, dv_f, dv_b, dgate):
    t, d = dv_f.shape
    kd = dq_f.shape[1]
    tr = _tile(t, 128)

    def body(a1, a2, b1, b2, c1, c2, g, o_ref):
        o_ref[:, 0:kd] = (a1[...] + a2[...]).astype(BF16)
        o_ref[:, kd:2 * kd] = (b1[...] + b2[...]).astype(BF16)
        o_ref[:, 2 * kd:2 * kd + d] = (c1[...] + c2[...]).astype(BF16)
        o_ref[:, 2 * kd + d:] = g[...].astype(BF16)

    sk = pl.BlockSpec((tr, kd), lambda i: (i, 0))
    sd = pl.BlockSpec((tr, d), lambda i: (i, 0))
    return pl.pallas_call(
        body, grid=(t // tr,), in_specs=[sk, sk, sk, sk, sd, sd, sd],
        out_specs=pl.BlockSpec((tr, 3 * d), lambda i: (i, 0)), out_shape=_sds((t, 3 * d), BF16),
        compiler_params=_params("parallel"), name="gla_dp")(dq_f, dq_b, dk_f, dk_b, dv_f, dv_b, dgate)


def _adamw(w, g, m, v, name):
    r, c = w.shape
    tr = _tile(r, 128)
    bc1 = 1.0 - ADAM_B1 ** ADAM_STEP
    bc2 = 1.0 - ADAM_B2 ** ADAM_STEP

    def body(w_ref, g_ref, m_ref, v_ref, d_ref, nm_ref, nv_ref):
        gv = g_ref[...]
        mn = ADAM_B1 * m_ref[...] + (1.0 - ADAM_B1) * gv
        vn = ADAM_B2 * v_ref[...] + (1.0 - ADAM_B2) * (gv * gv)
        m_hat = mn / bc1
        v_hat = vn / bc2
        d_ref[...] = -ADAM_LR * (m_hat / (jnp.sqrt(v_hat) + ADAM_EPS) + ADAM_WD * w_ref[...])
        nm_ref[...] = mn
        nv_ref[...] = vn

    blk = pl.BlockSpec((tr, c), lambda i: (i, 0))
    return pl.pallas_call(
        body, grid=(r // tr,), in_specs=[blk] * 4, out_specs=[blk] * 3, out_shape=[_sds((r, c), F32)] * 3,
        compiler_params=_params("parallel"), name=name)(w, g, m, v)


def _sum8(parts):
    _, n, _ = parts.shape

    def body(p_ref, o_ref):
        acc = p_ref[0]
        for i in range(1, 8):
            acc = acc + p_ref[i]
        o_ref[...] = acc

    return pl.pallas_call(body, out_shape=_sds((n, 128), F32), name="sum8")(parts)


def _pair_sum(ga, recv, c):
    _, _, rh, cols = ga.shape
    tr = _tile(rh, 256)

    def body(c_ref, a_ref, b_ref, o_ref):
        o_ref[...] = (a_ref[...].astype(F32) + b_ref[...].astype(F32)).astype(BF16)

    grid_spec = pltpu.PrefetchScalarGridSpec(
        num_scalar_prefetch=1, grid=(N_SHARDS, rh // tr),
        in_specs=[pl.BlockSpec((None, None, tr, cols), lambda s, i, c_ref: (s, c_ref[0], i, 0)),
                  pl.BlockSpec((None, tr, cols), lambda s, i, c_ref: (s, i, 0))],
        out_specs=pl.BlockSpec((None, tr, cols), lambda s, i, c_ref: (s, i, 0)))
    return pl.pallas_call(
        body, grid_spec=grid_spec, out_shape=_sds((N_SHARDS, rh, cols), BF16),
        compiler_params=_params("parallel", "parallel"), name="grad_pair_sum")(c, ga, recv)


def _quad_sum(parts):
    _, rh, cols = parts.shape
    tr = _tile(rh, 256)

    def body(p_ref, o_ref):
        o_ref[...] = ((p_ref[0].astype(F32) + p_ref[1].astype(F32)) + p_ref[2].astype(F32)) + p_ref[3].astype(F32)

    return pl.pallas_call(
        body, grid=(rh // tr,), in_specs=[pl.BlockSpec((N_SHARDS, tr, cols), lambda i: (0, i, 0))],
        out_specs=pl.BlockSpec((tr, cols), lambda i: (i, 0)), out_shape=_sds((rh, cols), F32),
        compiler_params=_params("parallel"), name="grad_quad_sum")(parts)


def _coords():
    return lax.axis_index("x"), lax.axis_index("y"), lax.axis_index("c")


def _other_chips(x, y):
    return [(x, 1 - y), (1 - x, y), (1 - x, 1 - y)]


def _bcast8(buf):
    n = buf.shape[0]

    def body(b_ref, o_ref, send_sems, recv_sems):
        x, y, c = _coords()
        me = 4 * x + 2 * y + c
        o_ref[me] = b_ref[...]
        copies = []
        for k in range(1, 8):
            peer = (x ^ (k >> 2), y ^ ((k >> 1) & 1), c ^ (k & 1))
            copies.append(pltpu.make_async_remote_copy(
                src_ref=b_ref, dst_ref=o_ref.at[me], send_sem=send_sems.at[k - 1], recv_sem=recv_sems.at[k - 1],
                device_id=peer, device_id_type=MESH))
        for cp in copies:
            cp.start()
        for k in range(1, 8):
            pltpu.make_async_remote_copy(
                src_ref=b_ref, dst_ref=o_ref.at[me ^ k], send_sem=send_sems.at[k - 1], recv_sem=recv_sems.at[k - 1],
                device_id=(x, y, c), device_id_type=MESH).wait_recv()
        for cp in copies:
            cp.wait_send()

    return pl.pallas_call(
        body, out_shape=_sds((8, n, 128), F32),
        in_specs=[pl.BlockSpec(memory_space=pltpu.VMEM)], out_specs=pl.BlockSpec(memory_space=pltpu.VMEM),
        scratch_shapes=[pltpu.SemaphoreType.DMA((7,)), pltpu.SemaphoreType.DMA((7,))],
        compiler_params=pltpu.CompilerParams(vmem_limit_bytes=VMEM_LIMIT), name="bcast8")(buf)


def _gather_weights(shards):
    n = len(shards)

    def body(*refs):
        ins, outs = refs[:n], refs[n:2 * n]
        loc_sem, send1, recv1, send2, recv2 = refs[2 * n:]
        x, y, c = _coords()
        s_me = 2 * x + y
        chips = _other_chips(x, y)
        sib = (x, y, 1 - c)
        local, first, passed = [], [], []
        for i in range(n):
            cp = pltpu.make_async_copy(ins[i], outs[i].at[s_me], loc_sem.at[i])
            cp.start()
            local.append(cp)
        for i in range(n):
            for j, (px, py) in enumerate(chips):
                cp = pltpu.make_async_remote_copy(
                    src_ref=ins[i].at[c], dst_ref=outs[i].at[s_me, c], send_sem=send1.at[i, j],
                    recv_sem=recv1.at[i, j], device_id=(px, py, c), device_id_type=MESH)
                cp.start()
                first.append(cp)
        for i in range(n):
            for j, (px, py) in enumerate(chips):
                got = outs[i].at[2 * px + py, c]
                pltpu.make_async_remote_copy(
                    src_ref=got, dst_ref=got, send_sem=send1.at[i, j], recv_sem=recv1.at[i, j],
                    device_id=(x, y, c), device_id_type=MESH).wait_recv()
                cp = pltpu.make_async_remote_copy(
                    src_ref=got, dst_ref=got, send_sem=send2.at[i, j], recv_sem=recv2.at[i, j],
                    device_id=sib, device_id_type=MESH)
                cp.start()
                passed.append(cp)
        for i in range(n):
            for j, (px, py) in enumerate(chips):
                other = outs[i].at[2 * px + py, 1 - c]
                pltpu.make_async_remote_copy(
                    src_ref=other, dst_ref=other, send_sem=send2.at[i, j], recv_sem=recv2.at[i, j],
                    device_id=(x, y, c), device_id_type=MESH).wait_recv()
        for cp in first + passed:
            cp.wait_send()
        for cp in local:
            cp.wait()

    return pl.pallas_call(
        body, out_shape=[_sds((N_SHARDS,) + s.shape, s.dtype) for s in shards],
        in_specs=[ANY] * n, out_specs=[ANY] * n,
        scratch_shapes=[pltpu.SemaphoreType.DMA((n,)), pltpu.SemaphoreType.DMA((n, 3)),
                        pltpu.SemaphoreType.DMA((n, 3)), pltpu.SemaphoreType.DMA((n, 3)),
                        pltpu.SemaphoreType.DMA((n, 3))],
        name="gather_weights")(*shards)


def _swap_halves(grads):
    n = len(grads)

    def body(*refs):
        ins, outs = refs[:n], refs[n:2 * n]
        send, recv = refs[2 * n:]
        x, y, c = _coords()
        copies = []
        for i in range(n):
            for s in range(N_SHARDS):
                cp = pltpu.make_async_remote_copy(
                    src_ref=ins[i].at[s, 1 - c], dst_ref=outs[i].at[s], send_sem=send.at[i, s], recv_sem=recv.at[i, s],
                    device_id=(x, y, 1 - c), device_id_type=MESH)
                cp.start()
                copies.append(cp)
        for cp in copies:
            cp.wait()

    return pl.pallas_call(
        body, out_shape=[_sds((N_SHARDS,) + g.shape[2:], g.dtype) for g in grads],
        in_specs=[ANY] * n, out_specs=[ANY] * n,
        scratch_shapes=[pltpu.SemaphoreType.DMA((n, N_SHARDS)), pltpu.SemaphoreType.DMA((n, N_SHARDS))],
        name="grad_swap_halves")(*grads)


def _scatter_chips(parts):
    n = len(parts)

    def body(*refs):
        ins, outs = refs[:n], refs[n:2 * n]
        loc_sem, send, recv = refs[2 * n:]
        x, y, c = _coords()
        s_me = 2 * x + y
        chips = _other_chips(x, y)
        local, copies = [], []
        for i in range(n):
            cp = pltpu.make_async_copy(ins[i].at[s_me], outs[i].at[s_me], loc_sem.at[i])
            cp.start()
            local.append(cp)
        for i in range(n):
            for j, (px, py) in enumerate(chips):
                cp = pltpu.make_async_remote_copy(
                    src_ref=ins[i].at[2 * px + py], dst_ref=outs[i].at[s_me], send_sem=send.at[i, j],
                    recv_sem=recv.at[i, j], device_id=(px, py, c), device_id_type=MESH)
                cp.start()
                copies.append(cp)
        for i in range(n):
            for j, (px, py) in enumerate(chips):
                got = outs[i].at[2 * px + py]
                pltpu.make_async_remote_copy(
                    src_ref=got, dst_ref=got, send_sem=send.at[i, j], recv_sem=recv.at[i, j],
                    device_id=(x, y, c), device_id_type=MESH).wait_recv()
        for cp in copies:
            cp.wait_send()
        for cp in local:
            cp.wait()

    return pl.pallas_call(
        body, out_shape=[_sds(p.shape, p.dtype) for p in parts],
        in_specs=[ANY] * n, out_specs=[ANY] * n,
        scratch_shapes=[pltpu.SemaphoreType.DMA((n,)), pltpu.SemaphoreType.DMA((n, 3)),
                        pltpu.SemaphoreType.DMA((n, 3))],
        name="grad_scatter_chips")(*parts)


def _join_halves(halves):
    n = len(halves)

    def body(*refs):
        ins, outs = refs[:n], refs[n:2 * n]
        loc_sem, send, recv = refs[2 * n:]
        x, y, c = _coords()
        local, copies = [], []
        for i in range(n):
            cp = pltpu.make_async_copy(ins[i], outs[i].at[c], loc_sem.at[i])
            cp.start()
            local.append(cp)
            cp = pltpu.make_async_remote_copy(
                src_ref=ins[i], dst_ref=outs[i].at[c], send_sem=send.at[i], recv_sem=recv.at[i],
                device_id=(x, y, 1 - c), device_id_type=MESH)
            cp.start()
            copies.append(cp)
        for i in range(n):
            got = outs[i].at[1 - c]
            pltpu.make_async_remote_copy(
                src_ref=got, dst_ref=got, send_sem=send.at[i], recv_sem=recv.at[i],
                device_id=(x, y, c), device_id_type=MESH).wait_recv()
        for cp in copies:
            cp.wait_send()
        for cp in local:
            cp.wait()

    return pl.pallas_call(
        body, out_shape=[_sds((2,) + h.shape, h.dtype) for h in halves],
        in_specs=[ANY] * n, out_specs=[ANY] * n,
        scratch_shapes=[pltpu.SemaphoreType.DMA((n,)), pltpu.SemaphoreType.DMA((n,)), pltpu.SemaphoreType.DMA((n,))],
        name="grad_join_halves")(*halves)


def _to_rows(vec):
    n = -(-vec.shape[0] // 1024) * 1024
    return jnp.pad(vec, (0, n - vec.shape[0])).reshape(-1, 128)


def _halves(a):
    cols = a.shape[-1]
    rows = math.prod(a.shape[:-1])
    return a.reshape(2, rows // 2, cols)


def kernel(x, norm_mix, norm_mlp, norm_final, pool_w, pool_scale, gla_w_in, gla_w_up_f, gla_b_up_f, gla_w_up_b, gla_b_up_b, gla_g_norm, gla_w_out, mlp_w_in, mlp_w_out, loss_target, m_norm_mix, m_norm_mlp, m_norm_final, m_pool_w, m_pool_scale, m_gla_w_in, m_gla_w_up_f, m_gla_b_up_f, m_gla_w_up_b, m_gla_b_up_b, m_gla_g_norm, m_gla_w_out, m_mlp_w_in, m_mlp_w_out, v_norm_mix, v_norm_mlp, v_norm_final, v_pool_w, v_pool_scale, v_gla_w_in, v_gla_w_up_f, v_gla_b_up_f, v_gla_w_up_b, v_gla_b_up_b, v_gla_g_norm, v_gla_w_out, v_mlp_w_in, v_mlp_w_out):
    nb, seq, d = x.shape
    t = nb * seq
    dg = d // N_GROUPS
    kd = d // 2
    dv = d // N_HEADS
    pw = gla_w_in.shape[2]
    f4 = mlp_w_in.shape[2]
    dff = N_SHARDS * f4
    cx, cy, cc = _coords()
    s_me = 2 * cx + cy
    c_arr = jnp.reshape(cc, (1,)).astype(jnp.int32)

    xf = x.reshape(t, d)
    tgt = loss_target.reshape(t, d)

    big = [pool_w[0].reshape(N_GROUPS * (dg // N_SHARDS), dg), mlp_w_in[0], mlp_w_out[0], gla_w_in[0], gla_w_out[0],
           mlp_w_in[1], mlp_w_out[1]]
    gathered = _gather_weights([_halves(w.astype(BF16)) for w in big])
    wp = gathered[0].reshape(N_SHARDS, N_GROUPS, dg // N_SHARDS, dg)
    w1g = [gathered[1].reshape(N_SHARDS, d, f4), gathered[5].reshape(N_SHARDS, d, f4)]
    w2g = [gathered[2].reshape(dff, d), gathered[6].reshape(dff, d)]
    win = jnp.transpose(gathered[3].reshape(N_SHARDS, d, pw), (1, 0, 2)).reshape(d, N_SHARDS * pw)
    w_main = win[:, :3 * d]
    w_r = jnp.pad(win[:, 3 * d:], ((0, 0), (0, R_PAD - 2 * GATE_RANK)))
    wout = gathered[4].reshape(d, d)

    ks = kd // N_SHARDS
    small = jnp.concatenate([gla_w_up_f[0].reshape(-1), gla_w_up_b[0].reshape(-1), gla_b_up_f[0], gla_b_up_b[0],
                             gla_g_norm[0]])
    small_all = _bcast8(_to_rows(small))[::2].reshape(N_SHARDS, -1)
    o = 0
    wuf = jnp.transpose(small_all[:, o:o + GATE_RANK * ks].reshape(N_SHARDS, GATE_RANK, ks), (1, 0, 2)).reshape(GATE_RANK, kd)
    o += GATE_RANK * ks
    wub = jnp.transpose(small_all[:, o:o + GATE_RANK * ks].reshape(N_SHARDS, GATE_RANK, ks), (1, 0, 2)).reshape(GATE_RANK, kd)
    o += GATE_RANK * ks
    buf = small_all[:, o:o + ks].reshape(1, kd)
    o += ks
    bub = small_all[:, o:o + ks].reshape(1, kd)
    o += ks
    gn = small_all[:, o:o + dv // N_SHARDS].reshape(1, dv)
    w_up = jnp.zeros((R_PAD, 2 * kd), F32).at[:GATE_RANK, :kd].set(wuf).at[GATE_RANK:2 * GATE_RANK, kd:].set(wub)
    w_up = w_up.astype(BF16)
    b_up = jnp.concatenate([buf, bub], axis=1)

    hn0 = _rmsnorm_fwd(xf, norm_mix[0:1], F32, "norm_mix0")
    dm = _pool_apply(hn0, seq, False, "pool_diff")
    h1, ypre = _pool_mm_fwd(dm, wp, xf, pool_scale)
    hn1 = _rmsnorm_fwd(h1, norm_mlp[0:1], BF16, "norm_mlp0")
    h2, r0, u0 = _mlp_fwd(h1, hn1, w1g[0], w2g[0])
    hn2 = _rmsnorm_fwd(h2, norm_mix[1:2], BF16, "norm_mix1")
    (pm,) = _mm(hn2, w_main, "nn", out_dtypes=[F32], name="gla_proj")
    (pr,) = _mm(hn2, w_r, "nn", out_dtypes=[BF16], name="gla_proj_r")
    pr = pr.at[:, ONES_COL].set(1.0)

    def log_decay(acc, bv):
        z = acc + bv
        return ((jnp.minimum(z, 0.0) - jnp.log(1.0 + jnp.exp(-jnp.abs(z)))) / GATE_TAU,)

    tm_, tn_ = _tile(t, 1024), _tile(2 * kd, 1024)
    (la,) = _matmul(
        pr, w_up, dims=_NN, grid=(t // tm_, 2 * kd // tn_, 1),
        a_spec=pl.BlockSpec((tm_, R_PAD), lambda i, j, kk: (i, 0)),
        b_spec=pl.BlockSpec((R_PAD, tn_), lambda i, j, kk: (0, j)), acc_shape=(tm_, tn_),
        out_shapes=[_sds((t, 2 * kd), F32)], out_specs=[pl.BlockSpec((tm_, tn_), lambda i, j, kk: (i, j))],
        epilogue=log_decay, extras=(b_up,), extra_specs=[pl.BlockSpec((1, tn_), lambda i, j, kk: (0, j))],
        name="gla_gate_fwd")
    o_f, o_b, st_f, st_b = _gla_fwd(pm, la, seq)
    og = _gla_post_fwd(o_f, o_b, pm, gn)
    (h3,) = _mm(og, wout, "nn", out_dtypes=[F32], epilogue=lambda acc, hv: (hv + acc,), extras=(h2,), name="gla_out")
    hn3 = _rmsnorm_fwd(h3, norm_mlp[1:2], BF16, "norm_mlp1")
    h4, r1, u1 = _mlp_fwd(h3, hn3, w1g[1], w2g[1])

    loss_part, dh4, dh4b, dg_final = _final_bwd(h4, norm_final.reshape(1, d), tgt)
    dhn3, dw1_1, dw2_1 = _mlp_bwd(dh4b, hn3, r1, u1, w1g[1], w2g[1])
    dh3, dh3b, dg_mlp1 = _rmsnorm_bwd(h3, norm_mlp[1:2], dhn3, dh4, "norm_mlp1_bwd")

    (dog,) = _mm(dh3b, wout, "nt", out_dtypes=[F32], name="gla_out_bwd_x")
    (dwout,) = _mm(og, dh3b, "tn", out_dtypes=[BF16], name="gla_out_bwd_w")
    do, dgate, dg_gn = _gla_post_bwd(dog, o_f, o_b, pm, gn)
    dq_f, dk_f, dv_f, dla_f, dq_b, dk_b, dv_b, dla_b = _gla_bwd(pm, la, do, st_f, st_b, seq)
    dla = jnp.concatenate([dla_f, dla_b], axis=1)

    def gate_bwd(acc, bv, dl):
        z = acc + bv
        return (dl * (1.0 / GATE_TAU) / (1.0 + jnp.exp(z)),)

    (dz,) = _matmul(
        pr, w_up, dims=_NN, grid=(t // tm_, 2 * kd // tn_, 1),
        a_spec=pl.BlockSpec((tm_, R_PAD), lambda i, j, kk: (i, 0)),
        b_spec=pl.BlockSpec((R_PAD, tn_), lambda i, j, kk: (0, j)), acc_shape=(tm_, tn_),
        out_shapes=[_sds((t, 2 * kd), BF16)], out_specs=[pl.BlockSpec((tm_, tn_), lambda i, j, kk: (i, j))],
        epilogue=gate_bwd, extras=(b_up, dla),
        extra_specs=[pl.BlockSpec((1, tn_), lambda i, j, kk: (0, j)), pl.BlockSpec((tm_, tn_), lambda i, j, kk: (i, j))],
        name="gla_gate_bwd")
    (dpr,) = _mm(dz, w_up, "nt", out_dtypes=[BF16], name="gla_gate_bwd_r")
    (dw_up,) = _mm(pr, dz, "tn", out_dtypes=[F32], name="gla_gate_bwd_w")
    dp = _gla_dp(dq_f, dq_b, dk_f, dk_b, dv_f, dv_b, dgate)
    (dhn2_r,) = _mm(dpr, w_r, "nt", out_dtypes=[F32], name="gla_proj_bwd_xr")
    (dhn2,) = _mm(dp, w_main, "nt", out_dtypes=[F32], epilogue=lambda acc, e: (acc + e,), extras=(dhn2_r,),
                  name="gla_proj_bwd_x")
    (dw_main,) = _mm(hn2, dp, "tn", out_dtypes=[BF16], name="gla_proj_bwd_w")
    (dw_r,) = _mm(hn2, dpr, "tn", out_dtypes=[BF16], name="gla_proj_bwd_wr")
    dh2, dh2b, dg_mix1 = _rmsnorm_bwd(h2, norm_mix[1:2], dhn2, dh3, "norm_mix1_bwd")

    dhn1, dw1_0, dw2_0 = _mlp_bwd(dh2b, hn1, r0, u0, w1g[0], w2g[0])
    dh1, _, dg_mlp0 = _rmsnorm_bwd(h1, norm_mlp[0:1], dhn1, dh2, "norm_mlp0_bwd")

    dys, dg_pscale = _pool_scale_bwd(dh1, ypre, pool_scale)
    dd = _pool_mm_bwd_x(dys, wp)
    dwp = _pool_mm_bwd_w(dm, dys)
    dhn0 = _pool_apply(dd, seq, True, "pool_diff_bwd")
    dx, _, dg_mix0 = _rmsnorm_bwd(xf, norm_mix[0:1], dhn0, dh1, "norm_mix0_bwd")

    dwin = jnp.concatenate([dw_main, dw_r[:, :2 * GATE_RANK]], axis=1)
    dwin = jnp.transpose(dwin.reshape(d, N_SHARDS, pw), (1, 0, 2))
    grads = [dwp.reshape(N_SHARDS, N_GROUPS * (dg // N_SHARDS), dg), dw1_0, dw2_0.reshape(N_SHARDS, f4, d), dwin,
             dwout.reshape(N_SHARDS, d // N_SHARDS, d), dw1_1, dw2_1.reshape(N_SHARDS, f4, d)]
    g5 = [g.reshape(N_SHARDS, 2, g.shape[1] // 2, g.shape[2]) for g in grads]
    recv_a = _swap_halves(g5)
    pair = [_pair_sum(a, b, c_arr) for a, b in zip(g5, recv_a)]
    recv_b = _scatter_chips(pair)
    red = [_quad_sum(p) for p in recv_b]
    full = _join_halves(red)
    g_pool, g_w1_0, g_w2_0, g_win, g_wout, g_w1_1, g_w2_1 = [f.reshape(b.shape) for f, b in zip(full, big)]

    dwuf, dwub = dw_up[:GATE_RANK, :kd], dw_up[GATE_RANK:2 * GATE_RANK, kd:]
    dbuf, dbub = dw_up[ONES_COL, :kd], dw_up[ONES_COL, kd:]
    pieces = [jnp.concatenate([dg_mix0, dg_mix1], 0), jnp.concatenate([dg_mlp0, dg_mlp1], 0), dg_final, dg_pscale,
              dwuf, dwub, dbuf, dbub, dg_gn]
    sizes = [p.size for p in pieces]
    packed = jnp.concatenate([p.reshape(-1) for p in pieces])
    summed = _sum8(_bcast8(_to_rows(packed))).reshape(-1)
    outs_small, o = [], 0
    for p, n in zip(pieces, sizes):
        outs_small.append(summed[o:o + n].reshape(p.shape))
        o += n
    g_nmix, g_nmlp, g_nfinal, g_pscale, g_wuf, g_wub, g_buf, g_bub, g_gn = outs_small
    g_nfinal = g_nfinal.reshape(d)
    g_wuf = lax.dynamic_slice_in_dim(g_wuf, s_me * ks, ks, axis=1)
    g_wub = lax.dynamic_slice_in_dim(g_wub, s_me * ks, ks, axis=1)
    g_buf = lax.dynamic_slice_in_dim(g_buf, s_me * ks, ks, axis=0)
    g_bub = lax.dynamic_slice_in_dim(g_bub, s_me * ks, ks, axis=0)
    g_gn = lax.dynamic_slice_in_dim(g_gn.reshape(dv), s_me * (dv // N_SHARDS), dv // N_SHARDS, axis=0)

    loss = lax.psum(loss_part[0, 0], ("x", "y", "c"))

    weights = [norm_mix, norm_mlp, norm_final, pool_w, pool_scale, gla_w_in, gla_w_up_f, gla_b_up_f, gla_w_up_b,
               gla_b_up_b, gla_g_norm, gla_w_out, mlp_w_in, mlp_w_out]
    moms = [m_norm_mix, m_norm_mlp, m_norm_final, m_pool_w, m_pool_scale, m_gla_w_in, m_gla_w_up_f, m_gla_b_up_f,
            m_gla_w_up_b, m_gla_b_up_b, m_gla_g_norm, m_gla_w_out, m_mlp_w_in, m_mlp_w_out]
    vels = [v_norm_mix, v_norm_mlp, v_norm_final, v_pool_w, v_pool_scale, v_gla_w_in, v_gla_w_up_f, v_gla_b_up_f,
            v_gla_w_up_b, v_gla_b_up_b, v_gla_g_norm, v_gla_w_out, v_mlp_w_in, v_mlp_w_out]
    g_w1 = jnp.stack([g_w1_0, g_w1_1])
    g_w2 = jnp.stack([g_w2_0, g_w2_1])
    grads_out = [g_nmix, g_nmlp, g_nfinal, g_pool, g_pscale, g_win, g_wuf, g_buf, g_wub, g_bub, g_gn, g_wout,
                 g_w1, g_w2]
    grads_out = [g.reshape(w.shape) for g, w in zip(grads_out, weights)]
    names = ["norm_mix", "norm_mlp", "norm_final", "pool_w", "pool_scale", "gla_w_in", "gla_w_up_f", "gla_b_up_f",
             "gla_w_up_b", "gla_b_up_b", "gla_g_norm", "gla_w_out", "mlp_w_in", "mlp_w_out"]
    deltas, new_m, new_v = [], [], []
    for w, g, m, v, nm in zip(weights, grads_out, moms, vels, names):
        cols = w.shape[-1]
        shp = (w.size // cols, cols)
        dl, mn, vn = _adamw(w.reshape(shp), g.reshape(shp), m.reshape(shp), v.reshape(shp), "adamw_" + nm)
        deltas.append(dl.reshape(w.shape))
        new_m.append(mn.reshape(w.shape))
        new_v.append(vn.reshape(w.shape))

    return (loss, dx.reshape(x.shape), *grads_out, *deltas, *new_m, *new_v)
```

```python
import functools
import math

import jax
import jax.numpy as jnp
from jax import lax
from jax.experimental import pallas as pl
from jax.experimental.pallas import tpu as pltpu

F32 = jnp.float32
BF16 = jnp.bfloat16

N_HEADS = 4
N_GROUPS = 4
POOL_HALF = (1, 2, 4, 8)
GATE_RANK = 16
GATE_TAU = 16.0
CHUNK = 64
EPS = 1e-6
N_SHARDS = 4
R_PAD = 128
ONES_COL = 2 * GATE_RANK

ADAM_LR = 0.001
ADAM_B1 = 0.9
ADAM_B2 = 0.999
ADAM_EPS = 1e-08
ADAM_WD = 0.01
ADAM_STEP = 10

_NN = (((1,), (0,)), ((), ()))
_NT = (((1,), (1,)), ((), ()))
_TN = (((0,), (0,)), ((), ()))

VMEM_LIMIT = 56 * 1024 * 1024
MESH = pl.DeviceIdType.MESH
ANY = pl.BlockSpec(memory_space=pl.ANY)


def _tile(dim, pref):
    return pref if dim % pref == 0 else dim


def _params(*sem):
    return pltpu.CompilerParams(dimension_semantics=sem, vmem_limit_bytes=VMEM_LIMIT)


def _sds(shape, dtype):
    return jax.ShapeDtypeStruct(shape, dtype)


def _matmul(a, b, *, dims, grid, a_spec, b_spec, acc_shape, out_shapes, out_specs, epilogue,
            extras=(), extra_specs=(), name):
    nk = grid[2]
    n_extra = len(extras)
    n_out = len(out_shapes)

    def body(a_ref, b_ref, *rest):
        extra_refs = rest[:n_extra]
        out_refs = rest[n_extra:n_extra + n_out]
        acc_ref = rest[n_extra + n_out]
        kk = pl.program_id(2)

        def part():
            return lax.dot_general(a_ref[...], b_ref[...], dims, preferred_element_type=F32)

        def finish(acc):
            outs = epilogue(acc, *[r[...] for r in extra_refs])
            for o_ref, o in zip(out_refs, outs):
                o_ref[...] = o.astype(o_ref.dtype)

        if nk == 1:
            finish(part())
        else:
            @pl.when(kk == 0)
            def _():
                acc_ref[...] = part()

            @pl.when((kk > 0) & (kk < nk - 1))
            def _():
                acc_ref[...] += part()

            @pl.when(kk == nk - 1)
            def _():
                finish(acc_ref[...] + part())

    return pl.pallas_call(
        body,
        grid=grid,
        in_specs=[a_spec, b_spec, *extra_specs],
        out_specs=list(out_specs),
        out_shape=list(out_shapes),
        scratch_shapes=[pltpu.VMEM(acc_shape if nk > 1 else (8, 128), F32)],
        compiler_params=_params("parallel", "parallel", "arbitrary"),
        name=name,
    )(a, b, *extras)


def _mm(a, b, kind, *, out_dtypes, epilogue=None, extras=(), name, tm=1024, tn=1024, tk=4096):
    if kind == "nn":
        (m, k), n = a.shape, b.shape[1]
    elif kind == "nt":
        (m, k), n = a.shape, b.shape[0]
    else:
        (k, m), n = a.shape, b.shape[1]
    tm, tn, tk = _tile(m, tm), _tile(n, tn), _tile(k, tk)
    if kind == "nn":
        a_spec = pl.BlockSpec((tm, tk), lambda i, j, kk: (i, kk))
        b_spec = pl.BlockSpec((tk, tn), lambda i, j, kk: (kk, j))
        dims = _NN
    elif kind == "nt":
        a_spec = pl.BlockSpec((tm, tk), lambda i, j, kk: (i, kk))
        b_spec = pl.BlockSpec((tn, tk), lambda i, j, kk: (j, kk))
        dims = _NT
    else:
        a_spec = pl.BlockSpec((tk, tm), lambda i, j, kk: (kk, i))
        b_spec = pl.BlockSpec((tk, tn), lambda i, j, kk: (kk, j))
        dims = _TN
    o_spec = pl.BlockSpec((tm, tn), lambda i, j, kk: (i, j))
    if epilogue is None:
        epilogue = lambda acc, *e: tuple(acc for _ in out_dtypes)
    return _matmul(
        a, b, dims=dims, grid=(m // tm, n // tn, k // tk), a_spec=a_spec, b_spec=b_spec, acc_shape=(tm, tn),
        out_shapes=[_sds((m, n), d) for d in out_dtypes], out_specs=[o_spec for _ in out_dtypes],
        epilogue=epilogue, extras=extras, extra_specs=[o_spec for _ in extras], name=name)


def _rmsnorm_fwd(h, g, out_dtype, name):
    t, d = h.shape
    tr = _tile(t, 256)

    def body(h_ref, g_ref, o_ref):
        x = h_ref[...]
        r = lax.rsqrt(jnp.mean(x * x, axis=-1, keepdims=True) + EPS)
        o_ref[...] = (x * r * g_ref[...]).astype(o_ref.dtype)

    return pl.pallas_call(
        body, grid=(t // tr,),
        in_specs=[pl.BlockSpec((tr, d), lambda i: (i, 0)), pl.BlockSpec((1, d), lambda i: (0, 0))],
        out_specs=pl.BlockSpec((tr, d), lambda i: (i, 0)),
        out_shape=_sds((t, d), out_dtype), compiler_params=_params("parallel"), name=name)(h, g)


def _rmsnorm_bwd(h, g, dy, resid, name):
    t, d = h.shape
    tr = _tile(t, 128)

    def body(h_ref, g_ref, dy_ref, res_ref, dh_ref, dhb_ref, dg_ref):
        x = h_ref[...]
        r = lax.rsqrt(jnp.mean(x * x, axis=-1, keepdims=True) + EPS)
        xn = x * r
        dyv = dy_ref[...]
        gdy = dyv * g_ref[...]
        dh = res_ref[...] + r * (gdy - xn * jnp.mean(gdy * xn, axis=-1, keepdims=True))
        dh_ref[...] = dh
        dhb_ref[...] = dh.astype(BF16)
        part = jnp.sum(dyv * xn, axis=0, keepdims=True)

        @pl.when(pl.program_id(0) == 0)
        def _():
            dg_ref[...] = part

        @pl.when(pl.program_id(0) > 0)
        def _():
            dg_ref[...] += part

    row = pl.BlockSpec((tr, d), lambda i: (i, 0))
    vec = pl.BlockSpec((1, d), lambda i: (0, 0))
    return pl.pallas_call(
        body, grid=(t // tr,), in_specs=[row, vec, row, row], out_specs=[row, row, vec],
        out_shape=[_sds((t, d), F32), _sds((t, d), BF16), _sds((1, d), F32)],
        compiler_params=_params("arbitrary"), name=name)(h, g, dy, resid)


def _final_bwd(h, g, tgt):
    t, d = h.shape
    tr = _tile(t, 128)

    def body(h_ref, g_ref, t_ref, loss_ref, dh_ref, dhb_ref, dg_ref):
        x = h_ref[...]
        r = lax.rsqrt(jnp.mean(x * x, axis=-1, keepdims=True) + EPS)
        xn = x * r
        gv = g_ref[...]
        e = xn * gv - t_ref[...]
        lpart = jnp.full((1, 128), 0.5 * jnp.sum(jnp.mean(e * e, axis=-1, keepdims=True)), F32)
        dyv = e * (1.0 / d)
        gdy = dyv * gv
        dh = r * (gdy - xn * jnp.mean(gdy * xn, axis=-1, keepdims=True))
        dh_ref[...] = dh
        dhb_ref[...] = dh.astype(BF16)
        part = jnp.sum(dyv * xn, axis=0, keepdims=True)

        @pl.when(pl.program_id(0) == 0)
        def _():
            dg_ref[...] = part
            loss_ref[...] = lpart

        @pl.when(pl.program_id(0) > 0)
        def _():
            dg_ref[...] += part
            loss_ref[...] += lpart

    row = pl.BlockSpec((tr, d), lambda i: (i, 0))
    vec = pl.BlockSpec((1, d), lambda i: (0, 0))
    return pl.pallas_call(
        body, grid=(t // tr,), in_specs=[row, vec, row],
        out_specs=[pl.BlockSpec((1, 128), lambda i: (0, 0)), row, row, vec],
        out_shape=[_sds((1, 128), F32), _sds((t, d), F32), _sds((t, d), BF16), _sds((1, d), F32)],
        compiler_params=_params("arbitrary"), name="final_loss_bwd")(h, g, tgt)


def _shift_rows(x, s, row):
    n = x.shape[0]
    y = pltpu.roll(x, (-s) % n, 0)
    return jnp.where((row + s >= 0) & (row + s < n), y, 0.0)


def _span_sum(x, start, length, row):
    if start >= 0:
        y, step = _shift_rows(x, start, row) if start else x, 1
    else:
        last = start + length - 1
        assert last <= 0
        y, step = _shift_rows(x, last, row) if last else x, -1
    n = 1
    while n < length:
        y = y + _shift_rows(y, step * n, row)
        n *= 2
    return y


def _pool_apply(x, seq, transpose, name):
    t, d = x.shape
    dg = d // N_GROUPS
    tc = _tile(dg, 256)
    nblk = dg // tc

    def body(x_ref, o_ref):
        grp = pl.program_id(1)
        row = lax.broadcasted_iota(jnp.int32, (seq, tc), 0)
        for gi, hw in enumerate(POOL_HALF):
            @pl.when(grp == gi)
            def _(hw=hw):
                xv = x_ref[...]
                cnt = (jnp.minimum(row + hw, seq) - jnp.maximum(row - hw, 0)).astype(F32)
                if not transpose:
                    w = _span_sum(xv, 0, hw, row) + _span_sum(xv, -hw, hw, row)
                    o_ref[...] = (w / cnt - xv).astype(o_ref.dtype)
                else:
                    u = xv / cnt
                    w = _span_sum(u, 1, hw, row) + _span_sum(u, -(hw - 1), hw, row)
                    o_ref[...] = (w - xv).astype(o_ref.dtype)

    spec = pl.BlockSpec((seq, tc), lambda b, g, j: (b, g * nblk + j))
    return pl.pallas_call(
        body, grid=(t // seq, N_GROUPS, nblk), in_specs=[spec], out_specs=spec,
        out_shape=_sds((t, d), F32 if transpose else BF16),
        compiler_params=_params("parallel", "parallel", "parallel"), name=name)(x)


def _pool_mm_fwd(dm, wp, x, scale):
    t, d = dm.shape
    dg = d // N_GROUPS
    rs = dg // N_SHARDS
    tm = _tile(t, 1024)
    o_spec = pl.BlockSpec((tm, dg), lambda i, j, kk: (i, j))
    return _matmul(
        dm, wp, dims=_NN, grid=(t // tm, N_GROUPS, N_SHARDS),
        a_spec=pl.BlockSpec((tm, rs), lambda i, j, kk: (i, j * N_SHARDS + kk)),
        b_spec=pl.BlockSpec((None, None, rs, dg), lambda i, j, kk: (kk, j, 0, 0)),
        acc_shape=(tm, dg), out_shapes=[_sds((t, d), F32), _sds((t, d), F32)], out_specs=[o_spec, o_spec],
        epilogue=lambda acc, xv, sc: (xv + acc * sc, acc),
        extras=(x, scale), extra_specs=[o_spec, pl.BlockSpec((1, dg), lambda i, j, kk: (0, j))], name="pool_mm_fwd")


def _pool_scale_bwd(dh, ypre, scale):
    t, d = dh.shape
    tr = _tile(t, 256)

    def body(dh_ref, y_ref, s_ref, o_ref, ds_ref):
        g = dh_ref[...]
        o_ref[...] = (g * s_ref[...]).astype(BF16)
        part = jnp.sum(g * y_ref[...], axis=0, keepdims=True)

        @pl.when(pl.program_id(0) == 0)
        def _():
            ds_ref[...] = part

        @pl.when(pl.program_id(0) > 0)
        def _():
            ds_ref[...] += part

    row = pl.BlockSpec((tr, d), lambda i: (i, 0))
    vec = pl.BlockSpec((1, d), lambda i: (0, 0))
    return pl.pallas_call(
        body, grid=(t // tr,), in_specs=[row, row, vec], out_specs=[row, vec],
        out_shape=[_sds((t, d), BF16), _sds((1, d), F32)], compiler_params=_params("arbitrary"),
        name="pool_scale_bwd")(dh, ypre, scale)


def _pool_mm_bwd_x(dys, wp):
    t, d = dys.shape
    dg = d // N_GROUPS
    rs = dg // N_SHARDS
    tm = _tile(t, 1024)
    return _matmul(
        dys, wp, dims=_NT, grid=(t // tm, N_GROUPS * N_SHARDS, 1),
        a_spec=pl.BlockSpec((tm, dg), lambda i, j, kk: (i, j // N_SHARDS)),
        b_spec=pl.BlockSpec((None, None, rs, dg), lambda i, j, kk: (j % N_SHARDS, j // N_SHARDS, 0, 0)),
        acc_shape=(tm, rs), out_shapes=[_sds((t, d), F32)],
        out_specs=[pl.BlockSpec((tm, rs), lambda i, j, kk: (i, j))],
        epilogue=lambda acc: (acc,), name="pool_mm_bwd_x")[0]


def _pool_mm_bwd_w(dm, dys):
    t, d = dm.shape
    dg = d // N_GROUPS
    rs = dg // N_SHARDS
    tk = _tile(t, 4096)
    return _matmul(
        dm, dys, dims=_TN, grid=(N_GROUPS * N_SHARDS, 1, t // tk),
        a_spec=pl.BlockSpec((tk, rs), lambda i, j, kk: (kk, i)),
        b_spec=pl.BlockSpec((tk, dg), lambda i, j, kk: (kk, i // N_SHARDS)),
        acc_shape=(rs, dg), out_shapes=[_sds((N_SHARDS, N_GROUPS, rs, dg), BF16)],
        out_specs=[pl.BlockSpec((None, None, rs, dg), lambda i, j, kk: (i % N_SHARDS, i // N_SHARDS, 0, 0))],
        epilogue=lambda acc: (acc,), name="pool_mm_bwd_w")[0]


def _mlp_fwd(h, hn, w1g, w2):
    t, d = hn.shape
    f4 = w1g.shape[2]
    dff = N_SHARDS * f4
    tm, tn, tk = _tile(t, 1024), _tile(f4, 1024), _tile(d, 4096)
    nb = f4 // tn
    o_spec = pl.BlockSpec((tm, tn), lambda i, j, kk: (i, j))

    def act(acc):
        r = jnp.maximum(acc, 0.0)
        return r, r * r

    r, u = _matmul(
        hn, w1g, dims=_NN, grid=(t // tm, dff // tn, d // tk),
        a_spec=pl.BlockSpec((tm, tk), lambda i, j, kk: (i, kk)),
        b_spec=pl.BlockSpec((None, tk, tn), lambda i, j, kk: (j // nb, kk, j % nb)),
        acc_shape=(tm, tn), out_shapes=[_sds((t, dff), BF16), _sds((t, dff), BF16)], out_specs=[o_spec, o_spec],
        epilogue=act, name="mlp_up")
    (out,) = _mm(u, w2, "nn", out_dtypes=[F32], epilogue=lambda acc, hv: (hv + acc,), extras=(h,), name="mlp_down",
                 tk=2048)
    return out, r, u


def _mlp_bwd(dhb, hn, r, u, w1g, w2):
    t, d = hn.shape
    f4 = w1g.shape[2]
    dff = N_SHARDS * f4
    (da,) = _mm(dhb, w2, "nt", out_dtypes=[BF16], epilogue=lambda acc, rv: (acc * (2.0 * rv.astype(F32)),),
                extras=(r,), name="mlp_bwd_da")
    (dw2,) = _mm(u, dhb, "tn", out_dtypes=[BF16], name="mlp_bwd_dw2")
    tm, tn, tk = _tile(t, 1024), _tile(d, 1024), _tile(f4, 4096)
    nbk = f4 // tk
    (dhn,) = _matmul(
        da, w1g, dims=_NT, grid=(t // tm, d // tn, dff // tk),
        a_spec=pl.BlockSpec((tm, tk), lambda i, j, kk: (i, kk)),
        b_spec=pl.BlockSpec((None, tn, tk), lambda i, j, kk: (kk // nbk, j, kk % nbk)),
        acc_shape=(tm, tn), out_shapes=[_sds((t, d), F32)],
        out_specs=[pl.BlockSpec((tm, tn), lambda i, j, kk: (i, j))], epilogue=lambda acc: (acc,), name="mlp_bwd_dhn")
    tm, tn, tk = _tile(d, 1024), _tile(f4, 1024), _tile(t, 4096)
    nb = f4 // tn
    (dw1,) = _matmul(
        hn, da, dims=_TN, grid=(d // tm, dff // tn, t // tk),
        a_spec=pl.BlockSpec((tk, tm), lambda i, j, kk: (kk, i)),
        b_spec=pl.BlockSpec((tk, tn), lambda i, j, kk: (kk, j)),
        acc_shape=(tm, tn), out_shapes=[_sds((N_SHARDS, d, f4), BF16)],
        out_specs=[pl.BlockSpec((None, tm, tn), lambda i, j, kk: (j // nb, i, j % nb))],
        epilogue=lambda acc: (acc,), name="mlp_bwd_dw1")
    return dhn, dw1, dw2


def _split3(x):
    a = x.astype(BF16)
    r1 = x - a.astype(F32)
    b = r1.astype(BF16)
    c = (r1 - b.astype(F32)).astype(BF16)
    return a, b, c


def _dot(a, b, dims):
    return lax.dot_general(a.astype(BF16), b.astype(BF16), dims, preferred_element_type=F32)


def _chunk_terms(q, k, g, rev, scale):
    c = q.shape[0]
    ri = lax.broadcasted_iota(jnp.int32, (c, c), 0)
    ci = lax.broadcasted_iota(jnp.int32, (c, c), 1)
    seen = (ci >= ri) if rev else (ci <= ri)
    tri = seen.astype(BF16)
    g1, g2, g3 = _split3(g)
    b = (lax.dot_general(tri, g1, _NN, preferred_element_type=F32)
         + lax.dot_general(tri, g2, _NN, preferred_element_type=F32)
         + lax.dot_general(tri, g3, _NN, preferred_element_type=F32))
    mid = c // 2 if rev else c // 2 - 1
    last = 0 if rev else c - 1
    rows = lax.broadcasted_iota(jnp.int32, b.shape, 0)
    b_mid = jnp.sum(jnp.where(rows == mid, b, 0.0), axis=0, keepdims=True)
    b_last = jnp.sum(jnp.where(rows == last, b, 0.0), axis=0, keepdims=True)
    qs = q * scale
    e1 = jnp.exp(b - b_mid)
    e2 = jnp.exp(b_mid - b)
    eb = jnp.exp(b)
    el = jnp.exp(b_last - b)
    return dict(seen=seen, tri=tri, mid=mid, last=last, e1=e1, e2=e2, eb=eb, el=el, a=jnp.exp(b_last),
                qe=qs * e1, ke=k * e2, qi=qs * eb, ks=k * el)


def _gla_fwd(pm, la, seq):
    t = pm.shape[0]
    d = pm.shape[1] // 3
    dk, dv = d // 2 // N_HEADS, d // N_HEADS
    nb, nc = t // seq, seq // CHUNK
    scale = dk ** -0.5
    kq, kk_, kv = 0, N_HEADS, (d // dv)

    def body(qf, kf, vf, gf, qb, kb, vb, gb, of_ref, ob_ref, stf_ref, stb_ref, sf, sb):
        n = pl.program_id(1)

        @pl.when(n == 0)
        def _():
            sf[...] = jnp.zeros_like(sf)
            sb[...] = jnp.zeros_like(sb)

        for (q_ref, k_ref, v_ref, g_ref, o_ref, st_ref, s_ref, rev) in (
                (qf, kf, vf, gf, of_ref, stf_ref, sf, False), (qb, kb, vb, gb, ob_ref, stb_ref, sb, True)):
            tm = _chunk_terms(q_ref[...], k_ref[...], g_ref[...], rev, scale)
            v = v_ref[...]
            st = s_ref[...]
            stb = st.astype(BF16)
            st_ref[...] = stb
            sc = jnp.where(tm["seen"], _dot(tm["qe"], tm["ke"], _NT), 0.0)
            o = _dot(sc, v, _NN) + lax.dot_general(tm["qi"].astype(BF16), stb, _NT, preferred_element_type=F32)
            o_ref[...] = o
            s_ref[...] = st * tm["a"] + _dot(v, tm["ks"], _TN)

    def row(bh, n, rev):
        return (bh // N_HEADS) * nc + (nc - 1 - n if rev else n)

    def specs(rev):
        return [
            pl.BlockSpec((CHUNK, dk), lambda bh, n: (row(bh, n, rev), kq + bh % N_HEADS)),
            pl.BlockSpec((CHUNK, dk), lambda bh, n: (row(bh, n, rev), kk_ + bh % N_HEADS)),
            pl.BlockSpec((CHUNK, dv), lambda bh, n: (row(bh, n, rev), kv + bh % N_HEADS)),
            pl.BlockSpec((CHUNK, dk), lambda bh, n: (row(bh, n, rev), (N_HEADS if rev else 0) + bh % N_HEADS)),
        ]

    def o_spec(rev):
        return pl.BlockSpec((CHUNK, dv), lambda bh, n: (row(bh, n, rev), bh % N_HEADS))

    def st_spec(rev):
        return pl.BlockSpec((None, None, dv, dk), lambda bh, n: (bh, nc - 1 - n if rev else n, 0, 0))

    sf_, sb_ = specs(False), specs(True)
    return pl.pallas_call(
        body, grid=(nb * N_HEADS, nc),
        in_specs=[*sf_, *sb_],
        out_specs=[o_spec(False), o_spec(True), st_spec(False), st_spec(True)],
        out_shape=[_sds((t, d), F32), _sds((t, d), F32),
                   _sds((nb * N_HEADS, nc, dv, dk), BF16), _sds((nb * N_HEADS, nc, dv, dk), BF16)],
        scratch_shapes=[pltpu.VMEM((dv, dk), F32), pltpu.VMEM((dv, dk), F32)],
        compiler_params=_params("parallel", "arbitrary"), name="gla_scan_fwd",
    )(pm, pm, pm, la, pm, pm, pm, la)


def _gla_bwd(pm, la, do, st_f, st_b, seq):
    t = pm.shape[0]
    d = pm.shape[1] // 3
    dk, dv = d // 2 // N_HEADS, d // N_HEADS
    kd = dk * N_HEADS
    nb, nc = t // seq, seq // CHUNK
    scale = dk ** -0.5
    kq, kk_, kv = 0, N_HEADS, (d // dv)

    def body(qf, kf, vf, gf, dof, stf, qb, kb, vb, gb, dob, stb_,
             dqf, dkf, dvf, dgf, dqb, dkb, dvb, dgb, dsf, dsb):
        n = pl.program_id(1)

        @pl.when(n == 0)
        def _():
            dsf[...] = jnp.zeros_like(dsf)
            dsb[...] = jnp.zeros_like(dsb)

        for (q_ref, k_ref, v_ref, g_ref, do_ref, st_ref, dq_ref, dk_ref, dv_ref, dg_ref, ds_ref, rev) in (
                (qf, kf, vf, gf, dof, stf, dqf, dkf, dvf, dgf, dsf, False),
                (qb, kb, vb, gb, dob, stb_, dqb, dkb, dvb, dgb, dsb, True)):
            tm = _chunk_terms(q_ref[...], k_ref[...], g_ref[...], rev, scale)
            v = v_ref[...]
            dov = do_ref[...]
            st = st_ref[...]
            ds = ds_ref[...]
            dsb16 = ds.astype(BF16)
            sc = jnp.where(tm["seen"], _dot(tm["qe"], tm["ke"], _NT), 0.0)
            dsc = jnp.where(tm["seen"], _dot(dov, v, _NT), 0.0)
            dv_ref[...] = _dot(sc, dov, _TN) + lax.dot_general(tm["ks"].astype(BF16), dsb16, _NT,
                                                               preferred_element_type=F32)
            dqe = _dot(dsc, tm["ke"], _NN)
            dke = _dot(dsc, tm["qe"], _TN)
            dqi = lax.dot_general(dov.astype(BF16), st, _NN, preferred_element_type=F32)
            dks = lax.dot_general(v.astype(BF16), dsb16, _NN, preferred_element_type=F32)
            da = jnp.sum(ds * st.astype(F32), axis=0, keepdims=True)
            dq_ref[...] = (dqe * tm["e1"] + dqi * tm["eb"]) * scale
            dk_ref[...] = dke * tm["e2"] + dks * tm["el"]
            t_q, t_k, t_s = dqe * tm["qe"], dke * tm["ke"], dks * tm["ks"]
            db = t_q - t_k + dqi * tm["qi"] - t_s
            mid_row = jnp.sum(t_k - t_q, axis=0, keepdims=True)
            last_row = jnp.sum(t_s, axis=0, keepdims=True) + da * tm["a"]
            ridx = lax.broadcasted_iota(jnp.int32, db.shape, 0)
            db = db + jnp.where(ridx == tm["mid"], mid_row, 0.0) + jnp.where(ridx == tm["last"], last_row, 0.0)
            d1, d2, d3 = _split3(db)
            dg_ref[...] = (lax.dot_general(tm["tri"], d1, _TN, preferred_element_type=F32)
                           + lax.dot_general(tm["tri"], d2, _TN, preferred_element_type=F32)
                           + lax.dot_general(tm["tri"], d3, _TN, preferred_element_type=F32))
            ds_ref[...] = ds * tm["a"] + _dot(dov, tm["qi"], _TN)

    def row(bh, n, rev):
        return (bh // N_HEADS) * nc + (n if rev else nc - 1 - n)

    def specs(rev):
        return [
            pl.BlockSpec((CHUNK, dk), lambda bh, n: (row(bh, n, rev), kq + bh % N_HEADS)),
            pl.BlockSpec((CHUNK, dk), lambda bh, n: (row(bh, n, rev), kk_ + bh % N_HEADS)),
            pl.BlockSpec((CHUNK, dv), lambda bh, n: (row(bh, n, rev), kv + bh % N_HEADS)),
            pl.BlockSpec((CHUNK, dk), lambda bh, n: (row(bh, n, rev), (N_HEADS if rev else 0) + bh % N_HEADS)),
            pl.BlockSpec((CHUNK, dv), lambda bh, n: (row(bh, n, rev), bh % N_HEADS)),
            pl.BlockSpec((None, None, dv, dk), lambda bh, n: (bh, n if rev else nc - 1 - n, 0, 0)),
        ]

    def outs(rev):
        return [
            pl.BlockSpec((CHUNK, dk), lambda bh, n: (row(bh, n, rev), bh % N_HEADS)),
            pl.BlockSpec((CHUNK, dk), lambda bh, n: (row(bh, n, rev), bh % N_HEADS)),
            pl.BlockSpec((CHUNK, dv), lambda bh, n: (row(bh, n, rev), bh % N_HEADS)),
            pl.BlockSpec((CHUNK, dk), lambda bh, n: (row(bh, n, rev), bh % N_HEADS)),
        ]

    of_, ob_ = outs(False), outs(True)
    res = pl.pallas_call(
        body, grid=(nb * N_HEADS, nc),
        in_specs=[*specs(False), *specs(True)],
        out_specs=[*of_, *ob_],
        out_shape=[_sds((t, kd), F32), _sds((t, kd), F32), _sds((t, d), F32), _sds((t, kd), F32),
                   _sds((t, kd), F32), _sds((t, kd), F32), _sds((t, d), F32), _sds((t, kd), F32)],
        scratch_shapes=[pltpu.VMEM((dv, dk), F32), pltpu.VMEM((dv, dk), F32)],
        compiler_params=_params("parallel", "arbitrary"), name="gla_scan_bwd",
    )(pm, pm, pm, la, do, st_f, pm, pm, pm, la, do, st_b)
    return res


def _sigmoid(x):
    return 1.0 / (1.0 + jnp.exp(-x))


def _gla_post_fwd(o_f, o_b, pm, gn):
    t, d = o_f.shape
    dv = d // N_HEADS
    tr = _tile(t, 512)
    gate_blk = 2 * d // dv

    def body(of_ref, ob_ref, gt_ref, gn_ref, out_ref):
        o = of_ref[...] + ob_ref[...]
        n = o * lax.rsqrt(jnp.mean(o * o, axis=-1, keepdims=True) + EPS) * gn_ref[...]
        gt = gt_ref[...]
        out_ref[...] = (n * (gt * _sigmoid(gt))).astype(BF16)

    blk = pl.BlockSpec((tr, dv), lambda i, h: (i, h))
    return pl.pallas_call(
        body, grid=(t // tr, N_HEADS),
        in_specs=[blk, blk, pl.BlockSpec((tr, dv), lambda i, h: (i, gate_blk + h)),
                  pl.BlockSpec((1, dv), lambda i, h: (0, 0))],
        out_specs=blk, out_shape=_sds((t, d), BF16), compiler_params=_params("parallel", "parallel"),
        name="gla_post_fwd")(o_f, o_b, pm, gn)


def _gla_post_bwd(dog, o_f, o_b, pm, gn):
    t, d = o_f.shape
    dv = d // N_HEADS
    tr = _tile(t, 256)
    gate_blk = 2 * d // dv

    def body(dog_ref, of_ref, ob_ref, gt_ref, gn_ref, do_ref, dgt_ref, dgn_ref):
        o = of_ref[...] + ob_ref[...]
        rr = lax.rsqrt(jnp.mean(o * o, axis=-1, keepdims=True) + EPS)
        on = o * rr
        gnv = gn_ref[...]
        gt = gt_ref[...]
        sg = _sigmoid(gt)
        sl = gt * sg
        dg_out = dog_ref[...]
        dn = dg_out * sl
        dgt_ref[...] = dg_out * (on * gnv) * (sg * (1.0 + gt * (1.0 - sg)))
        gdn = dn * gnv
        do_ref[...] = rr * (gdn - on * jnp.mean(gdn * on, axis=-1, keepdims=True))
        part = jnp.sum(dn * on, axis=0, keepdims=True)
        first = (pl.program_id(0) == 0) & (pl.program_id(1) == 0)

        @pl.when(first)
        def _():
            dgn_ref[...] = part

        @pl.when(jnp.logical_not(first))
        def _():
            dgn_ref[...] += part

    blk = pl.BlockSpec((tr, dv), lambda i, h: (i, h))
    vec = pl.BlockSpec((1, dv), lambda i, h: (0, 0))
    return pl.pallas_call(
        body, grid=(t // tr, N_HEADS),
        in_specs=[blk, blk, blk, pl.BlockSpec((tr, dv), lambda i, h: (i, gate_blk + h)), vec],
        out_specs=[blk, blk, vec], out_shape=[_sds((t, d), F32), _sds((t, d), F32), _sds((1, dv), F32)],
        compiler_params=_params("arbitrary", "arbitrary"), name="gla_post_bwd")(dog, o_f, o_b, pm, gn)


def _gla_dp(dq_f, dq_b, dk_f, dk_b, dv_f, dv_b, dgate):
    t, d = dv_f.shape
    kd = dq_f.shape[1]
    tr = _tile(t, 128)

    def body(a1, a2, b1, b2, c1, c2, g, o_ref):
        o_ref[:, 0:kd] = (a1[...] + a2[...]).astype(BF16)
        o_ref[:, kd:2 * kd] = (b1[...] + b2[...]).astype(BF16)
        o_ref[:, 2 * kd:2 * kd + d] = (c1[...] + c2[...]).astype(BF16)
        o_ref[:, 2 * kd + d:] = g[...].astype(BF16)

    sk = pl.BlockSpec((tr, kd), lambda i: (i, 0))
    sd = pl.BlockSpec((tr, d), lambda i: (i, 0))
    return pl.pallas_call(
        body, grid=(t // tr,), in_specs=[sk, sk, sk, sk, sd, sd, sd],
        out_specs=pl.BlockSpec((tr, 3 * d), lambda i: (i, 0)), out_shape=_sds((t, 3 * d), BF16),
        compiler_params=_params("parallel"), name="gla_dp")(dq_f, dq_b, dk_f, dk_b, dv_f, dv_b, dgate)


def _adamw(w, g, m, v, name):
    r, c = w.shape
    tr = _tile(r, 128)
    bc1 = 1.0 - ADAM_B1 ** ADAM_STEP
    bc2 = 1.0 - ADAM_B2 ** ADAM_STEP

    def body(w_ref, g_ref, m_ref, v_ref, d_ref, nm_ref, nv_ref):
        gv = g_ref[...]
        mn = ADAM_B1 * m_ref[...] + (1.0 - ADAM_B1) * gv
        vn = ADAM_B2 * v_ref[...] + (1.0 - ADAM_B2) * (gv * gv)
        m_hat = mn / bc1
        v_hat = vn / bc2
        d_ref[...] = -ADAM_LR * (m_hat / (jnp.sqrt(v_hat) + ADAM_EPS) + ADAM_WD * w_ref[...])
        nm_ref[...] = mn
        nv_ref[...] = vn

    blk = pl.BlockSpec((tr, c), lambda i: (i, 0))
    return pl.pallas_call(
        body, grid=(r // tr,), in_specs=[blk] * 4, out_specs=[blk] * 3, out_shape=[_sds((r, c), F32)] * 3,
        compiler_params=_params("parallel"), name=name)(w, g, m, v)


def _sum8(parts):
    _, n, _ = parts.shape

    def body(p_ref, o_ref):
        acc = p_ref[0]
        for i in range(1, 8):
            acc = acc + p_ref[i]
        o_ref[...] = acc

    return pl.pallas_call(body, out_shape=_sds((n, 128), F32), name="sum8")(parts)


def _pair_sum(ga, recv, c):
    _, _, rh, cols = ga.shape
    tr = _tile(rh, 256)

    def body(c_ref, a_ref, b_ref, o_ref):
        o_ref[...] = (a_ref[...].astype(F32) + b_ref[...].astype(F32)).astype(BF16)

    grid_spec = pltpu.PrefetchScalarGridSpec(
        num_scalar_prefetch=1, grid=(N_SHARDS, rh // tr),
        in_specs=[pl.BlockSpec((None, None, tr, cols), lambda s, i, c_ref: (s, c_ref[0], i, 0)),
                  pl.BlockSpec((None, tr, cols), lambda s, i, c_ref: (s, i, 0))],
        out_specs=pl.BlockSpec((None, tr, cols), lambda s, i, c_ref: (s, i, 0)))
    return pl.pallas_call(
        body, grid_spec=grid_spec, out_shape=_sds((N_SHARDS, rh, cols), BF16),
        compiler_params=_params("parallel", "parallel"), name="grad_pair_sum")(c, ga, recv)


def _quad_sum(pair, recv, s_me, c):
    _, rh, cols = pair.shape
    tr = _tile(rh, 256)

    def body(s_ref, c_ref, p_ref, r1_ref, r2_ref, r3_ref, o_ref):
        o_ref[...] = ((p_ref[...].astype(F32) + r1_ref[...].astype(F32)) + r2_ref[...].astype(F32)) \
            + r3_ref[...].astype(F32)

    def blk(off):
        return pl.BlockSpec((None, tr, cols), lambda i, s_ref, c_ref: ((s_ref[0] + off) % N_SHARDS, i, 0))

    grid_spec = pltpu.PrefetchScalarGridSpec(
        num_scalar_prefetch=2, grid=(rh // tr,), in_specs=[blk(0), blk(1), blk(2), blk(3)],
        out_specs=pl.BlockSpec((None, tr, cols), lambda i, s_ref, c_ref: (c_ref[0], i, 0)))
    return pl.pallas_call(
        body, grid_spec=grid_spec, out_shape=_sds((2, rh, cols), F32),
        compiler_params=_params("parallel"), name="grad_quad_sum")(s_me, c, pair, recv, recv, recv)


def _fill_own(w, layer, s_me):
    _, rows, cols = w.shape
    rh = rows // 2
    tr = _tile(rh, 256)
    nblk = rh // tr

    def body(s_ref, w_ref, o_ref):
        o_ref[...] = w_ref[...].astype(BF16)

    grid_spec = pltpu.PrefetchScalarGridSpec(
        num_scalar_prefetch=1, grid=(2, nblk),
        in_specs=[pl.BlockSpec((None, tr, cols), lambda h, i, s_ref: (layer, h * nblk + i, 0))],
        out_specs=pl.BlockSpec((None, None, tr, cols), lambda h, i, s_ref: (s_ref[0], h, i, 0)))
    return pl.pallas_call(
        body, grid_spec=grid_spec, out_shape=_sds((N_SHARDS, 2, rh, cols), BF16),
        compiler_params=_params("parallel", "parallel"), name="weight_fill_own")(s_me, w)


def _coords():
    return lax.axis_index("x"), lax.axis_index("y"), lax.axis_index("c")


def _other_chips(x, y):
    return [(x, 1 - y), (1 - x, y), (1 - x, 1 - y)]


def _bcast8(buf):
    n = buf.shape[0]

    def body(b_ref, o_ref, send_sems, recv_sems):
        x, y, c = _coords()
        me = 4 * x + 2 * y + c
        o_ref[me] = b_ref[...]
        copies = []
        for k in range(1, 8):
            peer = (x ^ (k >> 2), y ^ ((k >> 1) & 1), c ^ (k & 1))
            copies.append(pltpu.make_async_remote_copy(
                src_ref=b_ref, dst_ref=o_ref.at[me], send_sem=send_sems.at[k - 1], recv_sem=recv_sems.at[k - 1],
                device_id=peer, device_id_type=MESH))
        for cp in copies:
            cp.start()
        for k in range(1, 8):
            pltpu.make_async_remote_copy(
                src_ref=b_ref, dst_ref=o_ref.at[me ^ k], send_sem=send_sems.at[k - 1], recv_sem=recv_sems.at[k - 1],
                device_id=(x, y, c), device_id_type=MESH).wait_recv()
        for cp in copies:
            cp.wait_send()

    return pl.pallas_call(
        body, out_shape=_sds((8, n, 128), F32),
        in_specs=[pl.BlockSpec(memory_space=pltpu.VMEM)], out_specs=pl.BlockSpec(memory_space=pltpu.VMEM),
        scratch_shapes=[pltpu.SemaphoreType.DMA((7,)), pltpu.SemaphoreType.DMA((7,))],
        compiler_params=pltpu.CompilerParams(vmem_limit_bytes=VMEM_LIMIT), name="bcast8")(buf)


def _gather_weights(bufs):
    n = len(bufs)

    def body(*refs):
        ins, outs = refs[:n], refs[n:2 * n]
        send1, recv1, send2, recv2 = refs[2 * n:]
        x, y, c = _coords()
        s_me = 2 * x + y
        chips = _other_chips(x, y)
        sib = (x, y, 1 - c)
        first, passed = [], []
        for i in range(n):
            for j, (px, py) in enumerate(chips):
                cp = pltpu.make_async_remote_copy(
                    src_ref=ins[i].at[s_me, c], dst_ref=outs[i].at[s_me, c], send_sem=send1.at[i, j],
                    recv_sem=recv1.at[i, j], device_id=(px, py, c), device_id_type=MESH)
                cp.start()
                first.append(cp)
        for i in range(n):
            for j, (px, py) in enumerate(chips):
                got = outs[i].at[2 * px + py, c]
                pltpu.make_async_remote_copy(
                    src_ref=got, dst_ref=got, send_sem=send1.at[i, j], recv_sem=recv1.at[i, j],
                    device_id=(x, y, c), device_id_type=MESH).wait_recv()
                cp = pltpu.make_async_remote_copy(
                    src_ref=got, dst_ref=got, send_sem=send2.at[i, j], recv_sem=recv2.at[i, j],
                    device_id=sib, device_id_type=MESH)
                cp.start()
                passed.append(cp)
        for i in range(n):
            for j, (px, py) in enumerate(chips):
                other = outs[i].at[2 * px + py, 1 - c]
                pltpu.make_async_remote_copy(
                    src_ref=other, dst_ref=other, send_sem=send2.at[i, j], recv_sem=recv2.at[i, j],
                    device_id=(x, y, c), device_id_type=MESH).wait_recv()
        for cp in first + passed:
            cp.wait_send()

    return pl.pallas_call(
        body, out_shape=[_sds(b.shape, b.dtype) for b in bufs],
        in_specs=[ANY] * n, out_specs=[ANY] * n, input_output_aliases={i: i for i in range(n)},
        scratch_shapes=[pltpu.SemaphoreType.DMA((n, 3)), pltpu.SemaphoreType.DMA((n, 3)),
                        pltpu.SemaphoreType.DMA((n, 3)), pltpu.SemaphoreType.DMA((n, 3))],
        name="gather_weights")(*bufs)


def _swap_halves(grads):
    n = len(grads)

    def body(*refs):
        ins, outs = refs[:n], refs[n:2 * n]
        send, recv = refs[2 * n:]
        x, y, c = _coords()
        copies = []
        for i in range(n):
            for s in range(N_SHARDS):
                cp = pltpu.make_async_remote_copy(
                    src_ref=ins[i].at[s, 1 - c], dst_ref=outs[i].at[s], send_sem=send.at[i, s], recv_sem=recv.at[i, s],
                    device_id=(x, y, 1 - c), device_id_type=MESH)
                cp.start()
                copies.append(cp)
        for cp in copies:
            cp.wait()

    return pl.pallas_call(
        body, out_shape=[_sds((N_SHARDS,) + g.shape[2:], g.dtype) for g in grads],
        in_specs=[ANY] * n, out_specs=[ANY] * n,
        scratch_shapes=[pltpu.SemaphoreType.DMA((n, N_SHARDS)), pltpu.SemaphoreType.DMA((n, N_SHARDS))],
        name="grad_swap_halves")(*grads)


def _scatter_chips(parts):
    n = len(parts)

    def body(*refs):
        ins, outs = refs[:n], refs[n:2 * n]
        send, recv = refs[2 * n:]
        x, y, c = _coords()
        s_me = 2 * x + y
        chips = _other_chips(x, y)
        copies = []
        for i in range(n):
            for j, (px, py) in enumerate(chips):
                cp = pltpu.make_async_remote_copy(
                    src_ref=ins[i].at[2 * px + py], dst_ref=outs[i].at[s_me], send_sem=send.at[i, j],
                    recv_sem=recv.at[i, j], device_id=(px, py, c), device_id_type=MESH)
                cp.start()
                copies.append(cp)
        for i in range(n):
            for j, (px, py) in enumerate(chips):
                got = outs[i].at[2 * px + py]
                pltpu.make_async_remote_copy(
                    src_ref=got, dst_ref=got, send_sem=send.at[i, j], recv_sem=recv.at[i, j],
                    device_id=(x, y, c), device_id_type=MESH).wait_recv()
        for cp in copies:
            cp.wait_send()

    return pl.pallas_call(
        body, out_shape=[_sds(p.shape, p.dtype) for p in parts],
        in_specs=[ANY] * n, out_specs=[ANY] * n,
        scratch_shapes=[pltpu.SemaphoreType.DMA((n, 3)), pltpu.SemaphoreType.DMA((n, 3))],
        name="grad_scatter_chips")(*parts)


def _join_halves(bufs):
    n = len(bufs)

    def body(*refs):
        ins, outs = refs[:n], refs[n:2 * n]
        send, recv = refs[2 * n:]
        x, y, c = _coords()
        copies = []
        for i in range(n):
            cp = pltpu.make_async_remote_copy(
                src_ref=ins[i].at[c], dst_ref=outs[i].at[c], send_sem=send.at[i], recv_sem=recv.at[i],
                device_id=(x, y, 1 - c), device_id_type=MESH)
            cp.start()
            copies.append(cp)
        for i in range(n):
            got = outs[i].at[1 - c]
            pltpu.make_async_remote_copy(
                src_ref=got, dst_ref=got, send_sem=send.at[i], recv_sem=recv.at[i],
                device_id=(x, y, c), device_id_type=MESH).wait_recv()
        for cp in copies:
            cp.wait_send()

    return pl.pallas_call(
        body, out_shape=[_sds(b.shape, b.dtype) for b in bufs],
        in_specs=[ANY] * n, out_specs=[ANY] * n, input_output_aliases={i: i for i in range(n)},
        scratch_shapes=[pltpu.SemaphoreType.DMA((n,)), pltpu.SemaphoreType.DMA((n,))],
        name="grad_join_halves")(*bufs)


def _to_rows(vec):
    n = -(-vec.shape[0] // 1024) * 1024
    return jnp.pad(vec, (0, n - vec.shape[0])).reshape(-1, 128)


def _halves(a):
    cols = a.shape[-1]
    rows = math.prod(a.shape[:-1])
    return a.reshape(2, rows // 2, cols)


def kernel(x, norm_mix, norm_mlp, norm_final, pool_w, pool_scale, gla_w_in, gla_w_up_f, gla_b_up_f, gla_w_up_b, gla_b_up_b, gla_g_norm, gla_w_out, mlp_w_in, mlp_w_out, loss_target, m_norm_mix, m_norm_mlp, m_norm_final, m_pool_w, m_pool_scale, m_gla_w_in, m_gla_w_up_f, m_gla_b_up_f, m_gla_w_up_b, m_gla_b_up_b, m_gla_g_norm, m_gla_w_out, m_mlp_w_in, m_mlp_w_out, v_norm_mix, v_norm_mlp, v_norm_final, v_pool_w, v_pool_scale, v_gla_w_in, v_gla_w_up_f, v_gla_b_up_f, v_gla_w_up_b, v_gla_b_up_b, v_gla_g_norm, v_gla_w_out, v_mlp_w_in, v_mlp_w_out):
    nb, seq, d = x.shape
    t = nb * seq
    dg = d // N_GROUPS
    kd = d // 2
    dv = d // N_HEADS
    pw = gla_w_in.shape[2]
    f4 = mlp_w_in.shape[2]
    dff = N_SHARDS * f4
    cx, cy, cc = _coords()
    s_me = 2 * cx + cy
    c_arr = jnp.reshape(cc, (1,)).astype(jnp.int32)
    s_arr = jnp.reshape(s_me, (1,)).astype(jnp.int32)

    xf = x.reshape(t, d)
    tgt = loss_target.reshape(t, d)

    pool_rows = N_GROUPS * (dg // N_SHARDS)
    gathered = _gather_weights([
        _fill_own(pool_w.reshape(1, pool_rows, dg), 0, s_arr), _fill_own(mlp_w_in, 0, s_arr),
        _fill_own(mlp_w_out, 0, s_arr), _fill_own(gla_w_in, 0, s_arr), _fill_own(gla_w_out, 0, s_arr),
        _fill_own(mlp_w_in, 1, s_arr), _fill_own(mlp_w_out, 1, s_arr)])
    wp = gathered[0].reshape(N_SHARDS, N_GROUPS, dg // N_SHARDS, dg)
    w1g = [gathered[1].reshape(N_SHARDS, d, f4), gathered[5].reshape(N_SHARDS, d, f4)]
    w2g = [gathered[2].reshape(dff, d), gathered[6].reshape(dff, d)]
    win = jnp.transpose(gathered[3].reshape(N_SHARDS, d, pw), (1, 0, 2)).reshape(d, N_SHARDS * pw)
    w_main = win[:, :3 * d]
    w_r = jnp.pad(win[:, 3 * d:], ((0, 0), (0, R_PAD - 2 * GATE_RANK)))
    wout = gathered[4].reshape(d, d)

    ks = kd // N_SHARDS
    small = jnp.concatenate([gla_w_up_f[0].reshape(-1), gla_w_up_b[0].reshape(-1), gla_b_up_f[0], gla_b_up_b[0],
                             gla_g_norm[0]])
    small_all = _bcast8(_to_rows(small))[::2].reshape(N_SHARDS, -1)
    o = 0
    wuf = jnp.transpose(small_all[:, o:o + GATE_RANK * ks].reshape(N_SHARDS, GATE_RANK, ks), (1, 0, 2)).reshape(GATE_RANK, kd)
    o += GATE_RANK * ks
    wub = jnp.transpose(small_all[:, o:o + GATE_RANK * ks].reshape(N_SHARDS, GATE_RANK, ks), (1, 0, 2)).reshape(GATE_RANK, kd)
    o += GATE_RANK * ks
    buf = small_all[:, o:o + ks].reshape(1, kd)
    o += ks
    bub = small_all[:, o:o + ks].reshape(1, kd)
    o += ks
    gn = small_all[:, o:o + dv // N_SHARDS].reshape(1, dv)
    w_up = jnp.zeros((R_PAD, 2 * kd), F32).at[:GATE_RANK, :kd].set(wuf).at[GATE_RANK:2 * GATE_RANK, kd:].set(wub)
    w_up = w_up.astype(BF16)
    b_up = jnp.concatenate([buf, bub], axis=1)

    hn0 = _rmsnorm_fwd(xf, norm_mix[0:1], F32, "norm_mix0")
    dm = _pool_apply(hn0, seq, False, "pool_diff")
    h1, ypre = _pool_mm_fwd(dm, wp, xf, pool_scale)
    hn1 = _rmsnorm_fwd(h1, norm_mlp[0:1], BF16, "norm_mlp0")
    h2, r0, u0 = _mlp_fwd(h1, hn1, w1g[0], w2g[0])
    hn2 = _rmsnorm_fwd(h2, norm_mix[1:2], BF16, "norm_mix1")
    (pm,) = _mm(hn2, w_main, "nn", out_dtypes=[F32], name="gla_proj")
    (pr,) = _mm(hn2, w_r, "nn", out_dtypes=[BF16], name="gla_proj_r")
    pr = pr.at[:, ONES_COL].set(1.0)

    def log_decay(acc, bv):
        z = acc + bv
        return ((jnp.minimum(z, 0.0) - jnp.log(1.0 + jnp.exp(-jnp.abs(z)))) / GATE_TAU,)

    tm_, tn_ = _tile(t, 1024), _tile(2 * kd, 1024)
    (la,) = _matmul(
        pr, w_up, dims=_NN, grid=(t // tm_, 2 * kd // tn_, 1),
        a_spec=pl.BlockSpec((tm_, R_PAD), lambda i, j, kk: (i, 0)),
        b_spec=pl.BlockSpec((R_PAD, tn_), lambda i, j, kk: (0, j)), acc_shape=(tm_, tn_),
        out_shapes=[_sds((t, 2 * kd), F32)], out_specs=[pl.BlockSpec((tm_, tn_), lambda i, j, kk: (i, j))],
        epilogue=log_decay, extras=(b_up,), extra_specs=[pl.BlockSpec((1, tn_), lambda i, j, kk: (0, j))],
        name="gla_gate_fwd")
    o_f, o_b, st_f, st_b = _gla_fwd(pm, la, seq)
    og = _gla_post_fwd(o_f, o_b, pm, gn)
    (h3,) = _mm(og, wout, "nn", out_dtypes=[F32], epilogue=lambda acc, hv: (hv + acc,), extras=(h2,), name="gla_out")
    hn3 = _rmsnorm_fwd(h3, norm_mlp[1:2], BF16, "norm_mlp1")
    h4, r1, u1 = _mlp_fwd(h3, hn3, w1g[1], w2g[1])

    loss_part, dh4, dh4b, dg_final = _final_bwd(h4, norm_final.reshape(1, d), tgt)
    dhn3, dw1_1, dw2_1 = _mlp_bwd(dh4b, hn3, r1, u1, w1g[1], w2g[1])
    dh3, dh3b, dg_mlp1 = _rmsnorm_bwd(h3, norm_mlp[1:2], dhn3, dh4, "norm_mlp1_bwd")

    (dog,) = _mm(dh3b, wout, "nt", out_dtypes=[F32], name="gla_out_bwd_x")
    (dwout,) = _mm(og, dh3b, "tn", out_dtypes=[BF16], name="gla_out_bwd_w")
    do, dgate, dg_gn = _gla_post_bwd(dog, o_f, o_b, pm, gn)
    dq_f, dk_f, dv_f, dla_f, dq_b, dk_b, dv_b, dla_b = _gla_bwd(pm, la, do, st_f, st_b, seq)
    dla = jnp.concatenate([dla_f, dla_b], axis=1)

    def gate_bwd(acc, bv, dl):
        z = acc + bv
        return (dl * (1.0 / GATE_TAU) / (1.0 + jnp.exp(z)),)

    (dz,) = _matmul(
        pr, w_up, dims=_NN, grid=(t // tm_, 2 * kd // tn_, 1),
        a_spec=pl.BlockSpec((tm_, R_PAD), lambda i, j, kk: (i, 0)),
        b_spec=pl.BlockSpec((R_PAD, tn_), lambda i, j, kk: (0, j)), acc_shape=(tm_, tn_),
        out_shapes=[_sds((t, 2 * kd), BF16)], out_specs=[pl.BlockSpec((tm_, tn_), lambda i, j, kk: (i, j))],
        epilogue=gate_bwd, extras=(b_up, dla),
        extra_specs=[pl.BlockSpec((1, tn_), lambda i, j, kk: (0, j)), pl.BlockSpec((tm_, tn_), lambda i, j, kk: (i, j))],
        name="gla_gate_bwd")
    (dpr,) = _mm(dz, w_up, "nt", out_dtypes=[BF16], name="gla_gate_bwd_r")
    (dw_up,) = _mm(pr, dz, "tn", out_dtypes=[F32], name="gla_gate_bwd_w")
    dp = _gla_dp(dq_f, dq_b, dk_f, dk_b, dv_f, dv_b, dgate)
    (dhn2_r,) = _mm(dpr, w_r, "nt", out_dtypes=[F32], name="gla_proj_bwd_xr")
    (dhn2,) = _mm(dp, w_main, "nt", out_dtypes=[F32], epilogue=lambda acc, e: (acc + e,), extras=(dhn2_r,),
                  name="gla_proj_bwd_x", tk=2048)
    (dw_main,) = _mm(hn2, dp, "tn", out_dtypes=[BF16], name="gla_proj_bwd_w")
    (dw_r,) = _mm(hn2, dpr, "tn", out_dtypes=[BF16], name="gla_proj_bwd_wr")
    dh2, dh2b, dg_mix1 = _rmsnorm_bwd(h2, norm_mix[1:2], dhn2, dh3, "norm_mix1_bwd")

    dhn1, dw1_0, dw2_0 = _mlp_bwd(dh2b, hn1, r0, u0, w1g[0], w2g[0])
    dh1, _, dg_mlp0 = _rmsnorm_bwd(h1, norm_mlp[0:1], dhn1, dh2, "norm_mlp0_bwd")

    dys, dg_pscale = _pool_scale_bwd(dh1, ypre, pool_scale)
    dd = _pool_mm_bwd_x(dys, wp)
    dwp = _pool_mm_bwd_w(dm, dys)
    dhn0 = _pool_apply(dd, seq, True, "pool_diff_bwd")
    dx, _, dg_mix0 = _rmsnorm_bwd(xf, norm_mix[0:1], dhn0, dh1, "norm_mix0_bwd")

    dwin = jnp.concatenate([dw_main, dw_r[:, :2 * GATE_RANK]], axis=1)
    dwin = jnp.transpose(dwin.reshape(d, N_SHARDS, pw), (1, 0, 2))
    grads = [dwp.reshape(N_SHARDS, N_GROUPS * (dg // N_SHARDS), dg), dw1_0, dw2_0.reshape(N_SHARDS, f4, d), dwin,
             dwout.reshape(N_SHARDS, d // N_SHARDS, d), dw1_1, dw2_1.reshape(N_SHARDS, f4, d)]
    g5 = [g.reshape(N_SHARDS, 2, g.shape[1] // 2, g.shape[2]) for g in grads]
    recv_a = _swap_halves(g5)
    pair = [_pair_sum(a, b, c_arr) for a, b in zip(g5, recv_a)]
    recv_b = _scatter_chips(pair)
    red = [_quad_sum(p, r_, s_arr, c_arr) for p, r_ in zip(pair, recv_b)]
    g_pool, g_w1_0, g_w2_0, g_win, g_wout, g_w1_1, g_w2_1 = _join_halves(red)

    dwuf, dwub = dw_up[:GATE_RANK, :kd], dw_up[GATE_RANK:2 * GATE_RANK, kd:]
    dbuf, dbub = dw_up[ONES_COL, :kd], dw_up[ONES_COL, kd:]
    pieces = [jnp.concatenate([dg_mix0, dg_mix1], 0), jnp.concatenate([dg_mlp0, dg_mlp1], 0), dg_final, dg_pscale,
              dwuf, dwub, dbuf, dbub, dg_gn]
    sizes = [p.size for p in pieces]
    packed = jnp.concatenate([p.reshape(-1) for p in pieces])
    summed = _sum8(_bcast8(_to_rows(packed))).reshape(-1)
    outs_small, o = [], 0
    for p, n in zip(pieces, sizes):
        outs_small.append(summed[o:o + n].reshape(p.shape))
        o += n
    g_nmix, g_nmlp, g_nfinal, g_pscale, g_wuf, g_wub, g_buf, g_bub, g_gn = outs_small
    g_nfinal = g_nfinal.reshape(d)
    g_wuf = lax.dynamic_slice_in_dim(g_wuf, s_me * ks, ks, axis=1)
    g_wub = lax.dynamic_slice_in_dim(g_wub, s_me * ks, ks, axis=1)
    g_buf = lax.dynamic_slice_in_dim(g_buf, s_me * ks, ks, axis=0)
    g_bub = lax.dynamic_slice_in_dim(g_bub, s_me * ks, ks, axis=0)
    g_gn = lax.dynamic_slice_in_dim(g_gn.reshape(dv), s_me * (dv // N_SHARDS), dv // N_SHARDS, axis=0)

    loss = lax.psum(loss_part[0, 0], ("x", "y", "c"))

    weights = [norm_mix, norm_mlp, norm_final, pool_w, pool_scale, gla_w_in, gla_w_up_f, gla_b_up_f, gla_w_up_b,
               gla_b_up_b, gla_g_norm, gla_w_out, mlp_w_in, mlp_w_out]
    moms = [m_norm_mix, m_norm_mlp, m_norm_final, m_pool_w, m_pool_scale, m_gla_w_in, m_gla_w_up_f, m_gla_b_up_f,
            m_gla_w_up_b, m_gla_b_up_b, m_gla_g_norm, m_gla_w_out, m_mlp_w_in, m_mlp_w_out]
    vels = [v_norm_mix, v_norm_mlp, v_norm_final, v_pool_w, v_pool_scale, v_gla_w_in, v_gla_w_up_f, v_gla_b_up_f,
            v_gla_w_up_b, v_gla_b_up_b, v_gla_g_norm, v_gla_w_out, v_mlp_w_in, v_mlp_w_out]
    g_w1 = jnp.stack([g_w1_0, g_w1_1])
    g_w2 = jnp.stack([g_w2_0, g_w2_1])
    grads_out = [g_nmix, g_nmlp, g_nfinal, g_pool, g_pscale, g_win, g_wuf, g_buf, g_wub, g_bub, g_gn, g_wout,
                 g_w1, g_w2]
    grads_out = [g.reshape(w.shape) for g, w in zip(grads_out, weights)]
    names = ["norm_mix", "norm_mlp", "norm_final", "pool_w", "pool_scale", "gla_w_in", "gla_w_up_f", "gla_b_up_f",
             "gla_w_up_b", "gla_b_up_b", "gla_g_norm", "gla_w_out", "mlp_w_in", "mlp_w_out"]
    deltas, new_m, new_v = [], [], []
    for w, g, m, v, nm in zip(weights, grads_out, moms, vels, names):
        cols = w.shape[-1]
        shp = (w.size // cols, cols)
        dl, mn, vn = _adamw(w.reshape(shp), g.reshape(shp), m.reshape(shp), v.reshape(shp), "adamw_" + nm)
        deltas.append(dl.reshape(w.shape))
        new_m.append(mn.reshape(w.shape))
        new_v.append(vn.reshape(w.shape))

    return (loss, dx.reshape(x.shape), *grads_out, *deltas, *new_m, *new_v)
```

```python
import jax
import jax.numpy as jnp
from jax import lax
from jax.experimental import pallas as pl
from jax.experimental.pallas import tpu as pltpu

F32 = jnp.float32
BF16 = jnp.bfloat16

N_HEADS = 4
N_GROUPS = 4
POOL_HALF = (1, 2, 4, 8)
GATE_RANK = 16
GATE_TAU = 16.0
CHUNK = 64
EPS = 1e-6
N_SHARDS = 4
R_PAD = 128
ONES_COL = 2 * GATE_RANK

ADAM_LR = 0.001
ADAM_B1 = 0.9
ADAM_B2 = 0.999
ADAM_EPS = 1e-08
ADAM_WD = 0.01
ADAM_STEP = 10

_NN = (((1,), (0,)), ((), ()))
_NT = (((1,), (1,)), ((), ()))
_TN = (((0,), (0,)), ((), ()))

VMEM_LIMIT = 56 * 1024 * 1024
MESH = pl.DeviceIdType.MESH
ANY = pl.BlockSpec(memory_space=pl.ANY)


def _tile(dim, pref):
    return pref if dim % pref == 0 else dim


def _params(*sem):
    return pltpu.CompilerParams(dimension_semantics=sem, vmem_limit_bytes=VMEM_LIMIT)


def _sds(shape, dtype):
    return jax.ShapeDtypeStruct(shape, dtype)


def _matmul(a, b, *, dims, grid, a_spec, b_spec, acc_shape, out_shapes, out_specs, epilogue,
            extras=(), extra_specs=(), name, pin=None):
    nk = grid[2]
    n_extra = len(extras)
    n_out = len(out_shapes)
    pins = () if pin is None else (pin,)

    def body(a_ref, b_ref, *rest):
        extra_refs = rest[:n_extra]
        rest = rest[n_extra + len(pins):]
        out_refs = rest[:n_out]
        acc_ref = rest[n_out]
        kk = pl.program_id(2)

        def part():
            return lax.dot_general(a_ref[...], b_ref[...], dims, preferred_element_type=F32)

        def finish(acc):
            outs = epilogue(acc, *[r[...] for r in extra_refs])
            for o_ref, o in zip(out_refs, outs):
                o_ref[...] = o.astype(o_ref.dtype)

        if nk == 1:
            finish(part())
        else:
            @pl.when(kk == 0)
            def _():
                acc_ref[...] = part()

            @pl.when((kk > 0) & (kk < nk - 1))
            def _():
                acc_ref[...] += part()

            @pl.when(kk == nk - 1)
            def _():
                finish(acc_ref[...] + part())

    return pl.pallas_call(
        body,
        grid=grid,
        in_specs=[a_spec, b_spec, *extra_specs, *[pl.BlockSpec((8, 128), lambda i, j, kk: (0, 0)) for _ in pins]],
        out_specs=list(out_specs),
        out_shape=list(out_shapes),
        scratch_shapes=[pltpu.VMEM(acc_shape if nk > 1 else (8, 128), F32)],
        compiler_params=_params("parallel", "parallel", "arbitrary"),
        name=name,
    )(a, b, *extras, *pins)


def _mm(a, b, kind, *, out_dtypes, epilogue=None, extras=(), name, tm=1024, tn=1024, tk=4096, pin=None):
    if kind == "nn":
        (m, k), n = a.shape, b.shape[1]
    elif kind == "nt":
        (m, k), n = a.shape, b.shape[0]
    else:
        (k, m), n = a.shape, b.shape[1]
    tm, tn, tk = _tile(m, tm), _tile(n, tn), _tile(k, tk)
    if kind == "nn":
        a_spec = pl.BlockSpec((tm, tk), lambda i, j, kk: (i, kk))
        b_spec = pl.BlockSpec((tk, tn), lambda i, j, kk: (kk, j))
        dims = _NN
    elif kind == "nt":
        a_spec = pl.BlockSpec((tm, tk), lambda i, j, kk: (i, kk))
        b_spec = pl.BlockSpec((tn, tk), lambda i, j, kk: (j, kk))
        dims = _NT
    else:
        a_spec = pl.BlockSpec((tk, tm), lambda i, j, kk: (kk, i))
        b_spec = pl.BlockSpec((tk, tn), lambda i, j, kk: (kk, j))
        dims = _TN
    o_spec = pl.BlockSpec((tm, tn), lambda i, j, kk: (i, j))
    if epilogue is None:
        epilogue = lambda acc, *e: tuple(acc for _ in out_dtypes)
    return _matmul(
        a, b, dims=dims, grid=(m // tm, n // tn, k // tk), a_spec=a_spec, b_spec=b_spec, acc_shape=(tm, tn),
        out_shapes=[_sds((m, n), d) for d in out_dtypes], out_specs=[o_spec for _ in out_dtypes],
        epilogue=epilogue, extras=extras, extra_specs=[o_spec for _ in extras], name=name, pin=pin)


def _rmsnorm_fwd(h, g, out_dtype, name):
    t, d = h.shape
    tr = _tile(t, 256)

    def body(h_ref, g_ref, o_ref):
        x = h_ref[...]
        r = lax.rsqrt(jnp.mean(x * x, axis=-1, keepdims=True) + EPS)
        o_ref[...] = (x * r * g_ref[...]).astype(o_ref.dtype)

    return pl.pallas_call(
        body, grid=(t // tr,),
        in_specs=[pl.BlockSpec((tr, d), lambda i: (i, 0)), pl.BlockSpec((1, d), lambda i: (0, 0))],
        out_specs=pl.BlockSpec((tr, d), lambda i: (i, 0)),
        out_shape=_sds((t, d), out_dtype), compiler_params=_params("parallel"), name=name)(h, g)


def _rmsnorm_bwd(h, g, dy, resid, name):
    t, d = h.shape
    tr = _tile(t, 128)

    def body(h_ref, g_ref, dy_ref, res_ref, dh_ref, dhb_ref, dg_ref):
        x = h_ref[...]
        r = lax.rsqrt(jnp.mean(x * x, axis=-1, keepdims=True) + EPS)
        xn = x * r
        dyv = dy_ref[...]
        gdy = dyv * g_ref[...]
        dh = res_ref[...] + r * (gdy - xn * jnp.mean(gdy * xn, axis=-1, keepdims=True))
        dh_ref[...] = dh
        dhb_ref[...] = dh.astype(BF16)
        part = jnp.sum(dyv * xn, axis=0, keepdims=True)

        @pl.when(pl.program_id(0) == 0)
        def _():
            dg_ref[...] = part

        @pl.when(pl.program_id(0) > 0)
        def _():
            dg_ref[...] += part

    row = pl.BlockSpec((tr, d), lambda i: (i, 0))
    vec = pl.BlockSpec((1, d), lambda i: (0, 0))
    return pl.pallas_call(
        body, grid=(t // tr,), in_specs=[row, vec, row, row], out_specs=[row, row, vec],
        out_shape=[_sds((t, d), F32), _sds((t, d), BF16), _sds((1, d), F32)],
        compiler_params=_params("arbitrary"), name=name)(h, g, dy, resid)


def _final_bwd(h, g, tgt):
    t, d = h.shape
    tr = _tile(t, 128)

    def body(h_ref, g_ref, t_ref, loss_ref, dh_ref, dhb_ref, dg_ref):
        x = h_ref[...]
        r = lax.rsqrt(jnp.mean(x * x, axis=-1, keepdims=True) + EPS)
        xn = x * r
        gv = g_ref[...]
        e = xn * gv - t_ref[...]
        lpart = jnp.full((1, 128), 0.5 * jnp.sum(jnp.mean(e * e, axis=-1, keepdims=True)), F32)
        dyv = e * (1.0 / d)
        gdy = dyv * gv
        dh = r * (gdy - xn * jnp.mean(gdy * xn, axis=-1, keepdims=True))
        dh_ref[...] = dh
        dhb_ref[...] = dh.astype(BF16)
        part = jnp.sum(dyv * xn, axis=0, keepdims=True)

        @pl.when(pl.program_id(0) == 0)
        def _():
            dg_ref[...] = part
            loss_ref[...] = lpart

        @pl.when(pl.program_id(0) > 0)
        def _():
            dg_ref[...] += part
            loss_ref[...] += lpart

    row = pl.BlockSpec((tr, d), lambda i: (i, 0))
    vec = pl.BlockSpec((1, d), lambda i: (0, 0))
    return pl.pallas_call(
        body, grid=(t // tr,), in_specs=[row, vec, row],
        out_specs=[pl.BlockSpec((1, 128), lambda i: (0, 0)), row, row, vec],
        out_shape=[_sds((1, 128), F32), _sds((t, d), F32), _sds((t, d), BF16), _sds((1, d), F32)],
        compiler_params=_params("arbitrary"), name="final_loss_bwd")(h, g, tgt)


def _shift_rows(x, s, row):
    n = x.shape[0]
    y = pltpu.roll(x, (-s) % n, 0)
    return jnp.where((row + s >= 0) & (row + s < n), y, 0.0)


def _span_sum(x, start, length, row):
    if start >= 0:
        y, step = _shift_rows(x, start, row) if start else x, 1
    else:
        last = start + length - 1
        assert last <= 0
        y, step = _shift_rows(x, last, row) if last else x, -1
    n = 1
    while n < length:
        y = y + _shift_rows(y, step * n, row)
        n *= 2
    return y


def _pool_apply(x, seq, transpose, name):
    t, d = x.shape
    dg = d // N_GROUPS
    tc = _tile(dg, 256)
    nblk = dg // tc

    def body(x_ref, o_ref):
        grp = pl.program_id(1)
        row = lax.broadcasted_iota(jnp.int32, (seq, tc), 0)
        for gi, hw in enumerate(POOL_HALF):
            @pl.when(grp == gi)
            def _(hw=hw):
                xv = x_ref[...]
                cnt = (jnp.minimum(row + hw, seq) - jnp.maximum(row - hw, 0)).astype(F32)
                if not transpose:
                    w = _span_sum(xv, 0, hw, row) + _span_sum(xv, -hw, hw, row)
                    o_ref[...] = (w / cnt - xv).astype(o_ref.dtype)
                else:
                    u = xv / cnt
                    w = _span_sum(u, 1, hw, row) + _span_sum(u, -(hw - 1), hw, row)
                    o_ref[...] = (w - xv).astype(o_ref.dtype)

    spec = pl.BlockSpec((seq, tc), lambda b, g, j: (b, g * nblk + j))
    return pl.pallas_call(
        body, grid=(t // seq, N_GROUPS, nblk), in_specs=[spec], out_specs=spec,
        out_shape=_sds((t, d), F32 if transpose else BF16),
        compiler_params=_params("parallel", "parallel", "parallel"), name=name)(x)


def _pool_mm_fwd(dm, wp, x, scale):
    t, d = dm.shape
    dg = d // N_GROUPS
    rs = dg // N_SHARDS
    tm = _tile(t, 1024)
    o_spec = pl.BlockSpec((tm, dg), lambda i, j, kk: (i, j))
    return _matmul(
        dm, wp, dims=_NN, grid=(t // tm, N_GROUPS, N_SHARDS),
        a_spec=pl.BlockSpec((tm, rs), lambda i, j, kk: (i, j * N_SHARDS + kk)),
        b_spec=pl.BlockSpec((None, None, rs, dg), lambda i, j, kk: (kk, j, 0, 0)),
        acc_shape=(tm, dg), out_shapes=[_sds((t, d), F32), _sds((t, d), F32)], out_specs=[o_spec, o_spec],
        epilogue=lambda acc, xv, sc: (xv + acc * sc, acc),
        extras=(x, scale), extra_specs=[o_spec, pl.BlockSpec((1, dg), lambda i, j, kk: (0, j))], name="pool_mm_fwd")


def _pool_scale_bwd(dh, ypre, scale):
    t, d = dh.shape
    tr = _tile(t, 256)

    def body(dh_ref, y_ref, s_ref, o_ref, ds_ref):
        g = dh_ref[...]
        o_ref[...] = (g * s_ref[...]).astype(BF16)
        part = jnp.sum(g * y_ref[...], axis=0, keepdims=True)

        @pl.when(pl.program_id(0) == 0)
        def _():
            ds_ref[...] = part

        @pl.when(pl.program_id(0) > 0)
        def _():
            ds_ref[...] += part

    row = pl.BlockSpec((tr, d), lambda i: (i, 0))
    vec = pl.BlockSpec((1, d), lambda i: (0, 0))
    return pl.pallas_call(
        body, grid=(t // tr,), in_specs=[row, row, vec], out_specs=[row, vec],
        out_shape=[_sds((t, d), BF16), _sds((1, d), F32)], compiler_params=_params("arbitrary"),
        name="pool_scale_bwd")(dh, ypre, scale)


def _pool_mm_bwd_x(dys, wp, pin):
    t, d = dys.shape
    dg = d // N_GROUPS
    rs = dg // N_SHARDS
    tm = _tile(t, 1024)
    return _matmul(
        dys, wp, dims=_NT, grid=(t // tm, N_GROUPS * N_SHARDS, 1),
        a_spec=pl.BlockSpec((tm, dg), lambda i, j, kk: (i, j // N_SHARDS)),
        b_spec=pl.BlockSpec((None, None, rs, dg), lambda i, j, kk: (j % N_SHARDS, j // N_SHARDS, 0, 0)),
        acc_shape=(tm, rs), out_shapes=[_sds((t, d), F32)],
        out_specs=[pl.BlockSpec((tm, rs), lambda i, j, kk: (i, j))],
        epilogue=lambda acc: (acc,), name="pool_mm_bwd_x", pin=pin)[0]


def _pool_mm_bwd_w(dm, dys):
    t, d = dm.shape
    dg = d // N_GROUPS
    rs = dg // N_SHARDS
    tk = _tile(t, 4096)
    return _matmul(
        dm, dys, dims=_TN, grid=(N_GROUPS * N_SHARDS, 1, t // tk),
        a_spec=pl.BlockSpec((tk, rs), lambda i, j, kk: (kk, i)),
        b_spec=pl.BlockSpec((tk, dg), lambda i, j, kk: (kk, i // N_SHARDS)),
        acc_shape=(rs, dg), out_shapes=[_sds((N_SHARDS, N_GROUPS, rs, dg), BF16)],
        out_specs=[pl.BlockSpec((None, None, rs, dg), lambda i, j, kk: (i % N_SHARDS, i // N_SHARDS, 0, 0))],
        epilogue=lambda acc: (acc,), name="pool_mm_bwd_w")[0]


def _mlp_fwd(h, hn, w1g, w2_after):
    t, d = hn.shape
    f4 = w1g.shape[2]
    dff = N_SHARDS * f4
    tm, tn, tk = _tile(t, 1024), _tile(f4, 1024), _tile(d, 4096)
    nb = f4 // tn
    o_spec = pl.BlockSpec((tm, tn), lambda i, j, kk: (i, j))

    def act(acc):
        r = jnp.maximum(acc, 0.0)
        return r, r * r

    r, u = _matmul(
        hn, w1g, dims=_NN, grid=(t // tm, dff // tn, d // tk),
        a_spec=pl.BlockSpec((tm, tk), lambda i, j, kk: (i, kk)),
        b_spec=pl.BlockSpec((None, tk, tn), lambda i, j, kk: (j // nb, kk, j % nb)),
        acc_shape=(tm, tn), out_shapes=[_sds((t, dff), BF16), _sds((t, dff), BF16)], out_specs=[o_spec, o_spec],
        epilogue=act, name="mlp_up")
    w2 = w2_after(u)
    (out,) = _mm(u, w2, "nn", out_dtypes=[F32], epilogue=lambda acc, hv: (hv + acc,), extras=(h,), name="mlp_down",
                 tk=2048)
    return out, r, u, w2


def _mlp_bwd(dhb, hn, r, u, w1g, w2):
    t, d = hn.shape
    f4 = w1g.shape[2]
    dff = N_SHARDS * f4
    (da,) = _mm(dhb, w2, "nt", out_dtypes=[BF16], epilogue=lambda acc, rv: (acc * (2.0 * rv.astype(F32)),),
                extras=(r,), name="mlp_bwd_da")
    (dw2,) = _mm(u, dhb, "tn", out_dtypes=[BF16], name="mlp_bwd_dw2")
    tok = yield dw2.reshape(N_SHARDS, f4, d)
    tm, tn, tk = _tile(d, 1024), _tile(f4, 1024), _tile(t, 4096)
    nb = f4 // tn
    (dw1,) = _matmul(
        hn, da, dims=_TN, grid=(d // tm, dff // tn, t // tk),
        a_spec=pl.BlockSpec((tk, tm), lambda i, j, kk: (kk, i)),
        b_spec=pl.BlockSpec((tk, tn), lambda i, j, kk: (kk, j)),
        acc_shape=(tm, tn), out_shapes=[_sds((N_SHARDS, d, f4), BF16)],
        out_specs=[pl.BlockSpec((None, tm, tn), lambda i, j, kk: (j // nb, i, j % nb))],
        epilogue=lambda acc: (acc,), name="mlp_bwd_dw1", pin=tok)
    tok = yield dw1
    tm, tn, tk = _tile(t, 1024), _tile(d, 1024), _tile(f4, 4096)
    nbk = f4 // tk
    (dhn,) = _matmul(
        da, w1g, dims=_NT, grid=(t // tm, d // tn, dff // tk),
        a_spec=pl.BlockSpec((tm, tk), lambda i, j, kk: (i, kk)),
        b_spec=pl.BlockSpec((None, tn, tk), lambda i, j, kk: (kk // nbk, j, kk % nbk)),
        acc_shape=(tm, tn), out_shapes=[_sds((t, d), F32)],
        out_specs=[pl.BlockSpec((tm, tn), lambda i, j, kk: (i, j))], epilogue=lambda acc: (acc,), name="mlp_bwd_dhn",
        pin=tok)
    yield dhn


def _split3(x):
    a = x.astype(BF16)
    r1 = x - a.astype(F32)
    b = r1.astype(BF16)
    c = (r1 - b.astype(F32)).astype(BF16)
    return a, b, c


def _dot(a, b, dims):
    return lax.dot_general(a.astype(BF16), b.astype(BF16), dims, preferred_element_type=F32)


def _chunk_terms(q, k, g, rev, scale):
    c = q.shape[0]
    ri = lax.broadcasted_iota(jnp.int32, (c, c), 0)
    ci = lax.broadcasted_iota(jnp.int32, (c, c), 1)
    seen = (ci >= ri) if rev else (ci <= ri)
    tri = seen.astype(BF16)
    g1, g2, g3 = _split3(g)
    b = (lax.dot_general(tri, g1, _NN, preferred_element_type=F32)
         + lax.dot_general(tri, g2, _NN, preferred_element_type=F32)
         + lax.dot_general(tri, g3, _NN, preferred_element_type=F32))
    mid = c // 2 if rev else c // 2 - 1
    last = 0 if rev else c - 1
    rows = lax.broadcasted_iota(jnp.int32, b.shape, 0)
    b_mid = jnp.sum(jnp.where(rows == mid, b, 0.0), axis=0, keepdims=True)
    b_last = jnp.sum(jnp.where(rows == last, b, 0.0), axis=0, keepdims=True)
    qs = q * scale
    e1 = jnp.exp(b - b_mid)
    e2 = jnp.exp(b_mid - b)
    eb = jnp.exp(b)
    el = jnp.exp(b_last - b)
    return dict(seen=seen, tri=tri, mid=mid, last=last, e1=e1, e2=e2, eb=eb, el=el, a=jnp.exp(b_last),
                qe=qs * e1, ke=k * e2, qi=qs * eb, ks=k * el)


def _gla_fwd(pm, la, seq):
    t = pm.shape[0]
    d = pm.shape[1] // 3
    dk, dv = d // 2 // N_HEADS, d // N_HEADS
    nb, nc = t // seq, seq // CHUNK
    scale = dk ** -0.5
    kq, kk_, kv = 0, N_HEADS, (d // dv)

    def body(qf, kf, vf, gf, qb, kb, vb, gb, of_ref, ob_ref, stf_ref, stb_ref, sf, sb):
        n = pl.program_id(1)

        @pl.when(n == 0)
        def _():
            sf[...] = jnp.zeros_like(sf)
            sb[...] = jnp.zeros_like(sb)

        for (q_ref, k_ref, v_ref, g_ref, o_ref, st_ref, s_ref, rev) in (
                (qf, kf, vf, gf, of_ref, stf_ref, sf, False), (qb, kb, vb, gb, ob_ref, stb_ref, sb, True)):
            tm = _chunk_terms(q_ref[...], k_ref[...], g_ref[...], rev, scale)
            v = v_ref[...]
            st = s_ref[...]
            stb = st.astype(BF16)
            st_ref[...] = stb
            sc = jnp.where(tm["seen"], _dot(tm["qe"], tm["ke"], _NT), 0.0)
            o = _dot(sc, v, _NN) + lax.dot_general(tm["qi"].astype(BF16), stb, _NT, preferred_element_type=F32)
            o_ref[...] = o
            s_ref[...] = st * tm["a"] + _dot(v, tm["ks"], _TN)

    def row(bh, n, rev):
        return (bh // N_HEADS) * nc + (nc - 1 - n if rev else n)

    def specs(rev):
        return [
            pl.BlockSpec((CHUNK, dk), lambda bh, n: (row(bh, n, rev), kq + bh % N_HEADS)),
            pl.BlockSpec((CHUNK, dk), lambda bh, n: (row(bh, n, rev), kk_ + bh % N_HEADS)),
            pl.BlockSpec((CHUNK, dv), lambda bh, n: (row(bh, n, rev), kv + bh % N_HEADS)),
            pl.BlockSpec((CHUNK, dk), lambda bh, n: (row(bh, n, rev), (N_HEADS if rev else 0) + bh % N_HEADS)),
        ]

    def o_spec(rev):
        return pl.BlockSpec((CHUNK, dv), lambda bh, n: (row(bh, n, rev), bh % N_HEADS))

    def st_spec(rev):
        return pl.BlockSpec((None, None, dv, dk), lambda bh, n: (bh, nc - 1 - n if rev else n, 0, 0))

    sf_, sb_ = specs(False), specs(True)
    return pl.pallas_call(
        body, grid=(nb * N_HEADS, nc),
        in_specs=[*sf_, *sb_],
        out_specs=[o_spec(False), o_spec(True), st_spec(False), st_spec(True)],
        out_shape=[_sds((t, d), F32), _sds((t, d), F32),
                   _sds((nb * N_HEADS, nc, dv, dk), BF16), _sds((nb * N_HEADS, nc, dv, dk), BF16)],
        scratch_shapes=[pltpu.VMEM((dv, dk), F32), pltpu.VMEM((dv, dk), F32)],
        compiler_params=_params("parallel", "arbitrary"), name="gla_scan_fwd",
    )(pm, pm, pm, la, pm, pm, pm, la)


def _gla_bwd(pm, la, do, st_f, st_b, seq):
    t = pm.shape[0]
    d = pm.shape[1] // 3
    dk, dv = d // 2 // N_HEADS, d // N_HEADS
    kd = dk * N_HEADS
    nb, nc = t // seq, seq // CHUNK
    scale = dk ** -0.5
    kq, kk_, kv = 0, N_HEADS, (d // dv)

    def body(qf, kf, vf, gf, dof, stf, qb, kb, vb, gb, dob, stb_,
             dqf, dkf, dvf, dgf, dqb, dkb, dvb, dgb, dsf, dsb):
        n = pl.program_id(1)

        @pl.when(n == 0)
        def _():
            dsf[...] = jnp.zeros_like(dsf)
            dsb[...] = jnp.zeros_like(dsb)

        for (q_ref, k_ref, v_ref, g_ref, do_ref, st_ref, dq_ref, dk_ref, dv_ref, dg_ref, ds_ref, rev) in (
                (qf, kf, vf, gf, dof, stf, dqf, dkf, dvf, dgf, dsf, False),
                (qb, kb, vb, gb, dob, stb_, dqb, dkb, dvb, dgb, dsb, True)):
            tm = _chunk_terms(q_ref[...], k_ref[...], g_ref[...], rev, scale)
            v = v_ref[...]
            dov = do_ref[...]
            st = st_ref[...]
            ds = ds_ref[...]
            dsb16 = ds.astype(BF16)
            sc = jnp.where(tm["seen"], _dot(tm["qe"], tm["ke"], _NT), 0.0)
            dsc = jnp.where(tm["seen"], _dot(dov, v, _NT), 0.0)
            dv_ref[...] = _dot(sc, dov, _TN) + lax.dot_general(tm["ks"].astype(BF16), dsb16, _NT,
                                                               preferred_element_type=F32)
            dqe = _dot(dsc, tm["ke"], _NN)
            dke = _dot(dsc, tm["qe"], _TN)
            dqi = lax.dot_general(dov.astype(BF16), st, _NN, preferred_element_type=F32)
            dks = lax.dot_general(v.astype(BF16), dsb16, _NN, preferred_element_type=F32)
            da = jnp.sum(ds * st.astype(F32), axis=0, keepdims=True)
            dq_ref[...] = (dqe * tm["e1"] + dqi * tm["eb"]) * scale
            dk_ref[...] = dke * tm["e2"] + dks * tm["el"]
            t_q, t_k, t_s = dqe * tm["qe"], dke * tm["ke"], dks * tm["ks"]
            db = t_q - t_k + dqi * tm["qi"] - t_s
            mid_row = jnp.sum(t_k - t_q, axis=0, keepdims=True)
            last_row = jnp.sum(t_s, axis=0, keepdims=True) + da * tm["a"]
            ridx = lax.broadcasted_iota(jnp.int32, db.shape, 0)
            db = db + jnp.where(ridx == tm["mid"], mid_row, 0.0) + jnp.where(ridx == tm["last"], last_row, 0.0)
            d1, d2, d3 = _split3(db)
            dg_ref[...] = (lax.dot_general(tm["tri"], d1, _TN, preferred_element_type=F32)
                           + lax.dot_general(tm["tri"], d2, _TN, preferred_element_type=F32)
                           + lax.dot_general(tm["tri"], d3, _TN, preferred_element_type=F32))
            ds_ref[...] = ds * tm["a"] + _dot(dov, tm["qi"], _TN)

    def row(bh, n, rev):
        return (bh // N_HEADS) * nc + (n if rev else nc - 1 - n)

    def specs(rev):
        return [
            pl.BlockSpec((CHUNK, dk), lambda bh, n: (row(bh, n, rev), kq + bh % N_HEADS)),
            pl.BlockSpec((CHUNK, dk), lambda bh, n: (row(bh, n, rev), kk_ + bh % N_HEADS)),
            pl.BlockSpec((CHUNK, dv), lambda bh, n: (row(bh, n, rev), kv + bh % N_HEADS)),
            pl.BlockSpec((CHUNK, dk), lambda bh, n: (row(bh, n, rev), (N_HEADS if rev else 0) + bh % N_HEADS)),
            pl.BlockSpec((CHUNK, dv), lambda bh, n: (row(bh, n, rev), bh % N_HEADS)),
            pl.BlockSpec((None, None, dv, dk), lambda bh, n: (bh, n if rev else nc - 1 - n, 0, 0)),
        ]

    def outs(rev):
        return [
            pl.BlockSpec((CHUNK, dk), lambda bh, n: (row(bh, n, rev), bh % N_HEADS)),
            pl.BlockSpec((CHUNK, dk), lambda bh, n: (row(bh, n, rev), bh % N_HEADS)),
            pl.BlockSpec((CHUNK, dv), lambda bh, n: (row(bh, n, rev), bh % N_HEADS)),
            pl.BlockSpec((CHUNK, dk), lambda bh, n: (row(bh, n, rev), bh % N_HEADS)),
        ]

    of_, ob_ = outs(False), outs(True)
    res = pl.pallas_call(
        body, grid=(nb * N_HEADS, nc),
        in_specs=[*specs(False), *specs(True)],
        out_specs=[*of_, *ob_],
        out_shape=[_sds((t, kd), F32), _sds((t, kd), F32), _sds((t, d), F32), _sds((t, kd), F32),
                   _sds((t, kd), F32), _sds((t, kd), F32), _sds((t, d), F32), _sds((t, kd), F32)],
        scratch_shapes=[pltpu.VMEM((dv, dk), F32), pltpu.VMEM((dv, dk), F32)],
        compiler_params=_params("parallel", "arbitrary"), name="gla_scan_bwd",
    )(pm, pm, pm, la, do, st_f, pm, pm, pm, la, do, st_b)
    return res


def _sigmoid(x):
    return 1.0 / (1.0 + jnp.exp(-x))


def _gla_post_fwd(o_f, o_b, pm, gn):
    t, d = o_f.shape
    dv = d // N_HEADS
    tr = _tile(t, 512)
    gate_blk = 2 * d // dv

    def body(of_ref, ob_ref, gt_ref, gn_ref, out_ref):
        o = of_ref[...] + ob_ref[...]
        n = o * lax.rsqrt(jnp.mean(o * o, axis=-1, keepdims=True) + EPS) * gn_ref[...]
        gt = gt_ref[...]
        out_ref[...] = (n * (gt * _sigmoid(gt))).astype(BF16)

    blk = pl.BlockSpec((tr, dv), lambda i, h: (i, h))
    return pl.pallas_call(
        body, grid=(t // tr, N_HEADS),
        in_specs=[blk, blk, pl.BlockSpec((tr, dv), lambda i, h: (i, gate_blk + h)),
                  pl.BlockSpec((1, dv), lambda i, h: (0, 0))],
        out_specs=blk, out_shape=_sds((t, d), BF16), compiler_params=_params("parallel", "parallel"),
        name="gla_post_fwd")(o_f, o_b, pm, gn)


def _gla_post_bwd(dog, o_f, o_b, pm, gn):
    t, d = o_f.shape
    dv = d // N_HEADS
    tr = _tile(t, 256)
    gate_blk = 2 * d // dv

    def body(dog_ref, of_ref, ob_ref, gt_ref, gn_ref, do_ref, dgt_ref, dgn_ref):
        o = of_ref[...] + ob_ref[...]
        rr = lax.rsqrt(jnp.mean(o * o, axis=-1, keepdims=True) + EPS)
        on = o * rr
        gnv = gn_ref[...]
        gt = gt_ref[...]
        sg = _sigmoid(gt)
        sl = gt * sg
        dg_out = dog_ref[...]
        dn = dg_out * sl
        dgt_ref[...] = dg_out * (on * gnv) * (sg * (1.0 + gt * (1.0 - sg)))
        gdn = dn * gnv
        do_ref[...] = rr * (gdn - on * jnp.mean(gdn * on, axis=-1, keepdims=True))
        part = jnp.sum(dn * on, axis=0, keepdims=True)
        first = (pl.program_id(0) == 0) & (pl.program_id(1) == 0)

        @pl.when(first)
        def _():
            dgn_ref[...] = part

        @pl.when(jnp.logical_not(first))
        def _():
            dgn_ref[...] += part

    blk = pl.BlockSpec((tr, dv), lambda i, h: (i, h))
    vec = pl.BlockSpec((1, dv), lambda i, h: (0, 0))
    return pl.pallas_call(
        body, grid=(t // tr, N_HEADS),
        in_specs=[blk, blk, blk, pl.BlockSpec((tr, dv), lambda i, h: (i, gate_blk + h)), vec],
        out_specs=[blk, blk, vec], out_shape=[_sds((t, d), F32), _sds((t, d), F32), _sds((1, dv), F32)],
        compiler_params=_params("arbitrary", "arbitrary"), name="gla_post_bwd")(dog, o_f, o_b, pm, gn)


def _gla_dp(dq_f, dq_b, dk_f, dk_b, dv_f, dv_b, dgate):
    t, d = dv_f.shape
    kd = dq_f.shape[1]
    tr = _tile(t, 128)

    def body(a1, a2, b1, b2, c1, c2, g, o_ref):
        o_ref[:, 0:kd] = (a1[...] + a2[...]).astype(BF16)
        o_ref[:, kd:2 * kd] = (b1[...] + b2[...]).astype(BF16)
        o_ref[:, 2 * kd:2 * kd + d] = (c1[...] + c2[...]).astype(BF16)
        o_ref[:, 2 * kd + d:] = g[...].astype(BF16)

    sk = pl.BlockSpec((tr, kd), lambda i: (i, 0))
    sd = pl.BlockSpec((tr, d), lambda i: (i, 0))
    return pl.pallas_call(
        body, grid=(t // tr,), in_specs=[sk, sk, sk, sk, sd, sd, sd],
        out_specs=pl.BlockSpec((tr, 3 * d), lambda i: (i, 0)), out_shape=_sds((t, 3 * d), BF16),
        compiler_params=_params("parallel"), name="gla_dp")(dq_f, dq_b, dk_f, dk_b, dv_f, dv_b, dgate)


def _adamw(w, g, m, v, name):
    r, c = w.shape
    tr = _tile(r, 128)
    bc1 = 1.0 - ADAM_B1 ** ADAM_STEP
    bc2 = 1.0 - ADAM_B2 ** ADAM_STEP

    def body(w_ref, g_ref, m_ref, v_ref, d_ref, nm_ref, nv_ref):
        gv = g_ref[...]
        mn = ADAM_B1 * m_ref[...] + (1.0 - ADAM_B1) * gv
        vn = ADAM_B2 * v_ref[...] + (1.0 - ADAM_B2) * (gv * gv)
        m_hat = mn / bc1
        v_hat = vn / bc2
        d_ref[...] = -ADAM_LR * (m_hat / (jnp.sqrt(v_hat) + ADAM_EPS) + ADAM_WD * w_ref[...])
        nm_ref[...] = mn
        nv_ref[...] = vn

    blk = pl.BlockSpec((tr, c), lambda i: (i, 0))
    return pl.pallas_call(
        body, grid=(r // tr,), in_specs=[blk] * 4, out_specs=[blk] * 3, out_shape=[_sds((r, c), F32)] * 3,
        compiler_params=_params("parallel"), name=name)(w, g, m, v)


def _adamw_layer(w, g, m, v, layer, prev, name):
    nl, r, c = w.shape
    tr = _tile(r, 128)
    bc1 = 1.0 - ADAM_B1 ** ADAM_STEP
    bc2 = 1.0 - ADAM_B2 ** ADAM_STEP
    n_prev = 0 if prev is None else 4

    def body(w_ref, g_ref, m_ref, v_ref, *rest):
        go_ref, d_ref, nm_ref, nv_ref = rest[n_prev:]
        gv = g_ref[...]
        mn = ADAM_B1 * m_ref[...] + (1.0 - ADAM_B1) * gv
        vn = ADAM_B2 * v_ref[...] + (1.0 - ADAM_B2) * (gv * gv)
        m_hat = mn / bc1
        v_hat = vn / bc2
        d_ref[...] = -ADAM_LR * (m_hat / (jnp.sqrt(v_hat) + ADAM_EPS) + ADAM_WD * w_ref[...])
        nm_ref[...] = mn
        nv_ref[...] = vn
        go_ref[...] = gv

    lay = pl.BlockSpec((None, tr, c), lambda i: (layer, i, 0))
    return pl.pallas_call(
        body, grid=(r // tr,),
        in_specs=[lay, pl.BlockSpec((tr, c), lambda i: (i, 0)), lay, lay] + [ANY] * n_prev,
        out_specs=[lay] * 4, out_shape=[_sds((nl, r, c), F32)] * 4,
        input_output_aliases={4 + k: k for k in range(n_prev)},
        compiler_params=_params("parallel"), name=name)(w, g, m, v, *(prev or ()))


def _sum8(parts):
    _, n, _ = parts.shape

    def body(p_ref, o_ref):
        acc = p_ref[0]
        for i in range(1, 8):
            acc = acc + p_ref[i]
        o_ref[...] = acc

    return pl.pallas_call(body, out_shape=_sds((n, 128), F32), name="sum8")(parts)


def _pair_sum(ga, recv, c):
    _, _, rh, cols = ga.shape
    tr = _tile(rh, 256)

    def body(c_ref, a_ref, b_ref, o_ref):
        o_ref[...] = (a_ref[...].astype(F32) + b_ref[...].astype(F32)).astype(BF16)

    grid_spec = pltpu.PrefetchScalarGridSpec(
        num_scalar_prefetch=1, grid=(N_SHARDS, rh // tr),
        in_specs=[pl.BlockSpec((None, None, tr, cols), lambda s, i, c_ref: (s, c_ref[0], i, 0)),
                  pl.BlockSpec((None, tr, cols), lambda s, i, c_ref: (s, i, 0))],
        out_specs=pl.BlockSpec((None, tr, cols), lambda s, i, c_ref: (s, i, 0)))
    return pl.pallas_call(
        body, grid_spec=grid_spec, out_shape=_sds((N_SHARDS, rh, cols), BF16),
        compiler_params=_params("parallel", "parallel"), name="grad_pair_sum")(c, ga, recv)


def _quad_sum(pair, recv, s_me, c):
    _, rh, cols = pair.shape
    tr = _tile(rh, 256)

    def body(s_ref, c_ref, p_ref, r1_ref, r2_ref, r3_ref, o_ref):
        o_ref[...] = ((p_ref[...].astype(F32) + r1_ref[...].astype(F32)) + r2_ref[...].astype(F32)) \
            + r3_ref[...].astype(F32)

    def blk(off):
        return pl.BlockSpec((None, tr, cols), lambda i, s_ref, c_ref: ((s_ref[0] + off) % N_SHARDS, i, 0))

    grid_spec = pltpu.PrefetchScalarGridSpec(
        num_scalar_prefetch=2, grid=(rh // tr,), in_specs=[blk(0), blk(1), blk(2), blk(3)],
        out_specs=pl.BlockSpec((None, tr, cols), lambda i, s_ref, c_ref: (c_ref[0], i, 0)))
    return pl.pallas_call(
        body, grid_spec=grid_spec, out_shape=_sds((2, rh, cols), F32),
        compiler_params=_params("parallel"), name="grad_quad_sum")(s_me, c, pair, recv, recv, recv)


def _fill_own(w, layer, s_me):
    _, rows, cols = w.shape
    rh = rows // 2
    tr = _tile(rh, 256)
    nblk = rh // tr

    def body(s_ref, w_ref, o_ref):
        o_ref[...] = w_ref[...].astype(BF16)

    grid_spec = pltpu.PrefetchScalarGridSpec(
        num_scalar_prefetch=1, grid=(2, nblk),
        in_specs=[pl.BlockSpec((None, tr, cols), lambda h, i, s_ref: (layer, h * nblk + i, 0))],
        out_specs=pl.BlockSpec((None, None, tr, cols), lambda h, i, s_ref: (s_ref[0], h, i, 0)))
    return pl.pallas_call(
        body, grid_spec=grid_spec, out_shape=_sds((N_SHARDS, 2, rh, cols), BF16),
        compiler_params=_params("parallel", "parallel"), name="weight_fill_own")(s_me, w)


def _coords():
    return lax.axis_index("x"), lax.axis_index("y"), lax.axis_index("c")


def _other_chips(x, y):
    return [(x, 1 - y), (1 - x, y), (1 - x, 1 - y)]


def _bcast8(buf):
    n = buf.shape[0]

    def body(b_ref, o_ref, send_sems, recv_sems):
        x, y, c = _coords()
        me = 4 * x + 2 * y + c
        o_ref[me] = b_ref[...]
        copies = []
        for k in range(1, 8):
            peer = (x ^ (k >> 2), y ^ ((k >> 1) & 1), c ^ (k & 1))
            copies.append(pltpu.make_async_remote_copy(
                src_ref=b_ref, dst_ref=o_ref.at[me], send_sem=send_sems.at[k - 1], recv_sem=recv_sems.at[k - 1],
                device_id=peer, device_id_type=MESH))
        for cp in copies:
            cp.start()
        for k in range(1, 8):
            pltpu.make_async_remote_copy(
                src_ref=b_ref, dst_ref=o_ref.at[me ^ k], send_sem=send_sems.at[k - 1], recv_sem=recv_sems.at[k - 1],
                device_id=(x, y, c), device_id_type=MESH).wait_recv()
        for cp in copies:
            cp.wait_send()

    return pl.pallas_call(
        body, out_shape=_sds((8, n, 128), F32),
        in_specs=[pl.BlockSpec(memory_space=pltpu.VMEM)], out_specs=pl.BlockSpec(memory_space=pltpu.VMEM),
        scratch_shapes=[pltpu.SemaphoreType.DMA((7,)), pltpu.SemaphoreType.DMA((7,))],
        compiler_params=pltpu.CompilerParams(vmem_limit_bytes=VMEM_LIMIT), name="bcast8")(buf)


def _swap_halves(grads):
    n = len(grads)

    def body(*refs):
        ins, outs = refs[:n], refs[n:2 * n]
        send, recv = refs[2 * n:]
        x, y, c = _coords()
        copies = []
        for i in range(n):
            for s in range(N_SHARDS):
                cp = pltpu.make_async_remote_copy(
                    src_ref=ins[i].at[s, 1 - c], dst_ref=outs[i].at[s], send_sem=send.at[i, s], recv_sem=recv.at[i, s],
                    device_id=(x, y, 1 - c), device_id_type=MESH)
                cp.start()
                copies.append(cp)
        for cp in copies:
            cp.wait()

    return pl.pallas_call(
        body, out_shape=[_sds((N_SHARDS,) + g.shape[2:], g.dtype) for g in grads],
        in_specs=[ANY] * n, out_specs=[ANY] * n,
        scratch_shapes=[pltpu.SemaphoreType.DMA((n, N_SHARDS)), pltpu.SemaphoreType.DMA((n, N_SHARDS))],
        name="grad_swap_halves")(*grads)


def _join_halves(bufs):
    n = len(bufs)

    def body(*refs):
        ins, outs = refs[:n], refs[n:2 * n]
        send, recv = refs[2 * n:]
        x, y, c = _coords()
        copies = []
        for i in range(n):
            cp = pltpu.make_async_remote_copy(
                src_ref=ins[i].at[c], dst_ref=outs[i].at[c], send_sem=send.at[i], recv_sem=recv.at[i],
                device_id=(x, y, 1 - c), device_id_type=MESH)
            cp.start()
            copies.append(cp)
        for i in range(n):
            got = outs[i].at[1 - c]
            pltpu.make_async_remote_copy(
                src_ref=got, dst_ref=got, send_sem=send.at[i], recv_sem=recv.at[i],
                device_id=(x, y, c), device_id_type=MESH).wait_recv()
        for cp in copies:
            cp.wait_send()

    return pl.pallas_call(
        body, out_shape=[_sds(b.shape, b.dtype) for b in bufs],
        in_specs=[ANY] * n, out_specs=[ANY] * n, input_output_aliases={i: i for i in range(n)},
        scratch_shapes=[pltpu.SemaphoreType.DMA((n,)), pltpu.SemaphoreType.DMA((n,))],
        name="grad_join_halves")(*bufs)


HBM = pl.BlockSpec(memory_space=pltpu.HBM)
SEM = pl.BlockSpec(memory_space=pltpu.SEMAPHORE)
SIDE = pltpu.SideEffectType.DATAFLOW_SIDE_EFFECTING
TOKEN = (8, 128)


def _in_hbm(a):
    return pltpu.with_memory_space_constraint(a, pltpu.HBM)


def _scatter_start(part, name):
    def body(p_ref, land_ref, send, recv, p_thru, land_thru, token):
        x, y, c = _coords()
        s_me = 2 * x + y
        for j, (px, py) in enumerate(_other_chips(x, y)):
            pltpu.make_async_remote_copy(
                src_ref=p_ref.at[2 * px + py], dst_ref=land_ref.at[s_me], send_sem=send.at[j], recv_sem=recv.at[j],
                device_id=(px, py, c), device_id_type=MESH).start()
        token[...] = jnp.zeros_like(token)

    buf = pltpu.HBM(part.shape, part.dtype)
    return pl.pallas_call(
        body, name=name,
        out_shape=(pltpu.SemaphoreType.DMA((3,)), pltpu.SemaphoreType.DMA((3,)), buf, buf, _sds(TOKEN, F32)),
        in_specs=(HBM, HBM), out_specs=(SEM, SEM, HBM, HBM, pl.BlockSpec(memory_space=pltpu.VMEM)),
        input_output_aliases={0: 2, 1: 3}, compiler_params=pltpu.CompilerParams(has_side_effects=SIDE),
    )(_in_hbm(part), _in_hbm(lax.empty(part.shape, part.dtype)))


def _scatter_wait(send, recv, part, land, after, name):
    def body(p_ref, land_ref, send, recv, after_ref, p_out, land_out):
        x, y, c = _coords()
        for j, (px, py) in enumerate(_other_chips(x, y)):
            cp = pltpu.make_async_remote_copy(
                src_ref=p_ref.at[2 * px + py], dst_ref=land_ref.at[2 * px + py], send_sem=send.at[j],
                recv_sem=recv.at[j], device_id=(px, py, c), device_id_type=MESH)
            cp.wait_send()
            cp.wait_recv()

    buf = pltpu.HBM(part.shape, part.dtype)
    return pl.pallas_call(
        body, name=name, out_shape=(buf, buf), in_specs=(HBM, HBM, SEM, SEM, ANY), out_specs=(HBM, HBM),
        input_output_aliases={0: 0, 1: 1}, compiler_params=pltpu.CompilerParams(has_side_effects=SIDE),
    )(part, land, send, recv, after)


def _gather_start(bufs):
    n = len(bufs)

    def body(*refs):
        ins = refs[:n]
        sems = refs[n:3 * n]
        token = refs[4 * n]
        x, y, c = _coords()
        s_me = 2 * x + y
        for i in range(n):
            for j, (px, py) in enumerate(_other_chips(x, y)):
                pltpu.make_async_remote_copy(
                    src_ref=ins[i].at[s_me, c], dst_ref=ins[i].at[s_me, c], send_sem=sems[2 * i].at[j],
                    recv_sem=sems[2 * i + 1].at[j], device_id=(px, py, c), device_id_type=MESH).start()
        token[...] = jnp.zeros_like(token)

    res = pl.pallas_call(
        body, name="gather_start",
        out_shape=(*[pltpu.SemaphoreType.DMA((3,))] * (2 * n), *[pltpu.HBM(b.shape, b.dtype) for b in bufs],
                   _sds(TOKEN, F32)),
        in_specs=(HBM,) * n, out_specs=(*[SEM] * (2 * n), *[HBM] * n, pl.BlockSpec(memory_space=pltpu.VMEM)),
        input_output_aliases={i: 2 * n + i for i in range(n)},
        compiler_params=pltpu.CompilerParams(has_side_effects=SIDE),
    )(*[_in_hbm(b) for b in bufs])
    return [(res[2 * i], res[2 * i + 1]) for i in range(n)], list(res[2 * n:3 * n]), res[3 * n]


def _gather_wait(send, recv, buf, after, name):
    def body(b_ref, send, recv, after_ref, b_out):
        x, y, c = _coords()
        s_me = 2 * x + y
        for j, (px, py) in enumerate(_other_chips(x, y)):
            cp = pltpu.make_async_remote_copy(
                src_ref=b_ref.at[s_me, c], dst_ref=b_ref.at[2 * px + py, c], send_sem=send.at[j], recv_sem=recv.at[j],
                device_id=(px, py, c), device_id_type=MESH)
            cp.wait_send()
            cp.wait_recv()

    return pl.pallas_call(
        body, name=name, out_shape=pltpu.HBM(buf.shape, buf.dtype), in_specs=(HBM, SEM, SEM, ANY), out_specs=HBM,
        input_output_aliases={0: 0}, compiler_params=pltpu.CompilerParams(has_side_effects=SIDE),
    )(buf, send, recv, after)


def _pass_on_halves(bufs):
    n = len(bufs)

    def body(*refs):
        ins, outs = refs[:n], refs[n:2 * n]
        send, recv = refs[2 * n:]
        x, y, c = _coords()
        copies = []
        for i in range(n):
            for j, (px, py) in enumerate(_other_chips(x, y)):
                cp = pltpu.make_async_remote_copy(
                    src_ref=ins[i].at[2 * px + py, c], dst_ref=outs[i].at[2 * px + py, c], send_sem=send.at[i, j],
                    recv_sem=recv.at[i, j], device_id=(x, y, 1 - c), device_id_type=MESH)
                cp.start()
                copies.append(cp)
        for i in range(n):
            for j, (px, py) in enumerate(_other_chips(x, y)):
                other = outs[i].at[2 * px + py, 1 - c]
                pltpu.make_async_remote_copy(
                    src_ref=other, dst_ref=other, send_sem=send.at[i, j], recv_sem=recv.at[i, j],
                    device_id=(x, y, c), device_id_type=MESH).wait_recv()
        for cp in copies:
            cp.wait_send()

    return pl.pallas_call(
        body, out_shape=[_sds(b.shape, b.dtype) for b in bufs],
        in_specs=[ANY] * n, out_specs=[ANY] * n, input_output_aliases={i: i for i in range(n)},
        scratch_shapes=[pltpu.SemaphoreType.DMA((n, 3)), pltpu.SemaphoreType.DMA((n, 3))],
        name="gather_pass_on")(*bufs)


def _to_rows(vec):
    n = -(-vec.shape[0] // 1024) * 1024
    return jnp.pad(vec, (0, n - vec.shape[0])).reshape(-1, 128)


def kernel(x, norm_mix, norm_mlp, norm_final, pool_w, pool_scale, gla_w_in, gla_w_up_f, gla_b_up_f, gla_w_up_b, gla_b_up_b, gla_g_norm, gla_w_out, mlp_w_in, mlp_w_out, loss_target, m_norm_mix, m_norm_mlp, m_norm_final, m_pool_w, m_pool_scale, m_gla_w_in, m_gla_w_up_f, m_gla_b_up_f, m_gla_w_up_b, m_gla_b_up_b, m_gla_g_norm, m_gla_w_out, m_mlp_w_in, m_mlp_w_out, v_norm_mix, v_norm_mlp, v_norm_final, v_pool_w, v_pool_scale, v_gla_w_in, v_gla_w_up_f, v_gla_b_up_f, v_gla_w_up_b, v_gla_b_up_b, v_gla_g_norm, v_gla_w_out, v_mlp_w_in, v_mlp_w_out):
    nb, seq, d = x.shape
    t = nb * seq
    dg = d // N_GROUPS
    kd = d // 2
    dv = d // N_HEADS
    pw = gla_w_in.shape[2]
    f4 = mlp_w_in.shape[2]
    dff = N_SHARDS * f4
    cx, cy, cc = _coords()
    s_me = 2 * cx + cy
    c_arr = jnp.reshape(cc, (1,)).astype(jnp.int32)
    s_arr = jnp.reshape(s_me, (1,)).astype(jnp.int32)

    xf = x.reshape(t, d)
    tgt = loss_target.reshape(t, d)

    pool_rows = N_GROUPS * (dg // N_SHARDS)
    gsems, gbufs, gtok = _gather_start([
        _fill_own(pool_w.reshape(1, pool_rows, dg), 0, s_arr), _fill_own(mlp_w_in, 0, s_arr),
        _fill_own(mlp_w_out, 0, s_arr), _fill_own(gla_w_in, 0, s_arr), _fill_own(gla_w_out, 0, s_arr),
        _fill_own(mlp_w_in, 1, s_arr), _fill_own(mlp_w_out, 1, s_arr)])

    def weight(i, after, name):
        buf = _gather_wait(gsems[i][0], gsems[i][1], gbufs[i], after, "gather_wait_" + name)
        return _pass_on_halves([buf])[0]

    def pinned(vec, tok):
        return vec + tok[0:1, 0:1]

    ks = kd // N_SHARDS
    small = jnp.concatenate([gla_w_up_f[0].reshape(-1), gla_w_up_b[0].reshape(-1), gla_b_up_f[0], gla_b_up_b[0],
                             gla_g_norm[0]])
    small_all = _bcast8(_to_rows(small))[::2].reshape(N_SHARDS, -1)
    o = 0
    wuf = jnp.transpose(small_all[:, o:o + GATE_RANK * ks].reshape(N_SHARDS, GATE_RANK, ks), (1, 0, 2)).reshape(GATE_RANK, kd)
    o += GATE_RANK * ks
    wub = jnp.transpose(small_all[:, o:o + GATE_RANK * ks].reshape(N_SHARDS, GATE_RANK, ks), (1, 0, 2)).reshape(GATE_RANK, kd)
    o += GATE_RANK * ks
    buf = small_all[:, o:o + ks].reshape(1, kd)
    o += ks
    bub = small_all[:, o:o + ks].reshape(1, kd)
    o += ks
    gn = small_all[:, o:o + dv // N_SHARDS].reshape(1, dv)
    w_up = jnp.zeros((R_PAD, 2 * kd), F32).at[:GATE_RANK, :kd].set(wuf).at[GATE_RANK:2 * GATE_RANK, kd:].set(wub)
    w_up = w_up.astype(BF16)
    b_up = jnp.concatenate([buf, bub], axis=1)

    hn0 = _rmsnorm_fwd(xf, pinned(norm_mix[0:1], gtok), F32, "norm_mix0")
    dm = _pool_apply(hn0, seq, False, "pool_diff")
    wp = weight(0, dm, "pool_w").reshape(N_SHARDS, N_GROUPS, dg // N_SHARDS, dg)
    h1, ypre = _pool_mm_fwd(dm, wp, xf, pool_scale)
    hn1 = _rmsnorm_fwd(h1, norm_mlp[0:1], BF16, "norm_mlp0")
    w1g = [weight(1, hn1, "mlp_w_in0").reshape(N_SHARDS, d, f4), None]
    w2g = [None, None]
    h2, r0, u0, w2g[0] = _mlp_fwd(h1, hn1, w1g[0], lambda u: weight(2, u, "mlp_w_out0").reshape(dff, d))
    hn2 = _rmsnorm_fwd(h2, norm_mix[1:2], BF16, "norm_mix1")
    win = jnp.transpose(weight(3, hn2, "gla_w_in").reshape(N_SHARDS, d, pw), (1, 0, 2)).reshape(d, N_SHARDS * pw)
    w_main = win[:, :3 * d]
    w_r = jnp.pad(win[:, 3 * d:], ((0, 0), (0, R_PAD - 2 * GATE_RANK)))
    (pm,) = _mm(hn2, w_main, "nn", out_dtypes=[F32], name="gla_proj")
    (pr,) = _mm(hn2, w_r, "nn", out_dtypes=[BF16], name="gla_proj_r")
    pr = pr.at[:, ONES_COL].set(1.0)

    def log_decay(acc, bv):
        z = acc + bv
        return ((jnp.minimum(z, 0.0) - jnp.log(1.0 + jnp.exp(-jnp.abs(z)))) / GATE_TAU,)

    tm_, tn_ = _tile(t, 1024), _tile(kd, 1024)
    (la,) = _matmul(
        pr, w_up, dims=_NN, grid=(t // tm_, 2 * kd // tn_, 1),
        a_spec=pl.BlockSpec((tm_, R_PAD), lambda i, j, kk: (i, 0)),
        b_spec=pl.BlockSpec((R_PAD, tn_), lambda i, j, kk: (0, j)), acc_shape=(tm_, tn_),
        out_shapes=[_sds((t, 2 * kd), F32)], out_specs=[pl.BlockSpec((tm_, tn_), lambda i, j, kk: (i, j))],
        epilogue=log_decay, extras=(b_up,), extra_specs=[pl.BlockSpec((1, tn_), lambda i, j, kk: (0, j))],
        name="gla_gate_fwd")
    o_f, o_b, st_f, st_b = _gla_fwd(pm, la, seq)
    og = _gla_post_fwd(o_f, o_b, pm, gn)
    wout = weight(4, og, "gla_w_out").reshape(d, d)
    (h3,) = _mm(og, wout, "nn", out_dtypes=[F32], epilogue=lambda acc, hv: (hv + acc,), extras=(h2,), name="gla_out")
    hn3 = _rmsnorm_fwd(h3, norm_mlp[1:2], BF16, "norm_mlp1")
    w1g[1] = weight(5, hn3, "mlp_w_in1").reshape(N_SHARDS, d, f4)
    h4, r1, u1, w2g[1] = _mlp_fwd(h3, hn3, w1g[1], lambda u: weight(6, u, "mlp_w_out1").reshape(dff, d))

    chains = []

    def reduce_start(g, name):
        g5 = g.reshape(N_SHARDS, 2, g.shape[1] // 2, g.shape[2])
        (from_sibling,) = _swap_halves([g5])
        pair = _pair_sum(g5, from_sibling, c_arr)
        send, recv, pair, land, tok = _scatter_start(pair, "grad_scatter_start_" + name)
        chains.append((name, send, recv, pair, land))
        return tok

    def mlp_backward(dhb, hn, r, u, w1, w2, name):
        steps = _mlp_bwd(dhb, hn, r, u, w1, w2)
        tok = reduce_start(next(steps), "mlp_w_out" + name)
        tok = reduce_start(steps.send(tok), "mlp_w_in" + name)
        return steps.send(tok)

    loss_part, dh4, dh4b, dg_final = _final_bwd(h4, norm_final.reshape(1, d), tgt)
    dhn3 = mlp_backward(dh4b, hn3, r1, u1, w1g[1], w2g[1], "1")
    dh3, dh3b, dg_mlp1 = _rmsnorm_bwd(h3, norm_mlp[1:2], dhn3, dh4, "norm_mlp1_bwd")

    (dog,) = _mm(dh3b, wout, "nt", out_dtypes=[F32], name="gla_out_bwd_x")
    (dwout,) = _mm(og, dh3b, "tn", out_dtypes=[BF16], name="gla_out_bwd_w")
    tok = reduce_start(dwout.reshape(N_SHARDS, d // N_SHARDS, d), "gla_w_out")
    do, dgate, dg_gn = _gla_post_bwd(dog, o_f, o_b, pm, pinned(gn, tok))
    dq_f, dk_f, dv_f, dla_f, dq_b, dk_b, dv_b, dla_b = _gla_bwd(pm, la, do, st_f, st_b, seq)
    nkb = kd // tn_

    def gate_bwd(acc, bv, dl_f, dl_b):
        z = acc + bv
        dl = jnp.where(pl.program_id(1) < nkb, dl_f, dl_b)
        return (dl * (1.0 / GATE_TAU) / (1.0 + jnp.exp(z)),)

    (dz,) = _matmul(
        pr, w_up, dims=_NN, grid=(t // tm_, 2 * kd // tn_, 1),
        a_spec=pl.BlockSpec((tm_, R_PAD), lambda i, j, kk: (i, 0)),
        b_spec=pl.BlockSpec((R_PAD, tn_), lambda i, j, kk: (0, j)), acc_shape=(tm_, tn_),
        out_shapes=[_sds((t, 2 * kd), BF16)], out_specs=[pl.BlockSpec((tm_, tn_), lambda i, j, kk: (i, j))],
        epilogue=gate_bwd, extras=(b_up, dla_f, dla_b),
        extra_specs=[pl.BlockSpec((1, tn_), lambda i, j, kk: (0, j)),
                     pl.BlockSpec((tm_, tn_), lambda i, j, kk: (i, jnp.minimum(j, nkb - 1))),
                     pl.BlockSpec((tm_, tn_), lambda i, j, kk: (i, jnp.maximum(j - nkb, 0)))],
        name="gla_gate_bwd")
    (dpr,) = _mm(dz, w_up, "nt", out_dtypes=[BF16], name="gla_gate_bwd_r")
    (dw_up,) = _mm(pr, dz, "tn", out_dtypes=[F32], name="gla_gate_bwd_w")
    dp = _gla_dp(dq_f, dq_b, dk_f, dk_b, dv_f, dv_b, dgate)
    (dw_main,) = _mm(hn2, dp, "tn", out_dtypes=[BF16], name="gla_proj_bwd_w")
    (dw_r,) = _mm(hn2, dpr, "tn", out_dtypes=[BF16], name="gla_proj_bwd_wr")
    dwin = jnp.concatenate([dw_main, dw_r[:, :2 * GATE_RANK]], axis=1)
    tok = reduce_start(jnp.transpose(dwin.reshape(d, N_SHARDS, pw), (1, 0, 2)), "gla_w_in")
    (dhn2_r,) = _mm(dpr, w_r, "nt", out_dtypes=[F32], name="gla_proj_bwd_xr")
    (dhn2,) = _mm(dp, w_main, "nt", out_dtypes=[F32], epilogue=lambda acc, e: (acc + e,), extras=(dhn2_r,),
                  name="gla_proj_bwd_x", tk=2048, pin=tok)
    dh2, dh2b, dg_mix1 = _rmsnorm_bwd(h2, norm_mix[1:2], dhn2, dh3, "norm_mix1_bwd")

    dhn1 = mlp_backward(dh2b, hn1, r0, u0, w1g[0], w2g[0], "0")
    dh1, _, dg_mlp0 = _rmsnorm_bwd(h1, norm_mlp[0:1], dhn1, dh2, "norm_mlp0_bwd")

    dys, dg_pscale = _pool_scale_bwd(dh1, ypre, pool_scale)
    dwp = _pool_mm_bwd_w(dm, dys)
    tok = reduce_start(dwp.reshape(N_SHARDS, pool_rows, dg), "pool_w")
    dd = _pool_mm_bwd_x(dys, wp, tok)
    dhn0 = _pool_apply(dd, seq, True, "pool_diff_bwd")
    dx, _, dg_mix0 = _rmsnorm_bwd(xf, norm_mix[0:1], dhn0, dh1, "norm_mix0_bwd")

    dwuf, dwub = dw_up[:GATE_RANK, :kd], dw_up[GATE_RANK:2 * GATE_RANK, kd:]
    dbuf, dbub = dw_up[ONES_COL, :kd], dw_up[ONES_COL, kd:]
    pieces = [jnp.concatenate([dg_mix0, dg_mix1], 0), jnp.concatenate([dg_mlp0, dg_mlp1], 0), dg_final, dg_pscale,
              dwuf, dwub, dbuf, dbub, dg_gn]
    sizes = [p.size for p in pieces]
    packed = jnp.concatenate([p.reshape(-1) for p in pieces])
    summed = _sum8(_bcast8(_to_rows(packed))).reshape(-1)
    outs_small, o = [], 0
    for p, n in zip(pieces, sizes):
        outs_small.append(summed[o:o + n].reshape(p.shape))
        o += n
    g_nmix, g_nmlp, g_nfinal, g_pscale, g_wuf, g_wub, g_buf, g_bub, g_gn = outs_small
    g_nfinal = g_nfinal.reshape(d)
    g_wuf = lax.dynamic_slice_in_dim(g_wuf, s_me * ks, ks, axis=1)
    g_wub = lax.dynamic_slice_in_dim(g_wub, s_me * ks, ks, axis=1)
    g_buf = lax.dynamic_slice_in_dim(g_buf, s_me * ks, ks, axis=0)
    g_bub = lax.dynamic_slice_in_dim(g_bub, s_me * ks, ks, axis=0)
    g_gn = lax.dynamic_slice_in_dim(g_gn.reshape(dv), s_me * (dv // N_SHARDS), dv // N_SHARDS, axis=0)

    loss = lax.psum(loss_part[0, 0], ("x", "y", "c"))

    weights = [norm_mix, norm_mlp, norm_final, pool_w, pool_scale, gla_w_in, gla_w_up_f, gla_b_up_f, gla_w_up_b,
               gla_b_up_b, gla_g_norm, gla_w_out, mlp_w_in, mlp_w_out]
    moms = [m_norm_mix, m_norm_mlp, m_norm_final, m_pool_w, m_pool_scale, m_gla_w_in, m_gla_w_up_f, m_gla_b_up_f,
            m_gla_w_up_b, m_gla_b_up_b, m_gla_g_norm, m_gla_w_out, m_mlp_w_in, m_mlp_w_out]
    vels = [v_norm_mix, v_norm_mlp, v_norm_final, v_pool_w, v_pool_scale, v_gla_w_in, v_gla_w_up_f, v_gla_b_up_f,
            v_gla_w_up_b, v_gla_b_up_b, v_gla_g_norm, v_gla_w_out, v_mlp_w_in, v_mlp_w_out]
    names = ["norm_mix", "norm_mlp", "norm_final", "pool_w", "pool_scale", "gla_w_in", "gla_w_up_f", "gla_b_up_f",
             "gla_w_up_b", "gla_b_up_b", "gla_g_norm", "gla_w_out", "mlp_w_in", "mlp_w_out"]
    index = {nm: k for k, nm in enumerate(names)}
    small_grads = {"norm_mix": g_nmix, "norm_mlp": g_nmlp, "norm_final": g_nfinal, "pool_scale": g_pscale,
                   "gla_w_up_f": g_wuf, "gla_b_up_f": g_buf, "gla_w_up_b": g_wub, "gla_b_up_b": g_bub,
                   "gla_g_norm": g_gn}
    results = {}
    for nm, g in small_grads.items():
        w, m, v = weights[index[nm]], moms[index[nm]], vels[index[nm]]
        cols = w.shape[-1]
        shp = (w.size // cols, cols)
        dl, mn, vn = _adamw(w.reshape(shp), g.reshape(shp), m.reshape(shp), v.reshape(shp), "adamw_" + nm)
        results[nm] = (g, dl, mn, vn)

    stacked = {"mlp_w_out1": ("mlp_w_out", 1), "mlp_w_in1": ("mlp_w_in", 1), "gla_w_out": ("gla_w_out", 0),
               "gla_w_in": ("gla_w_in", 0), "mlp_w_out0": ("mlp_w_out", 0), "mlp_w_in0": ("mlp_w_in", 0),
               "pool_w": ("pool_w", 0)}
    after = dx
    for name, send, recv, pair, land in chains:
        pair, land = _scatter_wait(send, recv, pair, land, after, "grad_scatter_wait_" + name)
        (full,) = _join_halves([_quad_sum(pair, land, s_arr, c_arr)])
        nm, layer = stacked[name]
        w, m, v = weights[index[nm]], moms[index[nm]], vels[index[nm]]
        rows, cols = 2 * full.shape[1], full.shape[2]
        shp = (w.size // (rows * cols), rows, cols)
        results[nm] = _adamw_layer(w.reshape(shp), full.reshape(rows, cols), m.reshape(shp), v.reshape(shp), layer,
                                   results.get(nm), "adamw_" + name)
        after = results[nm][1]

    outs = [[results[nm][k].reshape(weights[index[nm]].shape) for nm in names] for k in range(4)]
    return (loss, dx.reshape(x.shape), *outs[0], *outs[1], *outs[2], *outs[3])
```

```python
import jax
import jax.numpy as jnp
from jax import lax
from jax.experimental import pallas as pl
from jax.experimental.pallas import tpu as pltpu

F32 = jnp.float32
BF16 = jnp.bfloat16

N_HEADS = 4
N_GROUPS = 4
POOL_HALF = (1, 2, 4, 8)
GATE_RANK = 16
GATE_TAU = 16.0
CHUNK = 64
EPS = 1e-6
N_SHARDS = 4
R_PAD = 128
ONES_COL = 2 * GATE_RANK

ADAM_LR = 0.001
ADAM_B1 = 0.9
ADAM_B2 = 0.999
ADAM_EPS = 1e-08
ADAM_WD = 0.01
ADAM_STEP = 10

_NN = (((1,), (0,)), ((), ()))
_NT = (((1,), (1,)), ((), ()))
_TN = (((0,), (0,)), ((), ()))

VMEM_LIMIT = 56 * 1024 * 1024
MESH = pl.DeviceIdType.MESH
ANY = pl.BlockSpec(memory_space=pl.ANY)


def _tile(dim, pref):
    return pref if dim % pref == 0 else dim


def _params(*sem):
    return pltpu.CompilerParams(dimension_semantics=sem, vmem_limit_bytes=VMEM_LIMIT)


def _sds(shape, dtype):
    return jax.ShapeDtypeStruct(shape, dtype)


def _matmul(a, b, *, dims, grid, a_spec, b_spec, acc_shape, out_shapes, out_specs, epilogue,
            extras=(), extra_specs=(), name, pin=None):
    nk = grid[2]
    n_extra = len(extras)
    n_out = len(out_shapes)
    pins = () if pin is None else (pin,)

    def body(a_ref, b_ref, *rest):
        extra_refs = rest[:n_extra]
        rest = rest[n_extra + len(pins):]
        out_refs = rest[:n_out]
        acc_ref = rest[n_out]
        kk = pl.program_id(2)

        def part():
            return lax.dot_general(a_ref[...], b_ref[...], dims, preferred_element_type=F32)

        def finish(acc):
            outs = epilogue(acc, *[r[...] for r in extra_refs])
            for o_ref, o in zip(out_refs, outs):
                o_ref[...] = o.astype(o_ref.dtype)

        if nk == 1:
            finish(part())
        else:
            @pl.when(kk == 0)
            def _():
                acc_ref[...] = part()

            @pl.when((kk > 0) & (kk < nk - 1))
            def _():
                acc_ref[...] += part()

            @pl.when(kk == nk - 1)
            def _():
                finish(acc_ref[...] + part())

    return pl.pallas_call(
        body,
        grid=grid,
        in_specs=[a_spec, b_spec, *extra_specs, *[pl.BlockSpec((8, 128), lambda i, j, kk: (0, 0)) for _ in pins]],
        out_specs=list(out_specs),
        out_shape=list(out_shapes),
        scratch_shapes=[pltpu.VMEM(acc_shape if nk > 1 else (8, 128), F32)],
        compiler_params=_params("parallel", "parallel", "arbitrary"),
        name=name,
    )(a, b, *extras, *pins)


def _mm(a, b, kind, *, out_dtypes, epilogue=None, extras=(), name, tm=1024, tn=1024, tk=4096, pin=None):
    if kind == "nn":
        (m, k), n = a.shape, b.shape[1]
    elif kind == "nt":
        (m, k), n = a.shape, b.shape[0]
    else:
        (k, m), n = a.shape, b.shape[1]
    tm, tn, tk = _tile(m, tm), _tile(n, tn), _tile(k, tk)
    if kind == "nn":
        a_spec = pl.BlockSpec((tm, tk), lambda i, j, kk: (i, kk))
        b_spec = pl.BlockSpec((tk, tn), lambda i, j, kk: (kk, j))
        dims = _NN
    elif kind == "nt":
        a_spec = pl.BlockSpec((tm, tk), lambda i, j, kk: (i, kk))
        b_spec = pl.BlockSpec((tn, tk), lambda i, j, kk: (j, kk))
        dims = _NT
    else:
        a_spec = pl.BlockSpec((tk, tm), lambda i, j, kk: (kk, i))
        b_spec = pl.BlockSpec((tk, tn), lambda i, j, kk: (kk, j))
        dims = _TN
    o_spec = pl.BlockSpec((tm, tn), lambda i, j, kk: (i, j))
    if epilogue is None:
        epilogue = lambda acc, *e: tuple(acc for _ in out_dtypes)
    return _matmul(
        a, b, dims=dims, grid=(m // tm, n // tn, k // tk), a_spec=a_spec, b_spec=b_spec, acc_shape=(tm, tn),
        out_shapes=[_sds((m, n), d) for d in out_dtypes], out_specs=[o_spec for _ in out_dtypes],
        epilogue=epilogue, extras=extras, extra_specs=[o_spec for _ in extras], name=name, pin=pin)


def _rmsnorm_fwd(h, g, out_dtype, name):
    t, d = h.shape
    tr = _tile(t, 256)

    def body(h_ref, g_ref, o_ref):
        x = h_ref[...]
        r = lax.rsqrt(jnp.mean(x * x, axis=-1, keepdims=True) + EPS)
        o_ref[...] = (x * r * g_ref[...]).astype(o_ref.dtype)

    return pl.pallas_call(
        body, grid=(t // tr,),
        in_specs=[pl.BlockSpec((tr, d), lambda i: (i, 0)), pl.BlockSpec((1, d), lambda i: (0, 0))],
        out_specs=pl.BlockSpec((tr, d), lambda i: (i, 0)),
        out_shape=_sds((t, d), out_dtype), compiler_params=_params("parallel"), name=name)(h, g)


def _rmsnorm_bwd(h, g, dy, resid, name):
    t, d = h.shape
    tr = _tile(t, 128)

    def body(h_ref, g_ref, dy_ref, res_ref, dh_ref, dhb_ref, dg_ref):
        x = h_ref[...]
        r = lax.rsqrt(jnp.mean(x * x, axis=-1, keepdims=True) + EPS)
        xn = x * r
        dyv = dy_ref[...]
        gdy = dyv * g_ref[...]
        dh = res_ref[...] + r * (gdy - xn * jnp.mean(gdy * xn, axis=-1, keepdims=True))
        dh_ref[...] = dh
        dhb_ref[...] = dh.astype(BF16)
        part = jnp.sum(dyv * xn, axis=0, keepdims=True)

        @pl.when(pl.program_id(0) == 0)
        def _():
            dg_ref[...] = part

        @pl.when(pl.program_id(0) > 0)
        def _():
            dg_ref[...] += part

    row = pl.BlockSpec((tr, d), lambda i: (i, 0))
    vec = pl.BlockSpec((1, d), lambda i: (0, 0))
    return pl.pallas_call(
        body, grid=(t // tr,), in_specs=[row, vec, row, row], out_specs=[row, row, vec],
        out_shape=[_sds((t, d), F32), _sds((t, d), BF16), _sds((1, d), F32)],
        compiler_params=_params("arbitrary"), name=name)(h, g, dy, resid)


def _final_bwd(h, g, tgt):
    t, d = h.shape
    tr = _tile(t, 128)

    def body(h_ref, g_ref, t_ref, loss_ref, dh_ref, dhb_ref, dg_ref):
        x = h_ref[...]
        r = lax.rsqrt(jnp.mean(x * x, axis=-1, keepdims=True) + EPS)
        xn = x * r
        gv = g_ref[...]
        e = xn * gv - t_ref[...]
        lpart = jnp.full((1, 128), 0.5 * jnp.sum(jnp.mean(e * e, axis=-1, keepdims=True)), F32)
        dyv = e * (1.0 / d)
        gdy = dyv * gv
        dh = r * (gdy - xn * jnp.mean(gdy * xn, axis=-1, keepdims=True))
        dh_ref[...] = dh
        dhb_ref[...] = dh.astype(BF16)
        part = jnp.sum(dyv * xn, axis=0, keepdims=True)

        @pl.when(pl.program_id(0) == 0)
        def _():
            dg_ref[...] = part
            loss_ref[...] = lpart

        @pl.when(pl.program_id(0) > 0)
        def _():
            dg_ref[...] += part
            loss_ref[...] += lpart

    row = pl.BlockSpec((tr, d), lambda i: (i, 0))
    vec = pl.BlockSpec((1, d), lambda i: (0, 0))
    return pl.pallas_call(
        body, grid=(t // tr,), in_specs=[row, vec, row],
        out_specs=[pl.BlockSpec((1, 128), lambda i: (0, 0)), row, row, vec],
        out_shape=[_sds((1, 128), F32), _sds((t, d), F32), _sds((t, d), BF16), _sds((1, d), F32)],
        compiler_params=_params("arbitrary"), name="final_loss_bwd")(h, g, tgt)


def _shift_rows(x, s, row):
    n = x.shape[0]
    y = pltpu.roll(x, (-s) % n, 0)
    return jnp.where((row + s >= 0) & (row + s < n), y, 0.0)


def _span_sum(x, start, length, row):
    if start >= 0:
        y, step = _shift_rows(x, start, row) if start else x, 1
    else:
        last = start + length - 1
        assert last <= 0
        y, step = _shift_rows(x, last, row) if last else x, -1
    n = 1
    while n < length:
        y = y + _shift_rows(y, step * n, row)
        n *= 2
    return y


def _pool_apply(x, seq, transpose, name):
    t, d = x.shape
    dg = d // N_GROUPS
    tc = _tile(dg, 256)
    nblk = dg // tc

    def body(x_ref, o_ref):
        grp = pl.program_id(1)
        row = lax.broadcasted_iota(jnp.int32, (seq, tc), 0)
        for gi, hw in enumerate(POOL_HALF):
            @pl.when(grp == gi)
            def _(hw=hw):
                xv = x_ref[...]
                cnt = (jnp.minimum(row + hw, seq) - jnp.maximum(row - hw, 0)).astype(F32)
                if not transpose:
                    w = _span_sum(xv, 0, hw, row) + _span_sum(xv, -hw, hw, row)
                    o_ref[...] = (w / cnt - xv).astype(o_ref.dtype)
                else:
                    u = xv / cnt
                    w = _span_sum(u, 1, hw, row) + _span_sum(u, -(hw - 1), hw, row)
                    o_ref[...] = (w - xv).astype(o_ref.dtype)

    spec = pl.BlockSpec((seq, tc), lambda b, g, j: (b, g * nblk + j))
    return pl.pallas_call(
        body, grid=(t // seq, N_GROUPS, nblk), in_specs=[spec], out_specs=spec,
        out_shape=_sds((t, d), F32 if transpose else BF16),
        compiler_params=_params("parallel", "parallel", "parallel"), name=name)(x)


def _pool_mm_fwd(dm, wp, x, scale):
    t, d = dm.shape
    dg = d // N_GROUPS
    rs = dg // N_SHARDS
    tm = _tile(t, 1024)
    o_spec = pl.BlockSpec((tm, dg), lambda i, j, kk: (i, j))
    return _matmul(
        dm, wp, dims=_NN, grid=(t // tm, N_GROUPS, N_SHARDS),
        a_spec=pl.BlockSpec((tm, rs), lambda i, j, kk: (i, j * N_SHARDS + kk)),
        b_spec=pl.BlockSpec((None, None, rs, dg), lambda i, j, kk: (kk, j, 0, 0)),
        acc_shape=(tm, dg), out_shapes=[_sds((t, d), F32), _sds((t, d), F32)], out_specs=[o_spec, o_spec],
        epilogue=lambda acc, xv, sc: (xv + acc * sc, acc),
        extras=(x, scale), extra_specs=[o_spec, pl.BlockSpec((1, dg), lambda i, j, kk: (0, j))], name="pool_mm_fwd")


def _pool_scale_bwd(dh, ypre, scale):
    t, d = dh.shape
    tr = _tile(t, 256)

    def body(dh_ref, y_ref, s_ref, o_ref, ds_ref):
        g = dh_ref[...]
        o_ref[...] = (g * s_ref[...]).astype(BF16)
        part = jnp.sum(g * y_ref[...], axis=0, keepdims=True)

        @pl.when(pl.program_id(0) == 0)
        def _():
            ds_ref[...] = part

        @pl.when(pl.program_id(0) > 0)
        def _():
            ds_ref[...] += part

    row = pl.BlockSpec((tr, d), lambda i: (i, 0))
    vec = pl.BlockSpec((1, d), lambda i: (0, 0))
    return pl.pallas_call(
        body, grid=(t // tr,), in_specs=[row, row, vec], out_specs=[row, vec],
        out_shape=[_sds((t, d), BF16), _sds((1, d), F32)], compiler_params=_params("arbitrary"),
        name="pool_scale_bwd")(dh, ypre, scale)


def _pool_mm_bwd_x(dys, wp, pin):
    t, d = dys.shape
    dg = d // N_GROUPS
    rs = dg // N_SHARDS
    tm = _tile(t, 1024)
    return _matmul(
        dys, wp, dims=_NT, grid=(t // tm, N_GROUPS * N_SHARDS, 1),
        a_spec=pl.BlockSpec((tm, dg), lambda i, j, kk: (i, j // N_SHARDS)),
        b_spec=pl.BlockSpec((None, None, rs, dg), lambda i, j, kk: (j % N_SHARDS, j // N_SHARDS, 0, 0)),
        acc_shape=(tm, rs), out_shapes=[_sds((t, d), F32)],
        out_specs=[pl.BlockSpec((tm, rs), lambda i, j, kk: (i, j))],
        epilogue=lambda acc: (acc,), name="pool_mm_bwd_x", pin=pin)[0]


def _pool_mm_bwd_w(dm, dys):
    t, d = dm.shape
    dg = d // N_GROUPS
    rs = dg // N_SHARDS
    tk = _tile(t, 4096)
    return _matmul(
        dm, dys, dims=_TN, grid=(N_GROUPS * N_SHARDS, 1, t // tk),
        a_spec=pl.BlockSpec((tk, rs), lambda i, j, kk: (kk, i)),
        b_spec=pl.BlockSpec((tk, dg), lambda i, j, kk: (kk, i // N_SHARDS)),
        acc_shape=(rs, dg), out_shapes=[_sds((N_SHARDS, N_GROUPS, rs, dg), BF16)],
        out_specs=[pl.BlockSpec((None, None, rs, dg), lambda i, j, kk: (i % N_SHARDS, i // N_SHARDS, 0, 0))],
        epilogue=lambda acc: (acc,), name="pool_mm_bwd_w")[0]


def _mlp_fwd(h, hn, w1g, w2_after):
    t, d = hn.shape
    f4 = w1g.shape[2]
    dff = N_SHARDS * f4
    tm, tn, tk = _tile(t, 1024), _tile(f4, 1024), _tile(d, 4096)
    nb = f4 // tn
    o_spec = pl.BlockSpec((tm, tn), lambda i, j, kk: (i, j))

    def act(acc):
        r = jnp.maximum(acc, 0.0)
        return r, r * r

    r, u = _matmul(
        hn, w1g, dims=_NN, grid=(t // tm, dff // tn, d // tk),
        a_spec=pl.BlockSpec((tm, tk), lambda i, j, kk: (i, kk)),
        b_spec=pl.BlockSpec((None, tk, tn), lambda i, j, kk: (j // nb, kk, j % nb)),
        acc_shape=(tm, tn), out_shapes=[_sds((t, dff), BF16), _sds((t, dff), BF16)], out_specs=[o_spec, o_spec],
        epilogue=act, name="mlp_up")
    w2 = w2_after(u)
    (out,) = _mm(u, w2, "nn", out_dtypes=[F32], epilogue=lambda acc, hv: (hv + acc,), extras=(h,), name="mlp_down",
                 tk=2048)
    return out, r, u, w2


def _mlp_bwd(dhb, hn, r, u, w1g, w2):
    t, d = hn.shape
    f4 = w1g.shape[2]
    dff = N_SHARDS * f4
    (da,) = _mm(dhb, w2, "nt", out_dtypes=[BF16], epilogue=lambda acc, rv: (acc * (2.0 * rv.astype(F32)),),
                extras=(r,), name="mlp_bwd_da")
    (dw2,) = _mm(u, dhb, "tn", out_dtypes=[BF16], name="mlp_bwd_dw2")
    tok = yield dw2.reshape(N_SHARDS, f4, d)
    tm, tn, tk = _tile(d, 1024), _tile(f4, 1024), _tile(t, 4096)
    nb = f4 // tn
    (dw1,) = _matmul(
        hn, da, dims=_TN, grid=(d // tm, dff // tn, t // tk),
        a_spec=pl.BlockSpec((tk, tm), lambda i, j, kk: (kk, i)),
        b_spec=pl.BlockSpec((tk, tn), lambda i, j, kk: (kk, j)),
        acc_shape=(tm, tn), out_shapes=[_sds((N_SHARDS, d, f4), BF16)],
        out_specs=[pl.BlockSpec((None, tm, tn), lambda i, j, kk: (j // nb, i, j % nb))],
        epilogue=lambda acc: (acc,), name="mlp_bwd_dw1", pin=tok)
    tok = yield dw1
    tm, tn, tk = _tile(t, 1024), _tile(d, 1024), _tile(f4, 4096)
    nbk = f4 // tk
    (dhn,) = _matmul(
        da, w1g, dims=_NT, grid=(t // tm, d // tn, dff // tk),
        a_spec=pl.BlockSpec((tm, tk), lambda i, j, kk: (i, kk)),
        b_spec=pl.BlockSpec((None, tn, tk), lambda i, j, kk: (kk // nbk, j, kk % nbk)),
        acc_shape=(tm, tn), out_shapes=[_sds((t, d), F32)],
        out_specs=[pl.BlockSpec((tm, tn), lambda i, j, kk: (i, j))], epilogue=lambda acc: (acc,), name="mlp_bwd_dhn",
        pin=tok)
    yield dhn


def _split3(x):
    a = x.astype(BF16)
    r1 = x - a.astype(F32)
    b = r1.astype(BF16)
    c = (r1 - b.astype(F32)).astype(BF16)
    return a, b, c


def _dot(a, b, dims):
    return lax.dot_general(a.astype(BF16), b.astype(BF16), dims, preferred_element_type=F32)


def _chunk_terms(q, k, g, rev, scale):
    c = q.shape[0]
    ri = lax.broadcasted_iota(jnp.int32, (c, c), 0)
    ci = lax.broadcasted_iota(jnp.int32, (c, c), 1)
    seen = (ci >= ri) if rev else (ci <= ri)
    tri = seen.astype(BF16)
    g1, g2, g3 = _split3(g)
    b = (lax.dot_general(tri, g1, _NN, preferred_element_type=F32)
         + lax.dot_general(tri, g2, _NN, preferred_element_type=F32)
         + lax.dot_general(tri, g3, _NN, preferred_element_type=F32))
    mid = c // 2 if rev else c // 2 - 1
    last = 0 if rev else c - 1
    rows = lax.broadcasted_iota(jnp.int32, b.shape, 0)
    b_mid = jnp.sum(jnp.where(rows == mid, b, 0.0), axis=0, keepdims=True)
    b_last = jnp.sum(jnp.where(rows == last, b, 0.0), axis=0, keepdims=True)
    qs = q * scale
    e1 = jnp.exp(b - b_mid)
    e2 = jnp.exp(b_mid - b)
    eb = jnp.exp(b)
    el = jnp.exp(b_last - b)
    return dict(seen=seen, tri=tri, mid=mid, last=last, e1=e1, e2=e2, eb=eb, el=el, a=jnp.exp(b_last),
                qe=qs * e1, ke=k * e2, qi=qs * eb, ks=k * el)


def _gla_fwd(pm, la, seq):
    t = pm.shape[0]
    d = pm.shape[1] // 3
    dk, dv = d // 2 // N_HEADS, d // N_HEADS
    nb, nc = t // seq, seq // CHUNK
    scale = dk ** -0.5
    kq, kk_, kv = 0, N_HEADS, (d // dv)

    def body(qf, kf, vf, gf, qb, kb, vb, gb, of_ref, ob_ref, stf_ref, stb_ref, sf, sb):
        n = pl.program_id(1)

        @pl.when(n == 0)
        def _():
            sf[...] = jnp.zeros_like(sf)
            sb[...] = jnp.zeros_like(sb)

        for (q_ref, k_ref, v_ref, g_ref, o_ref, st_ref, s_ref, rev) in (
                (qf, kf, vf, gf, of_ref, stf_ref, sf, False), (qb, kb, vb, gb, ob_ref, stb_ref, sb, True)):
            tm = _chunk_terms(q_ref[...], k_ref[...], g_ref[...], rev, scale)
            v = v_ref[...]
            st = s_ref[...]
            stb = st.astype(BF16)
            st_ref[...] = stb
            sc = jnp.where(tm["seen"], _dot(tm["qe"], tm["ke"], _NT), 0.0)
            o = _dot(sc, v, _NN) + lax.dot_general(tm["qi"].astype(BF16), stb, _NT, preferred_element_type=F32)
            o_ref[...] = o
            s_ref[...] = st * tm["a"] + _dot(v, tm["ks"], _TN)

    def row(bh, n, rev):
        return (bh // N_HEADS) * nc + (nc - 1 - n if rev else n)

    def specs(rev):
        return [
            pl.BlockSpec((CHUNK, dk), lambda bh, n: (row(bh, n, rev), kq + bh % N_HEADS)),
            pl.BlockSpec((CHUNK, dk), lambda bh, n: (row(bh, n, rev), kk_ + bh % N_HEADS)),
            pl.BlockSpec((CHUNK, dv), lambda bh, n: (row(bh, n, rev), kv + bh % N_HEADS)),
            pl.BlockSpec((CHUNK, dk), lambda bh, n: (row(bh, n, rev), (N_HEADS if rev else 0) + bh % N_HEADS)),
        ]

    def o_spec(rev):
        return pl.BlockSpec((CHUNK, dv), lambda bh, n: (row(bh, n, rev), bh % N_HEADS))

    def st_spec(rev):
        return pl.BlockSpec((None, None, dv, dk), lambda bh, n: (bh, nc - 1 - n if rev else n, 0, 0))

    sf_, sb_ = specs(False), specs(True)
    return pl.pallas_call(
        body, grid=(nb * N_HEADS, nc),
        in_specs=[*sf_, *sb_],
        out_specs=[o_spec(False), o_spec(True), st_spec(False), st_spec(True)],
        out_shape=[_sds((t, d), F32), _sds((t, d), F32),
                   _sds((nb * N_HEADS, nc, dv, dk), BF16), _sds((nb * N_HEADS, nc, dv, dk), BF16)],
        scratch_shapes=[pltpu.VMEM((dv, dk), F32), pltpu.VMEM((dv, dk), F32)],
        compiler_params=_params("parallel", "arbitrary"), name="gla_scan_fwd",
    )(pm, pm, pm, la, pm, pm, pm, la)


def _gla_bwd(pm, la, do, st_f, st_b, seq):
    t = pm.shape[0]
    d = pm.shape[1] // 3
    dk, dv = d // 2 // N_HEADS, d // N_HEADS
    kd = dk * N_HEADS
    nb, nc = t // seq, seq // CHUNK
    scale = dk ** -0.5
    kq, kk_, kv = 0, N_HEADS, (d // dv)

    def body(qf, kf, vf, gf, dof, stf, qb, kb, vb, gb, dob, stb_,
             dqf, dkf, dvf, dgf, dqb, dkb, dvb, dgb, dsf, dsb):
        n = pl.program_id(1)

        @pl.when(n == 0)
        def _():
            dsf[...] = jnp.zeros_like(dsf)
            dsb[...] = jnp.zeros_like(dsb)

        for (q_ref, k_ref, v_ref, g_ref, do_ref, st_ref, dq_ref, dk_ref, dv_ref, dg_ref, ds_ref, rev) in (
                (qf, kf, vf, gf, dof, stf, dqf, dkf, dvf, dgf, dsf, False),
                (qb, kb, vb, gb, dob, stb_, dqb, dkb, dvb, dgb, dsb, True)):
            tm = _chunk_terms(q_ref[...], k_ref[...], g_ref[...], rev, scale)
            v = v_ref[...]
            dov = do_ref[...]
            st = st_ref[...]
            ds = ds_ref[...]
            dsb16 = ds.astype(BF16)
            sc = jnp.where(tm["seen"], _dot(tm["qe"], tm["ke"], _NT), 0.0)
            dsc = jnp.where(tm["seen"], _dot(dov, v, _NT), 0.0)
            dv_ref[...] = _dot(sc, dov, _TN) + lax.dot_general(tm["ks"].astype(BF16), dsb16, _NT,
                                                               preferred_element_type=F32)
            dqe = _dot(dsc, tm["ke"], _NN)
            dke = _dot(dsc, tm["qe"], _TN)
            dqi = lax.dot_general(dov.astype(BF16), st, _NN, preferred_element_type=F32)
            dks = lax.dot_general(v.astype(BF16), dsb16, _NN, preferred_element_type=F32)
            da = jnp.sum(ds * st.astype(F32), axis=0, keepdims=True)
            dq_ref[...] = (dqe * tm["e1"] + dqi * tm["eb"]) * scale
            dk_ref[...] = dke * tm["e2"] + dks * tm["el"]
            t_q, t_k, t_s = dqe * tm["qe"], dke * tm["ke"], dks * tm["ks"]
            db = t_q - t_k + dqi * tm["qi"] - t_s
            mid_row = jnp.sum(t_k - t_q, axis=0, keepdims=True)
            last_row = jnp.sum(t_s, axis=0, keepdims=True) + da * tm["a"]
            ridx = lax.broadcasted_iota(jnp.int32, db.shape, 0)
            db = db + jnp.where(ridx == tm["mid"], mid_row, 0.0) + jnp.where(ridx == tm["last"], last_row, 0.0)
            d1, d2, d3 = _split3(db)
            dg_ref[...] = (lax.dot_general(tm["tri"], d1, _TN, preferred_element_type=F32)
                           + lax.dot_general(tm["tri"], d2, _TN, preferred_element_type=F32)
                           + lax.dot_general(tm["tri"], d3, _TN, preferred_element_type=F32))
            ds_ref[...] = ds * tm["a"] + _dot(dov, tm["qi"], _TN)

    def row(bh, n, rev):
        return (bh // N_HEADS) * nc + (n if rev else nc - 1 - n)

    def specs(rev):
        return [
            pl.BlockSpec((CHUNK, dk), lambda bh, n: (row(bh, n, rev), kq + bh % N_HEADS)),
            pl.BlockSpec((CHUNK, dk), lambda bh, n: (row(bh, n, rev), kk_ + bh % N_HEADS)),
            pl.BlockSpec((CHUNK, dv), lambda bh, n: (row(bh, n, rev), kv + bh % N_HEADS)),
            pl.BlockSpec((CHUNK, dk), lambda bh, n: (row(bh, n, rev), (N_HEADS if rev else 0) + bh % N_HEADS)),
            pl.BlockSpec((CHUNK, dv), lambda bh, n: (row(bh, n, rev), bh % N_HEADS)),
            pl.BlockSpec((None, None, dv, dk), lambda bh, n: (bh, n if rev else nc - 1 - n, 0, 0)),
        ]

    def outs(rev):
        return [
            pl.BlockSpec((CHUNK, dk), lambda bh, n: (row(bh, n, rev), bh % N_HEADS)),
            pl.BlockSpec((CHUNK, dk), lambda bh, n: (row(bh, n, rev), bh % N_HEADS)),
            pl.BlockSpec((CHUNK, dv), lambda bh, n: (row(bh, n, rev), bh % N_HEADS)),
            pl.BlockSpec((CHUNK, dk), lambda bh, n: (row(bh, n, rev), bh % N_HEADS)),
        ]

    of_, ob_ = outs(False), outs(True)
    res = pl.pallas_call(
        body, grid=(nb * N_HEADS, nc),
        in_specs=[*specs(False), *specs(True)],
        out_specs=[*of_, *ob_],
        out_shape=[_sds((t, kd), F32), _sds((t, kd), F32), _sds((t, d), F32), _sds((t, kd), F32),
                   _sds((t, kd), F32), _sds((t, kd), F32), _sds((t, d), F32), _sds((t, kd), F32)],
        scratch_shapes=[pltpu.VMEM((dv, dk), F32), pltpu.VMEM((dv, dk), F32)],
        compiler_params=_params("parallel", "arbitrary"), name="gla_scan_bwd",
    )(pm, pm, pm, la, do, st_f, pm, pm, pm, la, do, st_b)
    return res


def _sigmoid(x):
    return 1.0 / (1.0 + jnp.exp(-x))


def _gla_post_fwd(o_f, o_b, pm, gn):
    t, d = o_f.shape
    dv = d // N_HEADS
    tr = _tile(t, 512)
    gate_blk = 2 * d // dv

    def body(of_ref, ob_ref, gt_ref, gn_ref, out_ref):
        o = of_ref[...] + ob_ref[...]
        n = o * lax.rsqrt(jnp.mean(o * o, axis=-1, keepdims=True) + EPS) * gn_ref[...]
        gt = gt_ref[...]
        out_ref[...] = (n * (gt * _sigmoid(gt))).astype(BF16)

    blk = pl.BlockSpec((tr, dv), lambda i, h: (i, h))
    return pl.pallas_call(
        body, grid=(t // tr, N_HEADS),
        in_specs=[blk, blk, pl.BlockSpec((tr, dv), lambda i, h: (i, gate_blk + h)),
                  pl.BlockSpec((1, dv), lambda i, h: (0, 0))],
        out_specs=blk, out_shape=_sds((t, d), BF16), compiler_params=_params("parallel", "parallel"),
        name="gla_post_fwd")(o_f, o_b, pm, gn)


def _gla_post_bwd(dog, o_f, o_b, pm, gn):
    t, d = o_f.shape
    dv = d // N_HEADS
    tr = _tile(t, 256)
    gate_blk = 2 * d // dv

    def body(dog_ref, of_ref, ob_ref, gt_ref, gn_ref, do_ref, dgt_ref, dgn_ref):
        o = of_ref[...] + ob_ref[...]
        rr = lax.rsqrt(jnp.mean(o * o, axis=-1, keepdims=True) + EPS)
        on = o * rr
        gnv = gn_ref[...]
        gt = gt_ref[...]
        sg = _sigmoid(gt)
        sl = gt * sg
        dg_out = dog_ref[...]
        dn = dg_out * sl
        dgt_ref[...] = dg_out * (on * gnv) * (sg * (1.0 + gt * (1.0 - sg)))
        gdn = dn * gnv
        do_ref[...] = rr * (gdn - on * jnp.mean(gdn * on, axis=-1, keepdims=True))
        part = jnp.sum(dn * on, axis=0, keepdims=True)
        first = (pl.program_id(0) == 0) & (pl.program_id(1) == 0)

        @pl.when(first)
        def _():
            dgn_ref[...] = part

        @pl.when(jnp.logical_not(first))
        def _():
            dgn_ref[...] += part

    blk = pl.BlockSpec((tr, dv), lambda i, h: (i, h))
    vec = pl.BlockSpec((1, dv), lambda i, h: (0, 0))
    return pl.pallas_call(
        body, grid=(t // tr, N_HEADS),
        in_specs=[blk, blk, blk, pl.BlockSpec((tr, dv), lambda i, h: (i, gate_blk + h)), vec],
        out_specs=[blk, blk, vec], out_shape=[_sds((t, d), F32), _sds((t, d), F32), _sds((1, dv), F32)],
        compiler_params=_params("arbitrary", "arbitrary"), name="gla_post_bwd")(dog, o_f, o_b, pm, gn)


def _gla_dp(dq_f, dq_b, dk_f, dk_b, dv_f, dv_b, dgate):
    t, d = dv_f.shape
    kd = dq_f.shape[1]
    tr = _tile(t, 128)

    def body(a1, a2, b1, b2, c1, c2, g, o_ref):
        o_ref[:, 0:kd] = (a1[...] + a2[...]).astype(BF16)
        o_ref[:, kd:2 * kd] = (b1[...] + b2[...]).astype(BF16)
        o_ref[:, 2 * kd:2 * kd + d] = (c1[...] + c2[...]).astype(BF16)
        o_ref[:, 2 * kd + d:] = g[...].astype(BF16)

    sk = pl.BlockSpec((tr, kd), lambda i: (i, 0))
    sd = pl.BlockSpec((tr, d), lambda i: (i, 0))
    return pl.pallas_call(
        body, grid=(t // tr,), in_specs=[sk, sk, sk, sk, sd, sd, sd],
        out_specs=pl.BlockSpec((tr, 3 * d), lambda i: (i, 0)), out_shape=_sds((t, 3 * d), BF16),
        compiler_params=_params("parallel"), name="gla_dp")(dq_f, dq_b, dk_f, dk_b, dv_f, dv_b, dgate)


def _adamw(w, g, m, v, name):
    r, c = w.shape
    tr = _tile(r, 128)
    bc1 = 1.0 - ADAM_B1 ** ADAM_STEP
    bc2 = 1.0 - ADAM_B2 ** ADAM_STEP

    def body(w_ref, g_ref, m_ref, v_ref, d_ref, nm_ref, nv_ref):
        gv = g_ref[...]
        mn = ADAM_B1 * m_ref[...] + (1.0 - ADAM_B1) * gv
        vn = ADAM_B2 * v_ref[...] + (1.0 - ADAM_B2) * (gv * gv)
        m_hat = mn / bc1
        v_hat = vn / bc2
        d_ref[...] = -ADAM_LR * (m_hat / (jnp.sqrt(v_hat) + ADAM_EPS) + ADAM_WD * w_ref[...])
        nm_ref[...] = mn
        nv_ref[...] = vn

    blk = pl.BlockSpec((tr, c), lambda i: (i, 0))
    return pl.pallas_call(
        body, grid=(r // tr,), in_specs=[blk] * 4, out_specs=[blk] * 3, out_shape=[_sds((r, c), F32)] * 3,
        compiler_params=_params("parallel"), name=name)(w, g, m, v)


def _adamw_half(w, g, m, v, layer, half, prev, pin, name):
    nl, r, c = w.shape
    rh = r // 2
    tr = _tile(rh, 128)
    nblk = rh // tr
    bc1 = 1.0 - ADAM_B1 ** ADAM_STEP
    bc2 = 1.0 - ADAM_B2 ** ADAM_STEP
    n_skip = 1 + (0 if prev is None else 4)

    def body(h_ref, w_ref, g_ref, m_ref, v_ref, *rest):
        go_ref, d_ref, nm_ref, nv_ref = rest[n_skip:]
        gv = g_ref[...]
        mn = ADAM_B1 * m_ref[...] + (1.0 - ADAM_B1) * gv
        vn = ADAM_B2 * v_ref[...] + (1.0 - ADAM_B2) * (gv * gv)
        m_hat = mn / bc1
        v_hat = vn / bc2
        d_ref[...] = -ADAM_LR * (m_hat / (jnp.sqrt(v_hat) + ADAM_EPS) + ADAM_WD * w_ref[...])
        nm_ref[...] = mn
        nv_ref[...] = vn
        go_ref[...] = gv

    lay = pl.BlockSpec((None, tr, c), lambda i, h_ref: (layer, h_ref[0] * nblk + i, 0))
    prev = tuple(prev or ())
    grid_spec = pltpu.PrefetchScalarGridSpec(
        num_scalar_prefetch=1, grid=(nblk,),
        in_specs=[lay, pl.BlockSpec((tr, c), lambda i, h_ref: (i, 0)), lay, lay,
                  pl.BlockSpec(TOKEN, lambda i, h_ref: (0, 0))] + [ANY] * len(prev),
        out_specs=[lay] * 4)
    return pl.pallas_call(
        body, grid_spec=grid_spec, out_shape=[_sds((nl, r, c), F32)] * 4,
        input_output_aliases={6 + k: k for k in range(len(prev))},
        compiler_params=_params("parallel"), name=name)(half, w, g, m, v, pin, *prev)


def _sum8(parts):
    _, n, _ = parts.shape

    def body(p_ref, o_ref):
        acc = p_ref[0]
        for i in range(1, 8):
            acc = acc + p_ref[i]
        o_ref[...] = acc

    return pl.pallas_call(body, out_shape=_sds((n, 128), F32), name="sum8")(parts)


def _pair_sum(ga, recv, c):
    _, _, rh, cols = ga.shape
    tr = _tile(rh, 256)

    def body(c_ref, a_ref, b_ref, o_ref):
        o_ref[...] = (a_ref[...].astype(F32) + b_ref[...].astype(F32)).astype(BF16)

    grid_spec = pltpu.PrefetchScalarGridSpec(
        num_scalar_prefetch=1, grid=(N_SHARDS, rh // tr),
        in_specs=[pl.BlockSpec((None, None, tr, cols), lambda s, i, c_ref: (s, c_ref[0], i, 0)),
                  pl.BlockSpec((None, tr, cols), lambda s, i, c_ref: (s, i, 0))],
        out_specs=pl.BlockSpec((None, tr, cols), lambda s, i, c_ref: (s, i, 0)))
    return pl.pallas_call(
        body, grid_spec=grid_spec, out_shape=_sds((N_SHARDS, rh, cols), BF16),
        compiler_params=_params("parallel", "parallel"), name="grad_pair_sum")(c, ga, recv)


def _quad_sum(pair, recv, s_me):
    _, rh, cols = pair.shape
    tr = _tile(rh, 256)

    def body(s_ref, p_ref, r1_ref, r2_ref, r3_ref, o_ref):
        o_ref[...] = ((p_ref[...].astype(F32) + r1_ref[...].astype(F32)) + r2_ref[...].astype(F32)) \
            + r3_ref[...].astype(F32)

    def blk(off):
        return pl.BlockSpec((None, tr, cols), lambda i, s_ref: ((s_ref[0] + off) % N_SHARDS, i, 0))

    grid_spec = pltpu.PrefetchScalarGridSpec(
        num_scalar_prefetch=1, grid=(rh // tr,), in_specs=[blk(0), blk(1), blk(2), blk(3)],
        out_specs=pl.BlockSpec((tr, cols), lambda i, s_ref: (i, 0)))
    return pl.pallas_call(
        body, grid_spec=grid_spec, out_shape=_sds((rh, cols), F32),
        compiler_params=_params("parallel"), name="grad_quad_sum")(s_me, pair, recv, recv, recv)


def _fill_own(w, layer, s_me):
    _, rows, cols = w.shape
    rh = rows // 2
    tr = _tile(rh, 256)
    nblk = rh // tr

    def body(s_ref, w_ref, o_ref):
        o_ref[...] = w_ref[...].astype(BF16)

    grid_spec = pltpu.PrefetchScalarGridSpec(
        num_scalar_prefetch=1, grid=(2, nblk),
        in_specs=[pl.BlockSpec((None, tr, cols), lambda h, i, s_ref: (layer, h * nblk + i, 0))],
        out_specs=pl.BlockSpec((None, None, tr, cols), lambda h, i, s_ref: (s_ref[0], h, i, 0)))
    return pl.pallas_call(
        body, grid_spec=grid_spec, out_shape=_sds((N_SHARDS, 2, rh, cols), BF16),
        compiler_params=_params("parallel", "parallel"), name="weight_fill_own")(s_me, w)


def _coords():
    return lax.axis_index("x"), lax.axis_index("y"), lax.axis_index("c")


def _other_chips(x, y):
    return [(x, 1 - y), (1 - x, y), (1 - x, 1 - y)]


def _bcast8(buf):
    n = buf.shape[0]

    def body(b_ref, o_ref, send_sems, recv_sems):
        x, y, c = _coords()
        me = 4 * x + 2 * y + c
        o_ref[me] = b_ref[...]
        copies = []
        for k in range(1, 8):
            peer = (x ^ (k >> 2), y ^ ((k >> 1) & 1), c ^ (k & 1))
            copies.append(pltpu.make_async_remote_copy(
                src_ref=b_ref, dst_ref=o_ref.at[me], send_sem=send_sems.at[k - 1], recv_sem=recv_sems.at[k - 1],
                device_id=peer, device_id_type=MESH))
        for cp in copies:
            cp.start()
        for k in range(1, 8):
            pltpu.make_async_remote_copy(
                src_ref=b_ref, dst_ref=o_ref.at[me ^ k], send_sem=send_sems.at[k - 1], recv_sem=recv_sems.at[k - 1],
                device_id=(x, y, c), device_id_type=MESH).wait_recv()
        for cp in copies:
            cp.wait_send()

    return pl.pallas_call(
        body, out_shape=_sds((8, n, 128), F32),
        in_specs=[pl.BlockSpec(memory_space=pltpu.VMEM)], out_specs=pl.BlockSpec(memory_space=pltpu.VMEM),
        scratch_shapes=[pltpu.SemaphoreType.DMA((7,)), pltpu.SemaphoreType.DMA((7,))],
        compiler_params=pltpu.CompilerParams(vmem_limit_bytes=VMEM_LIMIT), name="bcast8")(buf)


HBM = pl.BlockSpec(memory_space=pltpu.HBM)
SEM = pl.BlockSpec(memory_space=pltpu.SEMAPHORE)
SIDE = pltpu.SideEffectType.DATAFLOW_SIDE_EFFECTING
TOKEN = (8, 128)


def _in_hbm(a):
    return pltpu.with_memory_space_constraint(a, pltpu.HBM)


def _sibling_start(src, pieces, land_shape, after, name):
    n = len(pieces(None, None, None))

    def body(s_ref, land_ref, after_ref, send, recv, land_thru, token):
        x, y, c = _coords()
        for k, (a, b) in enumerate(pieces(s_ref, land_ref, c)):
            pltpu.make_async_remote_copy(
                src_ref=a, dst_ref=b, send_sem=send.at[k], recv_sem=recv.at[k], device_id=(x, y, 1 - c),
                device_id_type=MESH).start()
        token[...] = jnp.zeros_like(token)

    return pl.pallas_call(
        body, name=name,
        out_shape=(pltpu.SemaphoreType.DMA((n,)), pltpu.SemaphoreType.DMA((n,)), pltpu.HBM(land_shape, src.dtype),
                   _sds(TOKEN, F32)),
        in_specs=(HBM, HBM, ANY), out_specs=(SEM, SEM, HBM, pl.BlockSpec(memory_space=pltpu.VMEM)),
        input_output_aliases={1: 2}, compiler_params=pltpu.CompilerParams(has_side_effects=SIDE),
    )(_in_hbm(src), _in_hbm(lax.empty(land_shape, src.dtype)), after)


def _sibling_wait(send, recv, src, land, pieces, after, name):
    def body(s_ref, land_ref, send, recv, after_ref, land_out):
        x, y, c = _coords()
        for k, (a, b) in enumerate(pieces(s_ref, land_ref, c)):
            cp = pltpu.make_async_remote_copy(
                src_ref=a, dst_ref=b, send_sem=send.at[k], recv_sem=recv.at[k], device_id=(x, y, 1 - c),
                device_id_type=MESH)
            cp.wait_send()
            cp.wait_recv()

    return pl.pallas_call(
        body, name=name, out_shape=pltpu.HBM(land.shape, land.dtype), in_specs=(HBM, HBM, SEM, SEM, ANY),
        out_specs=HBM, input_output_aliases={1: 0}, compiler_params=pltpu.CompilerParams(has_side_effects=SIDE),
    )(src, land, send, recv, after)


def _swap_pieces(g_ref, land_ref, c):
    if g_ref is None:
        return [None] * N_SHARDS
    return [(g_ref.at[s, 1 - c], land_ref.at[s]) for s in range(N_SHARDS)]


def _whole_piece(r_ref, land_ref, c):
    return [(r_ref, land_ref)]


def _scatter_start(part, name):
    def body(p_ref, land_ref, send, recv, p_thru, land_thru, token):
        x, y, c = _coords()
        s_me = 2 * x + y
        for j, (px, py) in enumerate(_other_chips(x, y)):
            pltpu.make_async_remote_copy(
                src_ref=p_ref.at[2 * px + py], dst_ref=land_ref.at[s_me], send_sem=send.at[j], recv_sem=recv.at[j],
                device_id=(px, py, c), device_id_type=MESH).start()
        token[...] = jnp.zeros_like(token)

    buf = pltpu.HBM(part.shape, part.dtype)
    return pl.pallas_call(
        body, name=name,
        out_shape=(pltpu.SemaphoreType.DMA((3,)), pltpu.SemaphoreType.DMA((3,)), buf, buf, _sds(TOKEN, F32)),
        in_specs=(HBM, HBM), out_specs=(SEM, SEM, HBM, HBM, pl.BlockSpec(memory_space=pltpu.VMEM)),
        input_output_aliases={0: 2, 1: 3}, compiler_params=pltpu.CompilerParams(has_side_effects=SIDE),
    )(_in_hbm(part), _in_hbm(lax.empty(part.shape, part.dtype)))


def _scatter_wait(send, recv, part, land, after, name):
    def body(p_ref, land_ref, send, recv, after_ref, p_out, land_out):
        x, y, c = _coords()
        for j, (px, py) in enumerate(_other_chips(x, y)):
            cp = pltpu.make_async_remote_copy(
                src_ref=p_ref.at[2 * px + py], dst_ref=land_ref.at[2 * px + py], send_sem=send.at[j],
                recv_sem=recv.at[j], device_id=(px, py, c), device_id_type=MESH)
            cp.wait_send()
            cp.wait_recv()

    buf = pltpu.HBM(part.shape, part.dtype)
    return pl.pallas_call(
        body, name=name, out_shape=(buf, buf), in_specs=(HBM, HBM, SEM, SEM, ANY), out_specs=(HBM, HBM),
        input_output_aliases={0: 0, 1: 1}, compiler_params=pltpu.CompilerParams(has_side_effects=SIDE),
    )(part, land, send, recv, after)


def _gather_start(bufs, after):
    n = len(bufs)

    def body(*refs):
        ins = refs[:n]
        sems = refs[n + 1:3 * n + 1]
        token = refs[4 * n + 1]
        x, y, c = _coords()
        s_me = 2 * x + y
        for i in range(n):
            for j, (px, py) in enumerate(_other_chips(x, y)):
                pltpu.make_async_remote_copy(
                    src_ref=ins[i].at[s_me, c], dst_ref=ins[i].at[s_me, c], send_sem=sems[2 * i].at[j],
                    recv_sem=sems[2 * i + 1].at[j], device_id=(px, py, c), device_id_type=MESH).start()
        token[...] = jnp.zeros_like(token)

    res = pl.pallas_call(
        body, name="gather_start",
        out_shape=(*[pltpu.SemaphoreType.DMA((3,))] * (2 * n), *[pltpu.HBM(b.shape, b.dtype) for b in bufs],
                   _sds(TOKEN, F32)),
        in_specs=(*[HBM] * n, ANY), out_specs=(*[SEM] * (2 * n), *[HBM] * n, pl.BlockSpec(memory_space=pltpu.VMEM)),
        input_output_aliases={i: 2 * n + i for i in range(n)},
        compiler_params=pltpu.CompilerParams(has_side_effects=SIDE),
    )(*[_in_hbm(b) for b in bufs], after)
    return [(res[2 * i], res[2 * i + 1]) for i in range(n)], list(res[2 * n:3 * n]), res[3 * n]


def _gather_wait(send, recv, buf, after, name):
    def body(b_ref, send, recv, after_ref, b_out):
        x, y, c = _coords()
        s_me = 2 * x + y
        for j, (px, py) in enumerate(_other_chips(x, y)):
            cp = pltpu.make_async_remote_copy(
                src_ref=b_ref.at[s_me, c], dst_ref=b_ref.at[2 * px + py, c], send_sem=send.at[j], recv_sem=recv.at[j],
                device_id=(px, py, c), device_id_type=MESH)
            cp.wait_send()
            cp.wait_recv()

    return pl.pallas_call(
        body, name=name, out_shape=pltpu.HBM(buf.shape, buf.dtype), in_specs=(HBM, SEM, SEM, ANY), out_specs=HBM,
        input_output_aliases={0: 0}, compiler_params=pltpu.CompilerParams(has_side_effects=SIDE),
    )(buf, send, recv, after)


def _pass_on_halves(bufs):
    n = len(bufs)

    def body(*refs):
        ins, outs = refs[:n], refs[n:2 * n]
        send, recv = refs[2 * n:]
        x, y, c = _coords()
        copies = []
        for i in range(n):
            for j, (px, py) in enumerate(_other_chips(x, y)):
                cp = pltpu.make_async_remote_copy(
                    src_ref=ins[i].at[2 * px + py, c], dst_ref=outs[i].at[2 * px + py, c], send_sem=send.at[i, j],
                    recv_sem=recv.at[i, j], device_id=(x, y, 1 - c), device_id_type=MESH)
                cp.start()
                copies.append(cp)
        for i in range(n):
            for j, (px, py) in enumerate(_other_chips(x, y)):
                other = outs[i].at[2 * px + py, 1 - c]
                pltpu.make_async_remote_copy(
                    src_ref=other, dst_ref=other, send_sem=send.at[i, j], recv_sem=recv.at[i, j],
                    device_id=(x, y, c), device_id_type=MESH).wait_recv()
        for cp in copies:
            cp.wait_send()

    return pl.pallas_call(
        body, out_shape=[_sds(b.shape, b.dtype) for b in bufs],
        in_specs=[ANY] * n, out_specs=[ANY] * n, input_output_aliases={i: i for i in range(n)},
        scratch_shapes=[pltpu.SemaphoreType.DMA((n, 3)), pltpu.SemaphoreType.DMA((n, 3))],
        name="gather_pass_on")(*bufs)


def _to_rows(vec):
    n = -(-vec.shape[0] // 1024) * 1024
    return jnp.pad(vec, (0, n - vec.shape[0])).reshape(-1, 128)


def kernel(x, norm_mix, norm_mlp, norm_final, pool_w, pool_scale, gla_w_in, gla_w_up_f, gla_b_up_f, gla_w_up_b, gla_b_up_b, gla_g_norm, gla_w_out, mlp_w_in, mlp_w_out, loss_target, m_norm_mix, m_norm_mlp, m_norm_final, m_pool_w, m_pool_scale, m_gla_w_in, m_gla_w_up_f, m_gla_b_up_f, m_gla_w_up_b, m_gla_b_up_b, m_gla_g_norm, m_gla_w_out, m_mlp_w_in, m_mlp_w_out, v_norm_mix, v_norm_mlp, v_norm_final, v_pool_w, v_pool_scale, v_gla_w_in, v_gla_w_up_f, v_gla_b_up_f, v_gla_w_up_b, v_gla_b_up_b, v_gla_g_norm, v_gla_w_out, v_mlp_w_in, v_mlp_w_out):
    nb, seq, d = x.shape
    t = nb * seq
    dg = d // N_GROUPS
    kd = d // 2
    dv = d // N_HEADS
    pw = gla_w_in.shape[2]
    f4 = mlp_w_in.shape[2]
    dff = N_SHARDS * f4
    cx, cy, cc = _coords()
    s_me = 2 * cx + cy
    c_arr = jnp.reshape(cc, (1,)).astype(jnp.int32)
    s_arr = jnp.reshape(s_me, (1,)).astype(jnp.int32)

    xf = x.reshape(t, d)
    tgt = loss_target.reshape(t, d)

    ks = kd // N_SHARDS
    small = jnp.concatenate([gla_w_up_f[0].reshape(-1), gla_w_up_b[0].reshape(-1), gla_b_up_f[0], gla_b_up_b[0],
                             gla_g_norm[0]])
    small_all = _bcast8(_to_rows(small))
    pool_rows = N_GROUPS * (dg // N_SHARDS)
    gsems, gbufs, gtok = _gather_start([
        _fill_own(pool_w.reshape(1, pool_rows, dg), 0, s_arr), _fill_own(mlp_w_in, 0, s_arr),
        _fill_own(mlp_w_out, 0, s_arr), _fill_own(gla_w_in, 0, s_arr), _fill_own(gla_w_out, 0, s_arr),
        _fill_own(mlp_w_in, 1, s_arr), _fill_own(mlp_w_out, 1, s_arr)], small_all)
    small_all = small_all[::2].reshape(N_SHARDS, -1)

    def weight(i, after, name):
        buf = _gather_wait(gsems[i][0], gsems[i][1], gbufs[i], after, "gather_wait_" + name)
        return _pass_on_halves([buf])[0]

    def pinned(vec, tok):
        return vec + tok[0:1, 0:1]

    o = 0
    wuf = jnp.transpose(small_all[:, o:o + GATE_RANK * ks].reshape(N_SHARDS, GATE_RANK, ks), (1, 0, 2)).reshape(GATE_RANK, kd)
    o += GATE_RANK * ks
    wub = jnp.transpose(small_all[:, o:o + GATE_RANK * ks].reshape(N_SHARDS, GATE_RANK, ks), (1, 0, 2)).reshape(GATE_RANK, kd)
    o += GATE_RANK * ks
    buf = small_all[:, o:o + ks].reshape(1, kd)
    o += ks
    bub = small_all[:, o:o + ks].reshape(1, kd)
    o += ks
    gn = small_all[:, o:o + dv // N_SHARDS].reshape(1, dv)
    w_up = jnp.zeros((R_PAD, 2 * kd), F32).at[:GATE_RANK, :kd].set(wuf).at[GATE_RANK:2 * GATE_RANK, kd:].set(wub)
    w_up = w_up.astype(BF16)
    b_up = jnp.concatenate([buf, bub], axis=1)

    hn0 = _rmsnorm_fwd(xf, pinned(norm_mix[0:1], gtok), F32, "norm_mix0")
    dm = _pool_apply(hn0, seq, False, "pool_diff")
    wp = weight(0, dm, "pool_w").reshape(N_SHARDS, N_GROUPS, dg // N_SHARDS, dg)
    h1, ypre = _pool_mm_fwd(dm, wp, xf, pool_scale)
    hn1 = _rmsnorm_fwd(h1, norm_mlp[0:1], BF16, "norm_mlp0")
    w1g = [weight(1, hn1, "mlp_w_in0").reshape(N_SHARDS, d, f4), None]
    w2g = [None, None]
    h2, r0, u0, w2g[0] = _mlp_fwd(h1, hn1, w1g[0], lambda u: weight(2, u, "mlp_w_out0").reshape(dff, d))
    hn2 = _rmsnorm_fwd(h2, norm_mix[1:2], BF16, "norm_mix1")
    win = jnp.transpose(weight(3, hn2, "gla_w_in").reshape(N_SHARDS, d, pw), (1, 0, 2)).reshape(d, N_SHARDS * pw)
    w_main = win[:, :3 * d]
    w_r = jnp.pad(win[:, 3 * d:], ((0, 0), (0, R_PAD - 2 * GATE_RANK)))
    (pm,) = _mm(hn2, w_main, "nn", out_dtypes=[F32], name="gla_proj")
    (pr,) = _mm(hn2, w_r, "nn", out_dtypes=[BF16], name="gla_proj_r")
    pr = pr.at[:, ONES_COL].set(1.0)

    def log_decay(acc, bv):
        z = acc + bv
        return ((jnp.minimum(z, 0.0) - jnp.log(1.0 + jnp.exp(-jnp.abs(z)))) / GATE_TAU,)

    tm_, tn_ = _tile(t, 1024), _tile(kd, 1024)
    (la,) = _matmul(
        pr, w_up, dims=_NN, grid=(t // tm_, 2 * kd // tn_, 1),
        a_spec=pl.BlockSpec((tm_, R_PAD), lambda i, j, kk: (i, 0)),
        b_spec=pl.BlockSpec((R_PAD, tn_), lambda i, j, kk: (0, j)), acc_shape=(tm_, tn_),
        out_shapes=[_sds((t, 2 * kd), F32)], out_specs=[pl.BlockSpec((tm_, tn_), lambda i, j, kk: (i, j))],
        epilogue=log_decay, extras=(b_up,), extra_specs=[pl.BlockSpec((1, tn_), lambda i, j, kk: (0, j))],
        name="gla_gate_fwd")
    o_f, o_b, st_f, st_b = _gla_fwd(pm, la, seq)
    og = _gla_post_fwd(o_f, o_b, pm, gn)
    wout = weight(4, og, "gla_w_out").reshape(d, d)
    (h3,) = _mm(og, wout, "nn", out_dtypes=[F32], epilogue=lambda acc, hv: (hv + acc,), extras=(h2,), name="gla_out")
    hn3 = _rmsnorm_fwd(h3, norm_mlp[1:2], BF16, "norm_mlp1")
    w1g[1] = weight(5, hn3, "mlp_w_in1").reshape(N_SHARDS, d, f4)
    h4, r1, u1, w2g[1] = _mlp_fwd(h3, hn3, w1g[1], lambda u: weight(6, u, "mlp_w_out1").reshape(dff, d))

    chains = []
    last_tok = [gtok]

    def swap_begin(g, name):
        g5 = g.reshape(N_SHARDS, 2, g.shape[1] // 2, g.shape[2])
        send, recv, land, tok = _sibling_start(g5, _swap_pieces, (N_SHARDS,) + g5.shape[2:], last_tok[0],
                                               "grad_swap_start_" + name)
        last_tok[0] = tok
        return (name, send, recv, g5, land), tok

    def swap_end(state, after):
        name, send, recv, g5, land = state
        land = _sibling_wait(send, recv, g5, land, _swap_pieces, after, "grad_swap_wait_" + name)
        pair = _pair_sum(g5, land, c_arr)
        send, recv, pair, land, tok = _scatter_start(pair, "grad_scatter_start_" + name)
        chains.append((name, send, recv, pair, land))
        last_tok[0] = tok
        return tok

    def mlp_backward(dhb, hn, r, u, w1, w2, name):
        steps = _mlp_bwd(dhb, hn, r, u, w1, w2)
        state, tok = swap_begin(next(steps), "mlp_w_out" + name)
        dw1 = steps.send(tok)
        swap_end(state, dw1)
        state, tok = swap_begin(dw1, "mlp_w_in" + name)
        dhn = steps.send(tok)
        return dhn, swap_end(state, dhn)

    loss_part, dh4, dh4b, dg_final = _final_bwd(h4, norm_final.reshape(1, d), tgt)
    dhn3, tok = mlp_backward(dh4b, hn3, r1, u1, w1g[1], w2g[1], "1")
    dh3, dh3b, dg_mlp1 = _rmsnorm_bwd(h3, pinned(norm_mlp[1:2], tok), dhn3, dh4, "norm_mlp1_bwd")

    (dog,) = _mm(dh3b, wout, "nt", out_dtypes=[F32], name="gla_out_bwd_x")
    (dwout,) = _mm(og, dh3b, "tn", out_dtypes=[BF16], name="gla_out_bwd_w")
    state, tok = swap_begin(dwout.reshape(N_SHARDS, d // N_SHARDS, d), "gla_w_out")
    do, dgate, dg_gn = _gla_post_bwd(dog, o_f, o_b, pm, pinned(gn, tok))
    dq_f, dk_f, dv_f, dla_f, dq_b, dk_b, dv_b, dla_b = _gla_bwd(pm, la, do, st_f, st_b, seq)
    b_up_p = pinned(b_up, swap_end(state, dq_f))
    nkb = kd // tn_

    def gate_bwd(acc, bv, dl_f, dl_b):
        z = acc + bv
        dl = jnp.where(pl.program_id(1) < nkb, dl_f, dl_b)
        return (dl * (1.0 / GATE_TAU) / (1.0 + jnp.exp(z)),)

    (dz,) = _matmul(
        pr, w_up, dims=_NN, grid=(t // tm_, 2 * kd // tn_, 1),
        a_spec=pl.BlockSpec((tm_, R_PAD), lambda i, j, kk: (i, 0)),
        b_spec=pl.BlockSpec((R_PAD, tn_), lambda i, j, kk: (0, j)), acc_shape=(tm_, tn_),
        out_shapes=[_sds((t, 2 * kd), BF16)], out_specs=[pl.BlockSpec((tm_, tn_), lambda i, j, kk: (i, j))],
        epilogue=gate_bwd, extras=(b_up_p, dla_f, dla_b),
        extra_specs=[pl.BlockSpec((1, tn_), lambda i, j, kk: (0, j)),
                     pl.BlockSpec((tm_, tn_), lambda i, j, kk: (i, jnp.minimum(j, nkb - 1))),
                     pl.BlockSpec((tm_, tn_), lambda i, j, kk: (i, jnp.maximum(j - nkb, 0)))],
        name="gla_gate_bwd")
    (dpr,) = _mm(dz, w_up, "nt", out_dtypes=[BF16], name="gla_gate_bwd_r")
    (dw_up,) = _mm(pr, dz, "tn", out_dtypes=[F32], name="gla_gate_bwd_w")
    dp = _gla_dp(dq_f, dq_b, dk_f, dk_b, dv_f, dv_b, dgate)
    (dw_main,) = _mm(hn2, dp, "tn", out_dtypes=[BF16], name="gla_proj_bwd_w")
    (dw_r,) = _mm(hn2, dpr, "tn", out_dtypes=[BF16], name="gla_proj_bwd_wr")
    dwin = jnp.concatenate([dw_main, dw_r[:, :2 * GATE_RANK]], axis=1)
    state, tok = swap_begin(jnp.transpose(dwin.reshape(d, N_SHARDS, pw), (1, 0, 2)), "gla_w_in")
    (dhn2_r,) = _mm(dpr, w_r, "nt", out_dtypes=[F32], name="gla_proj_bwd_xr")
    (dhn2,) = _mm(dp, w_main, "nt", out_dtypes=[F32], epilogue=lambda acc, e: (acc + e,), extras=(dhn2_r,),
                  name="gla_proj_bwd_x", tk=2048, pin=tok)
    tok = swap_end(state, dhn2)
    dh2, dh2b, dg_mix1 = _rmsnorm_bwd(h2, pinned(norm_mix[1:2], tok), dhn2, dh3, "norm_mix1_bwd")

    dhn1, tok = mlp_backward(dh2b, hn1, r0, u0, w1g[0], w2g[0], "0")
    dh1, _, dg_mlp0 = _rmsnorm_bwd(h1, pinned(norm_mlp[0:1], tok), dhn1, dh2, "norm_mlp0_bwd")

    dys, dg_pscale = _pool_scale_bwd(dh1, ypre, pool_scale)
    dwp = _pool_mm_bwd_w(dm, dys)
    state, tok = swap_begin(dwp.reshape(N_SHARDS, pool_rows, dg), "pool_w")
    dd = _pool_mm_bwd_x(dys, wp, tok)
    tok = swap_end(state, dd)
    dhn0 = _pool_apply(dd, seq, True, "pool_diff_bwd")
    dx, _, dg_mix0 = _rmsnorm_bwd(xf, pinned(norm_mix[0:1], tok), dhn0, dh1, "norm_mix0_bwd")

    dwuf, dwub = dw_up[:GATE_RANK, :kd], dw_up[GATE_RANK:2 * GATE_RANK, kd:]
    dbuf, dbub = dw_up[ONES_COL, :kd], dw_up[ONES_COL, kd:]
    pieces = [jnp.concatenate([dg_mix0, dg_mix1], 0), jnp.concatenate([dg_mlp0, dg_mlp1], 0), dg_final, dg_pscale,
              dwuf, dwub, dbuf, dbub, dg_gn]
    sizes = [p.size for p in pieces]
    packed = jnp.concatenate([p.reshape(-1) for p in pieces])
    summed = _sum8(_bcast8(_to_rows(packed))).reshape(-1)
    outs_small, o = [], 0
    for p, n in zip(pieces, sizes):
        outs_small.append(summed[o:o + n].reshape(p.shape))
        o += n
    g_nmix, g_nmlp, g_nfinal, g_pscale, g_wuf, g_wub, g_buf, g_bub, g_gn = outs_small
    g_nfinal = g_nfinal.reshape(d)
    g_wuf = lax.dynamic_slice_in_dim(g_wuf, s_me * ks, ks, axis=1)
    g_wub = lax.dynamic_slice_in_dim(g_wub, s_me * ks, ks, axis=1)
    g_buf = lax.dynamic_slice_in_dim(g_buf, s_me * ks, ks, axis=0)
    g_bub = lax.dynamic_slice_in_dim(g_bub, s_me * ks, ks, axis=0)
    g_gn = lax.dynamic_slice_in_dim(g_gn.reshape(dv), s_me * (dv // N_SHARDS), dv // N_SHARDS, axis=0)

    loss = lax.psum(loss_part[0, 0], ("x", "y", "c"))

    weights = [norm_mix, norm_mlp, norm_final, pool_w, pool_scale, gla_w_in, gla_w_up_f, gla_b_up_f, gla_w_up_b,
               gla_b_up_b, gla_g_norm, gla_w_out, mlp_w_in, mlp_w_out]
    moms = [m_norm_mix, m_norm_mlp, m_norm_final, m_pool_w, m_pool_scale, m_gla_w_in, m_gla_w_up_f, m_gla_b_up_f,
            m_gla_w_up_b, m_gla_b_up_b, m_gla_g_norm, m_gla_w_out, m_mlp_w_in, m_mlp_w_out]
    vels = [v_norm_mix, v_norm_mlp, v_norm_final, v_pool_w, v_pool_scale, v_gla_w_in, v_gla_w_up_f, v_gla_b_up_f,
            v_gla_w_up_b, v_gla_b_up_b, v_gla_g_norm, v_gla_w_out, v_mlp_w_in, v_mlp_w_out]
    names = ["norm_mix", "norm_mlp", "norm_final", "pool_w", "pool_scale", "gla_w_in", "gla_w_up_f", "gla_b_up_f",
             "gla_w_up_b", "gla_b_up_b", "gla_g_norm", "gla_w_out", "mlp_w_in", "mlp_w_out"]
    index = {nm: k for k, nm in enumerate(names)}
    small_grads = {"norm_mix": g_nmix, "norm_mlp": g_nmlp, "norm_final": g_nfinal, "pool_scale": g_pscale,
                   "gla_w_up_f": g_wuf, "gla_b_up_f": g_buf, "gla_w_up_b": g_wub, "gla_b_up_b": g_bub,
                   "gla_g_norm": g_gn}
    results = {}
    for nm, g in small_grads.items():
        w, m, v = weights[index[nm]], moms[index[nm]], vels[index[nm]]
        cols = w.shape[-1]
        shp = (w.size // cols, cols)
        dl, mn, vn = _adamw(w.reshape(shp), g.reshape(shp), m.reshape(shp), v.reshape(shp), "adamw_" + nm)
        results[nm] = (g, dl, mn, vn)

    stacked = {"mlp_w_out1": ("mlp_w_out", 1), "mlp_w_in1": ("mlp_w_in", 1), "gla_w_out": ("gla_w_out", 0),
               "gla_w_in": ("gla_w_in", 0), "mlp_w_out0": ("mlp_w_out", 0), "mlp_w_in0": ("mlp_w_in", 0),
               "pool_w": ("pool_w", 0)}
    oc_arr = 1 - c_arr
    after = dx
    for name, send, recv, pair, land in chains:
        pair, land = _scatter_wait(send, recv, pair, land, after, "grad_scatter_wait_" + name)
        mine = _quad_sum(pair, land, s_arr)
        jsend, jrecv, jland, jtok = _sibling_start(mine, _whole_piece, mine.shape, after, "grad_join_start_" + name)
        nm, layer = stacked[name]
        w, m, v = weights[index[nm]], moms[index[nm]], vels[index[nm]]
        rows, cols = 2 * mine.shape[0], mine.shape[1]
        shp = (w.size // (rows * cols), rows, cols)
        w, m, v = w.reshape(shp), m.reshape(shp), v.reshape(shp)
        res = _adamw_half(w, mine, m, v, layer, c_arr, results.get(nm), jtok, "adamw_mine_" + name)
        theirs = _sibling_wait(jsend, jrecv, mine, jland, _whole_piece, res[1], "grad_join_wait_" + name)
        results[nm] = _adamw_half(w, theirs, m, v, layer, oc_arr, res, jtok, "adamw_theirs_" + name)
        after = results[nm][1]

    outs = [[results[nm][k].reshape(weights[index[nm]].shape) for nm in names] for k in range(4)]
    return (loss, dx.reshape(x.shape), *outs[0], *outs[1], *outs[2], *outs[3])
```

```python
import jax
import jax.numpy as jnp
from jax import lax
from jax.experimental import pallas as pl
from jax.experimental.pallas import tpu as pltpu

F32 = jnp.float32
BF16 = jnp.bfloat16

N_HEADS = 4
N_GROUPS = 4
POOL_HALF = (1, 2, 4, 8)
GATE_RANK = 16
GATE_TAU = 16.0
CHUNK = 64
EPS = 1e-6
N_SHARDS = 4
R_PAD = 128
ONES_COL = 2 * GATE_RANK

ADAM_LR = 0.001
ADAM_B1 = 0.9
ADAM_B2 = 0.999
ADAM_EPS = 1e-08
ADAM_WD = 0.01
ADAM_STEP = 10

_NN = (((1,), (0,)), ((), ()))
_NT = (((1,), (1,)), ((), ()))
_TN = (((0,), (0,)), ((), ()))

VMEM_LIMIT = 56 * 1024 * 1024
MESH = pl.DeviceIdType.MESH
ANY = pl.BlockSpec(memory_space=pl.ANY)


def _tile(dim, pref):
    return pref if dim % pref == 0 else dim


def _params(*sem):
    return pltpu.CompilerParams(dimension_semantics=sem, vmem_limit_bytes=VMEM_LIMIT)


def _sds(shape, dtype):
    return jax.ShapeDtypeStruct(shape, dtype)


def _matmul(a, b, *, dims, grid, a_spec, b_spec, acc_shape, out_shapes, out_specs, epilogue,
            extras=(), extra_specs=(), name, pin=None):
    nk = grid[2]
    n_extra = len(extras)
    n_out = len(out_shapes)
    pins = () if pin is None else (pin,)

    def body(a_ref, b_ref, *rest):
        extra_refs = rest[:n_extra]
        rest = rest[n_extra + len(pins):]
        out_refs = rest[:n_out]
        acc_ref = rest[n_out]
        kk = pl.program_id(2)

        def part():
            return lax.dot_general(a_ref[...], b_ref[...], dims, preferred_element_type=F32)

        def finish(acc):
            outs = epilogue(acc, *[r[...] for r in extra_refs])
            for o_ref, o in zip(out_refs, outs):
                o_ref[...] = o.astype(o_ref.dtype)

        if nk == 1:
            finish(part())
        else:
            @pl.when(kk == 0)
            def _():
                acc_ref[...] = part()

            @pl.when((kk > 0) & (kk < nk - 1))
            def _():
                acc_ref[...] += part()

            @pl.when(kk == nk - 1)
            def _():
                finish(acc_ref[...] + part())

    return pl.pallas_call(
        body,
        grid=grid,
        in_specs=[a_spec, b_spec, *extra_specs, *[pl.BlockSpec((8, 128), lambda i, j, kk: (0, 0)) for _ in pins]],
        out_specs=list(out_specs),
        out_shape=list(out_shapes),
        scratch_shapes=[pltpu.VMEM(acc_shape if nk > 1 else (8, 128), F32)],
        compiler_params=_params("parallel", "parallel", "arbitrary"),
        name=name,
    )(a, b, *extras, *pins)


def _mm(a, b, kind, *, out_dtypes, epilogue=None, extras=(), name, tm=1024, tn=1024, tk=4096, pin=None):
    if kind == "nn":
        (m, k), n = a.shape, b.shape[1]
    elif kind == "nt":
        (m, k), n = a.shape, b.shape[0]
    else:
        (k, m), n = a.shape, b.shape[1]
    tm, tn, tk = _tile(m, tm), _tile(n, tn), _tile(k, tk)
    if kind == "nn":
        a_spec = pl.BlockSpec((tm, tk), lambda i, j, kk: (i, kk))
        b_spec = pl.BlockSpec((tk, tn), lambda i, j, kk: (kk, j))
        dims = _NN
    elif kind == "nt":
        a_spec = pl.BlockSpec((tm, tk), lambda i, j, kk: (i, kk))
        b_spec = pl.BlockSpec((tn, tk), lambda i, j, kk: (j, kk))
        dims = _NT
    else:
        a_spec = pl.BlockSpec((tk, tm), lambda i, j, kk: (kk, i))
        b_spec = pl.BlockSpec((tk, tn), lambda i, j, kk: (kk, j))
        dims = _TN
    o_spec = pl.BlockSpec((tm, tn), lambda i, j, kk: (i, j))
    if epilogue is None:
        epilogue = lambda acc, *e: tuple(acc for _ in out_dtypes)
    return _matmul(
        a, b, dims=dims, grid=(m // tm, n // tn, k // tk), a_spec=a_spec, b_spec=b_spec, acc_shape=(tm, tn),
        out_shapes=[_sds((m, n), d) for d in out_dtypes], out_specs=[o_spec for _ in out_dtypes],
        epilogue=epilogue, extras=extras, extra_specs=[o_spec for _ in extras], name=name, pin=pin)


def _rmsnorm_fwd(h, g, out_dtype, name):
    t, d = h.shape
    tr = _tile(t, 256)

    def body(h_ref, g_ref, o_ref):
        x = h_ref[...]
        r = lax.rsqrt(jnp.mean(x * x, axis=-1, keepdims=True) + EPS)
        o_ref[...] = (x * r * g_ref[...]).astype(o_ref.dtype)

    return pl.pallas_call(
        body, grid=(t // tr,),
        in_specs=[pl.BlockSpec((tr, d), lambda i: (i, 0)), pl.BlockSpec((1, d), lambda i: (0, 0))],
        out_specs=pl.BlockSpec((tr, d), lambda i: (i, 0)),
        out_shape=_sds((t, d), out_dtype), compiler_params=_params("parallel"), name=name)(h, g)


def _rmsnorm_bwd(h, g, dy, resid, name):
    t, d = h.shape
    tr = _tile(t, 128)

    def body(h_ref, g_ref, dy_ref, res_ref, dh_ref, dhb_ref, dg_ref):
        x = h_ref[...]
        r = lax.rsqrt(jnp.mean(x * x, axis=-1, keepdims=True) + EPS)
        xn = x * r
        dyv = dy_ref[...]
        gdy = dyv * g_ref[...]
        dh = res_ref[...] + r * (gdy - xn * jnp.mean(gdy * xn, axis=-1, keepdims=True))
        dh_ref[...] = dh
        dhb_ref[...] = dh.astype(BF16)
        part = jnp.sum(dyv * xn, axis=0, keepdims=True)

        @pl.when(pl.program_id(0) == 0)
        def _():
            dg_ref[...] = part

        @pl.when(pl.program_id(0) > 0)
        def _():
            dg_ref[...] += part

    row = pl.BlockSpec((tr, d), lambda i: (i, 0))
    vec = pl.BlockSpec((1, d), lambda i: (0, 0))
    return pl.pallas_call(
        body, grid=(t // tr,), in_specs=[row, vec, row, row], out_specs=[row, row, vec],
        out_shape=[_sds((t, d), F32), _sds((t, d), BF16), _sds((1, d), F32)],
        compiler_params=_params("arbitrary"), name=name)(h, g, dy, resid)


def _final_bwd(h, g, tgt):
    t, d = h.shape
    tr = _tile(t, 128)

    def body(h_ref, g_ref, t_ref, loss_ref, dh_ref, dhb_ref, dg_ref):
        x = h_ref[...]
        r = lax.rsqrt(jnp.mean(x * x, axis=-1, keepdims=True) + EPS)
        xn = x * r
        gv = g_ref[...]
        e = xn * gv - t_ref[...]
        lpart = jnp.full((1, 128), 0.5 * jnp.sum(jnp.mean(e * e, axis=-1, keepdims=True)), F32)
        dyv = e * (1.0 / d)
        gdy = dyv * gv
        dh = r * (gdy - xn * jnp.mean(gdy * xn, axis=-1, keepdims=True))
        dh_ref[...] = dh
        dhb_ref[...] = dh.astype(BF16)
        part = jnp.sum(dyv * xn, axis=0, keepdims=True)

        @pl.when(pl.program_id(0) == 0)
        def _():
            dg_ref[...] = part
            loss_ref[...] = lpart

        @pl.when(pl.program_id(0) > 0)
        def _():
            dg_ref[...] += part
            loss_ref[...] += lpart

    row = pl.BlockSpec((tr, d), lambda i: (i, 0))
    vec = pl.BlockSpec((1, d), lambda i: (0, 0))
    return pl.pallas_call(
        body, grid=(t // tr,), in_specs=[row, vec, row],
        out_specs=[pl.BlockSpec((1, 128), lambda i: (0, 0)), row, row, vec],
        out_shape=[_sds((1, 128), F32), _sds((t, d), F32), _sds((t, d), BF16), _sds((1, d), F32)],
        compiler_params=_params("arbitrary"), name="final_loss_bwd")(h, g, tgt)


def _shift_rows(x, s, row):
    n = x.shape[0]
    y = pltpu.roll(x, (-s) % n, 0)
    return jnp.where((row + s >= 0) & (row + s < n), y, 0.0)


def _span_sum(x, start, length, row):
    if start >= 0:
        y, step = _shift_rows(x, start, row) if start else x, 1
    else:
        last = start + length - 1
        assert last <= 0
        y, step = _shift_rows(x, last, row) if last else x, -1
    n = 1
    while n < length:
        y = y + _shift_rows(y, step * n, row)
        n *= 2
    return y


def _pool_apply(x, seq, transpose, name):
    t, d = x.shape
    dg = d // N_GROUPS
    tc = _tile(dg, 256)
    nblk = dg // tc

    def body(x_ref, o_ref):
        grp = pl.program_id(1)
        row = lax.broadcasted_iota(jnp.int32, (seq, tc), 0)
        for gi, hw in enumerate(POOL_HALF):
            @pl.when(grp == gi)
            def _(hw=hw):
                xv = x_ref[...]
                cnt = (jnp.minimum(row + hw, seq) - jnp.maximum(row - hw, 0)).astype(F32)
                if not transpose:
                    w = _span_sum(xv, 0, hw, row) + _span_sum(xv, -hw, hw, row)
                    o_ref[...] = (w / cnt - xv).astype(o_ref.dtype)
                else:
                    u = xv / cnt
                    w = _span_sum(u, 1, hw, row) + _span_sum(u, -(hw - 1), hw, row)
                    o_ref[...] = (w - xv).astype(o_ref.dtype)

    spec = pl.BlockSpec((seq, tc), lambda b, g, j: (b, g * nblk + j))
    return pl.pallas_call(
        body, grid=(t // seq, N_GROUPS, nblk), in_specs=[spec], out_specs=spec,
        out_shape=_sds((t, d), F32 if transpose else BF16),
        compiler_params=_params("parallel", "parallel", "parallel"), name=name)(x)


def _pool_mm_fwd(dm, wp, x, scale):
    t, d = dm.shape
    dg = d // N_GROUPS
    rs = dg // N_SHARDS
    tm = _tile(t, 1024)
    o_spec = pl.BlockSpec((tm, dg), lambda i, j, kk: (i, j))
    return _matmul(
        dm, wp, dims=_NN, grid=(t // tm, N_GROUPS, N_SHARDS),
        a_spec=pl.BlockSpec((tm, rs), lambda i, j, kk: (i, j * N_SHARDS + kk)),
        b_spec=pl.BlockSpec((None, None, rs, dg), lambda i, j, kk: (kk, j, 0, 0)),
        acc_shape=(tm, dg), out_shapes=[_sds((t, d), F32), _sds((t, d), F32)], out_specs=[o_spec, o_spec],
        epilogue=lambda acc, xv, sc: (xv + acc * sc, acc),
        extras=(x, scale), extra_specs=[o_spec, pl.BlockSpec((1, dg), lambda i, j, kk: (0, j))], name="pool_mm_fwd")


def _pool_scale_bwd(dh, ypre, scale):
    t, d = dh.shape
    tr = _tile(t, 256)

    def body(dh_ref, y_ref, s_ref, o_ref, ds_ref):
        g = dh_ref[...]
        o_ref[...] = (g * s_ref[...]).astype(BF16)
        part = jnp.sum(g * y_ref[...], axis=0, keepdims=True)

        @pl.when(pl.program_id(0) == 0)
        def _():
            ds_ref[...] = part

        @pl.when(pl.program_id(0) > 0)
        def _():
            ds_ref[...] += part

    row = pl.BlockSpec((tr, d), lambda i: (i, 0))
    vec = pl.BlockSpec((1, d), lambda i: (0, 0))
    return pl.pallas_call(
        body, grid=(t // tr,), in_specs=[row, row, vec], out_specs=[row, vec],
        out_shape=[_sds((t, d), BF16), _sds((1, d), F32)], compiler_params=_params("arbitrary"),
        name="pool_scale_bwd")(dh, ypre, scale)


def _pool_mm_bwd_x(dys, wp, pin):
    t, d = dys.shape
    dg = d // N_GROUPS
    rs = dg // N_SHARDS
    tm = _tile(t, 1024)
    return _matmul(
        dys, wp, dims=_NT, grid=(t // tm, N_GROUPS * N_SHARDS, 1),
        a_spec=pl.BlockSpec((tm, dg), lambda i, j, kk: (i, j // N_SHARDS)),
        b_spec=pl.BlockSpec((None, None, rs, dg), lambda i, j, kk: (j % N_SHARDS, j // N_SHARDS, 0, 0)),
        acc_shape=(tm, rs), out_shapes=[_sds((t, d), F32)],
        out_specs=[pl.BlockSpec((tm, rs), lambda i, j, kk: (i, j))],
        epilogue=lambda acc: (acc,), name="pool_mm_bwd_x", pin=pin)[0]


def _pool_mm_bwd_w(dm, dys):
    t, d = dm.shape
    dg = d // N_GROUPS
    rs = dg // N_SHARDS
    tk = _tile(t, 4096)
    return _matmul(
        dm, dys, dims=_TN, grid=(N_GROUPS * N_SHARDS, 1, t // tk),
        a_spec=pl.BlockSpec((tk, rs), lambda i, j, kk: (kk, i)),
        b_spec=pl.BlockSpec((tk, dg), lambda i, j, kk: (kk, i // N_SHARDS)),
        acc_shape=(rs, dg), out_shapes=[_sds((N_SHARDS, N_GROUPS, rs, dg), BF16)],
        out_specs=[pl.BlockSpec((None, None, rs, dg), lambda i, j, kk: (i % N_SHARDS, i // N_SHARDS, 0, 0))],
        epilogue=lambda acc: (acc,), name="pool_mm_bwd_w")[0]


def _mlp_fwd(h, hn, w1g, w2_after):
    t, d = hn.shape
    f4 = w1g.shape[2]
    dff = N_SHARDS * f4
    tm, tn, tk = _tile(t, 1024), _tile(f4, 1024), _tile(d, 4096)
    nb = f4 // tn
    o_spec = pl.BlockSpec((tm, tn), lambda i, j, kk: (i, j))

    def act(acc):
        r = jnp.maximum(acc, 0.0)
        return r, r * r

    r, u = _matmul(
        hn, w1g, dims=_NN, grid=(t // tm, dff // tn, d // tk),
        a_spec=pl.BlockSpec((tm, tk), lambda i, j, kk: (i, kk)),
        b_spec=pl.BlockSpec((None, tk, tn), lambda i, j, kk: (j // nb, kk, j % nb)),
        acc_shape=(tm, tn), out_shapes=[_sds((t, dff), BF16), _sds((t, dff), BF16)], out_specs=[o_spec, o_spec],
        epilogue=act, name="mlp_up")
    w2 = w2_after(u)
    (out,) = _mm(u, w2, "nn", out_dtypes=[F32], epilogue=lambda acc, hv: (hv + acc,), extras=(h,), name="mlp_down",
                 tk=2048)
    return out, r, u, w2


def _mlp_bwd(dhb, hn, r, u, w1g, w2):
    t, d = hn.shape
    f4 = w1g.shape[2]
    dff = N_SHARDS * f4
    (da,) = _mm(dhb, w2, "nt", out_dtypes=[BF16], epilogue=lambda acc, rv: (acc * (2.0 * rv.astype(F32)),),
                extras=(r,), name="mlp_bwd_da")
    (dw2,) = _mm(u, dhb, "tn", out_dtypes=[BF16], name="mlp_bwd_dw2")
    tok = yield dw2.reshape(N_SHARDS, f4, d)
    tm, tn, tk = _tile(d, 1024), _tile(f4, 1024), _tile(t, 4096)
    nb = f4 // tn
    (dw1,) = _matmul(
        hn, da, dims=_TN, grid=(d // tm, dff // tn, t // tk),
        a_spec=pl.BlockSpec((tk, tm), lambda i, j, kk: (kk, i)),
        b_spec=pl.BlockSpec((tk, tn), lambda i, j, kk: (kk, j)),
        acc_shape=(tm, tn), out_shapes=[_sds((N_SHARDS, d, f4), BF16)],
        out_specs=[pl.BlockSpec((None, tm, tn), lambda i, j, kk: (j // nb, i, j % nb))],
        epilogue=lambda acc: (acc,), name="mlp_bwd_dw1", pin=tok)
    tok = yield dw1
    tm, tn, tk = _tile(t, 1024), _tile(d, 1024), _tile(f4, 4096)
    nbk = f4 // tk
    (dhn,) = _matmul(
        da, w1g, dims=_NT, grid=(t // tm, d // tn, dff // tk),
        a_spec=pl.BlockSpec((tm, tk), lambda i, j, kk: (i, kk)),
        b_spec=pl.BlockSpec((None, tn, tk), lambda i, j, kk: (kk // nbk, j, kk % nbk)),
        acc_shape=(tm, tn), out_shapes=[_sds((t, d), F32)],
        out_specs=[pl.BlockSpec((tm, tn), lambda i, j, kk: (i, j))], epilogue=lambda acc: (acc,), name="mlp_bwd_dhn",
        pin=tok)
    yield dhn


def _split3(x):
    a = x.astype(BF16)
    r1 = x - a.astype(F32)
    b = r1.astype(BF16)
    c = (r1 - b.astype(F32)).astype(BF16)
    return a, b, c


def _dot(a, b, dims):
    return lax.dot_general(a.astype(BF16), b.astype(BF16), dims, preferred_element_type=F32)


def _chunk_terms(q, k, g, rev, scale):
    c = q.shape[0]
    ri = lax.broadcasted_iota(jnp.int32, (c, c), 0)
    ci = lax.broadcasted_iota(jnp.int32, (c, c), 1)
    seen = (ci >= ri) if rev else (ci <= ri)
    tri = seen.astype(BF16)
    g1, g2, g3 = _split3(g)
    b = (lax.dot_general(tri, g1, _NN, preferred_element_type=F32)
         + lax.dot_general(tri, g2, _NN, preferred_element_type=F32)
         + lax.dot_general(tri, g3, _NN, preferred_element_type=F32))
    mid = c // 2 if rev else c // 2 - 1
    last = 0 if rev else c - 1
    rows = lax.broadcasted_iota(jnp.int32, b.shape, 0)
    b_mid = jnp.sum(jnp.where(rows == mid, b, 0.0), axis=0, keepdims=True)
    b_last = jnp.sum(jnp.where(rows == last, b, 0.0), axis=0, keepdims=True)
    qs = q * scale
    e1 = jnp.exp(b - b_mid)
    e2 = jnp.exp(b_mid - b)
    eb = jnp.exp(b)
    el = jnp.exp(b_last - b)
    return dict(seen=seen, tri=tri, mid=mid, last=last, e1=e1, e2=e2, eb=eb, el=el, a=jnp.exp(b_last),
                qe=qs * e1, ke=k * e2, qi=qs * eb, ks=k * el)


def _gla_fwd(pm, la, seq):
    t = pm.shape[0]
    d = pm.shape[1] // 3
    dk, dv = d // 2 // N_HEADS, d // N_HEADS
    nb, nc = t // seq, seq // CHUNK
    scale = dk ** -0.5
    kq, kk_, kv = 0, N_HEADS, (d // dv)

    def body(qf, kf, vf, gf, qb, kb, vb, gb, of_ref, ob_ref, stf_ref, stb_ref, sf, sb):
        n = pl.program_id(1)

        @pl.when(n == 0)
        def _():
            sf[...] = jnp.zeros_like(sf)
            sb[...] = jnp.zeros_like(sb)

        for (q_ref, k_ref, v_ref, g_ref, o_ref, st_ref, s_ref, rev) in (
                (qf, kf, vf, gf, of_ref, stf_ref, sf, False), (qb, kb, vb, gb, ob_ref, stb_ref, sb, True)):
            tm = _chunk_terms(q_ref[...], k_ref[...], g_ref[...], rev, scale)
            v = v_ref[...]
            st = s_ref[...]
            stb = st.astype(BF16)
            st_ref[...] = stb
            sc = jnp.where(tm["seen"], _dot(tm["qe"], tm["ke"], _NT), 0.0)
            o = _dot(sc, v, _NN) + lax.dot_general(tm["qi"].astype(BF16), stb, _NT, preferred_element_type=F32)
            o_ref[...] = o
            s_ref[...] = st * tm["a"] + _dot(v, tm["ks"], _TN)

    def row(bh, n, rev):
        return (bh // N_HEADS) * nc + (nc - 1 - n if rev else n)

    def specs(rev):
        return [
            pl.BlockSpec((CHUNK, dk), lambda bh, n: (row(bh, n, rev), kq + bh % N_HEADS)),
            pl.BlockSpec((CHUNK, dk), lambda bh, n: (row(bh, n, rev), kk_ + bh % N_HEADS)),
            pl.BlockSpec((CHUNK, dv), lambda bh, n: (row(bh, n, rev), kv + bh % N_HEADS)),
            pl.BlockSpec((CHUNK, dk), lambda bh, n: (row(bh, n, rev), (N_HEADS if rev else 0) + bh % N_HEADS)),
        ]

    def o_spec(rev):
        return pl.BlockSpec((CHUNK, dv), lambda bh, n: (row(bh, n, rev), bh % N_HEADS))

    def st_spec(rev):
        return pl.BlockSpec((None, None, dv, dk), lambda bh, n: (bh, nc - 1 - n if rev else n, 0, 0))

    sf_, sb_ = specs(False), specs(True)
    return pl.pallas_call(
        body, grid=(nb * N_HEADS, nc),
        in_specs=[*sf_, *sb_],
        out_specs=[o_spec(False), o_spec(True), st_spec(False), st_spec(True)],
        out_shape=[_sds((t, d), F32), _sds((t, d), F32),
                   _sds((nb * N_HEADS, nc, dv, dk), BF16), _sds((nb * N_HEADS, nc, dv, dk), BF16)],
        scratch_shapes=[pltpu.VMEM((dv, dk), F32), pltpu.VMEM((dv, dk), F32)],
        compiler_params=_params("parallel", "arbitrary"), name="gla_scan_fwd",
    )(pm, pm, pm, la, pm, pm, pm, la)


def _gla_bwd(pm, la, do, st_f, st_b, seq):
    t = pm.shape[0]
    d = pm.shape[1] // 3
    dk, dv = d // 2 // N_HEADS, d // N_HEADS
    kd = dk * N_HEADS
    nb, nc = t // seq, seq // CHUNK
    scale = dk ** -0.5
    kq, kk_, kv = 0, N_HEADS, (d // dv)

    def body(qf, kf, vf, gf, dof, stf, qb, kb, vb, gb, dob, stb_,
             dqf, dkf, dvf, dgf, dqb, dkb, dvb, dgb, dsf, dsb):
        n = pl.program_id(1)

        @pl.when(n == 0)
        def _():
            dsf[...] = jnp.zeros_like(dsf)
            dsb[...] = jnp.zeros_like(dsb)

        for (q_ref, k_ref, v_ref, g_ref, do_ref, st_ref, dq_ref, dk_ref, dv_ref, dg_ref, ds_ref, rev) in (
                (qf, kf, vf, gf, dof, stf, dqf, dkf, dvf, dgf, dsf, False),
                (qb, kb, vb, gb, dob, stb_, dqb, dkb, dvb, dgb, dsb, True)):
            tm = _chunk_terms(q_ref[...], k_ref[...], g_ref[...], rev, scale)
            v = v_ref[...]
            dov = do_ref[...]
            st = st_ref[...]
            ds = ds_ref[...]
            dsb16 = ds.astype(BF16)
            sc = jnp.where(tm["seen"], _dot(tm["qe"], tm["ke"], _NT), 0.0)
            dsc = jnp.where(tm["seen"], _dot(dov, v, _NT), 0.0)
            dv_ref[...] = _dot(sc, dov, _TN) + lax.dot_general(tm["ks"].astype(BF16), dsb16, _NT,
                                                               preferred_element_type=F32)
            dqe = _dot(dsc, tm["ke"], _NN)
            dke = _dot(dsc, tm["qe"], _TN)
            dqi = lax.dot_general(dov.astype(BF16), st, _NN, preferred_element_type=F32)
            dks = lax.dot_general(v.astype(BF16), dsb16, _NN, preferred_element_type=F32)
            da = jnp.sum(ds * st.astype(F32), axis=0, keepdims=True)
            dq_ref[...] = (dqe * tm["e1"] + dqi * tm["eb"]) * scale
            dk_ref[...] = dke * tm["e2"] + dks * tm["el"]
            t_q, t_k, t_s = dqe * tm["qe"], dke * tm["ke"], dks * tm["ks"]
            db = t_q - t_k + dqi * tm["qi"] - t_s
            mid_row = jnp.sum(t_k - t_q, axis=0, keepdims=True)
            last_row = jnp.sum(t_s, axis=0, keepdims=True) + da * tm["a"]
            ridx = lax.broadcasted_iota(jnp.int32, db.shape, 0)
            db = db + jnp.where(ridx == tm["mid"], mid_row, 0.0) + jnp.where(ridx == tm["last"], last_row, 0.0)
            d1, d2, d3 = _split3(db)
            dg_ref[...] = (lax.dot_general(tm["tri"], d1, _TN, preferred_element_type=F32)
                           + lax.dot_general(tm["tri"], d2, _TN, preferred_element_type=F32)
                           + lax.dot_general(tm["tri"], d3, _TN, preferred_element_type=F32))
            ds_ref[...] = ds * tm["a"] + _dot(dov, tm["qi"], _TN)

    def row(bh, n, rev):
        return (bh // N_HEADS) * nc + (n if rev else nc - 1 - n)

    def specs(rev):
        return [
            pl.BlockSpec((CHUNK, dk), lambda bh, n: (row(bh, n, rev), kq + bh % N_HEADS)),
            pl.BlockSpec((CHUNK, dk), lambda bh, n: (row(bh, n, rev), kk_ + bh % N_HEADS)),
            pl.BlockSpec((CHUNK, dv), lambda bh, n: (row(bh, n, rev), kv + bh % N_HEADS)),
            pl.BlockSpec((CHUNK, dk), lambda bh, n: (row(bh, n, rev), (N_HEADS if rev else 0) + bh % N_HEADS)),
            pl.BlockSpec((CHUNK, dv), lambda bh, n: (row(bh, n, rev), bh % N_HEADS)),
            pl.BlockSpec((None, None, dv, dk), lambda bh, n: (bh, n if rev else nc - 1 - n, 0, 0)),
        ]

    def outs(rev):
        return [
            pl.BlockSpec((CHUNK, dk), lambda bh, n: (row(bh, n, rev), bh % N_HEADS)),
            pl.BlockSpec((CHUNK, dk), lambda bh, n: (row(bh, n, rev), bh % N_HEADS)),
            pl.BlockSpec((CHUNK, dv), lambda bh, n: (row(bh, n, rev), bh % N_HEADS)),
            pl.BlockSpec((CHUNK, dk), lambda bh, n: (row(bh, n, rev), bh % N_HEADS)),
        ]

    of_, ob_ = outs(False), outs(True)
    res = pl.pallas_call(
        body, grid=(nb * N_HEADS, nc),
        in_specs=[*specs(False), *specs(True)],
        out_specs=[*of_, *ob_],
        out_shape=[_sds((t, kd), F32), _sds((t, kd), F32), _sds((t, d), F32), _sds((t, kd), F32),
                   _sds((t, kd), F32), _sds((t, kd), F32), _sds((t, d), F32), _sds((t, kd), F32)],
        scratch_shapes=[pltpu.VMEM((dv, dk), F32), pltpu.VMEM((dv, dk), F32)],
        compiler_params=_params("parallel", "arbitrary"), name="gla_scan_bwd",
    )(pm, pm, pm, la, do, st_f, pm, pm, pm, la, do, st_b)
    return res


def _sigmoid(x):
    return 1.0 / (1.0 + jnp.exp(-x))


def _gla_post_fwd(o_f, o_b, pm, gn):
    t, d = o_f.shape
    dv = d // N_HEADS
    tr = _tile(t, 512)
    gate_blk = 2 * d // dv

    def body(of_ref, ob_ref, gt_ref, gn_ref, out_ref):
        o = of_ref[...] + ob_ref[...]
        n = o * lax.rsqrt(jnp.mean(o * o, axis=-1, keepdims=True) + EPS) * gn_ref[...]
        gt = gt_ref[...]
        out_ref[...] = (n * (gt * _sigmoid(gt))).astype(BF16)

    blk = pl.BlockSpec((tr, dv), lambda i, h: (i, h))
    return pl.pallas_call(
        body, grid=(t // tr, N_HEADS),
        in_specs=[blk, blk, pl.BlockSpec((tr, dv), lambda i, h: (i, gate_blk + h)),
                  pl.BlockSpec((1, dv), lambda i, h: (0, 0))],
        out_specs=blk, out_shape=_sds((t, d), BF16), compiler_params=_params("parallel", "parallel"),
        name="gla_post_fwd")(o_f, o_b, pm, gn)


def _gla_post_bwd(dog, o_f, o_b, pm, gn):
    t, d = o_f.shape
    dv = d // N_HEADS
    tr = _tile(t, 256)
    gate_blk = 2 * d // dv

    def body(dog_ref, of_ref, ob_ref, gt_ref, gn_ref, do_ref, dgt_ref, dgn_ref):
        o = of_ref[...] + ob_ref[...]
        rr = lax.rsqrt(jnp.mean(o * o, axis=-1, keepdims=True) + EPS)
        on = o * rr
        gnv = gn_ref[...]
        gt = gt_ref[...]
        sg = _sigmoid(gt)
        sl = gt * sg
        dg_out = dog_ref[...]
        dn = dg_out * sl
        dgt_ref[...] = dg_out * (on * gnv) * (sg * (1.0 + gt * (1.0 - sg)))
        gdn = dn * gnv
        do_ref[...] = rr * (gdn - on * jnp.mean(gdn * on, axis=-1, keepdims=True))
        part = jnp.sum(dn * on, axis=0, keepdims=True)
        first = (pl.program_id(0) == 0) & (pl.program_id(1) == 0)

        @pl.when(first)
        def _():
            dgn_ref[...] = part

        @pl.when(jnp.logical_not(first))
        def _():
            dgn_ref[...] += part

    blk = pl.BlockSpec((tr, dv), lambda i, h: (i, h))
    vec = pl.BlockSpec((1, dv), lambda i, h: (0, 0))
    return pl.pallas_call(
        body, grid=(t // tr, N_HEADS),
        in_specs=[blk, blk, blk, pl.BlockSpec((tr, dv), lambda i, h: (i, gate_blk + h)), vec],
        out_specs=[blk, blk, vec], out_shape=[_sds((t, d), F32), _sds((t, d), F32), _sds((1, dv), F32)],
        compiler_params=_params("arbitrary", "arbitrary"), name="gla_post_bwd")(dog, o_f, o_b, pm, gn)


def _gla_dp(dq_f, dq_b, dk_f, dk_b, dv_f, dv_b, dgate):
    t, d = dv_f.shape
    kd = dq_f.shape[1]
    tr = _tile(t, 128)

    def body(a1, a2, b1, b2, c1, c2, g, o_ref):
        o_ref[:, 0:kd] = (a1[...] + a2[...]).astype(BF16)
        o_ref[:, kd:2 * kd] = (b1[...] + b2[...]).astype(BF16)
        o_ref[:, 2 * kd:2 * kd + d] = (c1[...] + c2[...]).astype(BF16)
        o_ref[:, 2 * kd + d:] = g[...].astype(BF16)

    sk = pl.BlockSpec((tr, kd), lambda i: (i, 0))
    sd = pl.BlockSpec((tr, d), lambda i: (i, 0))
    return pl.pallas_call(
        body, grid=(t // tr,), in_specs=[sk, sk, sk, sk, sd, sd, sd],
        out_specs=pl.BlockSpec((tr, 3 * d), lambda i: (i, 0)), out_shape=_sds((t, 3 * d), BF16),
        compiler_params=_params("parallel"), name="gla_dp")(dq_f, dq_b, dk_f, dk_b, dv_f, dv_b, dgate)


def _adamw(w, g, m, v, name):
    r, c = w.shape
    tr = _tile(r, 128)
    bc1 = 1.0 - ADAM_B1 ** ADAM_STEP
    bc2 = 1.0 - ADAM_B2 ** ADAM_STEP

    def body(w_ref, g_ref, m_ref, v_ref, d_ref, nm_ref, nv_ref):
        gv = g_ref[...]
        mn = ADAM_B1 * m_ref[...] + (1.0 - ADAM_B1) * gv
        vn = ADAM_B2 * v_ref[...] + (1.0 - ADAM_B2) * (gv * gv)
        m_hat = mn / bc1
        v_hat = vn / bc2
        d_ref[...] = -ADAM_LR * (m_hat / (jnp.sqrt(v_hat) + ADAM_EPS) + ADAM_WD * w_ref[...])
        nm_ref[...] = mn
        nv_ref[...] = vn

    blk = pl.BlockSpec((tr, c), lambda i: (i, 0))
    return pl.pallas_call(
        body, grid=(r // tr,), in_specs=[blk] * 4, out_specs=[blk] * 3, out_shape=[_sds((r, c), F32)] * 3,
        compiler_params=_params("parallel"), name=name)(w, g, m, v)


def _adamw_half(w, g, m, v, layer, half, prev, pin, name, transposed=False):
    bc1 = 1.0 - ADAM_B1 ** ADAM_STEP
    bc2 = 1.0 - ADAM_B2 ** ADAM_STEP
    n_skip = 1 + (0 if prev is None else 4)
    if transposed:
        nl, c, r = w.shape
        tr = _tile(r // 2, 128)
        nblk = r // 2 // tr
        lay = pl.BlockSpec((None, c, tr), lambda i, h_ref: (layer, 0, h_ref[0] * nblk + i))
        g_spec = pl.BlockSpec((c, tr), lambda i, h_ref: (0, i))
    else:
        nl, r, c = w.shape
        tr = _tile(r // 2, 128)
        nblk = r // 2 // tr
        lay = pl.BlockSpec((None, tr, c), lambda i, h_ref: (layer, h_ref[0] * nblk + i, 0))
        g_spec = pl.BlockSpec((tr, c), lambda i, h_ref: (i, 0))

    def body(h_ref, w_ref, g_ref, m_ref, v_ref, *rest):
        go_ref, d_ref, nm_ref, nv_ref = rest[n_skip:]
        gv = g_ref[...]
        mn = ADAM_B1 * m_ref[...] + (1.0 - ADAM_B1) * gv
        vn = ADAM_B2 * v_ref[...] + (1.0 - ADAM_B2) * (gv * gv)
        m_hat = mn / bc1
        v_hat = vn / bc2
        d_ref[...] = -ADAM_LR * (m_hat / (jnp.sqrt(v_hat) + ADAM_EPS) + ADAM_WD * w_ref[...])
        nm_ref[...] = mn
        nv_ref[...] = vn
        go_ref[...] = gv

    prev = tuple(prev or ())
    grid_spec = pltpu.PrefetchScalarGridSpec(
        num_scalar_prefetch=1, grid=(nblk,),
        in_specs=[lay, g_spec, lay, lay, pl.BlockSpec(TOKEN, lambda i, h_ref: (0, 0))] + [ANY] * len(prev),
        out_specs=[lay] * 4)
    return pl.pallas_call(
        body, grid_spec=grid_spec, out_shape=[_sds(w.shape, F32)] * 4,
        input_output_aliases={6 + k: k for k in range(len(prev))},
        compiler_params=_params("parallel"), name=name)(half, w, g, m, v, pin, *prev)


def _sum8(parts, own, me):
    _, n, _ = parts.shape

    def body(me_ref, p_ref, own_ref, o_ref):
        acc = None
        for i in range(8):
            term = jnp.where(me_ref[0] == i, own_ref[...], p_ref[i])
            acc = term if acc is None else acc + term
        o_ref[...] = acc

    grid_spec = pltpu.PrefetchScalarGridSpec(
        num_scalar_prefetch=1, grid=(1,),
        in_specs=[pl.BlockSpec((8, n, 128), lambda i, me_ref: (0, 0, 0)),
                  pl.BlockSpec((n, 128), lambda i, me_ref: (0, 0))],
        out_specs=pl.BlockSpec((n, 128), lambda i, me_ref: (0, 0)))
    return pl.pallas_call(body, grid_spec=grid_spec, out_shape=_sds((n, 128), F32), name="sum8")(me, parts, own)


def _pair_sum(ga, recv, c):
    _, _, rh, cols = ga.shape
    tr = _tile(rh, 256)

    def body(c_ref, a_ref, b_ref, o_ref):
        o_ref[...] = (a_ref[...].astype(F32) + b_ref[...].astype(F32)).astype(BF16)

    grid_spec = pltpu.PrefetchScalarGridSpec(
        num_scalar_prefetch=1, grid=(N_SHARDS, rh // tr),
        in_specs=[pl.BlockSpec((None, None, tr, cols), lambda s, i, c_ref: (s, c_ref[0], i, 0)),
                  pl.BlockSpec((None, tr, cols), lambda s, i, c_ref: (s, i, 0))],
        out_specs=pl.BlockSpec((None, tr, cols), lambda s, i, c_ref: (s, i, 0)))
    return pl.pallas_call(
        body, grid_spec=grid_spec, out_shape=_sds((N_SHARDS, rh, cols), BF16),
        compiler_params=_params("parallel", "parallel"), name="grad_pair_sum")(c, ga, recv)


def _quad_sum(pair, recv, s_me):
    _, rh, cols = pair.shape
    tr = _tile(rh, 256)

    def body(s_ref, p_ref, r1_ref, r2_ref, r3_ref, o_ref):
        o_ref[...] = ((p_ref[...].astype(F32) + r1_ref[...].astype(F32)) + r2_ref[...].astype(F32)) \
            + r3_ref[...].astype(F32)

    def blk(off):
        return pl.BlockSpec((None, tr, cols), lambda i, s_ref: ((s_ref[0] + off) % N_SHARDS, i, 0))

    grid_spec = pltpu.PrefetchScalarGridSpec(
        num_scalar_prefetch=1, grid=(rh // tr,), in_specs=[blk(0), blk(1), blk(2), blk(3)],
        out_specs=pl.BlockSpec((tr, cols), lambda i, s_ref: (i, 0)))
    return pl.pallas_call(
        body, grid_spec=grid_spec, out_shape=_sds((rh, cols), F32),
        compiler_params=_params("parallel"), name="grad_quad_sum")(s_me, pair, recv, recv, recv)


def _fill_own(w, layer, s_me):
    _, rows, cols = w.shape
    rh = rows // 2
    tr = _tile(rh, 256)
    nblk = rh // tr

    def body(s_ref, w_ref, o_ref):
        o_ref[...] = w_ref[...].astype(BF16)

    grid_spec = pltpu.PrefetchScalarGridSpec(
        num_scalar_prefetch=1, grid=(2, nblk),
        in_specs=[pl.BlockSpec((None, tr, cols), lambda h, i, s_ref: (layer, h * nblk + i, 0))],
        out_specs=pl.BlockSpec((None, None, tr, cols), lambda h, i, s_ref: (s_ref[0], h, i, 0)))
    return pl.pallas_call(
        body, grid_spec=grid_spec, out_shape=_sds((N_SHARDS, 2, rh, cols), BF16),
        compiler_params=_params("parallel", "parallel"), name="weight_fill_own")(s_me, w)


def _coords():
    return lax.axis_index("x"), lax.axis_index("y"), lax.axis_index("c")


def _other_chips(x, y):
    return [(x, 1 - y), (1 - x, y), (1 - x, 1 - y)]


def _bcast8(buf):
    n = buf.shape[0]

    def body(b_ref, o_ref, send_sems, recv_sems):
        x, y, c = _coords()
        me = 4 * x + 2 * y + c
        o_ref[me] = b_ref[...]
        copies = []
        for k in range(1, 8):
            peer = (x ^ (k >> 2), y ^ ((k >> 1) & 1), c ^ (k & 1))
            copies.append(pltpu.make_async_remote_copy(
                src_ref=b_ref, dst_ref=o_ref.at[me], send_sem=send_sems.at[k - 1], recv_sem=recv_sems.at[k - 1],
                device_id=peer, device_id_type=MESH))
        for cp in copies:
            cp.start()
        for k in range(1, 8):
            pltpu.make_async_remote_copy(
                src_ref=b_ref, dst_ref=o_ref.at[me ^ k], send_sem=send_sems.at[k - 1], recv_sem=recv_sems.at[k - 1],
                device_id=(x, y, c), device_id_type=MESH).wait_recv()
        for cp in copies:
            cp.wait_send()

    return pl.pallas_call(
        body, out_shape=_sds((8, n, 128), F32),
        in_specs=[pl.BlockSpec(memory_space=pltpu.VMEM)], out_specs=pl.BlockSpec(memory_space=pltpu.VMEM),
        scratch_shapes=[pltpu.SemaphoreType.DMA((7,)), pltpu.SemaphoreType.DMA((7,))],
        compiler_params=pltpu.CompilerParams(vmem_limit_bytes=VMEM_LIMIT), name="bcast8")(buf)


HBM = pl.BlockSpec(memory_space=pltpu.HBM)
SEM = pl.BlockSpec(memory_space=pltpu.SEMAPHORE)
SIDE = pltpu.SideEffectType.DATAFLOW_SIDE_EFFECTING
TOKEN = (8, 128)


def _in_hbm(a):
    return pltpu.with_memory_space_constraint(a, pltpu.HBM)


def _sibling_start(src, pieces, land_shape, after, name):
    n = len(pieces(None, None, None))

    def body(s_ref, land_ref, after_ref, send, recv, land_thru, token):
        x, y, c = _coords()
        for k, (a, b) in enumerate(pieces(s_ref, land_ref, c)):
            pltpu.make_async_remote_copy(
                src_ref=a, dst_ref=b, send_sem=send.at[k], recv_sem=recv.at[k], device_id=(x, y, 1 - c),
                device_id_type=MESH).start()
        token[...] = jnp.zeros_like(token)

    return pl.pallas_call(
        body, name=name,
        out_shape=(pltpu.SemaphoreType.DMA((n,)), pltpu.SemaphoreType.DMA((n,)), pltpu.HBM(land_shape, src.dtype),
                   _sds(TOKEN, F32)),
        in_specs=(HBM, HBM, ANY), out_specs=(SEM, SEM, HBM, pl.BlockSpec(memory_space=pltpu.VMEM)),
        input_output_aliases={1: 2}, compiler_params=pltpu.CompilerParams(has_side_effects=SIDE),
    )(_in_hbm(src), _in_hbm(lax.empty(land_shape, src.dtype)), after)


def _sibling_wait(send, recv, src, land, pieces, after, name):
    def body(s_ref, land_ref, send, recv, after_ref, land_out):
        x, y, c = _coords()
        for k, (a, b) in enumerate(pieces(s_ref, land_ref, c)):
            cp = pltpu.make_async_remote_copy(
                src_ref=a, dst_ref=b, send_sem=send.at[k], recv_sem=recv.at[k], device_id=(x, y, 1 - c),
                device_id_type=MESH)
            cp.wait_send()
            cp.wait_recv()

    return pl.pallas_call(
        body, name=name, out_shape=pltpu.HBM(land.shape, land.dtype), in_specs=(HBM, HBM, SEM, SEM, ANY),
        out_specs=HBM, input_output_aliases={1: 0}, compiler_params=pltpu.CompilerParams(has_side_effects=SIDE),
    )(src, land, send, recv, after)


def _swap_pieces(g_ref, land_ref, c):
    if g_ref is None:
        return [None] * N_SHARDS
    return [(g_ref.at[s, 1 - c], land_ref.at[s]) for s in range(N_SHARDS)]


def _whole_piece(r_ref, land_ref, c):
    return [(r_ref, land_ref)]


def _bcast_copies(b_ref, land_ref, send, recv):
    x, y, c = _coords()
    me = 4 * x + 2 * y + c
    return [pltpu.make_async_remote_copy(
        src_ref=b_ref, dst_ref=land_ref.at[me], send_sem=send.at[k - 1], recv_sem=recv.at[k - 1],
        device_id=(x ^ (k >> 2), y ^ ((k >> 1) & 1), c ^ (k & 1)), device_id_type=MESH) for k in range(1, 8)]


def _bcast_start(buf, after, name):
    def body(b_ref, land_ref, after_ref, send, recv, land_thru, token):
        for cp in _bcast_copies(b_ref, land_ref, send, recv):
            cp.start()
        token[...] = jnp.zeros_like(token)

    shape = (8,) + buf.shape
    return pl.pallas_call(
        body, name=name,
        out_shape=(pltpu.SemaphoreType.DMA((7,)), pltpu.SemaphoreType.DMA((7,)), pltpu.HBM(shape, buf.dtype),
                   _sds(TOKEN, F32)),
        in_specs=(HBM, HBM, ANY), out_specs=(SEM, SEM, HBM, pl.BlockSpec(memory_space=pltpu.VMEM)),
        input_output_aliases={1: 2}, compiler_params=pltpu.CompilerParams(has_side_effects=SIDE),
    )(_in_hbm(buf), _in_hbm(lax.empty(shape, buf.dtype)), after)


def _bcast_wait(send, recv, buf, land, after, name):
    def body(b_ref, land_ref, send, recv, after_ref, land_out):
        for cp in _bcast_copies(b_ref, land_ref, send, recv):
            cp.wait_send()
            cp.wait_recv()

    return pl.pallas_call(
        body, name=name, out_shape=pltpu.HBM(land.shape, land.dtype), in_specs=(HBM, HBM, SEM, SEM, ANY),
        out_specs=HBM, input_output_aliases={1: 0}, compiler_params=pltpu.CompilerParams(has_side_effects=SIDE),
    )(buf, land, send, recv, after)


def _scatter_start(part, name):
    def body(p_ref, land_ref, send, recv, p_thru, land_thru, token):
        x, y, c = _coords()
        s_me = 2 * x + y
        for j, (px, py) in enumerate(_other_chips(x, y)):
            pltpu.make_async_remote_copy(
                src_ref=p_ref.at[2 * px + py], dst_ref=land_ref.at[s_me], send_sem=send.at[j], recv_sem=recv.at[j],
                device_id=(px, py, c), device_id_type=MESH).start()
        token[...] = jnp.zeros_like(token)

    buf = pltpu.HBM(part.shape, part.dtype)
    return pl.pallas_call(
        body, name=name,
        out_shape=(pltpu.SemaphoreType.DMA((3,)), pltpu.SemaphoreType.DMA((3,)), buf, buf, _sds(TOKEN, F32)),
        in_specs=(HBM, HBM), out_specs=(SEM, SEM, HBM, HBM, pl.BlockSpec(memory_space=pltpu.VMEM)),
        input_output_aliases={0: 2, 1: 3}, compiler_params=pltpu.CompilerParams(has_side_effects=SIDE),
    )(_in_hbm(part), _in_hbm(lax.empty(part.shape, part.dtype)))


def _scatter_wait(send, recv, part, land, after, name):
    def body(p_ref, land_ref, send, recv, after_ref, p_out, land_out):
        x, y, c = _coords()
        for j, (px, py) in enumerate(_other_chips(x, y)):
            cp = pltpu.make_async_remote_copy(
                src_ref=p_ref.at[2 * px + py], dst_ref=land_ref.at[2 * px + py], send_sem=send.at[j],
                recv_sem=recv.at[j], device_id=(px, py, c), device_id_type=MESH)
            cp.wait_send()
            cp.wait_recv()

    buf = pltpu.HBM(part.shape, part.dtype)
    return pl.pallas_call(
        body, name=name, out_shape=(buf, buf), in_specs=(HBM, HBM, SEM, SEM, ANY), out_specs=(HBM, HBM),
        input_output_aliases={0: 0, 1: 1}, compiler_params=pltpu.CompilerParams(has_side_effects=SIDE),
    )(part, land, send, recv, after)


def _gather_start(bufs, after):
    n = len(bufs)

    def body(*refs):
        ins = refs[:n]
        sems = refs[n + 1:3 * n + 1]
        token = refs[4 * n + 1]
        x, y, c = _coords()
        s_me = 2 * x + y
        for i in range(n):
            for j, (px, py) in enumerate(_other_chips(x, y)):
                pltpu.make_async_remote_copy(
                    src_ref=ins[i].at[s_me, c], dst_ref=ins[i].at[s_me, c], send_sem=sems[2 * i].at[j],
                    recv_sem=sems[2 * i + 1].at[j], device_id=(px, py, c), device_id_type=MESH).start()
        token[...] = jnp.zeros_like(token)

    res = pl.pallas_call(
        body, name="gather_start",
        out_shape=(*[pltpu.SemaphoreType.DMA((3,))] * (2 * n), *[pltpu.HBM(b.shape, b.dtype) for b in bufs],
                   _sds(TOKEN, F32)),
        in_specs=(*[HBM] * n, ANY), out_specs=(*[SEM] * (2 * n), *[HBM] * n, pl.BlockSpec(memory_space=pltpu.VMEM)),
        input_output_aliases={i: 2 * n + i for i in range(n)},
        compiler_params=pltpu.CompilerParams(has_side_effects=SIDE),
    )(*[_in_hbm(b) for b in bufs], after)
    return [(res[2 * i], res[2 * i + 1]) for i in range(n)], list(res[2 * n:3 * n]), res[3 * n]


def _gather_wait(send, recv, buf, after, name):
    def body(b_ref, send, recv, after_ref, b_out):
        x, y, c = _coords()
        s_me = 2 * x + y
        for j, (px, py) in enumerate(_other_chips(x, y)):
            cp = pltpu.make_async_remote_copy(
                src_ref=b_ref.at[s_me, c], dst_ref=b_ref.at[2 * px + py, c], send_sem=send.at[j], recv_sem=recv.at[j],
                device_id=(px, py, c), device_id_type=MESH)
            cp.wait_send()
            cp.wait_recv()

    return pl.pallas_call(
        body, name=name, out_shape=pltpu.HBM(buf.shape, buf.dtype), in_specs=(HBM, SEM, SEM, ANY), out_specs=HBM,
        input_output_aliases={0: 0}, compiler_params=pltpu.CompilerParams(has_side_effects=SIDE),
    )(buf, send, recv, after)


def _pass_on_halves(bufs):
    n = len(bufs)

    def body(*refs):
        ins, outs = refs[:n], refs[n:2 * n]
        send, recv = refs[2 * n:]
        x, y, c = _coords()
        copies = []
        for i in range(n):
            for j, (px, py) in enumerate(_other_chips(x, y)):
                cp = pltpu.make_async_remote_copy(
                    src_ref=ins[i].at[2 * px + py, c], dst_ref=outs[i].at[2 * px + py, c], send_sem=send.at[i, j],
                    recv_sem=recv.at[i, j], device_id=(x, y, 1 - c), device_id_type=MESH)
                cp.start()
                copies.append(cp)
        for i in range(n):
            for j, (px, py) in enumerate(_other_chips(x, y)):
                other = outs[i].at[2 * px + py, 1 - c]
                pltpu.make_async_remote_copy(
                    src_ref=other, dst_ref=other, send_sem=send.at[i, j], recv_sem=recv.at[i, j],
                    device_id=(x, y, c), device_id_type=MESH).wait_recv()
        for cp in copies:
            cp.wait_send()

    return pl.pallas_call(
        body, out_shape=[_sds(b.shape, b.dtype) for b in bufs],
        in_specs=[ANY] * n, out_specs=[ANY] * n, input_output_aliases={i: i for i in range(n)},
        scratch_shapes=[pltpu.SemaphoreType.DMA((n, 3)), pltpu.SemaphoreType.DMA((n, 3))],
        name="gather_pass_on")(*bufs)


def _to_rows(vec):
    n = -(-vec.shape[0] // 1024) * 1024
    return jnp.pad(vec, (0, n - vec.shape[0])).reshape(-1, 128)


def kernel(x, norm_mix, norm_mlp, norm_final, pool_w, pool_scale, gla_w_in, gla_w_up_f, gla_b_up_f, gla_w_up_b, gla_b_up_b, gla_g_norm, gla_w_out, mlp_w_in, mlp_w_out, loss_target, m_norm_mix, m_norm_mlp, m_norm_final, m_pool_w, m_pool_scale, m_gla_w_in, m_gla_w_up_f, m_gla_b_up_f, m_gla_w_up_b, m_gla_b_up_b, m_gla_g_norm, m_gla_w_out, m_mlp_w_in, m_mlp_w_out, v_norm_mix, v_norm_mlp, v_norm_final, v_pool_w, v_pool_scale, v_gla_w_in, v_gla_w_up_f, v_gla_b_up_f, v_gla_w_up_b, v_gla_b_up_b, v_gla_g_norm, v_gla_w_out, v_mlp_w_in, v_mlp_w_out):
    nb, seq, d = x.shape
    t = nb * seq
    dg = d // N_GROUPS
    kd = d // 2
    dv = d // N_HEADS
    pw = gla_w_in.shape[2]
    f4 = mlp_w_in.shape[2]
    dff = N_SHARDS * f4
    cx, cy, cc = _coords()
    s_me = 2 * cx + cy
    c_arr = jnp.reshape(cc, (1,)).astype(jnp.int32)
    s_arr = jnp.reshape(s_me, (1,)).astype(jnp.int32)

    xf = x.reshape(t, d)
    tgt = loss_target.reshape(t, d)

    ks = kd // N_SHARDS
    small = jnp.concatenate([gla_w_up_f[0].reshape(-1), gla_w_up_b[0].reshape(-1), gla_b_up_f[0], gla_b_up_b[0],
                             gla_g_norm[0]])
    small_all = _bcast8(_to_rows(small))
    pool_rows = N_GROUPS * (dg // N_SHARDS)
    gsems, gbufs, gtok = _gather_start([
        _fill_own(pool_w.reshape(1, pool_rows, dg), 0, s_arr), _fill_own(mlp_w_in, 0, s_arr),
        _fill_own(mlp_w_out, 0, s_arr), _fill_own(gla_w_in, 0, s_arr), _fill_own(gla_w_out, 0, s_arr),
        _fill_own(mlp_w_in, 1, s_arr), _fill_own(mlp_w_out, 1, s_arr)], small_all)
    small_all = small_all[::2].reshape(N_SHARDS, -1)

    def weight(i, after, name):
        buf = _gather_wait(gsems[i][0], gsems[i][1], gbufs[i], after, "gather_wait_" + name)
        return _pass_on_halves([buf])[0]

    def pinned(vec, tok):
        return vec + tok[0:1, 0:1]

    o = 0
    wuf = jnp.transpose(small_all[:, o:o + GATE_RANK * ks].reshape(N_SHARDS, GATE_RANK, ks), (1, 0, 2)).reshape(GATE_RANK, kd)
    o += GATE_RANK * ks
    wub = jnp.transpose(small_all[:, o:o + GATE_RANK * ks].reshape(N_SHARDS, GATE_RANK, ks), (1, 0, 2)).reshape(GATE_RANK, kd)
    o += GATE_RANK * ks
    buf = small_all[:, o:o + ks].reshape(1, kd)
    o += ks
    bub = small_all[:, o:o + ks].reshape(1, kd)
    o += ks
    gn = small_all[:, o:o + dv // N_SHARDS].reshape(1, dv)
    w_up = jnp.zeros((R_PAD, 2 * kd), F32).at[:GATE_RANK, :kd].set(wuf).at[GATE_RANK:2 * GATE_RANK, kd:].set(wub)
    w_up = w_up.astype(BF16)
    b_up = jnp.concatenate([buf, bub], axis=1)

    hn0 = _rmsnorm_fwd(xf, pinned(norm_mix[0:1], gtok), F32, "norm_mix0")
    dm = _pool_apply(hn0, seq, False, "pool_diff")
    wp = weight(0, dm, "pool_w").reshape(N_SHARDS, N_GROUPS, dg // N_SHARDS, dg)
    h1, ypre = _pool_mm_fwd(dm, wp, xf, pool_scale)
    hn1 = _rmsnorm_fwd(h1, norm_mlp[0:1], BF16, "norm_mlp0")
    w1g = [weight(1, hn1, "mlp_w_in0").reshape(N_SHARDS, d, f4), None]
    w2g = [None, None]
    h2, r0, u0, w2g[0] = _mlp_fwd(h1, hn1, w1g[0], lambda u: weight(2, u, "mlp_w_out0").reshape(dff, d))
    hn2 = _rmsnorm_fwd(h2, norm_mix[1:2], BF16, "norm_mix1")
    win = jnp.transpose(weight(3, hn2, "gla_w_in").reshape(N_SHARDS, d, pw), (1, 0, 2)).reshape(d, N_SHARDS * pw)
    w_main = win[:, :3 * d]
    w_r = jnp.pad(win[:, 3 * d:], ((0, 0), (0, R_PAD - 2 * GATE_RANK)))
    (pm,) = _mm(hn2, w_main, "nn", out_dtypes=[F32], name="gla_proj")
    (pr,) = _mm(hn2, w_r, "nn", out_dtypes=[BF16], name="gla_proj_r")
    pr = pr.at[:, ONES_COL].set(1.0)

    def log_decay(acc, bv):
        z = acc + bv
        return ((jnp.minimum(z, 0.0) - jnp.log(1.0 + jnp.exp(-jnp.abs(z)))) / GATE_TAU,)

    tm_, tn_ = _tile(t, 1024), _tile(kd, 1024)
    (la,) = _matmul(
        pr, w_up, dims=_NN, grid=(t // tm_, 2 * kd // tn_, 1),
        a_spec=pl.BlockSpec((tm_, R_PAD), lambda i, j, kk: (i, 0)),
        b_spec=pl.BlockSpec((R_PAD, tn_), lambda i, j, kk: (0, j)), acc_shape=(tm_, tn_),
        out_shapes=[_sds((t, 2 * kd), F32)], out_specs=[pl.BlockSpec((tm_, tn_), lambda i, j, kk: (i, j))],
        epilogue=log_decay, extras=(b_up,), extra_specs=[pl.BlockSpec((1, tn_), lambda i, j, kk: (0, j))],
        name="gla_gate_fwd")
    o_f, o_b, st_f, st_b = _gla_fwd(pm, la, seq)
    og = _gla_post_fwd(o_f, o_b, pm, gn)
    wout = weight(4, og, "gla_w_out").reshape(d, d)
    (h3,) = _mm(og, wout, "nn", out_dtypes=[F32], epilogue=lambda acc, hv: (hv + acc,), extras=(h2,), name="gla_out")
    hn3 = _rmsnorm_fwd(h3, norm_mlp[1:2], BF16, "norm_mlp1")
    w1g[1] = weight(5, hn3, "mlp_w_in1").reshape(N_SHARDS, d, f4)
    h4, r1, u1, w2g[1] = _mlp_fwd(h3, hn3, w1g[1], lambda u: weight(6, u, "mlp_w_out1").reshape(dff, d))

    chains = []
    last_tok = [gtok]

    def swap_begin(g, name):
        g5 = g.reshape(N_SHARDS, 2, g.shape[1] // 2, g.shape[2])
        send, recv, land, tok = _sibling_start(g5, _swap_pieces, (N_SHARDS,) + g5.shape[2:], last_tok[0],
                                               "grad_swap_start_" + name)
        last_tok[0] = tok
        return (name, send, recv, g5, land), tok

    def swap_end(state, after):
        name, send, recv, g5, land = state
        land = _sibling_wait(send, recv, g5, land, _swap_pieces, after, "grad_swap_wait_" + name)
        pair = _pair_sum(g5, land, c_arr)
        send, recv, pair, land, tok = _scatter_start(pair, "grad_scatter_start_" + name)
        chains.append((name, send, recv, pair, land))
        last_tok[0] = tok
        return tok

    def mlp_backward(dhb, hn, r, u, w1, w2, name):
        steps = _mlp_bwd(dhb, hn, r, u, w1, w2)
        state, tok = swap_begin(next(steps), "mlp_w_out" + name)
        dw1 = steps.send(tok)
        swap_end(state, dw1)
        state, tok = swap_begin(dw1, "mlp_w_in" + name)
        dhn = steps.send(tok)
        return dhn, swap_end(state, dhn)

    loss_part, dh4, dh4b, dg_final = _final_bwd(h4, norm_final.reshape(1, d), tgt)
    dhn3, tok = mlp_backward(dh4b, hn3, r1, u1, w1g[1], w2g[1], "1")
    dh3, dh3b, dg_mlp1 = _rmsnorm_bwd(h3, pinned(norm_mlp[1:2], tok), dhn3, dh4, "norm_mlp1_bwd")

    (dog,) = _mm(dh3b, wout, "nt", out_dtypes=[F32], name="gla_out_bwd_x")
    (dwout,) = _mm(og, dh3b, "tn", out_dtypes=[BF16], name="gla_out_bwd_w")
    state, tok = swap_begin(dwout.reshape(N_SHARDS, d // N_SHARDS, d), "gla_w_out")
    do, dgate, dg_gn = _gla_post_bwd(dog, o_f, o_b, pm, pinned(gn, tok))
    dq_f, dk_f, dv_f, dla_f, dq_b, dk_b, dv_b, dla_b = _gla_bwd(pm, la, do, st_f, st_b, seq)
    b_up_p = pinned(b_up, swap_end(state, dq_f))
    nkb = kd // tn_

    def gate_bwd(acc, bv, dl_f, dl_b):
        z = acc + bv
        dl = jnp.where(pl.program_id(1) < nkb, dl_f, dl_b)
        return (dl * (1.0 / GATE_TAU) / (1.0 + jnp.exp(z)),)

    (dz,) = _matmul(
        pr, w_up, dims=_NN, grid=(t // tm_, 2 * kd // tn_, 1),
        a_spec=pl.BlockSpec((tm_, R_PAD), lambda i, j, kk: (i, 0)),
        b_spec=pl.BlockSpec((R_PAD, tn_), lambda i, j, kk: (0, j)), acc_shape=(tm_, tn_),
        out_shapes=[_sds((t, 2 * kd), BF16)], out_specs=[pl.BlockSpec((tm_, tn_), lambda i, j, kk: (i, j))],
        epilogue=gate_bwd, extras=(b_up_p, dla_f, dla_b),
        extra_specs=[pl.BlockSpec((1, tn_), lambda i, j, kk: (0, j)),
                     pl.BlockSpec((tm_, tn_), lambda i, j, kk: (i, jnp.minimum(j, nkb - 1))),
                     pl.BlockSpec((tm_, tn_), lambda i, j, kk: (i, jnp.maximum(j - nkb, 0)))],
        name="gla_gate_bwd")
    (dpr,) = _mm(dz, w_up, "nt", out_dtypes=[BF16], name="gla_gate_bwd_r")
    (dw_up,) = _mm(pr, dz, "tn", out_dtypes=[F32], name="gla_gate_bwd_w")
    dp = _gla_dp(dq_f, dq_b, dk_f, dk_b, dv_f, dv_b, dgate)
    (dw_main,) = _mm(hn2, dp, "tn", out_dtypes=[BF16], name="gla_proj_bwd_w")
    (dw_r,) = _mm(hn2, dpr, "tn", out_dtypes=[BF16], name="gla_proj_bwd_wr")
    dwin = jnp.concatenate([dw_main, dw_r[:, :2 * GATE_RANK]], axis=1)
    state, tok = swap_begin(jnp.transpose(dwin.reshape(d, N_SHARDS, pw), (1, 0, 2)), "gla_w_in")
    (dhn2_r,) = _mm(dpr, w_r, "nt", out_dtypes=[F32], name="gla_proj_bwd_xr")
    (dhn2,) = _mm(dp, w_main, "nt", out_dtypes=[F32], epilogue=lambda acc, e: (acc + e,), extras=(dhn2_r,),
                  name="gla_proj_bwd_x", tk=2048, pin=tok)
    tok = swap_end(state, dhn2)
    dh2, dh2b, dg_mix1 = _rmsnorm_bwd(h2, pinned(norm_mix[1:2], tok), dhn2, dh3, "norm_mix1_bwd")

    dhn1, tok = mlp_backward(dh2b, hn1, r0, u0, w1g[0], w2g[0], "0")
    dh1, _, dg_mlp0 = _rmsnorm_bwd(h1, pinned(norm_mlp[0:1], tok), dhn1, dh2, "norm_mlp0_bwd")

    dys, dg_pscale = _pool_scale_bwd(dh1, ypre, pool_scale)
    dwp = _pool_mm_bwd_w(dm, dys)
    state, tok = swap_begin(dwp.reshape(N_SHARDS, pool_rows, dg), "pool_w")
    dd = _pool_mm_bwd_x(dys, wp, tok)
    tok = swap_end(state, dd)
    dhn0 = _pool_apply(dd, seq, True, "pool_diff_bwd")
    dx, _, dg_mix0 = _rmsnorm_bwd(xf, pinned(norm_mix[0:1], tok), dhn0, dh1, "norm_mix0_bwd")

    dwuf, dwub = dw_up[:GATE_RANK, :kd], dw_up[GATE_RANK:2 * GATE_RANK, kd:]
    dbuf, dbub = dw_up[ONES_COL, :kd], dw_up[ONES_COL, kd:]
    pieces = [jnp.concatenate([dg_mix0, dg_mix1], 0), jnp.concatenate([dg_mlp0, dg_mlp1], 0), dg_final, dg_pscale,
              dwuf, dwub, dbuf, dbub, dg_gn]
    sizes = [p.size for p in pieces]
    packed = _to_rows(jnp.concatenate([p.reshape(-1) for p in pieces]))
    bsend, brecv, bland, _ = _bcast_start(packed, dx, "small_grads_start")

    loss = lax.psum(loss_part[0, 0], ("x", "y", "c"))

    weights = [norm_mix, norm_mlp, norm_final, pool_w, pool_scale, gla_w_in, gla_w_up_f, gla_b_up_f, gla_w_up_b,
               gla_b_up_b, gla_g_norm, gla_w_out, mlp_w_in, mlp_w_out]
    moms = [m_norm_mix, m_norm_mlp, m_norm_final, m_pool_w, m_pool_scale, m_gla_w_in, m_gla_w_up_f, m_gla_b_up_f,
            m_gla_w_up_b, m_gla_b_up_b, m_gla_g_norm, m_gla_w_out, m_mlp_w_in, m_mlp_w_out]
    vels = [v_norm_mix, v_norm_mlp, v_norm_final, v_pool_w, v_pool_scale, v_gla_w_in, v_gla_w_up_f, v_gla_b_up_f,
            v_gla_w_up_b, v_gla_b_up_b, v_gla_g_norm, v_gla_w_out, v_mlp_w_in, v_mlp_w_out]
    names = ["norm_mix", "norm_mlp", "norm_final", "pool_w", "pool_scale", "gla_w_in", "gla_w_up_f", "gla_b_up_f",
             "gla_w_up_b", "gla_b_up_b", "gla_g_norm", "gla_w_out", "mlp_w_in", "mlp_w_out"]
    index = {nm: k for k, nm in enumerate(names)}
    results = {}

    stacked = {"mlp_w_out1": ("mlp_w_out", 1), "mlp_w_in1": ("mlp_w_in", 1), "gla_w_out": ("gla_w_out", 0),
               "gla_w_in": ("gla_w_in", 0), "mlp_w_out0": ("mlp_w_out", 0), "mlp_w_in0": ("mlp_w_in", 0),
               "pool_w": ("pool_w", 0)}
    oc_arr = 1 - c_arr
    after = dx
    for name, send, recv, pair, land in chains:
        pair, land = _scatter_wait(send, recv, pair, land, after, "grad_scatter_wait_" + name)
        mine = _quad_sum(pair, land, s_arr)
        jsend, jrecv, jland, jtok = _sibling_start(mine, _whole_piece, mine.shape, after, "grad_join_start_" + name)
        nm, layer = stacked[name]
        w, m, v = weights[index[nm]], moms[index[nm]], vels[index[nm]]
        rows, cols = 2 * mine.shape[0], mine.shape[1]
        shp = (w.size // (rows * cols), rows, cols)
        w, m, v = w.reshape(shp), m.reshape(shp), v.reshape(shp)
        tr_ = nm == "gla_w_in"
        flip = (lambda a: jnp.swapaxes(a, -1, -2)) if tr_ else (lambda a: a)
        w, m, v = flip(w), flip(m), flip(v)
        res = _adamw_half(w, flip(mine), m, v, layer, c_arr, results.get(nm), jtok, "adamw_mine_" + name, tr_)
        theirs = _sibling_wait(jsend, jrecv, mine, jland, _whole_piece, res[1], "grad_join_wait_" + name)
        res = _adamw_half(w, flip(theirs), m, v, layer, oc_arr, res, jtok, "adamw_theirs_" + name, tr_)
        results[nm] = [flip(a) for a in res]
        after = res[1]

    bland = _bcast_wait(bsend, brecv, packed, bland, after, "small_grads_wait")
    me_arr = jnp.reshape(4 * cx + 2 * cy + cc, (1,)).astype(jnp.int32)
    summed = _sum8(bland, packed, me_arr).reshape(-1)
    outs_small, o = [], 0
    for p, n in zip(pieces, sizes):
        outs_small.append(summed[o:o + n].reshape(p.shape))
        o += n
    g_nmix, g_nmlp, g_nfinal, g_pscale, g_wuf, g_wub, g_buf, g_bub, g_gn = outs_small
    g_wuf = lax.dynamic_slice_in_dim(g_wuf, s_me * ks, ks, axis=1)
    g_wub = lax.dynamic_slice_in_dim(g_wub, s_me * ks, ks, axis=1)
    g_buf = lax.dynamic_slice_in_dim(g_buf, s_me * ks, ks, axis=0)
    g_bub = lax.dynamic_slice_in_dim(g_bub, s_me * ks, ks, axis=0)
    g_gn = lax.dynamic_slice_in_dim(g_gn.reshape(dv), s_me * (dv // N_SHARDS), dv // N_SHARDS, axis=0)
    small_grads = {"norm_mix": g_nmix, "norm_mlp": g_nmlp, "norm_final": g_nfinal, "pool_scale": g_pscale,
                   "gla_w_up_f": g_wuf, "gla_b_up_f": g_buf, "gla_w_up_b": g_wub, "gla_b_up_b": g_bub,
                   "gla_g_norm": g_gn}
    for nm, g in small_grads.items():
        w, m, v = weights[index[nm]], moms[index[nm]], vels[index[nm]]
        cols = w.shape[-1]
        shp = (w.size // cols, cols)
        dl, mn, vn = _adamw(w.reshape(shp), g.reshape(shp), m.reshape(shp), v.reshape(shp), "adamw_" + nm)
        results[nm] = (g, dl, mn, vn)

    outs = [[results[nm][k].reshape(weights[index[nm]].shape) for nm in names] for k in range(4)]
    return (loss, dx.reshape(x.shape), *outs[0], *outs[1], *outs[2], *outs[3])
```

```python
import jax
import jax.numpy as jnp
from jax import lax
from jax.experimental import pallas as pl
from jax.experimental.pallas import tpu as pltpu

F32 = jnp.float32
BF16 = jnp.bfloat16

N_HEADS = 4
N_GROUPS = 4
POOL_HALF = (1, 2, 4, 8)
GATE_RANK = 16
GATE_TAU = 16.0
CHUNK = 64
STEP = 2 * CHUNK
EPS = 1e-6
N_SHARDS = 4
R_PAD = 128
ONES_COL = 2 * GATE_RANK

ADAM_LR = 0.001
ADAM_B1 = 0.9
ADAM_B2 = 0.999
ADAM_EPS = 1e-08
ADAM_WD = 0.01
ADAM_STEP = 10

_NN = (((1,), (0,)), ((), ()))
_NT = (((1,), (1,)), ((), ()))
_TN = (((0,), (0,)), ((), ()))

VMEM_LIMIT = 56 * 1024 * 1024
MESH = pl.DeviceIdType.MESH
ANY = pl.BlockSpec(memory_space=pl.ANY)


def _tile(dim, pref):
    return pref if dim % pref == 0 else dim


def _params(*sem):
    return pltpu.CompilerParams(dimension_semantics=sem, vmem_limit_bytes=VMEM_LIMIT)


def _sds(shape, dtype):
    return jax.ShapeDtypeStruct(shape, dtype)


def _matmul(a, b, *, dims, grid, a_spec, b_spec, acc_shape, out_shapes, out_specs, epilogue,
            extras=(), extra_specs=(), name, pin=None):
    nk = grid[2]
    n_extra = len(extras)
    n_out = len(out_shapes)
    pins = () if pin is None else (pin,)

    def body(a_ref, b_ref, *rest):
        extra_refs = rest[:n_extra]
        rest = rest[n_extra + len(pins):]
        out_refs = rest[:n_out]
        acc_ref = rest[n_out]
        kk = pl.program_id(2)

        def part():
            return lax.dot_general(a_ref[...], b_ref[...], dims, preferred_element_type=F32)

        def finish(acc):
            outs = epilogue(acc, *[r[...] for r in extra_refs])
            for o_ref, o in zip(out_refs, outs):
                o_ref[...] = o.astype(o_ref.dtype)

        if nk == 1:
            finish(part())
        else:
            @pl.when(kk == 0)
            def _():
                acc_ref[...] = part()

            @pl.when((kk > 0) & (kk < nk - 1))
            def _():
                acc_ref[...] += part()

            @pl.when(kk == nk - 1)
            def _():
                finish(acc_ref[...] + part())

    return pl.pallas_call(
        body,
        grid=grid,
        in_specs=[a_spec, b_spec, *extra_specs, *[pl.BlockSpec((8, 128), lambda i, j, kk: (0, 0)) for _ in pins]],
        out_specs=list(out_specs),
        out_shape=list(out_shapes),
        scratch_shapes=[pltpu.VMEM(acc_shape if nk > 1 else (8, 128), F32)],
        compiler_params=_params("parallel", "parallel", "arbitrary"),
        name=name,
    )(a, b, *extras, *pins)


def _mm(a, b, kind, *, out_dtypes, epilogue=None, extras=(), name, tm=1024, tn=1024, tk=4096, pin=None):
    if kind == "nn":
        (m, k), n = a.shape, b.shape[1]
    elif kind == "nt":
        (m, k), n = a.shape, b.shape[0]
    else:
        (k, m), n = a.shape, b.shape[1]
    tm, tn, tk = _tile(m, tm), _tile(n, tn), _tile(k, tk)
    if kind == "nn":
        a_spec = pl.BlockSpec((tm, tk), lambda i, j, kk: (i, kk))
        b_spec = pl.BlockSpec((tk, tn), lambda i, j, kk: (kk, j))
        dims = _NN
    elif kind == "nt":
        a_spec = pl.BlockSpec((tm, tk), lambda i, j, kk: (i, kk))
        b_spec = pl.BlockSpec((tn, tk), lambda i, j, kk: (j, kk))
        dims = _NT
    else:
        a_spec = pl.BlockSpec((tk, tm), lambda i, j, kk: (kk, i))
        b_spec = pl.BlockSpec((tk, tn), lambda i, j, kk: (kk, j))
        dims = _TN
    o_spec = pl.BlockSpec((tm, tn), lambda i, j, kk: (i, j))
    if epilogue is None:
        epilogue = lambda acc, *e: tuple(acc for _ in out_dtypes)
    return _matmul(
        a, b, dims=dims, grid=(m // tm, n // tn, k // tk), a_spec=a_spec, b_spec=b_spec, acc_shape=(tm, tn),
        out_shapes=[_sds((m, n), d) for d in out_dtypes], out_specs=[o_spec for _ in out_dtypes],
        epilogue=epilogue, extras=extras, extra_specs=[o_spec for _ in extras], name=name, pin=pin)


def _rmsnorm_fwd(h, g, out_dtype, name):
    t, d = h.shape
    tr = _tile(t, 256)

    def body(h_ref, g_ref, o_ref):
        x = h_ref[...]
        r = lax.rsqrt(jnp.mean(x * x, axis=-1, keepdims=True) + EPS)
        o_ref[...] = (x * r * g_ref[...]).astype(o_ref.dtype)

    return pl.pallas_call(
        body, grid=(t // tr,),
        in_specs=[pl.BlockSpec((tr, d), lambda i: (i, 0)), pl.BlockSpec((1, d), lambda i: (0, 0))],
        out_specs=pl.BlockSpec((tr, d), lambda i: (i, 0)),
        out_shape=_sds((t, d), out_dtype), compiler_params=_params("parallel"), name=name)(h, g)


def _rmsnorm_bwd(h, g, dy, resid, name):
    t, d = h.shape
    tr = _tile(t, 128)

    def body(h_ref, g_ref, dy_ref, res_ref, dh_ref, dhb_ref, dg_ref):
        x = h_ref[...]
        r = lax.rsqrt(jnp.mean(x * x, axis=-1, keepdims=True) + EPS)
        xn = x * r
        dyv = dy_ref[...]
        gdy = dyv * g_ref[...]
        dh = res_ref[...] + r * (gdy - xn * jnp.mean(gdy * xn, axis=-1, keepdims=True))
        dh_ref[...] = dh
        dhb_ref[...] = dh.astype(BF16)
        part = jnp.sum(dyv * xn, axis=0, keepdims=True)

        @pl.when(pl.program_id(0) == 0)
        def _():
            dg_ref[...] = part

        @pl.when(pl.program_id(0) > 0)
        def _():
            dg_ref[...] += part

    row = pl.BlockSpec((tr, d), lambda i: (i, 0))
    vec = pl.BlockSpec((1, d), lambda i: (0, 0))
    return pl.pallas_call(
        body, grid=(t // tr,), in_specs=[row, vec, row, row], out_specs=[row, row, vec],
        out_shape=[_sds((t, d), F32), _sds((t, d), BF16), _sds((1, d), F32)],
        compiler_params=_params("arbitrary"), name=name)(h, g, dy, resid)


def _final_bwd(h, g, tgt):
    t, d = h.shape
    tr = _tile(t, 128)

    def body(h_ref, g_ref, t_ref, loss_ref, dh_ref, dhb_ref, dg_ref):
        x = h_ref[...]
        r = lax.rsqrt(jnp.mean(x * x, axis=-1, keepdims=True) + EPS)
        xn = x * r
        gv = g_ref[...]
        e = xn * gv - t_ref[...]
        lpart = jnp.full((1, 128), 0.5 * jnp.sum(jnp.mean(e * e, axis=-1, keepdims=True)), F32)
        dyv = e * (1.0 / d)
        gdy = dyv * gv
        dh = r * (gdy - xn * jnp.mean(gdy * xn, axis=-1, keepdims=True))
        dh_ref[...] = dh
        dhb_ref[...] = dh.astype(BF16)
        part = jnp.sum(dyv * xn, axis=0, keepdims=True)

        @pl.when(pl.program_id(0) == 0)
        def _():
            dg_ref[...] = part
            loss_ref[...] = lpart

        @pl.when(pl.program_id(0) > 0)
        def _():
            dg_ref[...] += part
            loss_ref[...] += lpart

    row = pl.BlockSpec((tr, d), lambda i: (i, 0))
    vec = pl.BlockSpec((1, d), lambda i: (0, 0))
    return pl.pallas_call(
        body, grid=(t // tr,), in_specs=[row, vec, row],
        out_specs=[pl.BlockSpec((1, 128), lambda i: (0, 0)), row, row, vec],
        out_shape=[_sds((1, 128), F32), _sds((t, d), F32), _sds((t, d), BF16), _sds((1, d), F32)],
        compiler_params=_params("arbitrary"), name="final_loss_bwd")(h, g, tgt)


def _shift_rows(x, s, row):
    n = x.shape[0]
    y = pltpu.roll(x, (-s) % n, 0)
    return jnp.where((row + s >= 0) & (row + s < n), y, 0.0)


def _span_sum(x, start, length, row):
    if start >= 0:
        y, step = _shift_rows(x, start, row) if start else x, 1
    else:
        last = start + length - 1
        assert last <= 0
        y, step = _shift_rows(x, last, row) if last else x, -1
    n = 1
    while n < length:
        y = y + _shift_rows(y, step * n, row)
        n *= 2
    return y


def _pool_apply(x, seq, transpose, name):
    t, d = x.shape
    dg = d // N_GROUPS
    tc = _tile(dg, 256)
    nblk = dg // tc

    def body(x_ref, o_ref):
        grp = pl.program_id(1)
        row = lax.broadcasted_iota(jnp.int32, (seq, tc), 0)
        for gi, hw in enumerate(POOL_HALF):
            @pl.when(grp == gi)
            def _(hw=hw):
                xv = x_ref[...]
                cnt = (jnp.minimum(row + hw, seq) - jnp.maximum(row - hw, 0)).astype(F32)
                if not transpose:
                    w = _span_sum(xv, 0, hw, row) + _span_sum(xv, -hw, hw, row)
                    o_ref[...] = (w / cnt - xv).astype(o_ref.dtype)
                else:
                    u = xv / cnt
                    w = _span_sum(u, 1, hw, row) + _span_sum(u, -(hw - 1), hw, row)
                    o_ref[...] = (w - xv).astype(o_ref.dtype)

    spec = pl.BlockSpec((seq, tc), lambda b, g, j: (b, g * nblk + j))
    return pl.pallas_call(
        body, grid=(t // seq, N_GROUPS, nblk), in_specs=[spec], out_specs=spec,
        out_shape=_sds((t, d), F32 if transpose else BF16),
        compiler_params=_params("parallel", "parallel", "parallel"), name=name)(x)


def _pool_mm_fwd(dm, wp, x, scale):
    t, d = dm.shape
    dg = d // N_GROUPS
    rs = dg // N_SHARDS
    tm = _tile(t, 1024)
    o_spec = pl.BlockSpec((tm, dg), lambda i, j, kk: (i, j))
    return _matmul(
        dm, wp, dims=_NN, grid=(t // tm, N_GROUPS, N_SHARDS),
        a_spec=pl.BlockSpec((tm, rs), lambda i, j, kk: (i, j * N_SHARDS + kk)),
        b_spec=pl.BlockSpec((None, None, rs, dg), lambda i, j, kk: (kk, j, 0, 0)),
        acc_shape=(tm, dg), out_shapes=[_sds((t, d), F32), _sds((t, d), F32)], out_specs=[o_spec, o_spec],
        epilogue=lambda acc, xv, sc: (xv + acc * sc, acc),
        extras=(x, scale), extra_specs=[o_spec, pl.BlockSpec((1, dg), lambda i, j, kk: (0, j))], name="pool_mm_fwd")


def _pool_scale_bwd(dh, ypre, scale):
    t, d = dh.shape
    tr = _tile(t, 256)

    def body(dh_ref, y_ref, s_ref, o_ref, ds_ref):
        g = dh_ref[...]
        o_ref[...] = (g * s_ref[...]).astype(BF16)
        part = jnp.sum(g * y_ref[...], axis=0, keepdims=True)

        @pl.when(pl.program_id(0) == 0)
        def _():
            ds_ref[...] = part

        @pl.when(pl.program_id(0) > 0)
        def _():
            ds_ref[...] += part

    row = pl.BlockSpec((tr, d), lambda i: (i, 0))
    vec = pl.BlockSpec((1, d), lambda i: (0, 0))
    return pl.pallas_call(
        body, grid=(t // tr,), in_specs=[row, row, vec], out_specs=[row, vec],
        out_shape=[_sds((t, d), BF16), _sds((1, d), F32)], compiler_params=_params("arbitrary"),
        name="pool_scale_bwd")(dh, ypre, scale)


def _pool_mm_bwd_x(dys, wp, pin):
    t, d = dys.shape
    dg = d // N_GROUPS
    rs = dg // N_SHARDS
    tm = _tile(t, 1024)
    return _matmul(
        dys, wp, dims=_NT, grid=(t // tm, N_GROUPS * N_SHARDS, 1),
        a_spec=pl.BlockSpec((tm, dg), lambda i, j, kk: (i, j // N_SHARDS)),
        b_spec=pl.BlockSpec((None, None, rs, dg), lambda i, j, kk: (j % N_SHARDS, j // N_SHARDS, 0, 0)),
        acc_shape=(tm, rs), out_shapes=[_sds((t, d), F32)],
        out_specs=[pl.BlockSpec((tm, rs), lambda i, j, kk: (i, j))],
        epilogue=lambda acc: (acc,), name="pool_mm_bwd_x", pin=pin)[0]


def _pool_mm_bwd_w(dm, dys):
    t, d = dm.shape
    dg = d // N_GROUPS
    rs = dg // N_SHARDS
    tk = _tile(t, 4096)
    return _matmul(
        dm, dys, dims=_TN, grid=(N_GROUPS * N_SHARDS, 1, t // tk),
        a_spec=pl.BlockSpec((tk, rs), lambda i, j, kk: (kk, i)),
        b_spec=pl.BlockSpec((tk, dg), lambda i, j, kk: (kk, i // N_SHARDS)),
        acc_shape=(rs, dg), out_shapes=[_sds((N_SHARDS, N_GROUPS, rs, dg), BF16)],
        out_specs=[pl.BlockSpec((None, None, rs, dg), lambda i, j, kk: (i % N_SHARDS, i // N_SHARDS, 0, 0))],
        epilogue=lambda acc: (acc,), name="pool_mm_bwd_w")[0]


def _mlp_fwd(h, hn, w1g, w2_after):
    t, d = hn.shape
    f4 = w1g.shape[2]
    dff = N_SHARDS * f4
    tm, tn, tk = _tile(t, 1024), _tile(f4, 1024), _tile(d, 4096)
    nb = f4 // tn
    o_spec = pl.BlockSpec((tm, tn), lambda i, j, kk: (i, j))

    def act(acc):
        r = jnp.maximum(acc, 0.0)
        return r, r * r

    r, u = _matmul(
        hn, w1g, dims=_NN, grid=(t // tm, dff // tn, d // tk),
        a_spec=pl.BlockSpec((tm, tk), lambda i, j, kk: (i, kk)),
        b_spec=pl.BlockSpec((None, tk, tn), lambda i, j, kk: (j // nb, kk, j % nb)),
        acc_shape=(tm, tn), out_shapes=[_sds((t, dff), BF16), _sds((t, dff), BF16)], out_specs=[o_spec, o_spec],
        epilogue=act, name="mlp_up")
    w2 = w2_after(u)
    (out,) = _mm(u, w2, "nn", out_dtypes=[F32], epilogue=lambda acc, hv: (hv + acc,), extras=(h,), name="mlp_down",
                 tk=2048)
    return out, r, u, w2


def _mlp_bwd(dhb, hn, r, u, w1g, w2):
    t, d = hn.shape
    f4 = w1g.shape[2]
    dff = N_SHARDS * f4
    (da,) = _mm(dhb, w2, "nt", out_dtypes=[BF16], epilogue=lambda acc, rv: (acc * (2.0 * rv.astype(F32)),),
                extras=(r,), name="mlp_bwd_da")
    (dw2,) = _mm(u, dhb, "tn", out_dtypes=[BF16], name="mlp_bwd_dw2")
    tok = yield dw2.reshape(N_SHARDS, f4, d)
    tm, tn, tk = _tile(d, 1024), _tile(f4, 1024), _tile(t, 4096)
    nb = f4 // tn
    (dw1,) = _matmul(
        hn, da, dims=_TN, grid=(d // tm, dff // tn, t // tk),
        a_spec=pl.BlockSpec((tk, tm), lambda i, j, kk: (kk, i)),
        b_spec=pl.BlockSpec((tk, tn), lambda i, j, kk: (kk, j)),
        acc_shape=(tm, tn), out_shapes=[_sds((N_SHARDS, d, f4), BF16)],
        out_specs=[pl.BlockSpec((None, tm, tn), lambda i, j, kk: (j // nb, i, j % nb))],
        epilogue=lambda acc: (acc,), name="mlp_bwd_dw1", pin=tok)
    tok = yield dw1
    tm, tn, tk = _tile(t, 1024), _tile(d, 1024), _tile(f4, 4096)
    nbk = f4 // tk
    (dhn,) = _matmul(
        da, w1g, dims=_NT, grid=(t // tm, d // tn, dff // tk),
        a_spec=pl.BlockSpec((tm, tk), lambda i, j, kk: (i, kk)),
        b_spec=pl.BlockSpec((None, tn, tk), lambda i, j, kk: (kk // nbk, j, kk % nbk)),
        acc_shape=(tm, tn), out_shapes=[_sds((t, d), F32)],
        out_specs=[pl.BlockSpec((tm, tn), lambda i, j, kk: (i, j))], epilogue=lambda acc: (acc,), name="mlp_bwd_dhn",
        pin=tok)
    yield dhn


def _split3(x):
    a = x.astype(BF16)
    r1 = x - a.astype(F32)
    b = r1.astype(BF16)
    c = (r1 - b.astype(F32)).astype(BF16)
    return a, b, c


def _dot(a, b, dims):
    return lax.dot_general(a.astype(BF16), b.astype(BF16), dims, preferred_element_type=F32)


def _chunk_terms(q, k, g, rev, scale):
    c = q.shape[0]
    ri = lax.broadcasted_iota(jnp.int32, (c, c), 0)
    ci = lax.broadcasted_iota(jnp.int32, (c, c), 1)
    seen = (ci >= ri) if rev else (ci <= ri)
    tri = seen.astype(BF16)
    g1, g2, g3 = _split3(g)
    b = (lax.dot_general(tri, g1, _NN, preferred_element_type=F32)
         + lax.dot_general(tri, g2, _NN, preferred_element_type=F32)
         + lax.dot_general(tri, g3, _NN, preferred_element_type=F32))
    mid = c // 2 if rev else c // 2 - 1
    last = 0 if rev else c - 1
    rows = lax.broadcasted_iota(jnp.int32, b.shape, 0)
    b_mid = jnp.sum(jnp.where(rows == mid, b, 0.0), axis=0, keepdims=True)
    b_last = jnp.sum(jnp.where(rows == last, b, 0.0), axis=0, keepdims=True)
    qs = q * scale
    e1 = jnp.exp(b - b_mid)
    e2 = jnp.exp(b_mid - b)
    eb = jnp.exp(b)
    el = jnp.exp(b_last - b)
    return dict(seen=seen, tri=tri, mid=mid, last=last, e1=e1, e2=e2, eb=eb, el=el, a=jnp.exp(b_last),
                qe=qs * e1, ke=k * e2, qi=qs * eb, ks=k * el)


def _step_terms(q_ref, k_ref, g_ref, rev, scale):
    lo, hi = slice(0, CHUNK), slice(CHUNK, STEP)
    rows = (hi, lo) if rev else (lo, hi)
    return rows, [_chunk_terms(q_ref[r, :], k_ref[r, :], g_ref[r, :], rev, scale) for r in rows]


def _gla_fwd(pm, la, seq):
    t = pm.shape[0]
    d = pm.shape[1] // 3
    dk, dv = d // 2 // N_HEADS, d // N_HEADS
    nb, nc = t // seq, seq // STEP
    scale = dk ** -0.5
    kq, kk_, kv = 0, N_HEADS, (d // dv)

    def body(qf, kf, vf, gf, qb, kb, vb, gb, of_ref, ob_ref, stf_ref, stb_ref, sf, sb):
        n = pl.program_id(1)

        @pl.when(n == 0)
        def _():
            sf[...] = jnp.zeros_like(sf)
            sb[...] = jnp.zeros_like(sb)

        for (q_ref, k_ref, v_ref, g_ref, o_ref, st_ref, s_ref, rev) in (
                (qf, kf, vf, gf, of_ref, stf_ref, sf, False), (qb, kb, vb, gb, ob_ref, stb_ref, sb, True)):
            rows, (t0, t1) = _step_terms(q_ref, k_ref, g_ref, rev, scale)
            v0, v1 = v_ref[rows[0], :], v_ref[rows[1], :]
            st = s_ref[...]
            stb = st.astype(BF16)
            st_ref[...] = stb
            sc0 = jnp.where(t0["seen"], _dot(t0["qe"], t0["ke"], _NT), 0.0)
            sc1 = jnp.where(t1["seen"], _dot(t1["qe"], t1["ke"], _NT), 0.0)
            cross = _dot(t1["qi"], t0["ks"], _NT)
            qi_all = jnp.concatenate([t0["qi"], t1["qi"] * t0["a"]], axis=0)
            o_inter = lax.dot_general(qi_all.astype(BF16), stb, _NT, preferred_element_type=F32)
            o_ref[rows[0], :] = _dot(sc0, v0, _NN) + o_inter[:CHUNK]
            o_ref[rows[1], :] = _dot(sc1, v1, _NN) + _dot(cross, v0, _NN) + o_inter[CHUNK:]
            ks_all = jnp.concatenate([t0["ks"] * t1["a"], t1["ks"]], axis=0)
            s_ref[...] = st * (t0["a"] * t1["a"]) + _dot(jnp.concatenate([v0, v1], axis=0), ks_all, _TN)

    def row(bh, n, rev):
        return (bh // N_HEADS) * nc + (nc - 1 - n if rev else n)

    def specs(rev):
        return [
            pl.BlockSpec((STEP,dk), lambda bh, n: (row(bh, n, rev), kq + bh % N_HEADS)),
            pl.BlockSpec((STEP,dk), lambda bh, n: (row(bh, n, rev), kk_ + bh % N_HEADS)),
            pl.BlockSpec((STEP,dv), lambda bh, n: (row(bh, n, rev), kv + bh % N_HEADS)),
            pl.BlockSpec((STEP,dk), lambda bh, n: (row(bh, n, rev), (N_HEADS if rev else 0) + bh % N_HEADS)),
        ]

    def o_spec(rev):
        return pl.BlockSpec((STEP,dv), lambda bh, n: (row(bh, n, rev), bh % N_HEADS))

    def st_spec(rev):
        return pl.BlockSpec((None, None, dv, dk), lambda bh, n: (bh, nc - 1 - n if rev else n, 0, 0))

    sf_, sb_ = specs(False), specs(True)
    return pl.pallas_call(
        body, grid=(nb * N_HEADS, nc),
        in_specs=[*sf_, *sb_],
        out_specs=[o_spec(False), o_spec(True), st_spec(False), st_spec(True)],
        out_shape=[_sds((t, d), F32), _sds((t, d), F32),
                   _sds((nb * N_HEADS, nc, dv, dk), BF16), _sds((nb * N_HEADS, nc, dv, dk), BF16)],
        scratch_shapes=[pltpu.VMEM((dv, dk), F32), pltpu.VMEM((dv, dk), F32)],
        compiler_params=_params("parallel", "arbitrary"), name="gla_scan_fwd",
    )(pm, pm, pm, la, pm, pm, pm, la)


def _gla_bwd(pm, la, do, st_f, st_b, seq):
    t = pm.shape[0]
    d = pm.shape[1] // 3
    dk, dv = d // 2 // N_HEADS, d // N_HEADS
    kd = dk * N_HEADS
    nb, nc = t // seq, seq // STEP
    scale = dk ** -0.5
    kq, kk_, kv = 0, N_HEADS, (d // dv)

    def body(qf, kf, vf, gf, dof, stf, qb, kb, vb, gb, dob, stb_,
             dqf, dkf, dvf, dgf, dqb, dkb, dvb, dgb, dsf, dsb):
        n = pl.program_id(1)

        @pl.when(n == 0)
        def _():
            dsf[...] = jnp.zeros_like(dsf)
            dsb[...] = jnp.zeros_like(dsb)

        for (q_ref, k_ref, v_ref, g_ref, do_ref, st_ref, dq_ref, dk_ref, dv_ref, dg_ref, ds_ref, rev) in (
                (qf, kf, vf, gf, dof, stf, dqf, dkf, dvf, dgf, dsf, False),
                (qb, kb, vb, gb, dob, stb_, dqb, dkb, dvb, dgb, dsb, True)):
            rows, (t0, t1) = _step_terms(q_ref, k_ref, g_ref, rev, scale)
            v0, v1 = v_ref[rows[0], :], v_ref[rows[1], :]
            do0, do1 = do_ref[rows[0], :], do_ref[rows[1], :]
            st = st_ref[...]
            ds = ds_ref[...]
            dsb16 = ds.astype(BF16)
            a0, a1 = t0["a"], t1["a"]
            sc0 = jnp.where(t0["seen"], _dot(t0["qe"], t0["ke"], _NT), 0.0)
            sc1 = jnp.where(t1["seen"], _dot(t1["qe"], t1["ke"], _NT), 0.0)
            cross = _dot(t1["qi"], t0["ks"], _NT)
            dsc0 = jnp.where(t0["seen"], _dot(do0, v0, _NT), 0.0)
            dsc1 = jnp.where(t1["seen"], _dot(do1, v1, _NT), 0.0)
            dcross = _dot(do1, v0, _NT)
            qi_all = jnp.concatenate([t0["qi"], t1["qi"] * a0], axis=0)
            ks_all = jnp.concatenate([t0["ks"] * a1, t1["ks"]], axis=0)
            v_all = jnp.concatenate([v0, v1], axis=0).astype(BF16)
            do_all = jnp.concatenate([do0, do1], axis=0).astype(BF16)
            dks_all = lax.dot_general(v_all, dsb16, _NN, preferred_element_type=F32)
            dv_all = lax.dot_general(ks_all.astype(BF16), dsb16, _NT, preferred_element_type=F32)
            dqi_all = lax.dot_general(do_all, st, _NN, preferred_element_type=F32)
            da01 = jnp.sum(ds * st.astype(F32), axis=0, keepdims=True)
            dv_ref[rows[0], :] = _dot(sc0, do0, _TN) + _dot(cross, do1, _TN) + dv_all[:CHUNK]
            dv_ref[rows[1], :] = _dot(sc1, do1, _TN) + dv_all[CHUNK:]
            dqi0 = dqi_all[:CHUNK]
            dqi1 = dqi_all[CHUNK:] * a0 + _dot(dcross, t0["ks"], _NN)
            dks0 = dks_all[:CHUNK] * a1 + _dot(dcross, t1["qi"], _TN)
            dks1 = dks_all[CHUNK:]
            da0 = da01 * a1 + jnp.sum(dqi_all[CHUNK:] * t1["qi"], axis=0, keepdims=True)
            da1 = da01 * a0 + jnp.sum(dks_all[:CHUNK] * t0["ks"], axis=0, keepdims=True)
            for r, tm, dsc, dqi, dks, da in ((rows[0], t0, dsc0, dqi0, dks0, da0), (rows[1], t1, dsc1, dqi1, dks1, da1)):
                dqe = _dot(dsc, tm["ke"], _NN)
                dke = _dot(dsc, tm["qe"], _TN)
                dq_ref[r, :] = (dqe * tm["e1"] + dqi * tm["eb"]) * scale
                dk_ref[r, :] = dke * tm["e2"] + dks * tm["el"]
                t_q, t_k, t_s = dqe * tm["qe"], dke * tm["ke"], dks * tm["ks"]
                db = t_q - t_k + dqi * tm["qi"] - t_s
                mid_row = jnp.sum(t_k - t_q, axis=0, keepdims=True)
                last_row = jnp.sum(t_s, axis=0, keepdims=True) + da * tm["a"]
                ridx = lax.broadcasted_iota(jnp.int32, db.shape, 0)
                db = db + jnp.where(ridx == tm["mid"], mid_row, 0.0) + jnp.where(ridx == tm["last"], last_row, 0.0)
                d1, d2, d3 = _split3(db)
                dg_ref[r, :] = (lax.dot_general(tm["tri"], d1, _TN, preferred_element_type=F32)
                                + lax.dot_general(tm["tri"], d2, _TN, preferred_element_type=F32)
                                + lax.dot_general(tm["tri"], d3, _TN, preferred_element_type=F32))
            ds_ref[...] = ds * (a0 * a1) + lax.dot_general(do_all, qi_all.astype(BF16), _TN,
                                                           preferred_element_type=F32)

    def row(bh, n, rev):
        return (bh // N_HEADS) * nc + (n if rev else nc - 1 - n)

    def specs(rev):
        return [
            pl.BlockSpec((STEP,dk), lambda bh, n: (row(bh, n, rev), kq + bh % N_HEADS)),
            pl.BlockSpec((STEP,dk), lambda bh, n: (row(bh, n, rev), kk_ + bh % N_HEADS)),
            pl.BlockSpec((STEP,dv), lambda bh, n: (row(bh, n, rev), kv + bh % N_HEADS)),
            pl.BlockSpec((STEP,dk), lambda bh, n: (row(bh, n, rev), (N_HEADS if rev else 0) + bh % N_HEADS)),
            pl.BlockSpec((STEP,dv), lambda bh, n: (row(bh, n, rev), bh % N_HEADS)),
            pl.BlockSpec((None, None, dv, dk), lambda bh, n: (bh, n if rev else nc - 1 - n, 0, 0)),
        ]

    def outs(rev):
        return [
            pl.BlockSpec((STEP,dk), lambda bh, n: (row(bh, n, rev), bh % N_HEADS)),
            pl.BlockSpec((STEP,dk), lambda bh, n: (row(bh, n, rev), bh % N_HEADS)),
            pl.BlockSpec((STEP,dv), lambda bh, n: (row(bh, n, rev), bh % N_HEADS)),
            pl.BlockSpec((STEP,dk), lambda bh, n: (row(bh, n, rev), bh % N_HEADS)),
        ]

    of_, ob_ = outs(False), outs(True)
    res = pl.pallas_call(
        body, grid=(nb * N_HEADS, nc),
        in_specs=[*specs(False), *specs(True)],
        out_specs=[*of_, *ob_],
        out_shape=[_sds((t, kd), F32), _sds((t, kd), F32), _sds((t, d), F32), _sds((t, kd), F32),
                   _sds((t, kd), F32), _sds((t, kd), F32), _sds((t, d), F32), _sds((t, kd), F32)],
        scratch_shapes=[pltpu.VMEM((dv, dk), F32), pltpu.VMEM((dv, dk), F32)],
        compiler_params=_params("parallel", "arbitrary"), name="gla_scan_bwd",
    )(pm, pm, pm, la, do, st_f, pm, pm, pm, la, do, st_b)
    return res


def _sigmoid(x):
    return 1.0 / (1.0 + jnp.exp(-x))


def _gla_post_fwd(o_f, o_b, pm, gn):
    t, d = o_f.shape
    dv = d // N_HEADS
    tr = _tile(t, 512)
    gate_blk = 2 * d // dv

    def body(of_ref, ob_ref, gt_ref, gn_ref, out_ref):
        o = of_ref[...] + ob_ref[...]
        n = o * lax.rsqrt(jnp.mean(o * o, axis=-1, keepdims=True) + EPS) * gn_ref[...]
        gt = gt_ref[...]
        out_ref[...] = (n * (gt * _sigmoid(gt))).astype(BF16)

    blk = pl.BlockSpec((tr, dv), lambda i, h: (i, h))
    return pl.pallas_call(
        body, grid=(t // tr, N_HEADS),
        in_specs=[blk, blk, pl.BlockSpec((tr, dv), lambda i, h: (i, gate_blk + h)),
                  pl.BlockSpec((1, dv), lambda i, h: (0, 0))],
        out_specs=blk, out_shape=_sds((t, d), BF16), compiler_params=_params("parallel", "parallel"),
        name="gla_post_fwd")(o_f, o_b, pm, gn)


def _gla_post_bwd(dog, o_f, o_b, pm, gn):
    t, d = o_f.shape
    dv = d // N_HEADS
    tr = _tile(t, 256)
    gate_blk = 2 * d // dv

    def body(dog_ref, of_ref, ob_ref, gt_ref, gn_ref, do_ref, dgt_ref, dgn_ref):
        o = of_ref[...] + ob_ref[...]
        rr = lax.rsqrt(jnp.mean(o * o, axis=-1, keepdims=True) + EPS)
        on = o * rr
        gnv = gn_ref[...]
        gt = gt_ref[...]
        sg = _sigmoid(gt)
        sl = gt * sg
        dg_out = dog_ref[...]
        dn = dg_out * sl
        dgt_ref[...] = dg_out * (on * gnv) * (sg * (1.0 + gt * (1.0 - sg)))
        gdn = dn * gnv
        do_ref[...] = rr * (gdn - on * jnp.mean(gdn * on, axis=-1, keepdims=True))
        part = jnp.sum(dn * on, axis=0, keepdims=True)
        first = (pl.program_id(0) == 0) & (pl.program_id(1) == 0)

        @pl.when(first)
        def _():
            dgn_ref[...] = part

        @pl.when(jnp.logical_not(first))
        def _():
            dgn_ref[...] += part

    blk = pl.BlockSpec((tr, dv), lambda i, h: (i, h))
    vec = pl.BlockSpec((1, dv), lambda i, h: (0, 0))
    return pl.pallas_call(
        body, grid=(t // tr, N_HEADS),
        in_specs=[blk, blk, blk, pl.BlockSpec((tr, dv), lambda i, h: (i, gate_blk + h)), vec],
        out_specs=[blk, blk, vec], out_shape=[_sds((t, d), F32), _sds((t, d), F32), _sds((1, dv), F32)],
        compiler_params=_params("arbitrary", "arbitrary"), name="gla_post_bwd")(dog, o_f, o_b, pm, gn)


def _gla_dp(dq_f, dq_b, dk_f, dk_b, dv_f, dv_b, dgate):
    t, d = dv_f.shape
    kd = dq_f.shape[1]
    tr = _tile(t, 128)

    def body(a1, a2, b1, b2, c1, c2, g, o_ref):
        o_ref[:, 0:kd] = (a1[...] + a2[...]).astype(BF16)
        o_ref[:, kd:2 * kd] = (b1[...] + b2[...]).astype(BF16)
        o_ref[:, 2 * kd:2 * kd + d] = (c1[...] + c2[...]).astype(BF16)
        o_ref[:, 2 * kd + d:] = g[...].astype(BF16)

    sk = pl.BlockSpec((tr, kd), lambda i: (i, 0))
    sd = pl.BlockSpec((tr, d), lambda i: (i, 0))
    return pl.pallas_call(
        body, grid=(t // tr,), in_specs=[sk, sk, sk, sk, sd, sd, sd],
        out_specs=pl.BlockSpec((tr, 3 * d), lambda i: (i, 0)), out_shape=_sds((t, 3 * d), BF16),
        compiler_params=_params("parallel"), name="gla_dp")(dq_f, dq_b, dk_f, dk_b, dv_f, dv_b, dgate)


def _adamw(w, g, m, v, name):
    r, c = w.shape
    tr = _tile(r, 128)
    bc1 = 1.0 - ADAM_B1 ** ADAM_STEP
    bc2 = 1.0 - ADAM_B2 ** ADAM_STEP

    def body(w_ref, g_ref, m_ref, v_ref, d_ref, nm_ref, nv_ref):
        gv = g_ref[...]
        mn = ADAM_B1 * m_ref[...] + (1.0 - ADAM_B1) * gv
        vn = ADAM_B2 * v_ref[...] + (1.0 - ADAM_B2) * (gv * gv)
        m_hat = mn / bc1
        v_hat = vn / bc2
        d_ref[...] = -ADAM_LR * (m_hat / (jnp.sqrt(v_hat) + ADAM_EPS) + ADAM_WD * w_ref[...])
        nm_ref[...] = mn
        nv_ref[...] = vn

    blk = pl.BlockSpec((tr, c), lambda i: (i, 0))
    return pl.pallas_call(
        body, grid=(r // tr,), in_specs=[blk] * 4, out_specs=[blk] * 3, out_shape=[_sds((r, c), F32)] * 3,
        compiler_params=_params("parallel"), name=name)(w, g, m, v)


def _adamw_half(w, g, m, v, layer, half, prev, pin, name, transposed=False):
    bc1 = 1.0 - ADAM_B1 ** ADAM_STEP
    bc2 = 1.0 - ADAM_B2 ** ADAM_STEP
    n_skip = 1 + (0 if prev is None else 4)
    if transposed:
        nl, c, r = w.shape
        tr = _tile(r // 2, 128)
        nblk = r // 2 // tr
        lay = pl.BlockSpec((None, c, tr), lambda i, h_ref: (layer, 0, h_ref[0] * nblk + i))
        g_spec = pl.BlockSpec((c, tr), lambda i, h_ref: (0, i))
    else:
        nl, r, c = w.shape
        tr = _tile(r // 2, 128)
        nblk = r // 2 // tr
        lay = pl.BlockSpec((None, tr, c), lambda i, h_ref: (layer, h_ref[0] * nblk + i, 0))
        g_spec = pl.BlockSpec((tr, c), lambda i, h_ref: (i, 0))

    def body(h_ref, w_ref, g_ref, m_ref, v_ref, *rest):
        go_ref, d_ref, nm_ref, nv_ref = rest[n_skip:]
        gv = g_ref[...]
        mn = ADAM_B1 * m_ref[...] + (1.0 - ADAM_B1) * gv
        vn = ADAM_B2 * v_ref[...] + (1.0 - ADAM_B2) * (gv * gv)
        m_hat = mn / bc1
        v_hat = vn / bc2
        d_ref[...] = -ADAM_LR * (m_hat / (jnp.sqrt(v_hat) + ADAM_EPS) + ADAM_WD * w_ref[...])
        nm_ref[...] = mn
        nv_ref[...] = vn
        go_ref[...] = gv

    prev = tuple(prev or ())
    grid_spec = pltpu.PrefetchScalarGridSpec(
        num_scalar_prefetch=1, grid=(nblk,),
        in_specs=[lay, g_spec, lay, lay, pl.BlockSpec(TOKEN, lambda i, h_ref: (0, 0))] + [ANY] * len(prev),
        out_specs=[lay] * 4)
    return pl.pallas_call(
        body, grid_spec=grid_spec, out_shape=[_sds(w.shape, F32)] * 4,
        input_output_aliases={6 + k: k for k in range(len(prev))},
        compiler_params=_params("parallel"), name=name)(half, w, g, m, v, pin, *prev)


def _sum8(parts, own, me):
    _, n, _ = parts.shape

    def body(me_ref, p_ref, own_ref, o_ref):
        acc = None
        for i in range(8):
            term = jnp.where(me_ref[0] == i, own_ref[...], p_ref[i])
            acc = term if acc is None else acc + term
        o_ref[...] = acc

    grid_spec = pltpu.PrefetchScalarGridSpec(
        num_scalar_prefetch=1, grid=(1,),
        in_specs=[pl.BlockSpec((8, n, 128), lambda i, me_ref: (0, 0, 0)),
                  pl.BlockSpec((n, 128), lambda i, me_ref: (0, 0))],
        out_specs=pl.BlockSpec((n, 128), lambda i, me_ref: (0, 0)))
    return pl.pallas_call(body, grid_spec=grid_spec, out_shape=_sds((n, 128), F32), name="sum8")(me, parts, own)


def _pair_sum(ga, recv, c):
    _, _, rh, cols = ga.shape
    tr = _tile(rh, 256)

    def body(c_ref, a_ref, b_ref, o_ref):
        o_ref[...] = (a_ref[...].astype(F32) + b_ref[...].astype(F32)).astype(BF16)

    grid_spec = pltpu.PrefetchScalarGridSpec(
        num_scalar_prefetch=1, grid=(N_SHARDS, rh // tr),
        in_specs=[pl.BlockSpec((None, None, tr, cols), lambda s, i, c_ref: (s, c_ref[0], i, 0)),
                  pl.BlockSpec((None, tr, cols), lambda s, i, c_ref: (s, i, 0))],
        out_specs=pl.BlockSpec((None, tr, cols), lambda s, i, c_ref: (s, i, 0)))
    return pl.pallas_call(
        body, grid_spec=grid_spec, out_shape=_sds((N_SHARDS, rh, cols), BF16),
        compiler_params=_params("parallel", "parallel"), name="grad_pair_sum")(c, ga, recv)


def _quad_sum(pair, recv, s_me):
    _, rh, cols = pair.shape
    tr = _tile(rh, 256)

    def body(s_ref, p_ref, r1_ref, r2_ref, r3_ref, o_ref):
        o_ref[...] = ((p_ref[...].astype(F32) + r1_ref[...].astype(F32)) + r2_ref[...].astype(F32)) \
            + r3_ref[...].astype(F32)

    def blk(off):
        return pl.BlockSpec((None, tr, cols), lambda i, s_ref: ((s_ref[0] + off) % N_SHARDS, i, 0))

    grid_spec = pltpu.PrefetchScalarGridSpec(
        num_scalar_prefetch=1, grid=(rh // tr,), in_specs=[blk(0), blk(1), blk(2), blk(3)],
        out_specs=pl.BlockSpec((tr, cols), lambda i, s_ref: (i, 0)))
    return pl.pallas_call(
        body, grid_spec=grid_spec, out_shape=_sds((rh, cols), F32),
        compiler_params=_params("parallel"), name="grad_quad_sum")(s_me, pair, recv, recv, recv)


def _fill_own(w, layer, s_me):
    _, rows, cols = w.shape
    rh = rows // 2
    tr = _tile(rh, 256)
    nblk = rh // tr

    def body(s_ref, w_ref, o_ref):
        o_ref[...] = w_ref[...].astype(BF16)

    grid_spec = pltpu.PrefetchScalarGridSpec(
        num_scalar_prefetch=1, grid=(2, nblk),
        in_specs=[pl.BlockSpec((None, tr, cols), lambda h, i, s_ref: (layer, h * nblk + i, 0))],
        out_specs=pl.BlockSpec((None, None, tr, cols), lambda h, i, s_ref: (s_ref[0], h, i, 0)))
    return pl.pallas_call(
        body, grid_spec=grid_spec, out_shape=_sds((N_SHARDS, 2, rh, cols), BF16),
        compiler_params=_params("parallel", "parallel"), name="weight_fill_own")(s_me, w)


def _coords():
    return lax.axis_index("x"), lax.axis_index("y"), lax.axis_index("c")


def _other_chips(x, y):
    return [(x, 1 - y), (1 - x, y), (1 - x, 1 - y)]


def _bcast8(buf):
    n = buf.shape[0]

    def body(b_ref, o_ref, send_sems, recv_sems):
        x, y, c = _coords()
        me = 4 * x + 2 * y + c
        o_ref[me] = b_ref[...]
        copies = []
        for k in range(1, 8):
            peer = (x ^ (k >> 2), y ^ ((k >> 1) & 1), c ^ (k & 1))
            copies.append(pltpu.make_async_remote_copy(
                src_ref=b_ref, dst_ref=o_ref.at[me], send_sem=send_sems.at[k - 1], recv_sem=recv_sems.at[k - 1],
                device_id=peer, device_id_type=MESH))
        for cp in copies:
            cp.start()
        for k in range(1, 8):
            pltpu.make_async_remote_copy(
                src_ref=b_ref, dst_ref=o_ref.at[me ^ k], send_sem=send_sems.at[k - 1], recv_sem=recv_sems.at[k - 1],
                device_id=(x, y, c), device_id_type=MESH).wait_recv()
        for cp in copies:
            cp.wait_send()

    return pl.pallas_call(
        body, out_shape=_sds((8, n, 128), F32),
        in_specs=[pl.BlockSpec(memory_space=pltpu.VMEM)], out_specs=pl.BlockSpec(memory_space=pltpu.VMEM),
        scratch_shapes=[pltpu.SemaphoreType.DMA((7,)), pltpu.SemaphoreType.DMA((7,))],
        compiler_params=pltpu.CompilerParams(vmem_limit_bytes=VMEM_LIMIT), name="bcast8")(buf)


HBM = pl.BlockSpec(memory_space=pltpu.HBM)
SEM = pl.BlockSpec(memory_space=pltpu.SEMAPHORE)
SIDE = pltpu.SideEffectType.DATAFLOW_SIDE_EFFECTING
TOKEN = (8, 128)


def _in_hbm(a):
    return pltpu.with_memory_space_constraint(a, pltpu.HBM)


def _sibling_start(src, pieces, land_shape, after, name):
    n = len(pieces(None, None, None))

    def body(s_ref, land_ref, after_ref, send, recv, land_thru, token):
        x, y, c = _coords()
        for k, (a, b) in enumerate(pieces(s_ref, land_ref, c)):
            pltpu.make_async_remote_copy(
                src_ref=a, dst_ref=b, send_sem=send.at[k], recv_sem=recv.at[k], device_id=(x, y, 1 - c),
                device_id_type=MESH).start()
        token[...] = jnp.zeros_like(token)

    return pl.pallas_call(
        body, name=name,
        out_shape=(pltpu.SemaphoreType.DMA((n,)), pltpu.SemaphoreType.DMA((n,)), pltpu.HBM(land_shape, src.dtype),
                   _sds(TOKEN, F32)),
        in_specs=(HBM, HBM, ANY), out_specs=(SEM, SEM, HBM, pl.BlockSpec(memory_space=pltpu.VMEM)),
        input_output_aliases={1: 2}, compiler_params=pltpu.CompilerParams(has_side_effects=SIDE),
    )(_in_hbm(src), _in_hbm(lax.empty(land_shape, src.dtype)), after)


def _sibling_wait(send, recv, src, land, pieces, after, name):
    def body(s_ref, land_ref, send, recv, after_ref, land_out):
        x, y, c = _coords()
        for k, (a, b) in enumerate(pieces(s_ref, land_ref, c)):
            cp = pltpu.make_async_remote_copy(
                src_ref=a, dst_ref=b, send_sem=send.at[k], recv_sem=recv.at[k], device_id=(x, y, 1 - c),
                device_id_type=MESH)
            cp.wait_send()
            cp.wait_recv()

    return pl.pallas_call(
        body, name=name, out_shape=pltpu.HBM(land.shape, land.dtype), in_specs=(HBM, HBM, SEM, SEM, ANY),
        out_specs=HBM, input_output_aliases={1: 0}, compiler_params=pltpu.CompilerParams(has_side_effects=SIDE),
    )(src, land, send, recv, after)


def _swap_pieces(g_ref, land_ref, c):
    if g_ref is None:
        return [None] * N_SHARDS
    return [(g_ref.at[s, 1 - c], land_ref.at[s]) for s in range(N_SHARDS)]


def _whole_piece(r_ref, land_ref, c):
    return [(r_ref, land_ref)]


def _bcast_copies(b_ref, land_ref, send, recv):
    x, y, c = _coords()
    me = 4 * x + 2 * y + c
    return [pltpu.make_async_remote_copy(
        src_ref=b_ref, dst_ref=land_ref.at[me], send_sem=send.at[k - 1], recv_sem=recv.at[k - 1],
        device_id=(x ^ (k >> 2), y ^ ((k >> 1) & 1), c ^ (k & 1)), device_id_type=MESH) for k in range(1, 8)]


def _bcast_start(buf, after, name):
    def body(b_ref, land_ref, after_ref, send, recv, land_thru, token):
        for cp in _bcast_copies(b_ref, land_ref, send, recv):
            cp.start()
        token[...] = jnp.zeros_like(token)

    shape = (8,) + buf.shape
    return pl.pallas_call(
        body, name=name,
        out_shape=(pltpu.SemaphoreType.DMA((7,)), pltpu.SemaphoreType.DMA((7,)), pltpu.HBM(shape, buf.dtype),
                   _sds(TOKEN, F32)),
        in_specs=(HBM, HBM, ANY), out_specs=(SEM, SEM, HBM, pl.BlockSpec(memory_space=pltpu.VMEM)),
        input_output_aliases={1: 2}, compiler_params=pltpu.CompilerParams(has_side_effects=SIDE),
    )(_in_hbm(buf), _in_hbm(lax.empty(shape, buf.dtype)), after)


def _bcast_wait(send, recv, buf, land, after, name):
    def body(b_ref, land_ref, send, recv, after_ref, land_out):
        for cp in _bcast_copies(b_ref, land_ref, send, recv):
            cp.wait_send()
            cp.wait_recv()

    return pl.pallas_call(
        body, name=name, out_shape=pltpu.HBM(land.shape, land.dtype), in_specs=(HBM, HBM, SEM, SEM, ANY),
        out_specs=HBM, input_output_aliases={1: 0}, compiler_params=pltpu.CompilerParams(has_side_effects=SIDE),
    )(buf, land, send, recv, after)


def _scatter_start(part, name):
    def body(p_ref, land_ref, send, recv, p_thru, land_thru, token):
        x, y, c = _coords()
        s_me = 2 * x + y
        for j, (px, py) in enumerate(_other_chips(x, y)):
            pltpu.make_async_remote_copy(
                src_ref=p_ref.at[2 * px + py], dst_ref=land_ref.at[s_me], send_sem=send.at[j], recv_sem=recv.at[j],
                device_id=(px, py, c), device_id_type=MESH).start()
        token[...] = jnp.zeros_like(token)

    buf = pltpu.HBM(part.shape, part.dtype)
    return pl.pallas_call(
        body, name=name,
        out_shape=(pltpu.SemaphoreType.DMA((3,)), pltpu.SemaphoreType.DMA((3,)), buf, buf, _sds(TOKEN, F32)),
        in_specs=(HBM, HBM), out_specs=(SEM, SEM, HBM, HBM, pl.BlockSpec(memory_space=pltpu.VMEM)),
        input_output_aliases={0: 2, 1: 3}, compiler_params=pltpu.CompilerParams(has_side_effects=SIDE),
    )(_in_hbm(part), _in_hbm(lax.empty(part.shape, part.dtype)))


def _scatter_wait(send, recv, part, land, after, name):
    def body(p_ref, land_ref, send, recv, after_ref, p_out, land_out):
        x, y, c = _coords()
        for j, (px, py) in enumerate(_other_chips(x, y)):
            cp = pltpu.make_async_remote_copy(
                src_ref=p_ref.at[2 * px + py], dst_ref=land_ref.at[2 * px + py], send_sem=send.at[j],
                recv_sem=recv.at[j], device_id=(px, py, c), device_id_type=MESH)
            cp.wait_send()
            cp.wait_recv()

    buf = pltpu.HBM(part.shape, part.dtype)
    return pl.pallas_call(
        body, name=name, out_shape=(buf, buf), in_specs=(HBM, HBM, SEM, SEM, ANY), out_specs=(HBM, HBM),
        input_output_aliases={0: 0, 1: 1}, compiler_params=pltpu.CompilerParams(has_side_effects=SIDE),
    )(part, land, send, recv, after)


def _gather_start(bufs, after):
    n = len(bufs)

    def body(*refs):
        ins = refs[:n]
        sems = refs[n + 1:3 * n + 1]
        token = refs[4 * n + 1]
        x, y, c = _coords()
        s_me = 2 * x + y
        for i in range(n):
            for j, (px, py) in enumerate(_other_chips(x, y)):
                pltpu.make_async_remote_copy(
                    src_ref=ins[i].at[s_me, c], dst_ref=ins[i].at[s_me, c], send_sem=sems[2 * i].at[j],
                    recv_sem=sems[2 * i + 1].at[j], device_id=(px, py, c), device_id_type=MESH).start()
        token[...] = jnp.zeros_like(token)

    res = pl.pallas_call(
        body, name="gather_start",
        out_shape=(*[pltpu.SemaphoreType.DMA((3,))] * (2 * n), *[pltpu.HBM(b.shape, b.dtype) for b in bufs],
                   _sds(TOKEN, F32)),
        in_specs=(*[HBM] * n, ANY), out_specs=(*[SEM] * (2 * n), *[HBM] * n, pl.BlockSpec(memory_space=pltpu.VMEM)),
        input_output_aliases={i: 2 * n + i for i in range(n)},
        compiler_params=pltpu.CompilerParams(has_side_effects=SIDE),
    )(*[_in_hbm(b) for b in bufs], after)
    return [(res[2 * i], res[2 * i + 1]) for i in range(n)], list(res[2 * n:3 * n]), res[3 * n]


def _gather_wait(send, recv, buf, after, name):
    def body(b_ref, send, recv, after_ref, b_out):
        x, y, c = _coords()
        s_me = 2 * x + y
        for j, (px, py) in enumerate(_other_chips(x, y)):
            cp = pltpu.make_async_remote_copy(
                src_ref=b_ref.at[s_me, c], dst_ref=b_ref.at[2 * px + py, c], send_sem=send.at[j], recv_sem=recv.at[j],
                device_id=(px, py, c), device_id_type=MESH)
            cp.wait_send()
            cp.wait_recv()

    return pl.pallas_call(
        body, name=name, out_shape=pltpu.HBM(buf.shape, buf.dtype), in_specs=(HBM, SEM, SEM, ANY), out_specs=HBM,
        input_output_aliases={0: 0}, compiler_params=pltpu.CompilerParams(has_side_effects=SIDE),
    )(buf, send, recv, after)


def _pass_on_halves(bufs):
    n = len(bufs)

    def body(*refs):
        ins, outs = refs[:n], refs[n:2 * n]
        send, recv = refs[2 * n:]
        x, y, c = _coords()
        copies = []
        for i in range(n):
            for j, (px, py) in enumerate(_other_chips(x, y)):
                cp = pltpu.make_async_remote_copy(
                    src_ref=ins[i].at[2 * px + py, c], dst_ref=outs[i].at[2 * px + py, c], send_sem=send.at[i, j],
                    recv_sem=recv.at[i, j], device_id=(x, y, 1 - c), device_id_type=MESH)
                cp.start()
                copies.append(cp)
        for i in range(n):
            for j, (px, py) in enumerate(_other_chips(x, y)):
                other = outs[i].at[2 * px + py, 1 - c]
                pltpu.make_async_remote_copy(
                    src_ref=other, dst_ref=other, send_sem=send.at[i, j], recv_sem=recv.at[i, j],
                    device_id=(x, y, c), device_id_type=MESH).wait_recv()
        for cp in copies:
            cp.wait_send()

    return pl.pallas_call(
        body, out_shape=[_sds(b.shape, b.dtype) for b in bufs],
        in_specs=[ANY] * n, out_specs=[ANY] * n, input_output_aliases={i: i for i in range(n)},
        scratch_shapes=[pltpu.SemaphoreType.DMA((n, 3)), pltpu.SemaphoreType.DMA((n, 3))],
        name="gather_pass_on")(*bufs)


def _to_rows(vec):
    n = -(-vec.shape[0] // 1024) * 1024
    return jnp.pad(vec, (0, n - vec.shape[0])).reshape(-1, 128)


def kernel(x, norm_mix, norm_mlp, norm_final, pool_w, pool_scale, gla_w_in, gla_w_up_f, gla_b_up_f, gla_w_up_b, gla_b_up_b, gla_g_norm, gla_w_out, mlp_w_in, mlp_w_out, loss_target, m_norm_mix, m_norm_mlp, m_norm_final, m_pool_w, m_pool_scale, m_gla_w_in, m_gla_w_up_f, m_gla_b_up_f, m_gla_w_up_b, m_gla_b_up_b, m_gla_g_norm, m_gla_w_out, m_mlp_w_in, m_mlp_w_out, v_norm_mix, v_norm_mlp, v_norm_final, v_pool_w, v_pool_scale, v_gla_w_in, v_gla_w_up_f, v_gla_b_up_f, v_gla_w_up_b, v_gla_b_up_b, v_gla_g_norm, v_gla_w_out, v_mlp_w_in, v_mlp_w_out):
    nb, seq, d = x.shape
    t = nb * seq
    dg = d // N_GROUPS
    kd = d // 2
    dv = d // N_HEADS
    pw = gla_w_in.shape[2]
    f4 = mlp_w_in.shape[2]
    dff = N_SHARDS * f4
    cx, cy, cc = _coords()
    s_me = 2 * cx + cy
    c_arr = jnp.reshape(cc, (1,)).astype(jnp.int32)
    s_arr = jnp.reshape(s_me, (1,)).astype(jnp.int32)

    xf = x.reshape(t, d)
    tgt = loss_target.reshape(t, d)

    ks = kd // N_SHARDS
    small = jnp.concatenate([gla_w_up_f[0].reshape(-1), gla_w_up_b[0].reshape(-1), gla_b_up_f[0], gla_b_up_b[0],
                             gla_g_norm[0]])
    small_all = _bcast8(_to_rows(small))
    pool_rows = N_GROUPS * (dg // N_SHARDS)
    gsems, gbufs, gtok = _gather_start([
        _fill_own(pool_w.reshape(1, pool_rows, dg), 0, s_arr), _fill_own(mlp_w_in, 0, s_arr),
        _fill_own(mlp_w_out, 0, s_arr), _fill_own(gla_w_in, 0, s_arr), _fill_own(gla_w_out, 0, s_arr),
        _fill_own(mlp_w_in, 1, s_arr), _fill_own(mlp_w_out, 1, s_arr)], small_all)
    small_all = small_all[::2].reshape(N_SHARDS, -1)

    def weight(i, after, name):
        buf = _gather_wait(gsems[i][0], gsems[i][1], gbufs[i], after, "gather_wait_" + name)
        return _pass_on_halves([buf])[0]

    def pinned(vec, tok):
        return vec + tok[0:1, 0:1]

    o = 0
    wuf = jnp.transpose(small_all[:, o:o + GATE_RANK * ks].reshape(N_SHARDS, GATE_RANK, ks), (1, 0, 2)).reshape(GATE_RANK, kd)
    o += GATE_RANK * ks
    wub = jnp.transpose(small_all[:, o:o + GATE_RANK * ks].reshape(N_SHARDS, GATE_RANK, ks), (1, 0, 2)).reshape(GATE_RANK, kd)
    o += GATE_RANK * ks
    buf = small_all[:, o:o + ks].reshape(1, kd)
    o += ks
    bub = small_all[:, o:o + ks].reshape(1, kd)
    o += ks
    gn = small_all[:, o:o + dv // N_SHARDS].reshape(1, dv)
    w_up = jnp.zeros((R_PAD, 2 * kd), F32).at[:GATE_RANK, :kd].set(wuf).at[GATE_RANK:2 * GATE_RANK, kd:].set(wub)
    w_up = w_up.astype(BF16)
    b_up = jnp.concatenate([buf, bub], axis=1)

    hn0 = _rmsnorm_fwd(xf, pinned(norm_mix[0:1], gtok), F32, "norm_mix0")
    dm = _pool_apply(hn0, seq, False, "pool_diff")
    wp = weight(0, dm, "pool_w").reshape(N_SHARDS, N_GROUPS, dg // N_SHARDS, dg)
    h1, ypre = _pool_mm_fwd(dm, wp, xf, pool_scale)
    hn1 = _rmsnorm_fwd(h1, norm_mlp[0:1], BF16, "norm_mlp0")
    w1g = [weight(1, hn1, "mlp_w_in0").reshape(N_SHARDS, d, f4), None]
    w2g = [None, None]
    h2, r0, u0, w2g[0] = _mlp_fwd(h1, hn1, w1g[0], lambda u: weight(2, u, "mlp_w_out0").reshape(dff, d))
    hn2 = _rmsnorm_fwd(h2, norm_mix[1:2], BF16, "norm_mix1")
    win = jnp.transpose(weight(3, hn2, "gla_w_in").reshape(N_SHARDS, d, pw), (1, 0, 2)).reshape(d, N_SHARDS * pw)
    w_main = win[:, :3 * d]
    w_r = jnp.pad(win[:, 3 * d:], ((0, 0), (0, R_PAD - 2 * GATE_RANK)))
    (pm,) = _mm(hn2, w_main, "nn", out_dtypes=[F32], name="gla_proj")
    (pr,) = _mm(hn2, w_r, "nn", out_dtypes=[BF16], name="gla_proj_r")
    pr = pr.at[:, ONES_COL].set(1.0)

    def log_decay(acc, bv):
        z = acc + bv
        return ((jnp.minimum(z, 0.0) - jnp.log(1.0 + jnp.exp(-jnp.abs(z)))) / GATE_TAU,)

    tm_, tn_ = _tile(t, 1024), _tile(kd, 1024)
    (la,) = _matmul(
        pr, w_up, dims=_NN, grid=(t // tm_, 2 * kd // tn_, 1),
        a_spec=pl.BlockSpec((tm_, R_PAD), lambda i, j, kk: (i, 0)),
        b_spec=pl.BlockSpec((R_PAD, tn_), lambda i, j, kk: (0, j)), acc_shape=(tm_, tn_),
        out_shapes=[_sds((t, 2 * kd), F32)], out_specs=[pl.BlockSpec((tm_, tn_), lambda i, j, kk: (i, j))],
        epilogue=log_decay, extras=(b_up,), extra_specs=[pl.BlockSpec((1, tn_), lambda i, j, kk: (0, j))],
        name="gla_gate_fwd")
    o_f, o_b, st_f, st_b = _gla_fwd(pm, la, seq)
    og = _gla_post_fwd(o_f, o_b, pm, gn)
    wout = weight(4, og, "gla_w_out").reshape(d, d)
    (h3,) = _mm(og, wout, "nn", out_dtypes=[F32], epilogue=lambda acc, hv: (hv + acc,), extras=(h2,), name="gla_out")
    hn3 = _rmsnorm_fwd(h3, norm_mlp[1:2], BF16, "norm_mlp1")
    w1g[1] = weight(5, hn3, "mlp_w_in1").reshape(N_SHARDS, d, f4)
    h4, r1, u1, w2g[1] = _mlp_fwd(h3, hn3, w1g[1], lambda u: weight(6, u, "mlp_w_out1").reshape(dff, d))

    chains = []
    last_tok = [gtok]

    def swap_begin(g, name):
        g5 = g.reshape(N_SHARDS, 2, g.shape[1] // 2, g.shape[2])
        send, recv, land, tok = _sibling_start(g5, _swap_pieces, (N_SHARDS,) + g5.shape[2:], last_tok[0],
                                               "grad_swap_start_" + name)
        last_tok[0] = tok
        return (name, send, recv, g5, land), tok

    def swap_end(state, after):
        name, send, recv, g5, land = state
        land = _sibling_wait(send, recv, g5, land, _swap_pieces, after, "grad_swap_wait_" + name)
        pair = _pair_sum(g5, land, c_arr)
        send, recv, pair, land, tok = _scatter_start(pair, "grad_scatter_start_" + name)
        chains.append((name, send, recv, pair, land))
        last_tok[0] = tok
        return tok

    def mlp_backward(dhb, hn, r, u, w1, w2, name):
        steps = _mlp_bwd(dhb, hn, r, u, w1, w2)
        state, tok = swap_begin(next(steps), "mlp_w_out" + name)
        dw1 = steps.send(tok)
        swap_end(state, dw1)
        state, tok = swap_begin(dw1, "mlp_w_in" + name)
        dhn = steps.send(tok)
        return dhn, swap_end(state, dhn)

    loss_part, dh4, dh4b, dg_final = _final_bwd(h4, norm_final.reshape(1, d), tgt)
    dhn3, tok = mlp_backward(dh4b, hn3, r1, u1, w1g[1], w2g[1], "1")
    dh3, dh3b, dg_mlp1 = _rmsnorm_bwd(h3, pinned(norm_mlp[1:2], tok), dhn3, dh4, "norm_mlp1_bwd")

    (dog,) = _mm(dh3b, wout, "nt", out_dtypes=[F32], name="gla_out_bwd_x")
    (dwout,) = _mm(og, dh3b, "tn", out_dtypes=[BF16], name="gla_out_bwd_w")
    state, tok = swap_begin(dwout.reshape(N_SHARDS, d // N_SHARDS, d), "gla_w_out")
    do, dgate, dg_gn = _gla_post_bwd(dog, o_f, o_b, pm, pinned(gn, tok))
    dq_f, dk_f, dv_f, dla_f, dq_b, dk_b, dv_b, dla_b = _gla_bwd(pm, la, do, st_f, st_b, seq)
    b_up_p = pinned(b_up, swap_end(state, dq_f))
    nkb = kd // tn_

    def gate_bwd(acc, bv, dl_f, dl_b):
        z = acc + bv
        dl = jnp.where(pl.program_id(1) < nkb, dl_f, dl_b)
        return (dl * (1.0 / GATE_TAU) / (1.0 + jnp.exp(z)),)

    (dz,) = _matmul(
        pr, w_up, dims=_NN, grid=(t // tm_, 2 * kd // tn_, 1),
        a_spec=pl.BlockSpec((tm_, R_PAD), lambda i, j, kk: (i, 0)),
        b_spec=pl.BlockSpec((R_PAD, tn_), lambda i, j, kk: (0, j)), acc_shape=(tm_, tn_),
        out_shapes=[_sds((t, 2 * kd), BF16)], out_specs=[pl.BlockSpec((tm_, tn_), lambda i, j, kk: (i, j))],
        epilogue=gate_bwd, extras=(b_up_p, dla_f, dla_b),
        extra_specs=[pl.BlockSpec((1, tn_), lambda i, j, kk: (0, j)),
                     pl.BlockSpec((tm_, tn_), lambda i, j, kk: (i, jnp.minimum(j, nkb - 1))),
                     pl.BlockSpec((tm_, tn_), lambda i, j, kk: (i, jnp.maximum(j - nkb, 0)))],
        name="gla_gate_bwd")
    (dpr,) = _mm(dz, w_up, "nt", out_dtypes=[BF16], name="gla_gate_bwd_r")
    (dw_up,) = _mm(pr, dz, "tn", out_dtypes=[F32], name="gla_gate_bwd_w")
    dp = _gla_dp(dq_f, dq_b, dk_f, dk_b, dv_f, dv_b, dgate)
    (dw_main,) = _mm(hn2, dp, "tn", out_dtypes=[BF16], name="gla_proj_bwd_w")
    (dw_r,) = _mm(hn2, dpr, "tn", out_dtypes=[BF16], name="gla_proj_bwd_wr")
    dwin = jnp.concatenate([dw_main, dw_r[:, :2 * GATE_RANK]], axis=1)
    state, tok = swap_begin(jnp.transpose(dwin.reshape(d, N_SHARDS, pw), (1, 0, 2)), "gla_w_in")
    (dhn2_r,) = _mm(dpr, w_r, "nt", out_dtypes=[F32], name="gla_proj_bwd_xr")
    (dhn2,) = _mm(dp, w_main, "nt", out_dtypes=[F32], epilogue=lambda acc, e: (acc + e,), extras=(dhn2_r,),
                  name="gla_proj_bwd_x", tk=2048, pin=tok)
    tok = swap_end(state, dhn2)
    dh2, dh2b, dg_mix1 = _rmsnorm_bwd(h2, pinned(norm_mix[1:2], tok), dhn2, dh3, "norm_mix1_bwd")

    dhn1, tok = mlp_backward(dh2b, hn1, r0, u0, w1g[0], w2g[0], "0")
    dh1, _, dg_mlp0 = _rmsnorm_bwd(h1, pinned(norm_mlp[0:1], tok), dhn1, dh2, "norm_mlp0_bwd")

    dys, dg_pscale = _pool_scale_bwd(dh1, ypre, pool_scale)
    dwp = _pool_mm_bwd_w(dm, dys)
    state, tok = swap_begin(dwp.reshape(N_SHARDS, pool_rows, dg), "pool_w")
    dd = _pool_mm_bwd_x(dys, wp, tok)
    tok = swap_end(state, dd)
    dhn0 = _pool_apply(dd, seq, True, "pool_diff_bwd")
    dx, _, dg_mix0 = _rmsnorm_bwd(xf, pinned(norm_mix[0:1], tok), dhn0, dh1, "norm_mix0_bwd")

    dwuf, dwub = dw_up[:GATE_RANK, :kd], dw_up[GATE_RANK:2 * GATE_RANK, kd:]
    dbuf, dbub = dw_up[ONES_COL, :kd], dw_up[ONES_COL, kd:]
    pieces = [jnp.concatenate([dg_mix0, dg_mix1], 0), jnp.concatenate([dg_mlp0, dg_mlp1], 0), dg_final, dg_pscale,
              dwuf, dwub, dbuf, dbub, dg_gn]
    sizes = [p.size for p in pieces]
    packed = _to_rows(jnp.concatenate([p.reshape(-1) for p in pieces]))
    bsend, brecv, bland, _ = _bcast_start(packed, dx, "small_grads_start")

    loss = lax.psum(loss_part[0, 0], ("x", "y", "c"))

    weights = [norm_mix, norm_mlp, norm_final, pool_w, pool_scale, gla_w_in, gla_w_up_f, gla_b_up_f, gla_w_up_b,
               gla_b_up_b, gla_g_norm, gla_w_out, mlp_w_in, mlp_w_out]
    moms = [m_norm_mix, m_norm_mlp, m_norm_final, m_pool_w, m_pool_scale, m_gla_w_in, m_gla_w_up_f, m_gla_b_up_f,
            m_gla_w_up_b, m_gla_b_up_b, m_gla_g_norm, m_gla_w_out, m_mlp_w_in, m_mlp_w_out]
    vels = [v_norm_mix, v_norm_mlp, v_norm_final, v_pool_w, v_pool_scale, v_gla_w_in, v_gla_w_up_f, v_gla_b_up_f,
            v_gla_w_up_b, v_gla_b_up_b, v_gla_g_norm, v_gla_w_out, v_mlp_w_in, v_mlp_w_out]
    names = ["norm_mix", "norm_mlp", "norm_final", "pool_w", "pool_scale", "gla_w_in", "gla_w_up_f", "gla_b_up_f",
             "gla_w_up_b", "gla_b_up_b", "gla_g_norm", "gla_w_out", "mlp_w_in", "mlp_w_out"]
    index = {nm: k for k, nm in enumerate(names)}
    results = {}

    stacked = {"mlp_w_out1": ("mlp_w_out", 1), "mlp_w_in1": ("mlp_w_in", 1), "gla_w_out": ("gla_w_out", 0),
               "gla_w_in": ("gla_w_in", 0), "mlp_w_out0": ("mlp_w_out", 0), "mlp_w_in0": ("mlp_w_in", 0),
               "pool_w": ("pool_w", 0)}
    oc_arr = 1 - c_arr
    after = dx
    for name, send, recv, pair, land in chains:
        pair, land = _scatter_wait(send, recv, pair, land, after, "grad_scatter_wait_" + name)
        mine = _quad_sum(pair, land, s_arr)
        jsend, jrecv, jland, jtok = _sibling_start(mine, _whole_piece, mine.shape, after, "grad_join_start_" + name)
        nm, layer = stacked[name]
        w, m, v = weights[index[nm]], moms[index[nm]], vels[index[nm]]
        rows, cols = 2 * mine.shape[0], mine.shape[1]
        shp = (w.size // (rows * cols), rows, cols)
        w, m, v = w.reshape(shp), m.reshape(shp), v.reshape(shp)
        tr_ = nm == "gla_w_in"
        flip = (lambda a: jnp.swapaxes(a, -1, -2)) if tr_ else (lambda a: a)
        w, m, v = flip(w), flip(m), flip(v)
        res = _adamw_half(w, flip(mine), m, v, layer, c_arr, results.get(nm), jtok, "adamw_mine_" + name, tr_)
        theirs = _sibling_wait(jsend, jrecv, mine, jland, _whole_piece, res[1], "grad_join_wait_" + name)
        res = _adamw_half(w, flip(theirs), m, v, layer, oc_arr, res, jtok, "adamw_theirs_" + name, tr_)
        results[nm] = [flip(a) for a in res]
        after = res[1]

    bland = _bcast_wait(bsend, brecv, packed, bland, after, "small_grads_wait")
    me_arr = jnp.reshape(4 * cx + 2 * cy + cc, (1,)).astype(jnp.int32)
    summed = _sum8(bland, packed, me_arr).reshape(-1)
    outs_small, o = [], 0
    for p, n in zip(pieces, sizes):
        outs_small.append(summed[o:o + n].reshape(p.shape))
        o += n
    g_nmix, g_nmlp, g_nfinal, g_pscale, g_wuf, g_wub, g_buf, g_bub, g_gn = outs_small
    g_wuf = lax.dynamic_slice_in_dim(g_wuf, s_me * ks, ks, axis=1)
    g_wub = lax.dynamic_slice_in_dim(g_wub, s_me * ks, ks, axis=1)
    g_buf = lax.dynamic_slice_in_dim(g_buf, s_me * ks, ks, axis=0)
    g_bub = lax.dynamic_slice_in_dim(g_bub, s_me * ks, ks, axis=0)
    g_gn = lax.dynamic_slice_in_dim(g_gn.reshape(dv), s_me * (dv // N_SHARDS), dv // N_SHARDS, axis=0)
    small_grads = {"norm_mix": g_nmix, "norm_mlp": g_nmlp, "norm_final": g_nfinal, "pool_scale": g_pscale,
                   "gla_w_up_f": g_wuf, "gla_b_up_f": g_buf, "gla_w_up_b": g_wub, "gla_b_up_b": g_bub,
                   "gla_g_norm": g_gn}
    for nm, g in small_grads.items():
        w, m, v = weights[index[nm]], moms[index[nm]], vels[index[nm]]
        cols = w.shape[-1]
        shp = (w.size // cols, cols)
        dl, mn, vn = _adamw(w.reshape(shp), g.reshape(shp), m.reshape(shp), v.reshape(shp), "adamw_" + nm)
        results[nm] = (g, dl, mn, vn)

    outs = [[results[nm][k].reshape(weights[index[nm]].shape) for nm in names] for k in range(4)]
    return (loss, dx.reshape(x.shape), *outs[0], *outs[1], *outs[2], *outs[3])
```

```python
import jax
import jax.numpy as jnp
from jax import lax
from jax.experimental import pallas as pl
from jax.experimental.pallas import tpu as pltpu

F32 = jnp.float32
BF16 = jnp.bfloat16

N_HEADS = 4
N_GROUPS = 4
POOL_HALF = (1, 2, 4, 8)
GATE_RANK = 16
GATE_TAU = 16.0
CHUNK = 64
STEP = 2 * CHUNK
EPS = 1e-6
N_SHARDS = 4
R_PAD = 128
ONES_COL = 2 * GATE_RANK

ADAM_LR = 0.001
ADAM_B1 = 0.9
ADAM_B2 = 0.999
ADAM_EPS = 1e-08
ADAM_WD = 0.01
ADAM_STEP = 10

_NN = (((1,), (0,)), ((), ()))
_NT = (((1,), (1,)), ((), ()))
_TN = (((0,), (0,)), ((), ()))

VMEM_LIMIT = 56 * 1024 * 1024
MESH = pl.DeviceIdType.MESH
ANY = pl.BlockSpec(memory_space=pl.ANY)


def _tile(dim, pref):
    return pref if dim % pref == 0 else dim


def _params(*sem):
    return pltpu.CompilerParams(dimension_semantics=sem, vmem_limit_bytes=VMEM_LIMIT)


def _sds(shape, dtype):
    return jax.ShapeDtypeStruct(shape, dtype)


def _matmul(a, b, *, dims, grid, a_spec, b_spec, acc_shape, out_shapes, out_specs, epilogue,
            extras=(), extra_specs=(), name, pin=None):
    nk = grid[2]
    n_extra = len(extras)
    n_out = len(out_shapes)
    pins = () if pin is None else (pin,)

    def body(a_ref, b_ref, *rest):
        extra_refs = rest[:n_extra]
        rest = rest[n_extra + len(pins):]
        out_refs = rest[:n_out]
        acc_ref = rest[n_out]
        kk = pl.program_id(2)

        def part():
            return lax.dot_general(a_ref[...], b_ref[...], dims, preferred_element_type=F32)

        def finish(acc):
            outs = epilogue(acc, *[r[...] for r in extra_refs])
            for o_ref, o in zip(out_refs, outs):
                o_ref[...] = o.astype(o_ref.dtype)

        if nk == 1:
            finish(part())
        else:
            @pl.when(kk == 0)
            def _():
                acc_ref[...] = part()

            @pl.when((kk > 0) & (kk < nk - 1))
            def _():
                acc_ref[...] += part()

            @pl.when(kk == nk - 1)
            def _():
                finish(acc_ref[...] + part())

    return pl.pallas_call(
        body,
        grid=grid,
        in_specs=[a_spec, b_spec, *extra_specs, *[pl.BlockSpec((8, 128), lambda i, j, kk: (0, 0)) for _ in pins]],
        out_specs=list(out_specs),
        out_shape=list(out_shapes),
        scratch_shapes=[pltpu.VMEM(acc_shape if nk > 1 else (8, 128), F32)],
        compiler_params=_params("parallel", "parallel", "arbitrary"),
        name=name,
    )(a, b, *extras, *pins)


def _mm(a, b, kind, *, out_dtypes, epilogue=None, extras=(), name, tm=1024, tn=1024, tk=4096, pin=None):
    if kind == "nn":
        (m, k), n = a.shape, b.shape[1]
    elif kind == "nt":
        (m, k), n = a.shape, b.shape[0]
    else:
        (k, m), n = a.shape, b.shape[1]
    tm, tn, tk = _tile(m, tm), _tile(n, tn), _tile(k, tk)
    if kind == "nn":
        a_spec = pl.BlockSpec((tm, tk), lambda i, j, kk: (i, kk))
        b_spec = pl.BlockSpec((tk, tn), lambda i, j, kk: (kk, j))
        dims = _NN
    elif kind == "nt":
        a_spec = pl.BlockSpec((tm, tk), lambda i, j, kk: (i, kk))
        b_spec = pl.BlockSpec((tn, tk), lambda i, j, kk: (j, kk))
        dims = _NT
    else:
        a_spec = pl.BlockSpec((tk, tm), lambda i, j, kk: (kk, i))
        b_spec = pl.BlockSpec((tk, tn), lambda i, j, kk: (kk, j))
        dims = _TN
    o_spec = pl.BlockSpec((tm, tn), lambda i, j, kk: (i, j))
    if epilogue is None:
        epilogue = lambda acc, *e: tuple(acc for _ in out_dtypes)
    return _matmul(
        a, b, dims=dims, grid=(m // tm, n // tn, k // tk), a_spec=a_spec, b_spec=b_spec, acc_shape=(tm, tn),
        out_shapes=[_sds((m, n), d) for d in out_dtypes], out_specs=[o_spec for _ in out_dtypes],
        epilogue=epilogue, extras=extras, extra_specs=[o_spec for _ in extras], name=name, pin=pin)


def _rmsnorm_fwd(h, g, out_dtype, name):
    t, d = h.shape
    tr = _tile(t, 256)

    def body(h_ref, g_ref, o_ref):
        x = h_ref[...]
        r = lax.rsqrt(jnp.mean(x * x, axis=-1, keepdims=True) + EPS)
        o_ref[...] = (x * r * g_ref[...]).astype(o_ref.dtype)

    return pl.pallas_call(
        body, grid=(t // tr,),
        in_specs=[pl.BlockSpec((tr, d), lambda i: (i, 0)), pl.BlockSpec((1, d), lambda i: (0, 0))],
        out_specs=pl.BlockSpec((tr, d), lambda i: (i, 0)),
        out_shape=_sds((t, d), out_dtype), compiler_params=_params("parallel"), name=name)(h, g)


def _rmsnorm_bwd(h, g, dy, resid, name):
    t, d = h.shape
    tr = _tile(t, 128)

    def body(h_ref, g_ref, dy_ref, res_ref, dh_ref, dhb_ref, dg_ref):
        x = h_ref[...]
        r = lax.rsqrt(jnp.mean(x * x, axis=-1, keepdims=True) + EPS)
        xn = x * r
        dyv = dy_ref[...]
        gdy = dyv * g_ref[...]
        dh = res_ref[...] + r * (gdy - xn * jnp.mean(gdy * xn, axis=-1, keepdims=True))
        dh_ref[...] = dh
        dhb_ref[...] = dh.astype(BF16)
        part = jnp.sum(dyv * xn, axis=0, keepdims=True)

        @pl.when(pl.program_id(0) == 0)
        def _():
            dg_ref[...] = part

        @pl.when(pl.program_id(0) > 0)
        def _():
            dg_ref[...] += part

    row = pl.BlockSpec((tr, d), lambda i: (i, 0))
    vec = pl.BlockSpec((1, d), lambda i: (0, 0))
    return pl.pallas_call(
        body, grid=(t // tr,), in_specs=[row, vec, row, row], out_specs=[row, row, vec],
        out_shape=[_sds((t, d), F32), _sds((t, d), BF16), _sds((1, d), F32)],
        compiler_params=_params("arbitrary"), name=name)(h, g, dy, resid)


def _final_bwd(h, g, tgt):
    t, d = h.shape
    tr = _tile(t, 128)

    def body(h_ref, g_ref, t_ref, loss_ref, dh_ref, dhb_ref, dg_ref):
        x = h_ref[...]
        r = lax.rsqrt(jnp.mean(x * x, axis=-1, keepdims=True) + EPS)
        xn = x * r
        gv = g_ref[...]
        e = xn * gv - t_ref[...]
        lpart = jnp.full((1, 128), 0.5 * jnp.sum(jnp.mean(e * e, axis=-1, keepdims=True)), F32)
        dyv = e * (1.0 / d)
        gdy = dyv * gv
        dh = r * (gdy - xn * jnp.mean(gdy * xn, axis=-1, keepdims=True))
        dh_ref[...] = dh
        dhb_ref[...] = dh.astype(BF16)
        part = jnp.sum(dyv * xn, axis=0, keepdims=True)

        @pl.when(pl.program_id(0) == 0)
        def _():
            dg_ref[...] = part
            loss_ref[...] = lpart

        @pl.when(pl.program_id(0) > 0)
        def _():
            dg_ref[...] += part
            loss_ref[...] += lpart

    row = pl.BlockSpec((tr, d), lambda i: (i, 0))
    vec = pl.BlockSpec((1, d), lambda i: (0, 0))
    return pl.pallas_call(
        body, grid=(t // tr,), in_specs=[row, vec, row],
        out_specs=[pl.BlockSpec((1, 128), lambda i: (0, 0)), row, row, vec],
        out_shape=[_sds((1, 128), F32), _sds((t, d), F32), _sds((t, d), BF16), _sds((1, d), F32)],
        compiler_params=_params("arbitrary"), name="final_loss_bwd")(h, g, tgt)


def _shift_rows(x, s, row):
    n = x.shape[0]
    y = pltpu.roll(x, (-s) % n, 0)
    return jnp.where((row + s >= 0) & (row + s < n), y, 0.0)


def _span_sum(x, start, length, row):
    if start >= 0:
        y, step = _shift_rows(x, start, row) if start else x, 1
    else:
        last = start + length - 1
        assert last <= 0
        y, step = _shift_rows(x, last, row) if last else x, -1
    n = 1
    while n < length:
        y = y + _shift_rows(y, step * n, row)
        n *= 2
    return y


def _pool_apply(x, seq, transpose, name):
    t, d = x.shape
    dg = d // N_GROUPS
    tc = _tile(dg, 256)
    nblk = dg // tc

    def body(x_ref, o_ref):
        grp = pl.program_id(1)
        row = lax.broadcasted_iota(jnp.int32, (seq, tc), 0)
        for gi, hw in enumerate(POOL_HALF):
            @pl.when(grp == gi)
            def _(hw=hw):
                xv = x_ref[...]
                cnt = (jnp.minimum(row + hw, seq) - jnp.maximum(row - hw, 0)).astype(F32)
                if not transpose:
                    w = _span_sum(xv, 0, hw, row) + _span_sum(xv, -hw, hw, row)
                    o_ref[...] = (w / cnt - xv).astype(o_ref.dtype)
                else:
                    u = xv / cnt
                    w = _span_sum(u, 1, hw, row) + _span_sum(u, -(hw - 1), hw, row)
                    o_ref[...] = (w - xv).astype(o_ref.dtype)

    spec = pl.BlockSpec((seq, tc), lambda b, g, j: (b, g * nblk + j))
    return pl.pallas_call(
        body, grid=(t // seq, N_GROUPS, nblk), in_specs=[spec], out_specs=spec,
        out_shape=_sds((t, d), F32 if transpose else BF16),
        compiler_params=_params("parallel", "parallel", "parallel"), name=name)(x)


def _pool_mm_fwd(dm, wp, x, scale):
    t, d = dm.shape
    dg = d // N_GROUPS
    rs = dg // N_SHARDS
    tm = _tile(t, 1024)
    o_spec = pl.BlockSpec((tm, dg), lambda i, j, kk: (i, j))
    return _matmul(
        dm, wp, dims=_NN, grid=(t // tm, N_GROUPS, N_SHARDS),
        a_spec=pl.BlockSpec((tm, rs), lambda i, j, kk: (i, j * N_SHARDS + kk)),
        b_spec=pl.BlockSpec((None, None, rs, dg), lambda i, j, kk: (kk, j, 0, 0)),
        acc_shape=(tm, dg), out_shapes=[_sds((t, d), F32), _sds((t, d), F32)], out_specs=[o_spec, o_spec],
        epilogue=lambda acc, xv, sc: (xv + acc * sc, acc),
        extras=(x, scale), extra_specs=[o_spec, pl.BlockSpec((1, dg), lambda i, j, kk: (0, j))], name="pool_mm_fwd")


def _pool_scale_bwd(dh, ypre, scale):
    t, d = dh.shape
    tr = _tile(t, 256)

    def body(dh_ref, y_ref, s_ref, o_ref, ds_ref):
        g = dh_ref[...]
        o_ref[...] = (g * s_ref[...]).astype(BF16)
        part = jnp.sum(g * y_ref[...], axis=0, keepdims=True)

        @pl.when(pl.program_id(0) == 0)
        def _():
            ds_ref[...] = part

        @pl.when(pl.program_id(0) > 0)
        def _():
            ds_ref[...] += part

    row = pl.BlockSpec((tr, d), lambda i: (i, 0))
    vec = pl.BlockSpec((1, d), lambda i: (0, 0))
    return pl.pallas_call(
        body, grid=(t // tr,), in_specs=[row, row, vec], out_specs=[row, vec],
        out_shape=[_sds((t, d), BF16), _sds((1, d), F32)], compiler_params=_params("arbitrary"),
        name="pool_scale_bwd")(dh, ypre, scale)


def _pool_mm_bwd_x(dys, wp, pin):
    t, d = dys.shape
    dg = d // N_GROUPS
    rs = dg // N_SHARDS
    tm = _tile(t, 1024)
    return _matmul(
        dys, wp, dims=_NT, grid=(t // tm, N_GROUPS * N_SHARDS, 1),
        a_spec=pl.BlockSpec((tm, dg), lambda i, j, kk: (i, j // N_SHARDS)),
        b_spec=pl.BlockSpec((None, None, rs, dg), lambda i, j, kk: (j % N_SHARDS, j // N_SHARDS, 0, 0)),
        acc_shape=(tm, rs), out_shapes=[_sds((t, d), F32)],
        out_specs=[pl.BlockSpec((tm, rs), lambda i, j, kk: (i, j))],
        epilogue=lambda acc: (acc,), name="pool_mm_bwd_x", pin=pin)[0]


def _pool_mm_bwd_w(dm, dys):
    t, d = dm.shape
    dg = d // N_GROUPS
    rs = dg // N_SHARDS
    tk = _tile(t, 4096)
    return _matmul(
        dm, dys, dims=_TN, grid=(N_GROUPS * N_SHARDS, 1, t // tk),
        a_spec=pl.BlockSpec((tk, rs), lambda i, j, kk: (kk, i)),
        b_spec=pl.BlockSpec((tk, dg), lambda i, j, kk: (kk, i // N_SHARDS)),
        acc_shape=(rs, dg), out_shapes=[_sds((N_SHARDS, N_GROUPS, rs, dg), BF16)],
        out_specs=[pl.BlockSpec((None, None, rs, dg), lambda i, j, kk: (i % N_SHARDS, i // N_SHARDS, 0, 0))],
        epilogue=lambda acc: (acc,), name="pool_mm_bwd_w")[0]


def _mlp_fwd(h, hn, w1g, w2_after, pin=None):
    t, d = hn.shape
    f4 = w1g.shape[2]
    dff = N_SHARDS * f4
    tm, tn, tk = _tile(t, 1024), _tile(f4, 1024), _tile(d, 4096)
    nb = f4 // tn
    o_spec = pl.BlockSpec((tm, tn), lambda i, j, kk: (i, j))

    def act(acc):
        r = jnp.maximum(acc, 0.0)
        return r, r * r

    r, u = _matmul(
        hn, w1g, dims=_NN, grid=(t // tm, dff // tn, d // tk),
        a_spec=pl.BlockSpec((tm, tk), lambda i, j, kk: (i, kk)),
        b_spec=pl.BlockSpec((None, tk, tn), lambda i, j, kk: (j // nb, kk, j % nb)),
        acc_shape=(tm, tn), out_shapes=[_sds((t, dff), BF16), _sds((t, dff), BF16)], out_specs=[o_spec, o_spec],
        epilogue=act, name="mlp_up", pin=pin)
    w2 = w2_after(u)
    (out,) = _mm(u, w2, "nn", out_dtypes=[F32], epilogue=lambda acc, hv: (hv + acc,), extras=(h,), name="mlp_down",
                 tk=2048)
    return out, r, u, w2


def _mlp_bwd(dhb, hn, r, u, w1g, w2):
    t, d = hn.shape
    f4 = w1g.shape[2]
    dff = N_SHARDS * f4
    (da,) = _mm(dhb, w2, "nt", out_dtypes=[BF16], epilogue=lambda acc, rv: (acc * (2.0 * rv.astype(F32)),),
                extras=(r,), name="mlp_bwd_da")
    (dw2,) = _mm(u, dhb, "tn", out_dtypes=[BF16], name="mlp_bwd_dw2")
    tok = yield dw2.reshape(N_SHARDS, f4, d)
    tm, tn, tk = _tile(d, 1024), _tile(f4, 1024), _tile(t, 4096)
    nb = f4 // tn
    (dw1,) = _matmul(
        hn, da, dims=_TN, grid=(d // tm, dff // tn, t // tk),
        a_spec=pl.BlockSpec((tk, tm), lambda i, j, kk: (kk, i)),
        b_spec=pl.BlockSpec((tk, tn), lambda i, j, kk: (kk, j)),
        acc_shape=(tm, tn), out_shapes=[_sds((N_SHARDS, d, f4), BF16)],
        out_specs=[pl.BlockSpec((None, tm, tn), lambda i, j, kk: (j // nb, i, j % nb))],
        epilogue=lambda acc: (acc,), name="mlp_bwd_dw1", pin=tok)
    tok = yield dw1
    tm, tn, tk = _tile(t, 1024), _tile(d, 1024), _tile(f4, 4096)
    nbk = f4 // tk
    (dhn,) = _matmul(
        da, w1g, dims=_NT, grid=(t // tm, d // tn, dff // tk),
        a_spec=pl.BlockSpec((tm, tk), lambda i, j, kk: (i, kk)),
        b_spec=pl.BlockSpec((None, tn, tk), lambda i, j, kk: (kk // nbk, j, kk % nbk)),
        acc_shape=(tm, tn), out_shapes=[_sds((t, d), F32)],
        out_specs=[pl.BlockSpec((tm, tn), lambda i, j, kk: (i, j))], epilogue=lambda acc: (acc,), name="mlp_bwd_dhn",
        pin=tok)
    yield dhn


def _split3(x):
    a = x.astype(BF16)
    r1 = x - a.astype(F32)
    b = r1.astype(BF16)
    c = (r1 - b.astype(F32)).astype(BF16)
    return a, b, c


def _dot(a, b, dims):
    return lax.dot_general(a.astype(BF16), b.astype(BF16), dims, preferred_element_type=F32)


def _chunk_terms(q, k, g, rev, scale):
    c = q.shape[0]
    ri = lax.broadcasted_iota(jnp.int32, (c, c), 0)
    ci = lax.broadcasted_iota(jnp.int32, (c, c), 1)
    seen = (ci >= ri) if rev else (ci <= ri)
    tri = seen.astype(BF16)
    g1, g2, g3 = _split3(g)
    b = (lax.dot_general(tri, g1, _NN, preferred_element_type=F32)
         + lax.dot_general(tri, g2, _NN, preferred_element_type=F32)
         + lax.dot_general(tri, g3, _NN, preferred_element_type=F32))
    mid = c // 2 if rev else c // 2 - 1
    last = 0 if rev else c - 1
    rows = lax.broadcasted_iota(jnp.int32, b.shape, 0)
    b_mid = jnp.sum(jnp.where(rows == mid, b, 0.0), axis=0, keepdims=True)
    b_last = jnp.sum(jnp.where(rows == last, b, 0.0), axis=0, keepdims=True)
    qs = q * scale
    e1 = jnp.exp(b - b_mid)
    e2 = jnp.exp(b_mid - b)
    eb = jnp.exp(b)
    el = jnp.exp(b_last - b)
    return dict(seen=seen, tri=tri, mid=mid, last=last, e1=e1, e2=e2, eb=eb, el=el, a=jnp.exp(b_last),
                qe=qs * e1, ke=k * e2, qi=qs * eb, ks=k * el)


def _step_terms(q_ref, k_ref, g_ref, rev, scale):
    lo, hi = slice(0, CHUNK), slice(CHUNK, STEP)
    rows = (hi, lo) if rev else (lo, hi)
    return rows, [_chunk_terms(q_ref[r, :], k_ref[r, :], g_ref[r, :], rev, scale) for r in rows]


def _gla_fwd(pm, la, seq):
    t = pm.shape[0]
    d = pm.shape[1] // 3
    dk, dv = d // 2 // N_HEADS, d // N_HEADS
    nb, nc = t // seq, seq // STEP
    scale = dk ** -0.5
    kq, kk_, kv = 0, N_HEADS, (d // dv)

    def body(qf, kf, vf, gf, qb, kb, vb, gb, of_ref, ob_ref, stf_ref, stb_ref, sf, sb):
        n = pl.program_id(1)

        @pl.when(n == 0)
        def _():
            sf[...] = jnp.zeros_like(sf)
            sb[...] = jnp.zeros_like(sb)

        for (q_ref, k_ref, v_ref, g_ref, o_ref, st_ref, s_ref, rev) in (
                (qf, kf, vf, gf, of_ref, stf_ref, sf, False), (qb, kb, vb, gb, ob_ref, stb_ref, sb, True)):
            rows, (t0, t1) = _step_terms(q_ref, k_ref, g_ref, rev, scale)
            v0, v1 = v_ref[rows[0], :], v_ref[rows[1], :]
            st = s_ref[...]
            stb = st.astype(BF16)
            st_ref[...] = stb
            sc0 = jnp.where(t0["seen"], _dot(t0["qe"], t0["ke"], _NT), 0.0)
            sc1 = jnp.where(t1["seen"], _dot(t1["qe"], t1["ke"], _NT), 0.0)
            cross = _dot(t1["qi"], t0["ks"], _NT)
            qi_all = jnp.concatenate([t0["qi"], t1["qi"] * t0["a"]], axis=0)
            o_inter = lax.dot_general(qi_all.astype(BF16), stb, _NT, preferred_element_type=F32)
            o_ref[rows[0], :] = _dot(sc0, v0, _NN) + o_inter[:CHUNK]
            o_ref[rows[1], :] = _dot(sc1, v1, _NN) + _dot(cross, v0, _NN) + o_inter[CHUNK:]
            ks_all = jnp.concatenate([t0["ks"] * t1["a"], t1["ks"]], axis=0)
            s_ref[...] = st * (t0["a"] * t1["a"]) + _dot(jnp.concatenate([v0, v1], axis=0), ks_all, _TN)

    def row(bh, n, rev):
        return (bh // N_HEADS) * nc + (nc - 1 - n if rev else n)

    def specs(rev):
        return [
            pl.BlockSpec((STEP,dk), lambda bh, n: (row(bh, n, rev), kq + bh % N_HEADS)),
            pl.BlockSpec((STEP,dk), lambda bh, n: (row(bh, n, rev), kk_ + bh % N_HEADS)),
            pl.BlockSpec((STEP,dv), lambda bh, n: (row(bh, n, rev), kv + bh % N_HEADS)),
            pl.BlockSpec((STEP,dk), lambda bh, n: (row(bh, n, rev), (N_HEADS if rev else 0) + bh % N_HEADS)),
        ]

    def o_spec(rev):
        return pl.BlockSpec((STEP,dv), lambda bh, n: (row(bh, n, rev), bh % N_HEADS))

    def st_spec(rev):
        return pl.BlockSpec((None, None, dv, dk), lambda bh, n: (bh, nc - 1 - n if rev else n, 0, 0))

    sf_, sb_ = specs(False), specs(True)
    return pl.pallas_call(
        body, grid=(nb * N_HEADS, nc),
        in_specs=[*sf_, *sb_],
        out_specs=[o_spec(False), o_spec(True), st_spec(False), st_spec(True)],
        out_shape=[_sds((t, d), F32), _sds((t, d), F32),
                   _sds((nb * N_HEADS, nc, dv, dk), BF16), _sds((nb * N_HEADS, nc, dv, dk), BF16)],
        scratch_shapes=[pltpu.VMEM((dv, dk), F32), pltpu.VMEM((dv, dk), F32)],
        compiler_params=_params("parallel", "arbitrary"), name="gla_scan_fwd",
    )(pm, pm, pm, la, pm, pm, pm, la)


def _gla_bwd(pm, la, do, st_f, st_b, seq):
    t = pm.shape[0]
    d = pm.shape[1] // 3
    dk, dv = d // 2 // N_HEADS, d // N_HEADS
    kd = dk * N_HEADS
    nb, nc = t // seq, seq // STEP
    scale = dk ** -0.5
    kq, kk_, kv = 0, N_HEADS, (d // dv)

    def body(qf, kf, vf, gf, dof, stf, qb, kb, vb, gb, dob, stb_,
             dqf, dkf, dvf, dgf, dqb, dkb, dvb, dgb, dsf, dsb):
        n = pl.program_id(1)

        @pl.when(n == 0)
        def _():
            dsf[...] = jnp.zeros_like(dsf)
            dsb[...] = jnp.zeros_like(dsb)

        for (q_ref, k_ref, v_ref, g_ref, do_ref, st_ref, dq_ref, dk_ref, dv_ref, dg_ref, ds_ref, rev) in (
                (qf, kf, vf, gf, dof, stf, dqf, dkf, dvf, dgf, dsf, False),
                (qb, kb, vb, gb, dob, stb_, dqb, dkb, dvb, dgb, dsb, True)):
            rows, (t0, t1) = _step_terms(q_ref, k_ref, g_ref, rev, scale)
            v0, v1 = v_ref[rows[0], :], v_ref[rows[1], :]
            do0, do1 = do_ref[rows[0], :], do_ref[rows[1], :]
            st = st_ref[...]
            ds = ds_ref[...]
            dsb16 = ds.astype(BF16)
            a0, a1 = t0["a"], t1["a"]
            sc0 = jnp.where(t0["seen"], _dot(t0["qe"], t0["ke"], _NT), 0.0)
            sc1 = jnp.where(t1["seen"], _dot(t1["qe"], t1["ke"], _NT), 0.0)
            cross = _dot(t1["qi"], t0["ks"], _NT)
            dsc0 = jnp.where(t0["seen"], _dot(do0, v0, _NT), 0.0)
            dsc1 = jnp.where(t1["seen"], _dot(do1, v1, _NT), 0.0)
            dcross = _dot(do1, v0, _NT)
            qi_all = jnp.concatenate([t0["qi"], t1["qi"] * a0], axis=0)
            ks_all = jnp.concatenate([t0["ks"] * a1, t1["ks"]], axis=0)
            v_all = jnp.concatenate([v0, v1], axis=0).astype(BF16)
            do_all = jnp.concatenate([do0, do1], axis=0).astype(BF16)
            dks_all = lax.dot_general(v_all, dsb16, _NN, preferred_element_type=F32)
            dv_all = lax.dot_general(ks_all.astype(BF16), dsb16, _NT, preferred_element_type=F32)
            dqi_all = lax.dot_general(do_all, st, _NN, preferred_element_type=F32)
            da01 = jnp.sum(ds * st.astype(F32), axis=0, keepdims=True)
            dv_ref[rows[0], :] = (_dot(sc0, do0, _TN) + _dot(cross, do1, _TN) + dv_all[:CHUNK]).astype(BF16)
            dv_ref[rows[1], :] = (_dot(sc1, do1, _TN) + dv_all[CHUNK:]).astype(BF16)
            dqi0 = dqi_all[:CHUNK]
            dqi1 = dqi_all[CHUNK:] * a0 + _dot(dcross, t0["ks"], _NN)
            dks0 = dks_all[:CHUNK] * a1 + _dot(dcross, t1["qi"], _TN)
            dks1 = dks_all[CHUNK:]
            da0 = da01 * a1 + jnp.sum(dqi_all[CHUNK:] * t1["qi"], axis=0, keepdims=True)
            da1 = da01 * a0 + jnp.sum(dks_all[:CHUNK] * t0["ks"], axis=0, keepdims=True)
            for r, tm, dsc, dqi, dks, da in ((rows[0], t0, dsc0, dqi0, dks0, da0), (rows[1], t1, dsc1, dqi1, dks1, da1)):
                dqe = _dot(dsc, tm["ke"], _NN)
                dke = _dot(dsc, tm["qe"], _TN)
                dq_ref[r, :] = ((dqe * tm["e1"] + dqi * tm["eb"]) * scale).astype(BF16)
                dk_ref[r, :] = (dke * tm["e2"] + dks * tm["el"]).astype(BF16)
                t_q, t_k, t_s = dqe * tm["qe"], dke * tm["ke"], dks * tm["ks"]
                db = t_q - t_k + dqi * tm["qi"] - t_s
                mid_row = jnp.sum(t_k - t_q, axis=0, keepdims=True)
                last_row = jnp.sum(t_s, axis=0, keepdims=True) + da * tm["a"]
                ridx = lax.broadcasted_iota(jnp.int32, db.shape, 0)
                db = db + jnp.where(ridx == tm["mid"], mid_row, 0.0) + jnp.where(ridx == tm["last"], last_row, 0.0)
                d1, d2, d3 = _split3(db)
                dg_ref[r, :] = (lax.dot_general(tm["tri"], d1, _TN, preferred_element_type=F32)
                                + lax.dot_general(tm["tri"], d2, _TN, preferred_element_type=F32)
                                + lax.dot_general(tm["tri"], d3, _TN, preferred_element_type=F32))
            ds_ref[...] = ds * (a0 * a1) + lax.dot_general(do_all, qi_all.astype(BF16), _TN,
                                                           preferred_element_type=F32)

    def row(bh, n, rev):
        return (bh // N_HEADS) * nc + (n if rev else nc - 1 - n)

    def specs(rev):
        return [
            pl.BlockSpec((STEP,dk), lambda bh, n: (row(bh, n, rev), kq + bh % N_HEADS)),
            pl.BlockSpec((STEP,dk), lambda bh, n: (row(bh, n, rev), kk_ + bh % N_HEADS)),
            pl.BlockSpec((STEP,dv), lambda bh, n: (row(bh, n, rev), kv + bh % N_HEADS)),
            pl.BlockSpec((STEP,dk), lambda bh, n: (row(bh, n, rev), (N_HEADS if rev else 0) + bh % N_HEADS)),
            pl.BlockSpec((STEP,dv), lambda bh, n: (row(bh, n, rev), bh % N_HEADS)),
            pl.BlockSpec((None, None, dv, dk), lambda bh, n: (bh, n if rev else nc - 1 - n, 0, 0)),
        ]

    def outs(rev):
        return [
            pl.BlockSpec((STEP,dk), lambda bh, n: (row(bh, n, rev), bh % N_HEADS)),
            pl.BlockSpec((STEP,dk), lambda bh, n: (row(bh, n, rev), bh % N_HEADS)),
            pl.BlockSpec((STEP,dv), lambda bh, n: (row(bh, n, rev), bh % N_HEADS)),
            pl.BlockSpec((STEP,dk), lambda bh, n: (row(bh, n, rev), bh % N_HEADS)),
        ]

    of_, ob_ = outs(False), outs(True)
    res = pl.pallas_call(
        body, grid=(nb * N_HEADS, nc),
        in_specs=[*specs(False), *specs(True)],
        out_specs=[*of_, *ob_],
        out_shape=[_sds((t, kd), BF16), _sds((t, kd), BF16), _sds((t, d), BF16), _sds((t, kd), F32),
                   _sds((t, kd), BF16), _sds((t, kd), BF16), _sds((t, d), BF16), _sds((t, kd), F32)],
        scratch_shapes=[pltpu.VMEM((dv, dk), F32), pltpu.VMEM((dv, dk), F32)],
        compiler_params=_params("parallel", "arbitrary"), name="gla_scan_bwd",
    )(pm, pm, pm, la, do, st_f, pm, pm, pm, la, do, st_b)
    return res


def _sigmoid(x):
    return 1.0 / (1.0 + jnp.exp(-x))


def _gla_post_fwd(o_f, o_b, pm, gn):
    t, d = o_f.shape
    dv = d // N_HEADS
    tr = _tile(t, 512)
    gate_blk = 2 * d // dv

    def body(of_ref, ob_ref, gt_ref, gn_ref, out_ref):
        o = of_ref[...] + ob_ref[...]
        n = o * lax.rsqrt(jnp.mean(o * o, axis=-1, keepdims=True) + EPS) * gn_ref[...]
        gt = gt_ref[...]
        out_ref[...] = (n * (gt * _sigmoid(gt))).astype(BF16)

    blk = pl.BlockSpec((tr, dv), lambda i, h: (i, h))
    return pl.pallas_call(
        body, grid=(t // tr, N_HEADS),
        in_specs=[blk, blk, pl.BlockSpec((tr, dv), lambda i, h: (i, gate_blk + h)),
                  pl.BlockSpec((1, dv), lambda i, h: (0, 0))],
        out_specs=blk, out_shape=_sds((t, d), BF16), compiler_params=_params("parallel", "parallel"),
        name="gla_post_fwd")(o_f, o_b, pm, gn)


def _gla_post_bwd(dog, o_f, o_b, pm, gn):
    t, d = o_f.shape
    dv = d // N_HEADS
    tr = _tile(t, 256)
    gate_blk = 2 * d // dv

    def body(dog_ref, of_ref, ob_ref, gt_ref, gn_ref, do_ref, dgt_ref, dgn_ref):
        o = of_ref[...] + ob_ref[...]
        rr = lax.rsqrt(jnp.mean(o * o, axis=-1, keepdims=True) + EPS)
        on = o * rr
        gnv = gn_ref[...]
        gt = gt_ref[...]
        sg = _sigmoid(gt)
        sl = gt * sg
        dg_out = dog_ref[...]
        dn = dg_out * sl
        dgt_ref[...] = dg_out * (on * gnv) * (sg * (1.0 + gt * (1.0 - sg)))
        gdn = dn * gnv
        do_ref[...] = rr * (gdn - on * jnp.mean(gdn * on, axis=-1, keepdims=True))
        part = jnp.sum(dn * on, axis=0, keepdims=True)
        first = (pl.program_id(0) == 0) & (pl.program_id(1) == 0)

        @pl.when(first)
        def _():
            dgn_ref[...] = part

        @pl.when(jnp.logical_not(first))
        def _():
            dgn_ref[...] += part

    blk = pl.BlockSpec((tr, dv), lambda i, h: (i, h))
    vec = pl.BlockSpec((1, dv), lambda i, h: (0, 0))
    return pl.pallas_call(
        body, grid=(t // tr, N_HEADS),
        in_specs=[blk, blk, blk, pl.BlockSpec((tr, dv), lambda i, h: (i, gate_blk + h)), vec],
        out_specs=[blk, blk, vec], out_shape=[_sds((t, d), F32), _sds((t, d), F32), _sds((1, dv), F32)],
        compiler_params=_params("arbitrary", "arbitrary"), name="gla_post_bwd")(dog, o_f, o_b, pm, gn)


def _gla_dp(dq_f, dq_b, dk_f, dk_b, dv_f, dv_b, dgate):
    t, d = dv_f.shape
    kd = dq_f.shape[1]
    tr = _tile(t, 128)

    def body(a1, a2, b1, b2, c1, c2, g, o_ref):
        o_ref[:, 0:kd] = (a1[...].astype(F32) + a2[...].astype(F32)).astype(BF16)
        o_ref[:, kd:2 * kd] = (b1[...].astype(F32) + b2[...].astype(F32)).astype(BF16)
        o_ref[:, 2 * kd:2 * kd + d] = (c1[...].astype(F32) + c2[...].astype(F32)).astype(BF16)
        o_ref[:, 2 * kd + d:] = g[...].astype(BF16)

    sk = pl.BlockSpec((tr, kd), lambda i: (i, 0))
    sd = pl.BlockSpec((tr, d), lambda i: (i, 0))
    return pl.pallas_call(
        body, grid=(t // tr,), in_specs=[sk, sk, sk, sk, sd, sd, sd],
        out_specs=pl.BlockSpec((tr, 3 * d), lambda i: (i, 0)), out_shape=_sds((t, 3 * d), BF16),
        compiler_params=_params("parallel"), name="gla_dp")(dq_f, dq_b, dk_f, dk_b, dv_f, dv_b, dgate)


def _adamw(w, g, m, v, name):
    r, c = w.shape
    tr = _tile(r, 128)
    bc1 = 1.0 - ADAM_B1 ** ADAM_STEP
    bc2 = 1.0 - ADAM_B2 ** ADAM_STEP

    def body(w_ref, g_ref, m_ref, v_ref, d_ref, nm_ref, nv_ref):
        gv = g_ref[...]
        mn = ADAM_B1 * m_ref[...] + (1.0 - ADAM_B1) * gv
        vn = ADAM_B2 * v_ref[...] + (1.0 - ADAM_B2) * (gv * gv)
        m_hat = mn / bc1
        v_hat = vn / bc2
        d_ref[...] = -ADAM_LR * (m_hat / (jnp.sqrt(v_hat) + ADAM_EPS) + ADAM_WD * w_ref[...])
        nm_ref[...] = mn
        nv_ref[...] = vn

    blk = pl.BlockSpec((tr, c), lambda i: (i, 0))
    return pl.pallas_call(
        body, grid=(r // tr,), in_specs=[blk] * 4, out_specs=[blk] * 3, out_shape=[_sds((r, c), F32)] * 3,
        compiler_params=_params("parallel"), name=name)(w, g, m, v)


def _adamw_half(w, g, m, v, layer, half, prev, pin, name, transposed=False):
    bc1 = 1.0 - ADAM_B1 ** ADAM_STEP
    bc2 = 1.0 - ADAM_B2 ** ADAM_STEP
    n_skip = 1 + (0 if prev is None else 4)
    if transposed:
        nl, c, r = w.shape
        tr = _tile(r // 2, 128)
        nblk = r // 2 // tr
        lay = pl.BlockSpec((None, c, tr), lambda i, h_ref: (layer, 0, h_ref[0] * nblk + i))
        g_spec = pl.BlockSpec((c, tr), lambda i, h_ref: (0, i))
    else:
        nl, r, c = w.shape
        tr = _tile(r // 2, 128)
        nblk = r // 2 // tr
        lay = pl.BlockSpec((None, tr, c), lambda i, h_ref: (layer, h_ref[0] * nblk + i, 0))
        g_spec = pl.BlockSpec((tr, c), lambda i, h_ref: (i, 0))

    def body(h_ref, w_ref, g_ref, m_ref, v_ref, *rest):
        go_ref, d_ref, nm_ref, nv_ref = rest[n_skip:]
        gv = g_ref[...]
        mn = ADAM_B1 * m_ref[...] + (1.0 - ADAM_B1) * gv
        vn = ADAM_B2 * v_ref[...] + (1.0 - ADAM_B2) * (gv * gv)
        m_hat = mn / bc1
        v_hat = vn / bc2
        d_ref[...] = -ADAM_LR * (m_hat / (jnp.sqrt(v_hat) + ADAM_EPS) + ADAM_WD * w_ref[...])
        nm_ref[...] = mn
        nv_ref[...] = vn
        go_ref[...] = gv

    prev = tuple(prev or ())
    grid_spec = pltpu.PrefetchScalarGridSpec(
        num_scalar_prefetch=1, grid=(nblk,),
        in_specs=[lay, g_spec, lay, lay, pl.BlockSpec(TOKEN, lambda i, h_ref: (0, 0))] + [ANY] * len(prev),
        out_specs=[lay] * 4)
    return pl.pallas_call(
        body, grid_spec=grid_spec, out_shape=[_sds(w.shape, F32)] * 4,
        input_output_aliases={6 + k: k for k in range(len(prev))},
        compiler_params=_params("parallel"), name=name)(half, w, g, m, v, pin, *prev)


def _sum8(parts, own, me):
    _, n, _ = parts.shape

    def body(me_ref, p_ref, own_ref, o_ref):
        acc = None
        for i in range(8):
            term = jnp.where(me_ref[0] == i, own_ref[...], p_ref[i])
            acc = term if acc is None else acc + term
        o_ref[...] = acc

    grid_spec = pltpu.PrefetchScalarGridSpec(
        num_scalar_prefetch=1, grid=(1,),
        in_specs=[pl.BlockSpec((8, n, 128), lambda i, me_ref: (0, 0, 0)),
                  pl.BlockSpec((n, 128), lambda i, me_ref: (0, 0))],
        out_specs=pl.BlockSpec((n, 128), lambda i, me_ref: (0, 0)))
    return pl.pallas_call(body, grid_spec=grid_spec, out_shape=_sds((n, 128), F32), name="sum8")(me, parts, own)


def _pair_sum(ga, recv, c):
    _, _, rh, cols = ga.shape
    tr = _tile(rh, 256)

    def body(c_ref, a_ref, b_ref, o_ref):
        o_ref[...] = (a_ref[...].astype(F32) + b_ref[...].astype(F32)).astype(BF16)

    grid_spec = pltpu.PrefetchScalarGridSpec(
        num_scalar_prefetch=1, grid=(N_SHARDS, rh // tr),
        in_specs=[pl.BlockSpec((None, None, tr, cols), lambda s, i, c_ref: (s, c_ref[0], i, 0)),
                  pl.BlockSpec((None, tr, cols), lambda s, i, c_ref: (s, i, 0))],
        out_specs=pl.BlockSpec((None, tr, cols), lambda s, i, c_ref: (s, i, 0)))
    return pl.pallas_call(
        body, grid_spec=grid_spec, out_shape=_sds((N_SHARDS, rh, cols), BF16),
        compiler_params=_params("parallel", "parallel"), name="grad_pair_sum")(c, ga, recv)


def _quad_sum(pair, recv, s_me):
    _, rh, cols = pair.shape
    tr = _tile(rh, 256)

    def body(s_ref, p_ref, r1_ref, r2_ref, r3_ref, o_ref):
        o_ref[...] = ((p_ref[...].astype(F32) + r1_ref[...].astype(F32)) + r2_ref[...].astype(F32)) \
            + r3_ref[...].astype(F32)

    def blk(off):
        return pl.BlockSpec((None, tr, cols), lambda i, s_ref: ((s_ref[0] + off) % N_SHARDS, i, 0))

    grid_spec = pltpu.PrefetchScalarGridSpec(
        num_scalar_prefetch=1, grid=(rh // tr,), in_specs=[blk(0), blk(1), blk(2), blk(3)],
        out_specs=pl.BlockSpec((tr, cols), lambda i, s_ref: (i, 0)))
    return pl.pallas_call(
        body, grid_spec=grid_spec, out_shape=_sds((rh, cols), F32),
        compiler_params=_params("parallel"), name="grad_quad_sum")(s_me, pair, recv, recv, recv)


def _fill_own(w, layer, s_me, pin):
    _, rows, cols = w.shape
    rh = rows // 2
    tr = _tile(rh, 256)
    nblk = rh // tr
    pins = () if pin is None else (pin,)

    def body(s_ref, w_ref, *rest):
        rest[-1][...] = w_ref[...].astype(BF16)

    grid_spec = pltpu.PrefetchScalarGridSpec(
        num_scalar_prefetch=1, grid=(2, nblk),
        in_specs=[pl.BlockSpec((None, tr, cols), lambda h, i, s_ref: (layer, h * nblk + i, 0)),
                  *[pl.BlockSpec(TOKEN, lambda h, i, s_ref: (0, 0)) for _ in pins]],
        out_specs=pl.BlockSpec((None, None, tr, cols), lambda h, i, s_ref: (s_ref[0], h, i, 0)))
    return pl.pallas_call(
        body, grid_spec=grid_spec, out_shape=_sds((N_SHARDS, 2, rh, cols), BF16),
        compiler_params=_params("parallel", "parallel"), name="weight_fill_own")(s_me, w, *pins)


def _coords():
    return lax.axis_index("x"), lax.axis_index("y"), lax.axis_index("c")


def _other_chips(x, y):
    return [(x, 1 - y), (1 - x, y), (1 - x, 1 - y)]


def _bcast8(buf):
    n = buf.shape[0]

    def body(b_ref, o_ref, send_sems, recv_sems):
        x, y, c = _coords()
        me = 4 * x + 2 * y + c
        o_ref[me] = b_ref[...]
        copies = []
        for k in range(1, 8):
            peer = (x ^ (k >> 2), y ^ ((k >> 1) & 1), c ^ (k & 1))
            copies.append(pltpu.make_async_remote_copy(
                src_ref=b_ref, dst_ref=o_ref.at[me], send_sem=send_sems.at[k - 1], recv_sem=recv_sems.at[k - 1],
                device_id=peer, device_id_type=MESH))
        for cp in copies:
            cp.start()
        for k in range(1, 8):
            pltpu.make_async_remote_copy(
                src_ref=b_ref, dst_ref=o_ref.at[me ^ k], send_sem=send_sems.at[k - 1], recv_sem=recv_sems.at[k - 1],
                device_id=(x, y, c), device_id_type=MESH).wait_recv()
        for cp in copies:
            cp.wait_send()

    return pl.pallas_call(
        body, out_shape=_sds((8, n, 128), F32),
        in_specs=[pl.BlockSpec(memory_space=pltpu.VMEM)], out_specs=pl.BlockSpec(memory_space=pltpu.VMEM),
        scratch_shapes=[pltpu.SemaphoreType.DMA((7,)), pltpu.SemaphoreType.DMA((7,))],
        compiler_params=pltpu.CompilerParams(vmem_limit_bytes=VMEM_LIMIT), name="bcast8")(buf)


HBM = pl.BlockSpec(memory_space=pltpu.HBM)
SEM = pl.BlockSpec(memory_space=pltpu.SEMAPHORE)
SIDE = pltpu.SideEffectType.DATAFLOW_SIDE_EFFECTING
TOKEN = (8, 128)


def _in_hbm(a):
    return pltpu.with_memory_space_constraint(a, pltpu.HBM)


def _sibling_start(src, pieces, land_shape, after, name):
    n = len(pieces(None, None, None))

    def body(s_ref, land_ref, after_ref, send, recv, land_thru, token):
        x, y, c = _coords()
        for k, (a, b) in enumerate(pieces(s_ref, land_ref, c)):
            pltpu.make_async_remote_copy(
                src_ref=a, dst_ref=b, send_sem=send.at[k], recv_sem=recv.at[k], device_id=(x, y, 1 - c),
                device_id_type=MESH).start()
        token[...] = jnp.zeros_like(token)

    return pl.pallas_call(
        body, name=name,
        out_shape=(pltpu.SemaphoreType.DMA((n,)), pltpu.SemaphoreType.DMA((n,)), pltpu.HBM(land_shape, src.dtype),
                   _sds(TOKEN, F32)),
        in_specs=(HBM, HBM, ANY), out_specs=(SEM, SEM, HBM, pl.BlockSpec(memory_space=pltpu.VMEM)),
        input_output_aliases={1: 2}, compiler_params=pltpu.CompilerParams(has_side_effects=SIDE),
    )(_in_hbm(src), _in_hbm(lax.empty(land_shape, src.dtype)), after)


def _sibling_wait(send, recv, src, land, pieces, after, name):
    def body(s_ref, land_ref, send, recv, after_ref, land_out):
        x, y, c = _coords()
        for k, (a, b) in enumerate(pieces(s_ref, land_ref, c)):
            cp = pltpu.make_async_remote_copy(
                src_ref=a, dst_ref=b, send_sem=send.at[k], recv_sem=recv.at[k], device_id=(x, y, 1 - c),
                device_id_type=MESH)
            cp.wait_send()
            cp.wait_recv()

    return pl.pallas_call(
        body, name=name, out_shape=pltpu.HBM(land.shape, land.dtype), in_specs=(HBM, HBM, SEM, SEM, ANY),
        out_specs=HBM, input_output_aliases={1: 0}, compiler_params=pltpu.CompilerParams(has_side_effects=SIDE),
    )(src, land, send, recv, after)


def _swap_pieces(g_ref, land_ref, c):
    if g_ref is None:
        return [None] * N_SHARDS
    return [(g_ref.at[s, 1 - c], land_ref.at[s]) for s in range(N_SHARDS)]


def _whole_piece(r_ref, land_ref, c):
    return [(r_ref, land_ref)]


def _bcast_copies(b_ref, land_ref, send, recv):
    x, y, c = _coords()
    me = 4 * x + 2 * y + c
    return [pltpu.make_async_remote_copy(
        src_ref=b_ref, dst_ref=land_ref.at[me], send_sem=send.at[k - 1], recv_sem=recv.at[k - 1],
        device_id=(x ^ (k >> 2), y ^ ((k >> 1) & 1), c ^ (k & 1)), device_id_type=MESH) for k in range(1, 8)]


def _bcast_start(buf, after, name):
    def body(b_ref, land_ref, after_ref, send, recv, land_thru, token):
        for cp in _bcast_copies(b_ref, land_ref, send, recv):
            cp.start()
        token[...] = jnp.zeros_like(token)

    shape = (8,) + buf.shape
    return pl.pallas_call(
        body, name=name,
        out_shape=(pltpu.SemaphoreType.DMA((7,)), pltpu.SemaphoreType.DMA((7,)), pltpu.HBM(shape, buf.dtype),
                   _sds(TOKEN, F32)),
        in_specs=(HBM, HBM, ANY), out_specs=(SEM, SEM, HBM, pl.BlockSpec(memory_space=pltpu.VMEM)),
        input_output_aliases={1: 2}, compiler_params=pltpu.CompilerParams(has_side_effects=SIDE),
    )(_in_hbm(buf), _in_hbm(lax.empty(shape, buf.dtype)), after)


def _bcast_wait(send, recv, buf, land, after, name):
    def body(b_ref, land_ref, send, recv, after_ref, land_out):
        for cp in _bcast_copies(b_ref, land_ref, send, recv):
            cp.wait_send()
            cp.wait_recv()

    return pl.pallas_call(
        body, name=name, out_shape=pltpu.HBM(land.shape, land.dtype), in_specs=(HBM, HBM, SEM, SEM, ANY),
        out_specs=HBM, input_output_aliases={1: 0}, compiler_params=pltpu.CompilerParams(has_side_effects=SIDE),
    )(buf, land, send, recv, after)


def _scatter_start(part, name):
    def body(p_ref, land_ref, send, recv, p_thru, land_thru, token):
        x, y, c = _coords()
        s_me = 2 * x + y
        for j, (px, py) in enumerate(_other_chips(x, y)):
            pltpu.make_async_remote_copy(
                src_ref=p_ref.at[2 * px + py], dst_ref=land_ref.at[s_me], send_sem=send.at[j], recv_sem=recv.at[j],
                device_id=(px, py, c), device_id_type=MESH).start()
        token[...] = jnp.zeros_like(token)

    buf = pltpu.HBM(part.shape, part.dtype)
    return pl.pallas_call(
        body, name=name,
        out_shape=(pltpu.SemaphoreType.DMA((3,)), pltpu.SemaphoreType.DMA((3,)), buf, buf, _sds(TOKEN, F32)),
        in_specs=(HBM, HBM), out_specs=(SEM, SEM, HBM, HBM, pl.BlockSpec(memory_space=pltpu.VMEM)),
        input_output_aliases={0: 2, 1: 3}, compiler_params=pltpu.CompilerParams(has_side_effects=SIDE),
    )(_in_hbm(part), _in_hbm(lax.empty(part.shape, part.dtype)))


def _scatter_wait(send, recv, part, land, after, name):
    def body(p_ref, land_ref, send, recv, after_ref, p_out, land_out):
        x, y, c = _coords()
        for j, (px, py) in enumerate(_other_chips(x, y)):
            cp = pltpu.make_async_remote_copy(
                src_ref=p_ref.at[2 * px + py], dst_ref=land_ref.at[2 * px + py], send_sem=send.at[j],
                recv_sem=recv.at[j], device_id=(px, py, c), device_id_type=MESH)
            cp.wait_send()
            cp.wait_recv()

    buf = pltpu.HBM(part.shape, part.dtype)
    return pl.pallas_call(
        body, name=name, out_shape=(buf, buf), in_specs=(HBM, HBM, SEM, SEM, ANY), out_specs=(HBM, HBM),
        input_output_aliases={0: 0, 1: 1}, compiler_params=pltpu.CompilerParams(has_side_effects=SIDE),
    )(part, land, send, recv, after)


def _gather_start(bufs, after, name):
    n = len(bufs)

    def body(*refs):
        ins = refs[:n]
        sems = refs[n + 1:3 * n + 1]
        token = refs[4 * n + 1]
        x, y, c = _coords()
        s_me = 2 * x + y
        for i in range(n):
            for j, (px, py) in enumerate(_other_chips(x, y)):
                pltpu.make_async_remote_copy(
                    src_ref=ins[i].at[s_me, c], dst_ref=ins[i].at[s_me, c], send_sem=sems[2 * i].at[j],
                    recv_sem=sems[2 * i + 1].at[j], device_id=(px, py, c), device_id_type=MESH).start()
        token[...] = jnp.zeros_like(token)

    res = pl.pallas_call(
        body, name=name,
        out_shape=(*[pltpu.SemaphoreType.DMA((3,))] * (2 * n), *[pltpu.HBM(b.shape, b.dtype) for b in bufs],
                   _sds(TOKEN, F32)),
        in_specs=(*[HBM] * n, ANY), out_specs=(*[SEM] * (2 * n), *[HBM] * n, pl.BlockSpec(memory_space=pltpu.VMEM)),
        input_output_aliases={i: 2 * n + i for i in range(n)},
        compiler_params=pltpu.CompilerParams(has_side_effects=SIDE),
    )(*[_in_hbm(b) for b in bufs], after)
    return [(res[2 * i], res[2 * i + 1]) for i in range(n)], list(res[2 * n:3 * n]), res[3 * n]


def _gather_wait(send, recv, buf, after, name):
    def body(b_ref, send, recv, after_ref, b_out):
        x, y, c = _coords()
        s_me = 2 * x + y
        for j, (px, py) in enumerate(_other_chips(x, y)):
            cp = pltpu.make_async_remote_copy(
                src_ref=b_ref.at[s_me, c], dst_ref=b_ref.at[2 * px + py, c], send_sem=send.at[j], recv_sem=recv.at[j],
                device_id=(px, py, c), device_id_type=MESH)
            cp.wait_send()
            cp.wait_recv()

    return pl.pallas_call(
        body, name=name, out_shape=pltpu.HBM(buf.shape, buf.dtype), in_specs=(HBM, SEM, SEM, ANY), out_specs=HBM,
        input_output_aliases={0: 0}, compiler_params=pltpu.CompilerParams(has_side_effects=SIDE),
    )(buf, send, recv, after)


def _pass_copies(b_ref, send, recv):
    x, y, c = _coords()
    return [pltpu.make_async_remote_copy(
        src_ref=b_ref.at[2 * px + py, c], dst_ref=b_ref.at[2 * px + py, c], send_sem=send.at[j], recv_sem=recv.at[j],
        device_id=(x, y, 1 - c), device_id_type=MESH) for j, (px, py) in enumerate(_other_chips(x, y))]


def _pass_start(buf, after, name):
    def body(b_ref, after_ref, send, recv, b_thru, token):
        for cp in _pass_copies(b_ref, send, recv):
            cp.start()
        token[...] = jnp.zeros_like(token)

    return pl.pallas_call(
        body, name=name,
        out_shape=(pltpu.SemaphoreType.DMA((3,)), pltpu.SemaphoreType.DMA((3,)), pltpu.HBM(buf.shape, buf.dtype),
                   _sds(TOKEN, F32)),
        in_specs=(HBM, ANY), out_specs=(SEM, SEM, HBM, pl.BlockSpec(memory_space=pltpu.VMEM)),
        input_output_aliases={0: 2}, compiler_params=pltpu.CompilerParams(has_side_effects=SIDE),
    )(_in_hbm(buf), after)


def _pass_wait(send, recv, buf, after, name):
    def body(b_ref, send, recv, after_ref, b_out):
        x, y, c = _coords()
        for j, (px, py) in enumerate(_other_chips(x, y)):
            pltpu.make_async_remote_copy(
                src_ref=b_ref.at[2 * px + py, c], dst_ref=b_ref.at[2 * px + py, 1 - c], send_sem=send.at[j],
                recv_sem=recv.at[j], device_id=(x, y, 1 - c), device_id_type=MESH).wait()

    return pl.pallas_call(
        body, name=name, out_shape=pltpu.HBM(buf.shape, buf.dtype), in_specs=(HBM, SEM, SEM, ANY), out_specs=HBM,
        input_output_aliases={0: 0}, compiler_params=pltpu.CompilerParams(has_side_effects=SIDE),
    )(buf, send, recv, after)


def _pass_on_halves(bufs):
    n = len(bufs)

    def body(*refs):
        ins, outs = refs[:n], refs[n:2 * n]
        send, recv = refs[2 * n:]
        x, y, c = _coords()
        copies = []
        for i in range(n):
            for j, (px, py) in enumerate(_other_chips(x, y)):
                cp = pltpu.make_async_remote_copy(
                    src_ref=ins[i].at[2 * px + py, c], dst_ref=outs[i].at[2 * px + py, c], send_sem=send.at[i, j],
                    recv_sem=recv.at[i, j], device_id=(x, y, 1 - c), device_id_type=MESH)
                cp.start()
                copies.append(cp)
        for i in range(n):
            for j, (px, py) in enumerate(_other_chips(x, y)):
                other = outs[i].at[2 * px + py, 1 - c]
                pltpu.make_async_remote_copy(
                    src_ref=other, dst_ref=other, send_sem=send.at[i, j], recv_sem=recv.at[i, j],
                    device_id=(x, y, c), device_id_type=MESH).wait_recv()
        for cp in copies:
            cp.wait_send()

    return pl.pallas_call(
        body, out_shape=[_sds(b.shape, b.dtype) for b in bufs],
        in_specs=[ANY] * n, out_specs=[ANY] * n, input_output_aliases={i: i for i in range(n)},
        scratch_shapes=[pltpu.SemaphoreType.DMA((n, 3)), pltpu.SemaphoreType.DMA((n, 3))],
        name="gather_pass_on")(*bufs)


def _to_rows(vec):
    n = -(-vec.shape[0] // 1024) * 1024
    return jnp.pad(vec, (0, n - vec.shape[0])).reshape(-1, 128)


def kernel(x, norm_mix, norm_mlp, norm_final, pool_w, pool_scale, gla_w_in, gla_w_up_f, gla_b_up_f, gla_w_up_b, gla_b_up_b, gla_g_norm, gla_w_out, mlp_w_in, mlp_w_out, loss_target, m_norm_mix, m_norm_mlp, m_norm_final, m_pool_w, m_pool_scale, m_gla_w_in, m_gla_w_up_f, m_gla_b_up_f, m_gla_w_up_b, m_gla_b_up_b, m_gla_g_norm, m_gla_w_out, m_mlp_w_in, m_mlp_w_out, v_norm_mix, v_norm_mlp, v_norm_final, v_pool_w, v_pool_scale, v_gla_w_in, v_gla_w_up_f, v_gla_b_up_f, v_gla_w_up_b, v_gla_b_up_b, v_gla_g_norm, v_gla_w_out, v_mlp_w_in, v_mlp_w_out):
    nb, seq, d = x.shape
    t = nb * seq
    dg = d // N_GROUPS
    kd = d // 2
    dv = d // N_HEADS
    pw = gla_w_in.shape[2]
    f4 = mlp_w_in.shape[2]
    dff = N_SHARDS * f4
    cx, cy, cc = _coords()
    s_me = 2 * cx + cy
    c_arr = jnp.reshape(cc, (1,)).astype(jnp.int32)
    s_arr = jnp.reshape(s_me, (1,)).astype(jnp.int32)

    xf = x.reshape(t, d)
    tgt = loss_target.reshape(t, d)

    ks = kd // N_SHARDS
    small = jnp.concatenate([gla_w_up_f[0].reshape(-1), gla_w_up_b[0].reshape(-1), gla_b_up_f[0], gla_b_up_b[0],
                             gla_g_norm[0]])
    small_all = _bcast8(_to_rows(small))
    pool_rows = N_GROUPS * (dg // N_SHARDS)
    gsems, gbufs, gtok = _gather_start([
        _fill_own(pool_w.reshape(1, pool_rows, dg), 0, s_arr, None), _fill_own(mlp_w_in, 0, s_arr, None)],
        small_all, "gather_start_first")
    gsems2, gbufs2, gtok = _gather_start([
        _fill_own(mlp_w_out, 0, s_arr, gtok), _fill_own(gla_w_in, 0, s_arr, gtok),
        _fill_own(gla_w_out, 0, s_arr, gtok), _fill_own(mlp_w_in, 1, s_arr, gtok),
        _fill_own(mlp_w_out, 1, s_arr, gtok)], gtok, "gather_start_rest")
    gsems, gbufs = gsems + gsems2, gbufs + gbufs2
    small_all = small_all[::2].reshape(N_SHARDS, -1)

    def arrived(i, after, name):
        return _gather_wait(gsems[i][0], gsems[i][1], gbufs[i], after, "gather_wait_" + name)

    def weight(i, after, name):
        return _pass_on_halves([arrived(i, after, name)])[0]

    def pinned(vec, tok):
        return vec + tok[0:1, 0:1]

    o = 0
    wuf = jnp.transpose(small_all[:, o:o + GATE_RANK * ks].reshape(N_SHARDS, GATE_RANK, ks), (1, 0, 2)).reshape(GATE_RANK, kd)
    o += GATE_RANK * ks
    wub = jnp.transpose(small_all[:, o:o + GATE_RANK * ks].reshape(N_SHARDS, GATE_RANK, ks), (1, 0, 2)).reshape(GATE_RANK, kd)
    o += GATE_RANK * ks
    buf = small_all[:, o:o + ks].reshape(1, kd)
    o += ks
    bub = small_all[:, o:o + ks].reshape(1, kd)
    o += ks
    gn = small_all[:, o:o + dv // N_SHARDS].reshape(1, dv)
    w_up = jnp.zeros((R_PAD, 2 * kd), F32).at[:GATE_RANK, :kd].set(wuf).at[GATE_RANK:2 * GATE_RANK, kd:].set(wub)
    w_up = w_up.astype(BF16)
    b_up = jnp.concatenate([buf, bub], axis=1)

    hn0 = _rmsnorm_fwd(xf, pinned(norm_mix[0:1], gtok), F32, "norm_mix0")
    dm = _pool_apply(hn0, seq, False, "pool_diff")
    wp = weight(0, dm, "pool_w").reshape(N_SHARDS, N_GROUPS, dg // N_SHARDS, dg)
    h1, ypre = _pool_mm_fwd(dm, wp, xf, pool_scale)
    hn1 = _rmsnorm_fwd(h1, norm_mlp[0:1], BF16, "norm_mlp0")
    w1g = [weight(1, hn1, "mlp_w_in0").reshape(N_SHARDS, d, f4), None]
    w2g = [None, None]
    h2, r0, u0, w2g[0] = _mlp_fwd(h1, hn1, w1g[0], lambda u: weight(2, u, "mlp_w_out0").reshape(dff, d))
    hn2 = _rmsnorm_fwd(h2, norm_mix[1:2], BF16, "norm_mix1")
    win = jnp.transpose(weight(3, hn2, "gla_w_in").reshape(N_SHARDS, d, pw), (1, 0, 2)).reshape(d, N_SHARDS * pw)
    w_main = win[:, :3 * d]
    w_r = jnp.pad(win[:, 3 * d:], ((0, 0), (0, R_PAD - 2 * GATE_RANK)))
    (pm,) = _mm(hn2, w_main, "nn", out_dtypes=[F32], name="gla_proj")
    (pr,) = _mm(hn2, w_r, "nn", out_dtypes=[BF16], name="gla_proj_r")
    pr = pr.at[:, ONES_COL].set(1.0)

    def log_decay(acc, bv):
        z = acc + bv
        return ((jnp.minimum(z, 0.0) - jnp.log(1.0 + jnp.exp(-jnp.abs(z)))) / GATE_TAU,)

    tm_, tn_ = _tile(t, 1024), _tile(kd, 1024)
    (la,) = _matmul(
        pr, w_up, dims=_NN, grid=(t // tm_, 2 * kd // tn_, 1),
        a_spec=pl.BlockSpec((tm_, R_PAD), lambda i, j, kk: (i, 0)),
        b_spec=pl.BlockSpec((R_PAD, tn_), lambda i, j, kk: (0, j)), acc_shape=(tm_, tn_),
        out_shapes=[_sds((t, 2 * kd), F32)], out_specs=[pl.BlockSpec((tm_, tn_), lambda i, j, kk: (i, j))],
        epilogue=log_decay, extras=(b_up,), extra_specs=[pl.BlockSpec((1, tn_), lambda i, j, kk: (0, j))],
        name="gla_gate_fwd")
    o_f, o_b, st_f, st_b = _gla_fwd(pm, la, seq)
    send4, recv4, buf4, tok = _pass_start(arrived(4, o_f, "gla_w_out"), o_f, "gather_pass_start_gla_w_out")
    og = _gla_post_fwd(o_f, o_b, pm, pinned(gn, tok))
    wout = _pass_wait(send4, recv4, buf4, og, "gather_pass_wait_gla_w_out").reshape(d, d)
    send5, recv5, buf5, tok = _pass_start(arrived(5, og, "mlp_w_in1"), og, "gather_pass_start_mlp_w_in1")
    (h3,) = _mm(og, wout, "nn", out_dtypes=[F32], epilogue=lambda acc, hv: (hv + acc,), extras=(h2,), name="gla_out",
                pin=tok)
    hn3 = _rmsnorm_fwd(h3, norm_mlp[1:2], BF16, "norm_mlp1")
    w1g[1] = _pass_wait(send5, recv5, buf5, hn3, "gather_pass_wait_mlp_w_in1").reshape(N_SHARDS, d, f4)
    send6, recv6, buf6, tok = _pass_start(arrived(6, hn3, "mlp_w_out1"), hn3, "gather_pass_start_mlp_w_out1")
    h4, r1, u1, w2g[1] = _mlp_fwd(
        h3, hn3, w1g[1], lambda u: _pass_wait(send6, recv6, buf6, u, "gather_pass_wait_mlp_w_out1").reshape(dff, d), tok)

    chains = []
    last_tok = [gtok]

    def swap_begin(g, name):
        g5 = g.reshape(N_SHARDS, 2, g.shape[1] // 2, g.shape[2])
        send, recv, land, tok = _sibling_start(g5, _swap_pieces, (N_SHARDS,) + g5.shape[2:], last_tok[0],
                                               "grad_swap_start_" + name)
        last_tok[0] = tok
        return (name, send, recv, g5, land), tok

    def swap_end(state, after):
        name, send, recv, g5, land = state
        land = _sibling_wait(send, recv, g5, land, _swap_pieces, after, "grad_swap_wait_" + name)
        pair = _pair_sum(g5, land, c_arr)
        send, recv, pair, land, tok = _scatter_start(pair, "grad_scatter_start_" + name)
        chains.append((name, send, recv, pair, land))
        last_tok[0] = tok
        return tok

    def mlp_backward(dhb, hn, r, u, w1, w2, name):
        steps = _mlp_bwd(dhb, hn, r, u, w1, w2)
        state, tok = swap_begin(next(steps), "mlp_w_out" + name)
        dw1 = steps.send(tok)
        swap_end(state, dw1)
        state, tok = swap_begin(dw1, "mlp_w_in" + name)
        dhn = steps.send(tok)
        return dhn, swap_end(state, dhn)

    loss_part, dh4, dh4b, dg_final = _final_bwd(h4, norm_final.reshape(1, d), tgt)
    dhn3, tok = mlp_backward(dh4b, hn3, r1, u1, w1g[1], w2g[1], "1")
    dh3, dh3b, dg_mlp1 = _rmsnorm_bwd(h3, pinned(norm_mlp[1:2], tok), dhn3, dh4, "norm_mlp1_bwd")

    (dog,) = _mm(dh3b, wout, "nt", out_dtypes=[F32], name="gla_out_bwd_x")
    (dwout,) = _mm(og, dh3b, "tn", out_dtypes=[BF16], name="gla_out_bwd_w")
    state, tok = swap_begin(dwout.reshape(N_SHARDS, d // N_SHARDS, d), "gla_w_out")
    do, dgate, dg_gn = _gla_post_bwd(dog, o_f, o_b, pm, pinned(gn, tok))
    dq_f, dk_f, dv_f, dla_f, dq_b, dk_b, dv_b, dla_b = _gla_bwd(pm, la, do, st_f, st_b, seq)
    b_up_p = pinned(b_up, swap_end(state, dq_f))
    nkb = kd // tn_

    def gate_bwd(acc, bv, dl_f, dl_b):
        z = acc + bv
        dl = jnp.where(pl.program_id(1) < nkb, dl_f, dl_b)
        return (dl * (1.0 / GATE_TAU) / (1.0 + jnp.exp(z)),)

    (dz,) = _matmul(
        pr, w_up, dims=_NN, grid=(t // tm_, 2 * kd // tn_, 1),
        a_spec=pl.BlockSpec((tm_, R_PAD), lambda i, j, kk: (i, 0)),
        b_spec=pl.BlockSpec((R_PAD, tn_), lambda i, j, kk: (0, j)), acc_shape=(tm_, tn_),
        out_shapes=[_sds((t, 2 * kd), BF16)], out_specs=[pl.BlockSpec((tm_, tn_), lambda i, j, kk: (i, j))],
        epilogue=gate_bwd, extras=(b_up_p, dla_f, dla_b),
        extra_specs=[pl.BlockSpec((1, tn_), lambda i, j, kk: (0, j)),
                     pl.BlockSpec((tm_, tn_), lambda i, j, kk: (i, jnp.minimum(j, nkb - 1))),
                     pl.BlockSpec((tm_, tn_), lambda i, j, kk: (i, jnp.maximum(j - nkb, 0)))],
        name="gla_gate_bwd")
    (dpr,) = _mm(dz, w_up, "nt", out_dtypes=[BF16], name="gla_gate_bwd_r")
    (dw_up,) = _mm(pr, dz, "tn", out_dtypes=[F32], name="gla_gate_bwd_w")
    dp = _gla_dp(dq_f, dq_b, dk_f, dk_b, dv_f, dv_b, dgate)
    (dw_main,) = _mm(hn2, dp, "tn", out_dtypes=[BF16], name="gla_proj_bwd_w")
    (dw_r,) = _mm(hn2, dpr, "tn", out_dtypes=[BF16], name="gla_proj_bwd_wr")
    dwin = jnp.concatenate([dw_main, dw_r[:, :2 * GATE_RANK]], axis=1)
    state, tok = swap_begin(jnp.transpose(dwin.reshape(d, N_SHARDS, pw), (1, 0, 2)), "gla_w_in")
    (dhn2_r,) = _mm(dpr, w_r, "nt", out_dtypes=[F32], name="gla_proj_bwd_xr")
    (dhn2,) = _mm(dp, w_main, "nt", out_dtypes=[F32], epilogue=lambda acc, e: (acc + e,), extras=(dhn2_r,),
                  name="gla_proj_bwd_x", tk=2048, pin=tok)
    tok = swap_end(state, dhn2)
    dh2, dh2b, dg_mix1 = _rmsnorm_bwd(h2, pinned(norm_mix[1:2], tok), dhn2, dh3, "norm_mix1_bwd")

    dhn1, tok = mlp_backward(dh2b, hn1, r0, u0, w1g[0], w2g[0], "0")
    dh1, _, dg_mlp0 = _rmsnorm_bwd(h1, pinned(norm_mlp[0:1], tok), dhn1, dh2, "norm_mlp0_bwd")

    dys, dg_pscale = _pool_scale_bwd(dh1, ypre, pool_scale)
    dwp = _pool_mm_bwd_w(dm, dys)
    state, tok = swap_begin(dwp.reshape(N_SHARDS, pool_rows, dg), "pool_w")
    dd = _pool_mm_bwd_x(dys, wp, tok)
    tok = swap_end(state, dd)
    dhn0 = _pool_apply(dd, seq, True, "pool_diff_bwd")
    dx, _, dg_mix0 = _rmsnorm_bwd(xf, pinned(norm_mix[0:1], tok), dhn0, dh1, "norm_mix0_bwd")

    dwuf, dwub = dw_up[:GATE_RANK, :kd], dw_up[GATE_RANK:2 * GATE_RANK, kd:]
    dbuf, dbub = dw_up[ONES_COL, :kd], dw_up[ONES_COL, kd:]
    pieces = [jnp.concatenate([dg_mix0, dg_mix1], 0), jnp.concatenate([dg_mlp0, dg_mlp1], 0), dg_final, dg_pscale,
              dwuf, dwub, dbuf, dbub, dg_gn]
    sizes = [p.size for p in pieces]
    packed = _to_rows(jnp.concatenate([p.reshape(-1) for p in pieces]))
    bsend, brecv, bland, _ = _bcast_start(packed, dx, "small_grads_start")

    loss = lax.psum(loss_part[0, 0], ("x", "y", "c"))

    weights = [norm_mix, norm_mlp, norm_final, pool_w, pool_scale, gla_w_in, gla_w_up_f, gla_b_up_f, gla_w_up_b,
               gla_b_up_b, gla_g_norm, gla_w_out, mlp_w_in, mlp_w_out]
    moms = [m_norm_mix, m_norm_mlp, m_norm_final, m_pool_w, m_pool_scale, m_gla_w_in, m_gla_w_up_f, m_gla_b_up_f,
            m_gla_w_up_b, m_gla_b_up_b, m_gla_g_norm, m_gla_w_out, m_mlp_w_in, m_mlp_w_out]
    vels = [v_norm_mix, v_norm_mlp, v_norm_final, v_pool_w, v_pool_scale, v_gla_w_in, v_gla_w_up_f, v_gla_b_up_f,
            v_gla_w_up_b, v_gla_b_up_b, v_gla_g_norm, v_gla_w_out, v_mlp_w_in, v_mlp_w_out]
    names = ["norm_mix", "norm_mlp", "norm_final", "pool_w", "pool_scale", "gla_w_in", "gla_w_up_f", "gla_b_up_f",
             "gla_w_up_b", "gla_b_up_b", "gla_g_norm", "gla_w_out", "mlp_w_in", "mlp_w_out"]
    index = {nm: k for k, nm in enumerate(names)}
    results = {}

    stacked = {"mlp_w_out1": ("mlp_w_out", 1), "mlp_w_in1": ("mlp_w_in", 1), "gla_w_out": ("gla_w_out", 0),
               "gla_w_in": ("gla_w_in", 0), "mlp_w_out0": ("mlp_w_out", 0), "mlp_w_in0": ("mlp_w_in", 0),
               "pool_w": ("pool_w", 0)}
    oc_arr = 1 - c_arr
    after = dx
    for name, send, recv, pair, land in chains:
        pair, land = _scatter_wait(send, recv, pair, land, after, "grad_scatter_wait_" + name)
        mine = _quad_sum(pair, land, s_arr)
        jsend, jrecv, jland, jtok = _sibling_start(mine, _whole_piece, mine.shape, after, "grad_join_start_" + name)
        nm, layer = stacked[name]
        w, m, v = weights[index[nm]], moms[index[nm]], vels[index[nm]]
        rows, cols = 2 * mine.shape[0], mine.shape[1]
        shp = (w.size // (rows * cols), rows, cols)
        w, m, v = w.reshape(shp), m.reshape(shp), v.reshape(shp)
        tr_ = nm == "gla_w_in"
        flip = (lambda a: jnp.swapaxes(a, -1, -2)) if tr_ else (lambda a: a)
        w, m, v = flip(w), flip(m), flip(v)
        res = _adamw_half(w, flip(mine), m, v, layer, c_arr, results.get(nm), jtok, "adamw_mine_" + name, tr_)
        theirs = _sibling_wait(jsend, jrecv, mine, jland, _whole_piece, res[1], "grad_join_wait_" + name)
        res = _adamw_half(w, flip(theirs), m, v, layer, oc_arr, res, jtok, "adamw_theirs_" + name, tr_)
        results[nm] = [flip(a) for a in res]
        after = res[1]

    bland = _bcast_wait(bsend, brecv, packed, bland, after, "small_grads_wait")
    me_arr = jnp.reshape(4 * cx + 2 * cy + cc, (1,)).astype(jnp.int32)
    summed = _sum8(bland, packed, me_arr).reshape(-1)
    outs_small, o = [], 0
    for p, n in zip(pieces, sizes):
        outs_small.append(summed[o:o + n].reshape(p.shape))
        o += n
    g_nmix, g_nmlp, g_nfinal, g_pscale, g_wuf, g_wub, g_buf, g_bub, g_gn = outs_small
    g_wuf = lax.dynamic_slice_in_dim(g_wuf, s_me * ks, ks, axis=1)
    g_wub = lax.dynamic_slice_in_dim(g_wub, s_me * ks, ks, axis=1)
    g_buf = lax.dynamic_slice_in_dim(g_buf, s_me * ks, ks, axis=0)
    g_bub = lax.dynamic_slice_in_dim(g_bub, s_me * ks, ks, axis=0)
    g_gn = lax.dynamic_slice_in_dim(g_gn.reshape(dv), s_me * (dv // N_SHARDS), dv // N_SHARDS, axis=0)
    small_grads = {"norm_mix": g_nmix, "norm_mlp": g_nmlp, "norm_final": g_nfinal, "pool_scale": g_pscale,
                   "gla_w_up_f": g_wuf, "gla_b_up_f": g_buf, "gla_w_up_b": g_wub, "gla_b_up_b": g_bub,
                   "gla_g_norm": g_gn}
    for nm, g in small_grads.items():
        w, m, v = weights[index[nm]], moms[index[nm]], vels[index[nm]]
        cols = w.shape[-1]
        shp = (w.size // cols, cols)
        dl, mn, vn = _adamw(w.reshape(shp), g.reshape(shp), m.reshape(shp), v.reshape(shp), "adamw_" + nm)
        results[nm] = (g, dl, mn, vn)

    outs = [[results[nm][k].reshape(weights[index[nm]].shape) for nm in names] for k in range(4)]
    return (loss, dx.reshape(x.shape), *outs[0], *outs[1], *outs[2], *outs[3])
```

```python
import jax
import jax.numpy as jnp
from jax import lax
from jax.experimental import pallas as pl
from jax.experimental.pallas import tpu as pltpu

F32 = jnp.float32
BF16 = jnp.bfloat16

N_HEADS = 4
N_GROUPS = 4
POOL_HALF = (1, 2, 4, 8)
GATE_RANK = 16
GATE_TAU = 16.0
CHUNK = 64
STEP = 2 * CHUNK
EPS = 1e-6
N_SHARDS = 4
R_PAD = 128
ONES_COL = 2 * GATE_RANK

ADAM_LR = 0.001
ADAM_B1 = 0.9
ADAM_B2 = 0.999
ADAM_EPS = 1e-08
ADAM_WD = 0.01
ADAM_STEP = 10

_NN = (((1,), (0,)), ((), ()))
_NT = (((1,), (1,)), ((), ()))
_TN = (((0,), (0,)), ((), ()))

VMEM_LIMIT = 56 * 1024 * 1024
MESH = pl.DeviceIdType.MESH
ANY = pl.BlockSpec(memory_space=pl.ANY)


def _tile(dim, pref):
    return pref if dim % pref == 0 else dim


def _params(*sem):
    return pltpu.CompilerParams(dimension_semantics=sem, vmem_limit_bytes=VMEM_LIMIT)


def _sds(shape, dtype):
    return jax.ShapeDtypeStruct(shape, dtype)


def _matmul(a, b, *, dims, grid, a_spec, b_spec, acc_shape, out_shapes, out_specs, epilogue,
            extras=(), extra_specs=(), name, pin=None):
    nk = grid[2]
    n_extra = len(extras)
    n_out = len(out_shapes)
    pins = () if pin is None else (pin,)

    def body(a_ref, b_ref, *rest):
        extra_refs = rest[:n_extra]
        rest = rest[n_extra + len(pins):]
        out_refs = rest[:n_out]
        acc_ref = rest[n_out]
        kk = pl.program_id(2)

        def part():
            return lax.dot_general(a_ref[...], b_ref[...], dims, preferred_element_type=F32)

        def finish(acc):
            outs = epilogue(acc, *[r[...] for r in extra_refs])
            for o_ref, o in zip(out_refs, outs):
                o_ref[...] = o.astype(o_ref.dtype)

        if nk == 1:
            finish(part())
        else:
            @pl.when(kk == 0)
            def _():
                acc_ref[...] = part()

            @pl.when((kk > 0) & (kk < nk - 1))
            def _():
                acc_ref[...] += part()

            @pl.when(kk == nk - 1)
            def _():
                finish(acc_ref[...] + part())

    return pl.pallas_call(
        body,
        grid=grid,
        in_specs=[a_spec, b_spec, *extra_specs, *[pl.BlockSpec((8, 128), lambda i, j, kk: (0, 0)) for _ in pins]],
        out_specs=list(out_specs),
        out_shape=list(out_shapes),
        scratch_shapes=[pltpu.VMEM(acc_shape if nk > 1 else (8, 128), F32)],
        compiler_params=_params("parallel", "parallel", "arbitrary"),
        name=name,
    )(a, b, *extras, *pins)


def _mm(a, b, kind, *, out_dtypes, epilogue=None, extras=(), name, tm=1024, tn=1024, tk=4096, pin=None):
    if kind == "nn":
        (m, k), n = a.shape, b.shape[1]
    elif kind == "nt":
        (m, k), n = a.shape, b.shape[0]
    else:
        (k, m), n = a.shape, b.shape[1]
    tm, tn, tk = _tile(m, tm), _tile(n, tn), _tile(k, tk)
    if kind == "nn":
        a_spec = pl.BlockSpec((tm, tk), lambda i, j, kk: (i, kk))
        b_spec = pl.BlockSpec((tk, tn), lambda i, j, kk: (kk, j))
        dims = _NN
    elif kind == "nt":
        a_spec = pl.BlockSpec((tm, tk), lambda i, j, kk: (i, kk))
        b_spec = pl.BlockSpec((tn, tk), lambda i, j, kk: (j, kk))
        dims = _NT
    else:
        a_spec = pl.BlockSpec((tk, tm), lambda i, j, kk: (kk, i))
        b_spec = pl.BlockSpec((tk, tn), lambda i, j, kk: (kk, j))
        dims = _TN
    o_spec = pl.BlockSpec((tm, tn), lambda i, j, kk: (i, j))
    if epilogue is None:
        epilogue = lambda acc, *e: tuple(acc for _ in out_dtypes)
    return _matmul(
        a, b, dims=dims, grid=(m // tm, n // tn, k // tk), a_spec=a_spec, b_spec=b_spec, acc_shape=(tm, tn),
        out_shapes=[_sds((m, n), d) for d in out_dtypes], out_specs=[o_spec for _ in out_dtypes],
        epilogue=epilogue, extras=extras, extra_specs=[o_spec for _ in extras], name=name, pin=pin)


def _rmsnorm_fwd(h, g, out_dtype, name):
    t, d = h.shape
    tr = _tile(t, 256)

    def body(h_ref, g_ref, o_ref):
        x = h_ref[...]
        r = lax.rsqrt(jnp.mean(x * x, axis=-1, keepdims=True) + EPS)
        o_ref[...] = (x * r * g_ref[...]).astype(o_ref.dtype)

    return pl.pallas_call(
        body, grid=(t // tr,),
        in_specs=[pl.BlockSpec((tr, d), lambda i: (i, 0)), pl.BlockSpec((1, d), lambda i: (0, 0))],
        out_specs=pl.BlockSpec((tr, d), lambda i: (i, 0)),
        out_shape=_sds((t, d), out_dtype), compiler_params=_params("parallel"), name=name)(h, g)


def _rmsnorm_bwd(h, g, dy, resid, name):
    t, d = h.shape
    tr = _tile(t, 128)

    def body(h_ref, g_ref, dy_ref, res_ref, dh_ref, dhb_ref, dg_ref):
        x = h_ref[...]
        r = lax.rsqrt(jnp.mean(x * x, axis=-1, keepdims=True) + EPS)
        xn = x * r
        dyv = dy_ref[...]
        gdy = dyv * g_ref[...]
        dh = res_ref[...] + r * (gdy - xn * jnp.mean(gdy * xn, axis=-1, keepdims=True))
        dh_ref[...] = dh
        dhb_ref[...] = dh.astype(BF16)
        part = jnp.sum(dyv * xn, axis=0, keepdims=True)

        @pl.when(pl.program_id(0) == 0)
        def _():
            dg_ref[...] = part

        @pl.when(pl.program_id(0) > 0)
        def _():
            dg_ref[...] += part

    row = pl.BlockSpec((tr, d), lambda i: (i, 0))
    vec = pl.BlockSpec((1, d), lambda i: (0, 0))
    return pl.pallas_call(
        body, grid=(t // tr,), in_specs=[row, vec, row, row], out_specs=[row, row, vec],
        out_shape=[_sds((t, d), F32), _sds((t, d), BF16), _sds((1, d), F32)],
        compiler_params=_params("arbitrary"), name=name)(h, g, dy, resid)


def _final_bwd(h, g, tgt):
    t, d = h.shape
    tr = _tile(t, 128)

    def body(h_ref, g_ref, t_ref, loss_ref, dh_ref, dhb_ref, dg_ref):
        x = h_ref[...]
        r = lax.rsqrt(jnp.mean(x * x, axis=-1, keepdims=True) + EPS)
        xn = x * r
        gv = g_ref[...]
        e = xn * gv - t_ref[...]
        lpart = jnp.full((1, 128), 0.5 * jnp.sum(jnp.mean(e * e, axis=-1, keepdims=True)), F32)
        dyv = e * (1.0 / d)
        gdy = dyv * gv
        dh = r * (gdy - xn * jnp.mean(gdy * xn, axis=-1, keepdims=True))
        dh_ref[...] = dh
        dhb_ref[...] = dh.astype(BF16)
        part = jnp.sum(dyv * xn, axis=0, keepdims=True)

        @pl.when(pl.program_id(0) == 0)
        def _():
            dg_ref[...] = part
            loss_ref[...] = lpart

        @pl.when(pl.program_id(0) > 0)
        def _():
            dg_ref[...] += part
            loss_ref[...] += lpart

    row = pl.BlockSpec((tr, d), lambda i: (i, 0))
    vec = pl.BlockSpec((1, d), lambda i: (0, 0))
    return pl.pallas_call(
        body, grid=(t // tr,), in_specs=[row, vec, row],
        out_specs=[pl.BlockSpec((1, 128), lambda i: (0, 0)), row, row, vec],
        out_shape=[_sds((1, 128), F32), _sds((t, d), F32), _sds((t, d), BF16), _sds((1, d), F32)],
        compiler_params=_params("arbitrary"), name="final_loss_bwd")(h, g, tgt)


def _shift_rows(x, s, row):
    n = x.shape[0]
    y = pltpu.roll(x, (-s) % n, 0)
    return jnp.where((row + s >= 0) & (row + s < n), y, 0.0)


def _span_sum(x, start, length, row):
    if start >= 0:
        y, step = _shift_rows(x, start, row) if start else x, 1
    else:
        last = start + length - 1
        assert last <= 0
        y, step = _shift_rows(x, last, row) if last else x, -1
    n = 1
    while n < length:
        y = y + _shift_rows(y, step * n, row)
        n *= 2
    return y


def _pool_apply(x, seq, transpose, name):
    t, d = x.shape
    dg = d // N_GROUPS
    tc = _tile(dg, 256)
    nblk = dg // tc

    def body(x_ref, o_ref):
        grp = pl.program_id(1)
        row = lax.broadcasted_iota(jnp.int32, (seq, tc), 0)
        for gi, hw in enumerate(POOL_HALF):
            @pl.when(grp == gi)
            def _(hw=hw):
                xv = x_ref[...]
                cnt = (jnp.minimum(row + hw, seq) - jnp.maximum(row - hw, 0)).astype(F32)
                if not transpose:
                    w = _span_sum(xv, 0, hw, row) + _span_sum(xv, -hw, hw, row)
                    o_ref[...] = (w / cnt - xv).astype(o_ref.dtype)
                else:
                    u = xv / cnt
                    w = _span_sum(u, 1, hw, row) + _span_sum(u, -(hw - 1), hw, row)
                    o_ref[...] = (w - xv).astype(o_ref.dtype)

    spec = pl.BlockSpec((seq, tc), lambda b, g, j: (b, g * nblk + j))
    return pl.pallas_call(
        body, grid=(t // seq, N_GROUPS, nblk), in_specs=[spec], out_specs=spec,
        out_shape=_sds((t, d), F32 if transpose else BF16),
        compiler_params=_params("parallel", "parallel", "parallel"), name=name)(x)


def _pool_mm_fwd(dm, wp, x, scale):
    t, d = dm.shape
    dg = d // N_GROUPS
    rs = dg // N_SHARDS
    tm = _tile(t, 1024)
    o_spec = pl.BlockSpec((tm, dg), lambda i, j, kk: (i, j))
    return _matmul(
        dm, wp, dims=_NN, grid=(t // tm, N_GROUPS, N_SHARDS),
        a_spec=pl.BlockSpec((tm, rs), lambda i, j, kk: (i, j * N_SHARDS + kk)),
        b_spec=pl.BlockSpec((None, None, rs, dg), lambda i, j, kk: (kk, j, 0, 0)),
        acc_shape=(tm, dg), out_shapes=[_sds((t, d), F32), _sds((t, d), F32)], out_specs=[o_spec, o_spec],
        epilogue=lambda acc, xv, sc: (xv + acc * sc, acc),
        extras=(x, scale), extra_specs=[o_spec, pl.BlockSpec((1, dg), lambda i, j, kk: (0, j))], name="pool_mm_fwd")


def _pool_scale_bwd(dh, ypre, scale):
    t, d = dh.shape
    tr = _tile(t, 256)

    def body(dh_ref, y_ref, s_ref, o_ref, ds_ref):
        g = dh_ref[...]
        o_ref[...] = (g * s_ref[...]).astype(BF16)
        part = jnp.sum(g * y_ref[...], axis=0, keepdims=True)

        @pl.when(pl.program_id(0) == 0)
        def _():
            ds_ref[...] = part

        @pl.when(pl.program_id(0) > 0)
        def _():
            ds_ref[...] += part

    row = pl.BlockSpec((tr, d), lambda i: (i, 0))
    vec = pl.BlockSpec((1, d), lambda i: (0, 0))
    return pl.pallas_call(
        body, grid=(t // tr,), in_specs=[row, row, vec], out_specs=[row, vec],
        out_shape=[_sds((t, d), BF16), _sds((1, d), F32)], compiler_params=_params("arbitrary"),
        name="pool_scale_bwd")(dh, ypre, scale)


def _pool_mm_bwd_x(dys, wp, pin):
    t, d = dys.shape
    dg = d // N_GROUPS
    rs = dg // N_SHARDS
    tm = _tile(t, 1024)
    return _matmul(
        dys, wp, dims=_NT, grid=(t // tm, N_GROUPS * N_SHARDS, 1),
        a_spec=pl.BlockSpec((tm, dg), lambda i, j, kk: (i, j // N_SHARDS)),
        b_spec=pl.BlockSpec((None, None, rs, dg), lambda i, j, kk: (j % N_SHARDS, j // N_SHARDS, 0, 0)),
        acc_shape=(tm, rs), out_shapes=[_sds((t, d), F32)],
        out_specs=[pl.BlockSpec((tm, rs), lambda i, j, kk: (i, j))],
        epilogue=lambda acc: (acc,), name="pool_mm_bwd_x", pin=pin)[0]


def _pool_mm_bwd_w(dm, dys):
    t, d = dm.shape
    dg = d // N_GROUPS
    rs = dg // N_SHARDS
    tk = _tile(t, 4096)
    return _matmul(
        dm, dys, dims=_TN, grid=(N_GROUPS * N_SHARDS, 1, t // tk),
        a_spec=pl.BlockSpec((tk, rs), lambda i, j, kk: (kk, i)),
        b_spec=pl.BlockSpec((tk, dg), lambda i, j, kk: (kk, i // N_SHARDS)),
        acc_shape=(rs, dg), out_shapes=[_sds((N_SHARDS, N_GROUPS, rs, dg), BF16)],
        out_specs=[pl.BlockSpec((None, None, rs, dg), lambda i, j, kk: (i % N_SHARDS, i // N_SHARDS, 0, 0))],
        epilogue=lambda acc: (acc,), name="pool_mm_bwd_w")[0]


def _mlp_fwd(h, hn, w1g, w2_after, pin=None):
    t, d = hn.shape
    f4 = w1g.shape[2]
    dff = N_SHARDS * f4
    tm, tn, tk = _tile(t, 1024), _tile(f4, 1024), _tile(d, 4096)
    nb = f4 // tn
    o_spec = pl.BlockSpec((tm, tn), lambda i, j, kk: (i, j))

    def act(acc):
        r = jnp.maximum(acc, 0.0)
        return r, r * r

    r, u = _matmul(
        hn, w1g, dims=_NN, grid=(t // tm, dff // tn, d // tk),
        a_spec=pl.BlockSpec((tm, tk), lambda i, j, kk: (i, kk)),
        b_spec=pl.BlockSpec((None, tk, tn), lambda i, j, kk: (j // nb, kk, j % nb)),
        acc_shape=(tm, tn), out_shapes=[_sds((t, dff), BF16), _sds((t, dff), BF16)], out_specs=[o_spec, o_spec],
        epilogue=act, name="mlp_up", pin=pin)
    w2 = w2_after(u)
    (out,) = _mm(u, w2, "nn", out_dtypes=[F32], epilogue=lambda acc, hv: (hv + acc,), extras=(h,), name="mlp_down",
                 tk=2048)
    return out, r, u, w2


def _mlp_bwd(dhb, hn, r, u, w1g, w2):
    t, d = hn.shape
    f4 = w1g.shape[2]
    dff = N_SHARDS * f4
    (da,) = _mm(dhb, w2, "nt", out_dtypes=[BF16], epilogue=lambda acc, rv: (acc * (2.0 * rv.astype(F32)),),
                extras=(r,), name="mlp_bwd_da")
    (dw2,) = _mm(u, dhb, "tn", out_dtypes=[BF16], name="mlp_bwd_dw2")
    tok = yield dw2.reshape(N_SHARDS, f4, d)
    tm, tn, tk = _tile(d, 1024), _tile(f4, 1024), _tile(t, 4096)
    nb = f4 // tn
    (dw1,) = _matmul(
        hn, da, dims=_TN, grid=(d // tm, dff // tn, t // tk),
        a_spec=pl.BlockSpec((tk, tm), lambda i, j, kk: (kk, i)),
        b_spec=pl.BlockSpec((tk, tn), lambda i, j, kk: (kk, j)),
        acc_shape=(tm, tn), out_shapes=[_sds((N_SHARDS, d, f4), BF16)],
        out_specs=[pl.BlockSpec((None, tm, tn), lambda i, j, kk: (j // nb, i, j % nb))],
        epilogue=lambda acc: (acc,), name="mlp_bwd_dw1", pin=tok)
    tok = yield dw1
    tm, tn, tk = _tile(t, 1024), _tile(d, 1024), _tile(f4, 4096)
    nbk = f4 // tk
    (dhn,) = _matmul(
        da, w1g, dims=_NT, grid=(t // tm, d // tn, dff // tk),
        a_spec=pl.BlockSpec((tm, tk), lambda i, j, kk: (i, kk)),
        b_spec=pl.BlockSpec((None, tn, tk), lambda i, j, kk: (kk // nbk, j, kk % nbk)),
        acc_shape=(tm, tn), out_shapes=[_sds((t, d), F32)],
        out_specs=[pl.BlockSpec((tm, tn), lambda i, j, kk: (i, j))], epilogue=lambda acc: (acc,), name="mlp_bwd_dhn",
        pin=tok)
    yield dhn


def _split3(x):
    a = x.astype(BF16)
    r1 = x - a.astype(F32)
    b = r1.astype(BF16)
    c = (r1 - b.astype(F32)).astype(BF16)
    return a, b, c


def _dot(a, b, dims):
    return lax.dot_general(a.astype(BF16), b.astype(BF16), dims, preferred_element_type=F32)


def _chunk_terms(q, k, g, rev, scale):
    c = q.shape[0]
    ri = lax.broadcasted_iota(jnp.int32, (c, c), 0)
    ci = lax.broadcasted_iota(jnp.int32, (c, c), 1)
    seen = (ci >= ri) if rev else (ci <= ri)
    tri = seen.astype(BF16)
    g1, g2, g3 = _split3(g)
    b = (lax.dot_general(tri, g1, _NN, preferred_element_type=F32)
         + lax.dot_general(tri, g2, _NN, preferred_element_type=F32)
         + lax.dot_general(tri, g3, _NN, preferred_element_type=F32))
    mid = c // 2 if rev else c // 2 - 1
    last = 0 if rev else c - 1
    rows = lax.broadcasted_iota(jnp.int32, b.shape, 0)
    b_mid = jnp.sum(jnp.where(rows == mid, b, 0.0), axis=0, keepdims=True)
    b_last = jnp.sum(jnp.where(rows == last, b, 0.0), axis=0, keepdims=True)
    qs = q * scale
    e1 = jnp.exp(b - b_mid)
    e2 = jnp.exp(b_mid - b)
    eb = jnp.exp(b)
    el = jnp.exp(b_last - b)
    return dict(seen=seen, tri=tri, mid=mid, last=last, e1=e1, e2=e2, eb=eb, el=el, a=jnp.exp(b_last),
                qe=qs * e1, ke=k * e2, qi=qs * eb, ks=k * el)


def _step_terms(q_ref, k_ref, g_ref, rev, scale):
    lo, hi = slice(0, CHUNK), slice(CHUNK, STEP)
    rows = (hi, lo) if rev else (lo, hi)
    return rows, [_chunk_terms(q_ref[r, :], k_ref[r, :], g_ref[r, :], rev, scale) for r in rows]


def _gla_fwd(pm, la, seq):
    t = pm.shape[0]
    d = pm.shape[1] // 3
    dk, dv = d // 2 // N_HEADS, d // N_HEADS
    nb, nc = t // seq, seq // STEP
    scale = dk ** -0.5
    kq, kk_, kv = 0, N_HEADS, (d // dv)

    def body(qf, kf, vf, gf, qb, kb, vb, gb, of_ref, ob_ref, stf_ref, stb_ref, sf, sb):
        n = pl.program_id(1)

        @pl.when(n == 0)
        def _():
            sf[...] = jnp.zeros_like(sf)
            sb[...] = jnp.zeros_like(sb)

        for (q_ref, k_ref, v_ref, g_ref, o_ref, st_ref, s_ref, rev) in (
                (qf, kf, vf, gf, of_ref, stf_ref, sf, False), (qb, kb, vb, gb, ob_ref, stb_ref, sb, True)):
            rows, (t0, t1) = _step_terms(q_ref, k_ref, g_ref, rev, scale)
            v0, v1 = v_ref[rows[0], :], v_ref[rows[1], :]
            st = s_ref[...]
            stb = st.astype(BF16)
            st_ref[...] = stb
            sc0 = jnp.where(t0["seen"], _dot(t0["qe"], t0["ke"], _NT), 0.0)
            sc1 = jnp.where(t1["seen"], _dot(t1["qe"], t1["ke"], _NT), 0.0)
            cross = _dot(t1["qi"], t0["ks"], _NT)
            qi_all = jnp.concatenate([t0["qi"], t1["qi"] * t0["a"]], axis=0)
            o_inter = lax.dot_general(qi_all.astype(BF16), stb, _NT, preferred_element_type=F32)
            o_ref[rows[0], :] = _dot(sc0, v0, _NN) + o_inter[:CHUNK]
            o_ref[rows[1], :] = _dot(sc1, v1, _NN) + _dot(cross, v0, _NN) + o_inter[CHUNK:]
            ks_all = jnp.concatenate([t0["ks"] * t1["a"], t1["ks"]], axis=0)
            s_ref[...] = st * (t0["a"] * t1["a"]) + _dot(jnp.concatenate([v0, v1], axis=0), ks_all, _TN)

    def row(bh, n, rev):
        return (bh // N_HEADS) * nc + (nc - 1 - n if rev else n)

    def specs(rev):
        return [
            pl.BlockSpec((STEP,dk), lambda bh, n: (row(bh, n, rev), kq + bh % N_HEADS)),
            pl.BlockSpec((STEP,dk), lambda bh, n: (row(bh, n, rev), kk_ + bh % N_HEADS)),
            pl.BlockSpec((STEP,dv), lambda bh, n: (row(bh, n, rev), kv + bh % N_HEADS)),
            pl.BlockSpec((STEP,dk), lambda bh, n: (row(bh, n, rev), (N_HEADS if rev else 0) + bh % N_HEADS)),
        ]

    def o_spec(rev):
        return pl.BlockSpec((STEP,dv), lambda bh, n: (row(bh, n, rev), bh % N_HEADS))

    def st_spec(rev):
        return pl.BlockSpec((None, None, dv, dk), lambda bh, n: (bh, nc - 1 - n if rev else n, 0, 0))

    sf_, sb_ = specs(False), specs(True)
    return pl.pallas_call(
        body, grid=(nb * N_HEADS, nc),
        in_specs=[*sf_, *sb_],
        out_specs=[o_spec(False), o_spec(True), st_spec(False), st_spec(True)],
        out_shape=[_sds((t, d), F32), _sds((t, d), F32),
                   _sds((nb * N_HEADS, nc, dv, dk), BF16), _sds((nb * N_HEADS, nc, dv, dk), BF16)],
        scratch_shapes=[pltpu.VMEM((dv, dk), F32), pltpu.VMEM((dv, dk), F32)],
        compiler_params=_params("parallel", "arbitrary"), name="gla_scan_fwd",
    )(pm, pm, pm, la, pm, pm, pm, la)


def _gla_bwd(pm, la, do, st_f, st_b, seq):
    t = pm.shape[0]
    d = pm.shape[1] // 3
    dk, dv = d // 2 // N_HEADS, d // N_HEADS
    kd = dk * N_HEADS
    nb, nc = t // seq, seq // STEP
    scale = dk ** -0.5
    kq, kk_, kv = 0, N_HEADS, (d // dv)

    def body(qf, kf, vf, gf, dof, stf, qb, kb, vb, gb, dob, stb_,
             dqf, dkf, dvf, dgf, dqb, dkb, dvb, dgb, dsf, dsb):
        n = pl.program_id(1)

        @pl.when(n == 0)
        def _():
            dsf[...] = jnp.zeros_like(dsf)
            dsb[...] = jnp.zeros_like(dsb)

        for (q_ref, k_ref, v_ref, g_ref, do_ref, st_ref, dq_ref, dk_ref, dv_ref, dg_ref, ds_ref, rev) in (
                (qf, kf, vf, gf, dof, stf, dqf, dkf, dvf, dgf, dsf, False),
                (qb, kb, vb, gb, dob, stb_, dqb, dkb, dvb, dgb, dsb, True)):
            rows, (t0, t1) = _step_terms(q_ref, k_ref, g_ref, rev, scale)
            v0, v1 = v_ref[rows[0], :], v_ref[rows[1], :]
            do0, do1 = do_ref[rows[0], :], do_ref[rows[1], :]
            st = st_ref[...]
            ds = ds_ref[...]
            dsb16 = ds.astype(BF16)
            a0, a1 = t0["a"], t1["a"]
            sc0 = jnp.where(t0["seen"], _dot(t0["qe"], t0["ke"], _NT), 0.0)
            sc1 = jnp.where(t1["seen"], _dot(t1["qe"], t1["ke"], _NT), 0.0)
            cross = _dot(t1["qi"], t0["ks"], _NT)
            dsc0 = jnp.where(t0["seen"], _dot(do0, v0, _NT), 0.0)
            dsc1 = jnp.where(t1["seen"], _dot(do1, v1, _NT), 0.0)
            dcross = _dot(do1, v0, _NT)
            qi_all = jnp.concatenate([t0["qi"], t1["qi"] * a0], axis=0)
            ks_all = jnp.concatenate([t0["ks"] * a1, t1["ks"]], axis=0)
            v_all = jnp.concatenate([v0, v1], axis=0).astype(BF16)
            do_all = jnp.concatenate([do0, do1], axis=0).astype(BF16)
            dks_all = lax.dot_general(v_all, dsb16, _NN, preferred_element_type=F32)
            dv_all = lax.dot_general(ks_all.astype(BF16), dsb16, _NT, preferred_element_type=F32)
            dqi_all = lax.dot_general(do_all, st, _NN, preferred_element_type=F32)
            da01 = jnp.sum(ds * st.astype(F32), axis=0, keepdims=True)
            dv_ref[rows[0], :] = (_dot(sc0, do0, _TN) + _dot(cross, do1, _TN) + dv_all[:CHUNK]).astype(BF16)
            dv_ref[rows[1], :] = (_dot(sc1, do1, _TN) + dv_all[CHUNK:]).astype(BF16)
            dqi0 = dqi_all[:CHUNK]
            dqi1 = dqi_all[CHUNK:] * a0 + _dot(dcross, t0["ks"], _NN)
            dks0 = dks_all[:CHUNK] * a1 + _dot(dcross, t1["qi"], _TN)
            dks1 = dks_all[CHUNK:]
            da0 = da01 * a1 + jnp.sum(dqi_all[CHUNK:] * t1["qi"], axis=0, keepdims=True)
            da1 = da01 * a0 + jnp.sum(dks_all[:CHUNK] * t0["ks"], axis=0, keepdims=True)
            for r, tm, dsc, dqi, dks, da in ((rows[0], t0, dsc0, dqi0, dks0, da0), (rows[1], t1, dsc1, dqi1, dks1, da1)):
                dqe = _dot(dsc, tm["ke"], _NN)
                dke = _dot(dsc, tm["qe"], _TN)
                dq_ref[r, :] = ((dqe * tm["e1"] + dqi * tm["eb"]) * scale).astype(BF16)
                dk_ref[r, :] = (dke * tm["e2"] + dks * tm["el"]).astype(BF16)
                t_q, t_k, t_s = dqe * tm["qe"], dke * tm["ke"], dks * tm["ks"]
                db = t_q - t_k + dqi * tm["qi"] - t_s
                mid_row = jnp.sum(t_k - t_q, axis=0, keepdims=True)
                last_row = jnp.sum(t_s, axis=0, keepdims=True) + da * tm["a"]
                ridx = lax.broadcasted_iota(jnp.int32, db.shape, 0)
                db = db + jnp.where(ridx == tm["mid"], mid_row, 0.0) + jnp.where(ridx == tm["last"], last_row, 0.0)
                d1, d2, d3 = _split3(db)
                dg_ref[r, :] = (lax.dot_general(tm["tri"], d1, _TN, preferred_element_type=F32)
                                + lax.dot_general(tm["tri"], d2, _TN, preferred_element_type=F32)
                                + lax.dot_general(tm["tri"], d3, _TN, preferred_element_type=F32))
            ds_ref[...] = ds * (a0 * a1) + lax.dot_general(do_all, qi_all.astype(BF16), _TN,
                                                           preferred_element_type=F32)

    def row(bh, n, rev):
        return (bh // N_HEADS) * nc + (n if rev else nc - 1 - n)

    def specs(rev):
        return [
            pl.BlockSpec((STEP,dk), lambda bh, n: (row(bh, n, rev), kq + bh % N_HEADS)),
            pl.BlockSpec((STEP,dk), lambda bh, n: (row(bh, n, rev), kk_ + bh % N_HEADS)),
            pl.BlockSpec((STEP,dv), lambda bh, n: (row(bh, n, rev), kv + bh % N_HEADS)),
            pl.BlockSpec((STEP,dk), lambda bh, n: (row(bh, n, rev), (N_HEADS if rev else 0) + bh % N_HEADS)),
            pl.BlockSpec((STEP,dv), lambda bh, n: (row(bh, n, rev), bh % N_HEADS)),
            pl.BlockSpec((None, None, dv, dk), lambda bh, n: (bh, n if rev else nc - 1 - n, 0, 0)),
        ]

    def outs(rev):
        return [
            pl.BlockSpec((STEP,dk), lambda bh, n: (row(bh, n, rev), bh % N_HEADS)),
            pl.BlockSpec((STEP,dk), lambda bh, n: (row(bh, n, rev), bh % N_HEADS)),
            pl.BlockSpec((STEP,dv), lambda bh, n: (row(bh, n, rev), bh % N_HEADS)),
            pl.BlockSpec((STEP,dk), lambda bh, n: (row(bh, n, rev), bh % N_HEADS)),
        ]

    of_, ob_ = outs(False), outs(True)
    res = pl.pallas_call(
        body, grid=(nb * N_HEADS, nc),
        in_specs=[*specs(False), *specs(True)],
        out_specs=[*of_, *ob_],
        out_shape=[_sds((t, kd), BF16), _sds((t, kd), BF16), _sds((t, d), BF16), _sds((t, kd), F32),
                   _sds((t, kd), BF16), _sds((t, kd), BF16), _sds((t, d), BF16), _sds((t, kd), F32)],
        scratch_shapes=[pltpu.VMEM((dv, dk), F32), pltpu.VMEM((dv, dk), F32)],
        compiler_params=_params("parallel", "arbitrary"), name="gla_scan_bwd",
    )(pm, pm, pm, la, do, st_f, pm, pm, pm, la, do, st_b)
    return res


def _sigmoid(x):
    return 1.0 / (1.0 + jnp.exp(-x))


def _gla_post_fwd(o_f, o_b, pm, gn):
    t, d = o_f.shape
    dv = d // N_HEADS
    tr = _tile(t, 512)
    gate_blk = 2 * d // dv

    def body(of_ref, ob_ref, gt_ref, gn_ref, out_ref):
        o = of_ref[...] + ob_ref[...]
        n = o * lax.rsqrt(jnp.mean(o * o, axis=-1, keepdims=True) + EPS) * gn_ref[...]
        gt = gt_ref[...]
        out_ref[...] = (n * (gt * _sigmoid(gt))).astype(BF16)

    blk = pl.BlockSpec((tr, dv), lambda i, h: (i, h))
    return pl.pallas_call(
        body, grid=(t // tr, N_HEADS),
        in_specs=[blk, blk, pl.BlockSpec((tr, dv), lambda i, h: (i, gate_blk + h)),
                  pl.BlockSpec((1, dv), lambda i, h: (0, 0))],
        out_specs=blk, out_shape=_sds((t, d), BF16), compiler_params=_params("parallel", "parallel"),
        name="gla_post_fwd")(o_f, o_b, pm, gn)


def _gla_post_bwd(dog, o_f, o_b, pm, gn):
    t, d = o_f.shape
    dv = d // N_HEADS
    tr = _tile(t, 256)
    gate_blk = 2 * d // dv

    def body(dog_ref, of_ref, ob_ref, gt_ref, gn_ref, do_ref, dgt_ref, dgn_ref):
        o = of_ref[...] + ob_ref[...]
        rr = lax.rsqrt(jnp.mean(o * o, axis=-1, keepdims=True) + EPS)
        on = o * rr
        gnv = gn_ref[...]
        gt = gt_ref[...]
        sg = _sigmoid(gt)
        sl = gt * sg
        dg_out = dog_ref[...]
        dn = dg_out * sl
        dgt_ref[...] = dg_out * (on * gnv) * (sg * (1.0 + gt * (1.0 - sg)))
        gdn = dn * gnv
        do_ref[...] = rr * (gdn - on * jnp.mean(gdn * on, axis=-1, keepdims=True))
        part = jnp.sum(dn * on, axis=0, keepdims=True)
        first = (pl.program_id(0) == 0) & (pl.program_id(1) == 0)

        @pl.when(first)
        def _():
            dgn_ref[...] = part

        @pl.when(jnp.logical_not(first))
        def _():
            dgn_ref[...] += part

    blk = pl.BlockSpec((tr, dv), lambda i, h: (i, h))
    vec = pl.BlockSpec((1, dv), lambda i, h: (0, 0))
    return pl.pallas_call(
        body, grid=(t // tr, N_HEADS),
        in_specs=[blk, blk, blk, pl.BlockSpec((tr, dv), lambda i, h: (i, gate_blk + h)), vec],
        out_specs=[blk, blk, vec], out_shape=[_sds((t, d), F32), _sds((t, d), F32), _sds((1, dv), F32)],
        compiler_params=_params("arbitrary", "arbitrary"), name="gla_post_bwd")(dog, o_f, o_b, pm, gn)


def _gla_dp(dq_f, dq_b, dk_f, dk_b, dv_f, dv_b, dgate):
    t, d = dv_f.shape
    kd = dq_f.shape[1]
    tr = _tile(t, 128)

    def body(a1, a2, b1, b2, c1, c2, g, o_ref):
        o_ref[:, 0:kd] = (a1[...].astype(F32) + a2[...].astype(F32)).astype(BF16)
        o_ref[:, kd:2 * kd] = (b1[...].astype(F32) + b2[...].astype(F32)).astype(BF16)
        o_ref[:, 2 * kd:2 * kd + d] = (c1[...].astype(F32) + c2[...].astype(F32)).astype(BF16)
        o_ref[:, 2 * kd + d:] = g[...].astype(BF16)

    sk = pl.BlockSpec((tr, kd), lambda i: (i, 0))
    sd = pl.BlockSpec((tr, d), lambda i: (i, 0))
    return pl.pallas_call(
        body, grid=(t // tr,), in_specs=[sk, sk, sk, sk, sd, sd, sd],
        out_specs=pl.BlockSpec((tr, 3 * d), lambda i: (i, 0)), out_shape=_sds((t, 3 * d), BF16),
        compiler_params=_params("parallel"), name="gla_dp")(dq_f, dq_b, dk_f, dk_b, dv_f, dv_b, dgate)


def _adamw(w, g, m, v, name):
    r, c = w.shape
    tr = _tile(r, 128)
    bc1 = 1.0 - ADAM_B1 ** ADAM_STEP
    bc2 = 1.0 - ADAM_B2 ** ADAM_STEP

    def body(w_ref, g_ref, m_ref, v_ref, d_ref, nm_ref, nv_ref):
        gv = g_ref[...]
        mn = ADAM_B1 * m_ref[...] + (1.0 - ADAM_B1) * gv
        vn = ADAM_B2 * v_ref[...] + (1.0 - ADAM_B2) * (gv * gv)
        m_hat = mn / bc1
        v_hat = vn / bc2
        d_ref[...] = -ADAM_LR * (m_hat / (jnp.sqrt(v_hat) + ADAM_EPS) + ADAM_WD * w_ref[...])
        nm_ref[...] = mn
        nv_ref[...] = vn

    blk = pl.BlockSpec((tr, c), lambda i: (i, 0))
    return pl.pallas_call(
        body, grid=(r // tr,), in_specs=[blk] * 4, out_specs=[blk] * 3, out_shape=[_sds((r, c), F32)] * 3,
        compiler_params=_params("parallel"), name=name)(w, g, m, v)


def _adamw_half(w, g, m, v, layer, half, prev, pin, name, transposed=False):
    bc1 = 1.0 - ADAM_B1 ** ADAM_STEP
    bc2 = 1.0 - ADAM_B2 ** ADAM_STEP
    n_skip = 1 + (0 if prev is None else 4)
    if transposed:
        nl, c, r = w.shape
        tr = _tile(r // 2, 128)
        nblk = r // 2 // tr
        lay = pl.BlockSpec((None, c, tr), lambda i, h_ref: (layer, 0, h_ref[0] * nblk + i))
        g_spec = pl.BlockSpec((c, tr), lambda i, h_ref: (0, i))
    else:
        nl, r, c = w.shape
        tr = _tile(r // 2, 128)
        nblk = r // 2 // tr
        lay = pl.BlockSpec((None, tr, c), lambda i, h_ref: (layer, h_ref[0] * nblk + i, 0))
        g_spec = pl.BlockSpec((tr, c), lambda i, h_ref: (i, 0))

    def body(h_ref, w_ref, g_ref, m_ref, v_ref, *rest):
        go_ref, d_ref, nm_ref, nv_ref = rest[n_skip:]
        gv = g_ref[...]
        mn = ADAM_B1 * m_ref[...] + (1.0 - ADAM_B1) * gv
        vn = ADAM_B2 * v_ref[...] + (1.0 - ADAM_B2) * (gv * gv)
        m_hat = mn / bc1
        v_hat = vn / bc2
        d_ref[...] = -ADAM_LR * (m_hat / (jnp.sqrt(v_hat) + ADAM_EPS) + ADAM_WD * w_ref[...])
        nm_ref[...] = mn
        nv_ref[...] = vn
        go_ref[...] = gv

    prev = tuple(prev or ())
    grid_spec = pltpu.PrefetchScalarGridSpec(
        num_scalar_prefetch=1, grid=(nblk,),
        in_specs=[lay, g_spec, lay, lay, pl.BlockSpec(TOKEN, lambda i, h_ref: (0, 0))] + [ANY] * len(prev),
        out_specs=[lay] * 4)
    return pl.pallas_call(
        body, grid_spec=grid_spec, out_shape=[_sds(w.shape, F32)] * 4,
        input_output_aliases={6 + k: k for k in range(len(prev))},
        compiler_params=_params("parallel"), name=name)(half, w, g, m, v, pin, *prev)


def _sum8(parts, own, me):
    _, n, _ = parts.shape

    def body(me_ref, p_ref, own_ref, o_ref):
        acc = None
        for i in range(8):
            term = jnp.where(me_ref[0] == i, own_ref[...], p_ref[i])
            acc = term if acc is None else acc + term
        o_ref[...] = acc

    grid_spec = pltpu.PrefetchScalarGridSpec(
        num_scalar_prefetch=1, grid=(1,),
        in_specs=[pl.BlockSpec((8, n, 128), lambda i, me_ref: (0, 0, 0)),
                  pl.BlockSpec((n, 128), lambda i, me_ref: (0, 0))],
        out_specs=pl.BlockSpec((n, 128), lambda i, me_ref: (0, 0)))
    return pl.pallas_call(body, grid_spec=grid_spec, out_shape=_sds((n, 128), F32), name="sum8")(me, parts, own)


def _pair_sum(ga, recv, c):
    _, _, rh, cols = ga.shape
    tr = _tile(rh, 256)

    def body(c_ref, a_ref, b_ref, o_ref):
        o_ref[...] = (a_ref[...].astype(F32) + b_ref[...].astype(F32)).astype(BF16)

    grid_spec = pltpu.PrefetchScalarGridSpec(
        num_scalar_prefetch=1, grid=(N_SHARDS, rh // tr),
        in_specs=[pl.BlockSpec((None, None, tr, cols), lambda s, i, c_ref: (s, c_ref[0], i, 0)),
                  pl.BlockSpec((None, tr, cols), lambda s, i, c_ref: (s, i, 0))],
        out_specs=pl.BlockSpec((None, tr, cols), lambda s, i, c_ref: (s, i, 0)))
    return pl.pallas_call(
        body, grid_spec=grid_spec, out_shape=_sds((N_SHARDS, rh, cols), BF16),
        compiler_params=_params("parallel", "parallel"), name="grad_pair_sum")(c, ga, recv)


def _quad_sum(pair, recv, s_me):
    _, rh, cols = pair.shape
    tr = _tile(rh, 256)

    def body(s_ref, p_ref, r1_ref, r2_ref, r3_ref, o_ref):
        o_ref[...] = ((p_ref[...].astype(F32) + r1_ref[...].astype(F32)) + r2_ref[...].astype(F32)) \
            + r3_ref[...].astype(F32)

    def blk(off):
        return pl.BlockSpec((None, tr, cols), lambda i, s_ref: ((s_ref[0] + off) % N_SHARDS, i, 0))

    grid_spec = pltpu.PrefetchScalarGridSpec(
        num_scalar_prefetch=1, grid=(rh // tr,), in_specs=[blk(0), blk(1), blk(2), blk(3)],
        out_specs=pl.BlockSpec((tr, cols), lambda i, s_ref: (i, 0)))
    return pl.pallas_call(
        body, grid_spec=grid_spec, out_shape=_sds((rh, cols), F32),
        compiler_params=_params("parallel"), name="grad_quad_sum")(s_me, pair, recv, recv, recv)


def _fill_own(w, layer, s_me, pin):
    _, rows, cols = w.shape
    rh = rows // 2
    tr = _tile(rh, 256)
    nblk = rh // tr
    pins = () if pin is None else (pin,)

    def body(s_ref, w_ref, *rest):
        rest[-1][...] = w_ref[...].astype(BF16)

    grid_spec = pltpu.PrefetchScalarGridSpec(
        num_scalar_prefetch=1, grid=(2, nblk),
        in_specs=[pl.BlockSpec((None, tr, cols), lambda h, i, s_ref: (layer, h * nblk + i, 0)),
                  *[pl.BlockSpec(TOKEN, lambda h, i, s_ref: (0, 0)) for _ in pins]],
        out_specs=pl.BlockSpec((None, None, tr, cols), lambda h, i, s_ref: (s_ref[0], h, i, 0)))
    return pl.pallas_call(
        body, grid_spec=grid_spec, out_shape=_sds((N_SHARDS, 2, rh, cols), BF16),
        compiler_params=_params("parallel", "parallel"), name="weight_fill_own")(s_me, w, *pins)


def _coords():
    return lax.axis_index("x"), lax.axis_index("y"), lax.axis_index("c")


def _other_chips(x, y):
    return [(x, 1 - y), (1 - x, y), (1 - x, 1 - y)]


def _bcast8(buf):
    n = buf.shape[0]

    def body(b_ref, o_ref, send_sems, recv_sems):
        x, y, c = _coords()
        me = 4 * x + 2 * y + c
        o_ref[me] = b_ref[...]
        copies = []
        for k in range(1, 8):
            peer = (x ^ (k >> 2), y ^ ((k >> 1) & 1), c ^ (k & 1))
            copies.append(pltpu.make_async_remote_copy(
                src_ref=b_ref, dst_ref=o_ref.at[me], send_sem=send_sems.at[k - 1], recv_sem=recv_sems.at[k - 1],
                device_id=peer, device_id_type=MESH))
        for cp in copies:
            cp.start()
        for k in range(1, 8):
            pltpu.make_async_remote_copy(
                src_ref=b_ref, dst_ref=o_ref.at[me ^ k], send_sem=send_sems.at[k - 1], recv_sem=recv_sems.at[k - 1],
                device_id=(x, y, c), device_id_type=MESH).wait_recv()
        for cp in copies:
            cp.wait_send()

    return pl.pallas_call(
        body, out_shape=_sds((8, n, 128), F32),
        in_specs=[pl.BlockSpec(memory_space=pltpu.VMEM)], out_specs=pl.BlockSpec(memory_space=pltpu.VMEM),
        scratch_shapes=[pltpu.SemaphoreType.DMA((7,)), pltpu.SemaphoreType.DMA((7,))],
        compiler_params=pltpu.CompilerParams(vmem_limit_bytes=VMEM_LIMIT), name="bcast8")(buf)


HBM = pl.BlockSpec(memory_space=pltpu.HBM)
SEM = pl.BlockSpec(memory_space=pltpu.SEMAPHORE)
SIDE = pltpu.SideEffectType.DATAFLOW_SIDE_EFFECTING
TOKEN = (8, 128)


def _in_hbm(a):
    return pltpu.with_memory_space_constraint(a, pltpu.HBM)


def _sibling_start(src, pieces, land_shape, after, name):
    n = len(pieces(None, None, None))

    def body(s_ref, land_ref, after_ref, send, recv, land_thru, token):
        x, y, c = _coords()
        for k, (a, b) in enumerate(pieces(s_ref, land_ref, c)):
            pltpu.make_async_remote_copy(
                src_ref=a, dst_ref=b, send_sem=send.at[k], recv_sem=recv.at[k], device_id=(x, y, 1 - c),
                device_id_type=MESH).start()
        token[...] = jnp.zeros_like(token)

    return pl.pallas_call(
        body, name=name,
        out_shape=(pltpu.SemaphoreType.DMA((n,)), pltpu.SemaphoreType.DMA((n,)), pltpu.HBM(land_shape, src.dtype),
                   _sds(TOKEN, F32)),
        in_specs=(HBM, HBM, ANY), out_specs=(SEM, SEM, HBM, pl.BlockSpec(memory_space=pltpu.VMEM)),
        input_output_aliases={1: 2}, compiler_params=pltpu.CompilerParams(has_side_effects=SIDE),
    )(_in_hbm(src), _in_hbm(lax.empty(land_shape, src.dtype)), after)


def _sibling_wait(send, recv, src, land, pieces, after, name):
    def body(s_ref, land_ref, send, recv, after_ref, land_out):
        x, y, c = _coords()
        for k, (a, b) in enumerate(pieces(s_ref, land_ref, c)):
            cp = pltpu.make_async_remote_copy(
                src_ref=a, dst_ref=b, send_sem=send.at[k], recv_sem=recv.at[k], device_id=(x, y, 1 - c),
                device_id_type=MESH)
            cp.wait_send()
            cp.wait_recv()

    return pl.pallas_call(
        body, name=name, out_shape=pltpu.HBM(land.shape, land.dtype), in_specs=(HBM, HBM, SEM, SEM, ANY),
        out_specs=HBM, input_output_aliases={1: 0}, compiler_params=pltpu.CompilerParams(has_side_effects=SIDE),
    )(src, land, send, recv, after)


def _swap_pieces(g_ref, land_ref, c):
    if g_ref is None:
        return [None] * N_SHARDS
    return [(g_ref.at[s, 1 - c], land_ref.at[s]) for s in range(N_SHARDS)]


def _whole_piece(r_ref, land_ref, c):
    return [(r_ref, land_ref)]


def _bcast_copies(b_ref, land_ref, send, recv):
    x, y, c = _coords()
    me = 4 * x + 2 * y + c
    return [pltpu.make_async_remote_copy(
        src_ref=b_ref, dst_ref=land_ref.at[me], send_sem=send.at[k - 1], recv_sem=recv.at[k - 1],
        device_id=(x ^ (k >> 2), y ^ ((k >> 1) & 1), c ^ (k & 1)), device_id_type=MESH) for k in range(1, 8)]


def _bcast_start(buf, after, name):
    def body(b_ref, land_ref, after_ref, send, recv, land_thru, token):
        for cp in _bcast_copies(b_ref, land_ref, send, recv):
            cp.start()
        token[...] = jnp.zeros_like(token)

    shape = (8,) + buf.shape
    return pl.pallas_call(
        body, name=name,
        out_shape=(pltpu.SemaphoreType.DMA((7,)), pltpu.SemaphoreType.DMA((7,)), pltpu.HBM(shape, buf.dtype),
                   _sds(TOKEN, F32)),
        in_specs=(HBM, HBM, ANY), out_specs=(SEM, SEM, HBM, pl.BlockSpec(memory_space=pltpu.VMEM)),
        input_output_aliases={1: 2}, compiler_params=pltpu.CompilerParams(has_side_effects=SIDE),
    )(_in_hbm(buf), _in_hbm(lax.empty(shape, buf.dtype)), after)


def _bcast_wait(send, recv, buf, land, after, name):
    def body(b_ref, land_ref, send, recv, after_ref, land_out):
        for cp in _bcast_copies(b_ref, land_ref, send, recv):
            cp.wait_send()
            cp.wait_recv()

    return pl.pallas_call(
        body, name=name, out_shape=pltpu.HBM(land.shape, land.dtype), in_specs=(HBM, HBM, SEM, SEM, ANY),
        out_specs=HBM, input_output_aliases={1: 0}, compiler_params=pltpu.CompilerParams(has_side_effects=SIDE),
    )(buf, land, send, recv, after)


def _scatter_start(part, name):
    def body(p_ref, land_ref, send, recv, p_thru, land_thru, token):
        x, y, c = _coords()
        s_me = 2 * x + y
        for j, (px, py) in enumerate(_other_chips(x, y)):
            pltpu.make_async_remote_copy(
                src_ref=p_ref.at[2 * px + py], dst_ref=land_ref.at[s_me], send_sem=send.at[j], recv_sem=recv.at[j],
                device_id=(px, py, c), device_id_type=MESH).start()
        token[...] = jnp.zeros_like(token)

    buf = pltpu.HBM(part.shape, part.dtype)
    return pl.pallas_call(
        body, name=name,
        out_shape=(pltpu.SemaphoreType.DMA((3,)), pltpu.SemaphoreType.DMA((3,)), buf, buf, _sds(TOKEN, F32)),
        in_specs=(HBM, HBM), out_specs=(SEM, SEM, HBM, HBM, pl.BlockSpec(memory_space=pltpu.VMEM)),
        input_output_aliases={0: 2, 1: 3}, compiler_params=pltpu.CompilerParams(has_side_effects=SIDE),
    )(_in_hbm(part), _in_hbm(lax.empty(part.shape, part.dtype)))


def _scatter_wait(send, recv, part, land, after, name):
    def body(p_ref, land_ref, send, recv, after_ref, p_out, land_out):
        x, y, c = _coords()
        for j, (px, py) in enumerate(_other_chips(x, y)):
            cp = pltpu.make_async_remote_copy(
                src_ref=p_ref.at[2 * px + py], dst_ref=land_ref.at[2 * px + py], send_sem=send.at[j],
                recv_sem=recv.at[j], device_id=(px, py, c), device_id_type=MESH)
            cp.wait_send()
            cp.wait_recv()

    buf = pltpu.HBM(part.shape, part.dtype)
    return pl.pallas_call(
        body, name=name, out_shape=(buf, buf), in_specs=(HBM, HBM, SEM, SEM, ANY), out_specs=(HBM, HBM),
        input_output_aliases={0: 0, 1: 1}, compiler_params=pltpu.CompilerParams(has_side_effects=SIDE),
    )(part, land, send, recv, after)


def _gather_start(bufs, after, name):
    n = len(bufs)

    def body(*refs):
        ins = refs[:n]
        sems = refs[n + 1:3 * n + 1]
        token = refs[4 * n + 1]
        x, y, c = _coords()
        s_me = 2 * x + y
        for i in range(n):
            for j, (px, py) in enumerate(_other_chips(x, y)):
                pltpu.make_async_remote_copy(
                    src_ref=ins[i].at[s_me, c], dst_ref=ins[i].at[s_me, c], send_sem=sems[2 * i].at[j],
                    recv_sem=sems[2 * i + 1].at[j], device_id=(px, py, c), device_id_type=MESH).start()
        token[...] = jnp.zeros_like(token)

    res = pl.pallas_call(
        body, name=name,
        out_shape=(*[pltpu.SemaphoreType.DMA((3,))] * (2 * n), *[pltpu.HBM(b.shape, b.dtype) for b in bufs],
                   _sds(TOKEN, F32)),
        in_specs=(*[HBM] * n, ANY), out_specs=(*[SEM] * (2 * n), *[HBM] * n, pl.BlockSpec(memory_space=pltpu.VMEM)),
        input_output_aliases={i: 2 * n + i for i in range(n)},
        compiler_params=pltpu.CompilerParams(has_side_effects=SIDE),
    )(*[_in_hbm(b) for b in bufs], after)
    return [(res[2 * i], res[2 * i + 1]) for i in range(n)], list(res[2 * n:3 * n]), res[3 * n]


def _gather_wait(send, recv, buf, after, name):
    def body(b_ref, send, recv, after_ref, b_out):
        x, y, c = _coords()
        s_me = 2 * x + y
        for j, (px, py) in enumerate(_other_chips(x, y)):
            cp = pltpu.make_async_remote_copy(
                src_ref=b_ref.at[s_me, c], dst_ref=b_ref.at[2 * px + py, c], send_sem=send.at[j], recv_sem=recv.at[j],
                device_id=(px, py, c), device_id_type=MESH)
            cp.wait_send()
            cp.wait_recv()

    return pl.pallas_call(
        body, name=name, out_shape=pltpu.HBM(buf.shape, buf.dtype), in_specs=(HBM, SEM, SEM, ANY), out_specs=HBM,
        input_output_aliases={0: 0}, compiler_params=pltpu.CompilerParams(has_side_effects=SIDE),
    )(buf, send, recv, after)


def _pass_copies(b_ref, send, recv):
    x, y, c = _coords()
    return [pltpu.make_async_remote_copy(
        src_ref=b_ref.at[2 * px + py, c], dst_ref=b_ref.at[2 * px + py, c], send_sem=send.at[j], recv_sem=recv.at[j],
        device_id=(x, y, 1 - c), device_id_type=MESH) for j, (px, py) in enumerate(_other_chips(x, y))]


def _pass_start(buf, after, name):
    def body(b_ref, after_ref, send, recv, b_thru, token):
        for cp in _pass_copies(b_ref, send, recv):
            cp.start()
        token[...] = jnp.zeros_like(token)

    return pl.pallas_call(
        body, name=name,
        out_shape=(pltpu.SemaphoreType.DMA((3,)), pltpu.SemaphoreType.DMA((3,)), pltpu.HBM(buf.shape, buf.dtype),
                   _sds(TOKEN, F32)),
        in_specs=(HBM, ANY), out_specs=(SEM, SEM, HBM, pl.BlockSpec(memory_space=pltpu.VMEM)),
        input_output_aliases={0: 2}, compiler_params=pltpu.CompilerParams(has_side_effects=SIDE),
    )(_in_hbm(buf), after)


def _pass_wait(send, recv, buf, after, name):
    def body(b_ref, send, recv, after_ref, b_out):
        x, y, c = _coords()
        for j, (px, py) in enumerate(_other_chips(x, y)):
            pltpu.make_async_remote_copy(
                src_ref=b_ref.at[2 * px + py, c], dst_ref=b_ref.at[2 * px + py, 1 - c], send_sem=send.at[j],
                recv_sem=recv.at[j], device_id=(x, y, 1 - c), device_id_type=MESH).wait()

    return pl.pallas_call(
        body, name=name, out_shape=pltpu.HBM(buf.shape, buf.dtype), in_specs=(HBM, SEM, SEM, ANY), out_specs=HBM,
        input_output_aliases={0: 0}, compiler_params=pltpu.CompilerParams(has_side_effects=SIDE),
    )(buf, send, recv, after)


def _pass_on_halves(bufs):
    n = len(bufs)

    def body(*refs):
        ins, outs = refs[:n], refs[n:2 * n]
        send, recv = refs[2 * n:]
        x, y, c = _coords()
        copies = []
        for i in range(n):
            for j, (px, py) in enumerate(_other_chips(x, y)):
                cp = pltpu.make_async_remote_copy(
                    src_ref=ins[i].at[2 * px + py, c], dst_ref=outs[i].at[2 * px + py, c], send_sem=send.at[i, j],
                    recv_sem=recv.at[i, j], device_id=(x, y, 1 - c), device_id_type=MESH)
                cp.start()
                copies.append(cp)
        for i in range(n):
            for j, (px, py) in enumerate(_other_chips(x, y)):
                other = outs[i].at[2 * px + py, 1 - c]
                pltpu.make_async_remote_copy(
                    src_ref=other, dst_ref=other, send_sem=send.at[i, j], recv_sem=recv.at[i, j],
                    device_id=(x, y, c), device_id_type=MESH).wait_recv()
        for cp in copies:
            cp.wait_send()

    return pl.pallas_call(
        body, out_shape=[_sds(b.shape, b.dtype) for b in bufs],
        in_specs=[ANY] * n, out_specs=[ANY] * n, input_output_aliases={i: i for i in range(n)},
        scratch_shapes=[pltpu.SemaphoreType.DMA((n, 3)), pltpu.SemaphoreType.DMA((n, 3))],
        name="gather_pass_on")(*bufs)


def _to_rows(vec):
    n = -(-vec.shape[0] // 1024) * 1024
    return jnp.pad(vec, (0, n - vec.shape[0])).reshape(-1, 128)


def kernel(x, norm_mix, norm_mlp, norm_final, pool_w, pool_scale, gla_w_in, gla_w_up_f, gla_b_up_f, gla_w_up_b, gla_b_up_b, gla_g_norm, gla_w_out, mlp_w_in, mlp_w_out, loss_target, m_norm_mix, m_norm_mlp, m_norm_final, m_pool_w, m_pool_scale, m_gla_w_in, m_gla_w_up_f, m_gla_b_up_f, m_gla_w_up_b, m_gla_b_up_b, m_gla_g_norm, m_gla_w_out, m_mlp_w_in, m_mlp_w_out, v_norm_mix, v_norm_mlp, v_norm_final, v_pool_w, v_pool_scale, v_gla_w_in, v_gla_w_up_f, v_gla_b_up_f, v_gla_w_up_b, v_gla_b_up_b, v_gla_g_norm, v_gla_w_out, v_mlp_w_in, v_mlp_w_out):
    nb, seq, d = x.shape
    t = nb * seq
    dg = d // N_GROUPS
    kd = d // 2
    dv = d // N_HEADS
    pw = gla_w_in.shape[2]
    f4 = mlp_w_in.shape[2]
    dff = N_SHARDS * f4
    cx, cy, cc = _coords()
    s_me = 2 * cx + cy
    c_arr = jnp.reshape(cc, (1,)).astype(jnp.int32)
    s_arr = jnp.reshape(s_me, (1,)).astype(jnp.int32)

    xf = x.reshape(t, d)
    tgt = loss_target.reshape(t, d)

    ks = kd // N_SHARDS
    small = jnp.concatenate([gla_w_up_f[0].reshape(-1), gla_w_up_b[0].reshape(-1), gla_b_up_f[0], gla_b_up_b[0],
                             gla_g_norm[0]])
    small_all = _bcast8(_to_rows(small))
    pool_rows = N_GROUPS * (dg // N_SHARDS)
    gsems, gbufs, gtok = _gather_start([
        _fill_own(pool_w.reshape(1, pool_rows, dg), 0, s_arr, None), _fill_own(mlp_w_in, 0, s_arr, None)],
        small_all, "gather_start_first")
    gsems2, gbufs2, gtok = _gather_start([
        _fill_own(mlp_w_out, 0, s_arr, gtok), _fill_own(gla_w_in, 0, s_arr, gtok),
        _fill_own(gla_w_out, 0, s_arr, gtok), _fill_own(mlp_w_in, 1, s_arr, gtok),
        _fill_own(mlp_w_out, 1, s_arr, gtok)], gtok, "gather_start_rest")
    gsems, gbufs = gsems + gsems2, gbufs + gbufs2
    small_all = small_all[::2].reshape(N_SHARDS, -1)

    def arrived(i, after, name):
        return _gather_wait(gsems[i][0], gsems[i][1], gbufs[i], after, "gather_wait_" + name)

    def weight(i, after, name):
        return _pass_on_halves([arrived(i, after, name)])[0]

    def pinned(vec, tok):
        return vec + tok[0:1, 0:1]

    o = 0
    wuf = jnp.transpose(small_all[:, o:o + GATE_RANK * ks].reshape(N_SHARDS, GATE_RANK, ks), (1, 0, 2)).reshape(GATE_RANK, kd)
    o += GATE_RANK * ks
    wub = jnp.transpose(small_all[:, o:o + GATE_RANK * ks].reshape(N_SHARDS, GATE_RANK, ks), (1, 0, 2)).reshape(GATE_RANK, kd)
    o += GATE_RANK * ks
    buf = small_all[:, o:o + ks].reshape(1, kd)
    o += ks
    bub = small_all[:, o:o + ks].reshape(1, kd)
    o += ks
    gn = small_all[:, o:o + dv // N_SHARDS].reshape(1, dv)
    w_up = jnp.zeros((R_PAD, 2 * kd), F32).at[:GATE_RANK, :kd].set(wuf).at[GATE_RANK:2 * GATE_RANK, kd:].set(wub)
    w_up = w_up.astype(BF16)
    b_up = jnp.concatenate([buf, bub], axis=1)

    hn0 = _rmsnorm_fwd(xf, pinned(norm_mix[0:1], gtok), F32, "norm_mix0")
    dm = _pool_apply(hn0, seq, False, "pool_diff")
    wp = weight(0, dm, "pool_w").reshape(N_SHARDS, N_GROUPS, dg // N_SHARDS, dg)
    h1, ypre = _pool_mm_fwd(dm, wp, xf, pool_scale)
    hn1 = _rmsnorm_fwd(h1, norm_mlp[0:1], BF16, "norm_mlp0")
    w1g = [weight(1, hn1, "mlp_w_in0").reshape(N_SHARDS, d, f4), None]
    w2g = [None, None]
    h2, r0, u0, w2g[0] = _mlp_fwd(h1, hn1, w1g[0], lambda u: weight(2, u, "mlp_w_out0").reshape(dff, d))
    hn2 = _rmsnorm_fwd(h2, norm_mix[1:2], BF16, "norm_mix1")
    win = jnp.transpose(weight(3, hn2, "gla_w_in").reshape(N_SHARDS, d, pw), (1, 0, 2)).reshape(d, N_SHARDS * pw)
    w_main_t = jnp.transpose(win)[:3 * d]
    w_r = jnp.pad(win[:, 3 * d:], ((0, 0), (0, R_PAD - 2 * GATE_RANK)))
    (pm,) = _mm(hn2, w_main_t, "nt", out_dtypes=[F32], name="gla_proj")
    (pr,) = _mm(hn2, w_r, "nn", out_dtypes=[BF16], name="gla_proj_r")
    pr = pr.at[:, ONES_COL].set(1.0)

    def log_decay(acc, bv):
        z = acc + bv
        return ((jnp.minimum(z, 0.0) - jnp.log(1.0 + jnp.exp(-jnp.abs(z)))) / GATE_TAU,)

    tm_, tn_ = _tile(t, 1024), _tile(kd, 1024)
    (la,) = _matmul(
        pr, w_up, dims=_NN, grid=(t // tm_, 2 * kd // tn_, 1),
        a_spec=pl.BlockSpec((tm_, R_PAD), lambda i, j, kk: (i, 0)),
        b_spec=pl.BlockSpec((R_PAD, tn_), lambda i, j, kk: (0, j)), acc_shape=(tm_, tn_),
        out_shapes=[_sds((t, 2 * kd), F32)], out_specs=[pl.BlockSpec((tm_, tn_), lambda i, j, kk: (i, j))],
        epilogue=log_decay, extras=(b_up,), extra_specs=[pl.BlockSpec((1, tn_), lambda i, j, kk: (0, j))],
        name="gla_gate_fwd")
    o_f, o_b, st_f, st_b = _gla_fwd(pm, la, seq)
    send4, recv4, buf4, tok = _pass_start(arrived(4, o_f, "gla_w_out"), o_f, "gather_pass_start_gla_w_out")
    og = _gla_post_fwd(o_f, o_b, pm, pinned(gn, tok))
    wout = _pass_wait(send4, recv4, buf4, og, "gather_pass_wait_gla_w_out").reshape(d, d)
    send5, recv5, buf5, tok = _pass_start(arrived(5, og, "mlp_w_in1"), og, "gather_pass_start_mlp_w_in1")
    (h3,) = _mm(og, wout, "nn", out_dtypes=[F32], epilogue=lambda acc, hv: (hv + acc,), extras=(h2,), name="gla_out",
                pin=tok)
    hn3 = _rmsnorm_fwd(h3, norm_mlp[1:2], BF16, "norm_mlp1")
    w1g[1] = _pass_wait(send5, recv5, buf5, hn3, "gather_pass_wait_mlp_w_in1").reshape(N_SHARDS, d, f4)
    send6, recv6, buf6, tok = _pass_start(arrived(6, hn3, "mlp_w_out1"), hn3, "gather_pass_start_mlp_w_out1")
    h4, r1, u1, w2g[1] = _mlp_fwd(
        h3, hn3, w1g[1], lambda u: _pass_wait(send6, recv6, buf6, u, "gather_pass_wait_mlp_w_out1").reshape(dff, d), tok)

    chains = []
    last_tok = [gtok]

    def swap_begin(g, name):
        g5 = g.reshape(N_SHARDS, 2, g.shape[1] // 2, g.shape[2])
        send, recv, land, tok = _sibling_start(g5, _swap_pieces, (N_SHARDS,) + g5.shape[2:], last_tok[0],
                                               "grad_swap_start_" + name)
        last_tok[0] = tok
        return (name, send, recv, g5, land), tok

    def swap_end(state, after):
        name, send, recv, g5, land = state
        land = _sibling_wait(send, recv, g5, land, _swap_pieces, after, "grad_swap_wait_" + name)
        pair = _pair_sum(g5, land, c_arr)
        send, recv, pair, land, tok = _scatter_start(pair, "grad_scatter_start_" + name)
        chains.append((name, send, recv, pair, land))
        last_tok[0] = tok
        return tok

    def mlp_backward(dhb, hn, r, u, w1, w2, name):
        steps = _mlp_bwd(dhb, hn, r, u, w1, w2)
        state, tok = swap_begin(next(steps), "mlp_w_out" + name)
        dw1 = steps.send(tok)
        swap_end(state, dw1)
        state, tok = swap_begin(dw1, "mlp_w_in" + name)
        dhn = steps.send(tok)
        return dhn, swap_end(state, dhn)

    loss_part, dh4, dh4b, dg_final = _final_bwd(h4, norm_final.reshape(1, d), tgt)
    dhn3, tok = mlp_backward(dh4b, hn3, r1, u1, w1g[1], w2g[1], "1")
    dh3, dh3b, dg_mlp1 = _rmsnorm_bwd(h3, pinned(norm_mlp[1:2], tok), dhn3, dh4, "norm_mlp1_bwd")

    (dog,) = _mm(dh3b, wout, "nt", out_dtypes=[F32], name="gla_out_bwd_x")
    (dwout,) = _mm(og, dh3b, "tn", out_dtypes=[BF16], name="gla_out_bwd_w")
    state, tok = swap_begin(dwout.reshape(N_SHARDS, d // N_SHARDS, d), "gla_w_out")
    do, dgate, dg_gn = _gla_post_bwd(dog, o_f, o_b, pm, pinned(gn, tok))
    dq_f, dk_f, dv_f, dla_f, dq_b, dk_b, dv_b, dla_b = _gla_bwd(pm, la, do, st_f, st_b, seq)
    b_up_p = pinned(b_up, swap_end(state, dq_f))
    nkb = kd // tn_

    def gate_bwd(acc, bv, dl_f, dl_b):
        z = acc + bv
        dl = jnp.where(pl.program_id(1) < nkb, dl_f, dl_b)
        return (dl * (1.0 / GATE_TAU) / (1.0 + jnp.exp(z)),)

    (dz,) = _matmul(
        pr, w_up, dims=_NN, grid=(t // tm_, 2 * kd // tn_, 1),
        a_spec=pl.BlockSpec((tm_, R_PAD), lambda i, j, kk: (i, 0)),
        b_spec=pl.BlockSpec((R_PAD, tn_), lambda i, j, kk: (0, j)), acc_shape=(tm_, tn_),
        out_shapes=[_sds((t, 2 * kd), BF16)], out_specs=[pl.BlockSpec((tm_, tn_), lambda i, j, kk: (i, j))],
        epilogue=gate_bwd, extras=(b_up_p, dla_f, dla_b),
        extra_specs=[pl.BlockSpec((1, tn_), lambda i, j, kk: (0, j)),
                     pl.BlockSpec((tm_, tn_), lambda i, j, kk: (i, jnp.minimum(j, nkb - 1))),
                     pl.BlockSpec((tm_, tn_), lambda i, j, kk: (i, jnp.maximum(j - nkb, 0)))],
        name="gla_gate_bwd")
    (dpr,) = _mm(dz, w_up, "nt", out_dtypes=[BF16], name="gla_gate_bwd_r")
    (dw_up,) = _mm(pr, dz, "tn", out_dtypes=[F32], name="gla_gate_bwd_w")
    dp = _gla_dp(dq_f, dq_b, dk_f, dk_b, dv_f, dv_b, dgate)
    (dw_main,) = _mm(hn2, dp, "tn", out_dtypes=[BF16], name="gla_proj_bwd_w")
    (dw_r,) = _mm(hn2, dpr, "tn", out_dtypes=[BF16], name="gla_proj_bwd_wr")
    dwin = jnp.concatenate([dw_main, dw_r[:, :2 * GATE_RANK]], axis=1)
    state, tok = swap_begin(jnp.transpose(dwin.reshape(d, N_SHARDS, pw), (1, 0, 2)), "gla_w_in")
    (dhn2_r,) = _mm(dpr, w_r, "nt", out_dtypes=[F32], name="gla_proj_bwd_xr")
    (dhn2,) = _mm(dp, w_main_t, "nn", out_dtypes=[F32], epilogue=lambda acc, e: (acc + e,), extras=(dhn2_r,),
                  name="gla_proj_bwd_x", tk=2048, pin=tok)
    tok = swap_end(state, dhn2)
    dh2, dh2b, dg_mix1 = _rmsnorm_bwd(h2, pinned(norm_mix[1:2], tok), dhn2, dh3, "norm_mix1_bwd")

    dhn1, tok = mlp_backward(dh2b, hn1, r0, u0, w1g[0], w2g[0], "0")
    dh1, _, dg_mlp0 = _rmsnorm_bwd(h1, pinned(norm_mlp[0:1], tok), dhn1, dh2, "norm_mlp0_bwd")

    dys, dg_pscale = _pool_scale_bwd(dh1, ypre, pool_scale)
    dwp = _pool_mm_bwd_w(dm, dys)
    state, tok = swap_begin(dwp.reshape(N_SHARDS, pool_rows, dg), "pool_w")
    dd = _pool_mm_bwd_x(dys, wp, tok)
    tok = swap_end(state, dd)
    dhn0 = _pool_apply(dd, seq, True, "pool_diff_bwd")
    dx, _, dg_mix0 = _rmsnorm_bwd(xf, pinned(norm_mix[0:1], tok), dhn0, dh1, "norm_mix0_bwd")

    dwuf, dwub = dw_up[:GATE_RANK, :kd], dw_up[GATE_RANK:2 * GATE_RANK, kd:]
    dbuf, dbub = dw_up[ONES_COL, :kd], dw_up[ONES_COL, kd:]
    pieces = [jnp.concatenate([dg_mix0, dg_mix1], 0), jnp.concatenate([dg_mlp0, dg_mlp1], 0), dg_final, dg_pscale,
              dwuf, dwub, dbuf, dbub, dg_gn]
    sizes = [p.size for p in pieces]
    packed = _to_rows(jnp.concatenate([p.reshape(-1) for p in pieces]))
    bsend, brecv, bland, _ = _bcast_start(packed, dx, "small_grads_start")

    loss = lax.psum(loss_part[0, 0], ("x", "y", "c"))

    weights = [norm_mix, norm_mlp, norm_final, pool_w, pool_scale, gla_w_in, gla_w_up_f, gla_b_up_f, gla_w_up_b,
               gla_b_up_b, gla_g_norm, gla_w_out, mlp_w_in, mlp_w_out]
    moms = [m_norm_mix, m_norm_mlp, m_norm_final, m_pool_w, m_pool_scale, m_gla_w_in, m_gla_w_up_f, m_gla_b_up_f,
            m_gla_w_up_b, m_gla_b_up_b, m_gla_g_norm, m_gla_w_out, m_mlp_w_in, m_mlp_w_out]
    vels = [v_norm_mix, v_norm_mlp, v_norm_final, v_pool_w, v_pool_scale, v_gla_w_in, v_gla_w_up_f, v_gla_b_up_f,
            v_gla_w_up_b, v_gla_b_up_b, v_gla_g_norm, v_gla_w_out, v_mlp_w_in, v_mlp_w_out]
    names = ["norm_mix", "norm_mlp", "norm_final", "pool_w", "pool_scale", "gla_w_in", "gla_w_up_f", "gla_b_up_f",
             "gla_w_up_b", "gla_b_up_b", "gla_g_norm", "gla_w_out", "mlp_w_in", "mlp_w_out"]
    index = {nm: k for k, nm in enumerate(names)}
    results = {}

    stacked = {"mlp_w_out1": ("mlp_w_out", 1), "mlp_w_in1": ("mlp_w_in", 1), "gla_w_out": ("gla_w_out", 0),
               "gla_w_in": ("gla_w_in", 0), "mlp_w_out0": ("mlp_w_out", 0), "mlp_w_in0": ("mlp_w_in", 0),
               "pool_w": ("pool_w", 0)}
    oc_arr = 1 - c_arr
    after = dx
    for name, send, recv, pair, land in chains:
        pair, land = _scatter_wait(send, recv, pair, land, after, "grad_scatter_wait_" + name)
        mine = _quad_sum(pair, land, s_arr)
        jsend, jrecv, jland, jtok = _sibling_start(mine, _whole_piece, mine.shape, after, "grad_join_start_" + name)
        nm, layer = stacked[name]
        w, m, v = weights[index[nm]], moms[index[nm]], vels[index[nm]]
        rows, cols = 2 * mine.shape[0], mine.shape[1]
        shp = (w.size // (rows * cols), rows, cols)
        w, m, v = w.reshape(shp), m.reshape(shp), v.reshape(shp)
        tr_ = nm == "gla_w_in"
        flip = (lambda a: jnp.swapaxes(a, -1, -2)) if tr_ else (lambda a: a)
        w, m, v = flip(w), flip(m), flip(v)
        res = _adamw_half(w, flip(mine), m, v, layer, c_arr, results.get(nm), jtok, "adamw_mine_" + name, tr_)
        theirs = _sibling_wait(jsend, jrecv, mine, jland, _whole_piece, res[1], "grad_join_wait_" + name)
        res = _adamw_half(w, flip(theirs), m, v, layer, oc_arr, res, jtok, "adamw_theirs_" + name, tr_)
        results[nm] = [flip(a) for a in res]
        after = res[1]

    bland = _bcast_wait(bsend, brecv, packed, bland, after, "small_grads_wait")
    me_arr = jnp.reshape(4 * cx + 2 * cy + cc, (1,)).astype(jnp.int32)
    summed = _sum8(bland, packed, me_arr).reshape(-1)
    outs_small, o = [], 0
    for p, n in zip(pieces, sizes):
        outs_small.append(summed[o:o + n].reshape(p.shape))
        o += n
    g_nmix, g_nmlp, g_nfinal, g_pscale, g_wuf, g_wub, g_buf, g_bub, g_gn = outs_small
    g_wuf = lax.dynamic_slice_in_dim(g_wuf, s_me * ks, ks, axis=1)
    g_wub = lax.dynamic_slice_in_dim(g_wub, s_me * ks, ks, axis=1)
    g_buf = lax.dynamic_slice_in_dim(g_buf, s_me * ks, ks, axis=0)
    g_bub = lax.dynamic_slice_in_dim(g_bub, s_me * ks, ks, axis=0)
    g_gn = lax.dynamic_slice_in_dim(g_gn.reshape(dv), s_me * (dv // N_SHARDS), dv // N_SHARDS, axis=0)
    small_grads = {"norm_mix": g_nmix, "norm_mlp": g_nmlp, "norm_final": g_nfinal, "pool_scale": g_pscale,
                   "gla_w_up_f": g_wuf, "gla_b_up_f": g_buf, "gla_w_up_b": g_wub, "gla_b_up_b": g_bub,
                   "gla_g_norm": g_gn}
    for nm, g in small_grads.items():
        w, m, v = weights[index[nm]], moms[index[nm]], vels[index[nm]]
        cols = w.shape[-1]
        shp = (w.size // cols, cols)
        dl, mn, vn = _adamw(w.reshape(shp), g.reshape(shp), m.reshape(shp), v.reshape(shp), "adamw_" + nm)
        results[nm] = (g, dl, mn, vn)

    outs = [[results[nm][k].reshape(weights[index[nm]].shape) for nm in names] for k in range(4)]
    return (loss, dx.reshape(x.shape), *outs[0], *outs[1], *outs[2], *outs[3])
```

```python
import jax
import jax.numpy as jnp
from jax import lax
from jax.experimental import pallas as pl
from jax.experimental.pallas import tpu as pltpu

F32 = jnp.float32
BF16 = jnp.bfloat16

N_HEADS = 4
N_GROUPS = 4
POOL_HALF = (1, 2, 4, 8)
GATE_RANK = 16
GATE_TAU = 16.0
CHUNK = 64
STEP = 2 * CHUNK
EPS = 1e-6
N_SHARDS = 4
R_PAD = 128
ONES_COL = 2 * GATE_RANK

ADAM_LR = 0.001
ADAM_B1 = 0.9
ADAM_B2 = 0.999
ADAM_EPS = 1e-08
ADAM_WD = 0.01
ADAM_STEP = 10

_NN = (((1,), (0,)), ((), ()))
_NT = (((1,), (1,)), ((), ()))
_TN = (((0,), (0,)), ((), ()))

VMEM_LIMIT = 56 * 1024 * 1024
MESH = pl.DeviceIdType.MESH
ANY = pl.BlockSpec(memory_space=pl.ANY)


def _tile(dim, pref):
    return pref if dim % pref == 0 else dim


def _params(*sem):
    return pltpu.CompilerParams(dimension_semantics=sem, vmem_limit_bytes=VMEM_LIMIT)


def _sds(shape, dtype):
    return jax.ShapeDtypeStruct(shape, dtype)


def _matmul(a, b, *, dims, grid, a_spec, b_spec, acc_shape, out_shapes, out_specs, epilogue,
            extras=(), extra_specs=(), name, pin=None):
    nk = grid[2]
    n_extra = len(extras)
    n_out = len(out_shapes)
    pins = () if pin is None else (pin,)

    def body(a_ref, b_ref, *rest):
        extra_refs = rest[:n_extra]
        rest = rest[n_extra + len(pins):]
        out_refs = rest[:n_out]
        acc_ref = rest[n_out]
        kk = pl.program_id(2)

        def part():
            return lax.dot_general(a_ref[...], b_ref[...], dims, preferred_element_type=F32)

        def finish(acc):
            outs = epilogue(acc, *[r[...] for r in extra_refs])
            for o_ref, o in zip(out_refs, outs):
                o_ref[...] = o.astype(o_ref.dtype)

        if nk == 1:
            finish(part())
        else:
            @pl.when(kk == 0)
            def _():
                acc_ref[...] = part()

            @pl.when((kk > 0) & (kk < nk - 1))
            def _():
                acc_ref[...] += part()

            @pl.when(kk == nk - 1)
            def _():
                finish(acc_ref[...] + part())

    return pl.pallas_call(
        body,
        grid=grid,
        in_specs=[a_spec, b_spec, *extra_specs, *[pl.BlockSpec((8, 128), lambda i, j, kk: (0, 0)) for _ in pins]],
        out_specs=list(out_specs),
        out_shape=list(out_shapes),
        scratch_shapes=[pltpu.VMEM(acc_shape if nk > 1 else (8, 128), F32)],
        compiler_params=_params("parallel", "parallel", "arbitrary"),
        name=name,
    )(a, b, *extras, *pins)


def _mm(a, b, kind, *, out_dtypes, epilogue=None, extras=(), name, tm=1024, tn=1024, tk=4096, pin=None):
    if kind == "nn":
        (m, k), n = a.shape, b.shape[1]
    elif kind == "nt":
        (m, k), n = a.shape, b.shape[0]
    else:
        (k, m), n = a.shape, b.shape[1]
    tm, tn, tk = _tile(m, tm), _tile(n, tn), _tile(k, tk)
    if kind == "nn":
        a_spec = pl.BlockSpec((tm, tk), lambda i, j, kk: (i, kk))
        b_spec = pl.BlockSpec((tk, tn), lambda i, j, kk: (kk, j))
        dims = _NN
    elif kind == "nt":
        a_spec = pl.BlockSpec((tm, tk), lambda i, j, kk: (i, kk))
        b_spec = pl.BlockSpec((tn, tk), lambda i, j, kk: (j, kk))
        dims = _NT
    else:
        a_spec = pl.BlockSpec((tk, tm), lambda i, j, kk: (kk, i))
        b_spec = pl.BlockSpec((tk, tn), lambda i, j, kk: (kk, j))
        dims = _TN
    o_spec = pl.BlockSpec((tm, tn), lambda i, j, kk: (i, j))
    if epilogue is None:
        epilogue = lambda acc, *e: tuple(acc for _ in out_dtypes)
    return _matmul(
        a, b, dims=dims, grid=(m // tm, n // tn, k // tk), a_spec=a_spec, b_spec=b_spec, acc_shape=(tm, tn),
        out_shapes=[_sds((m, n), d) for d in out_dtypes], out_specs=[o_spec for _ in out_dtypes],
        epilogue=epilogue, extras=extras, extra_specs=[o_spec for _ in extras], name=name, pin=pin)


def _rmsnorm_fwd(h, g, out_dtype, name):
    t, d = h.shape
    tr = _tile(t, 256)

    def body(h_ref, g_ref, o_ref):
        x = h_ref[...]
        r = lax.rsqrt(jnp.mean(x * x, axis=-1, keepdims=True) + EPS)
        o_ref[...] = (x * r * g_ref[...]).astype(o_ref.dtype)

    return pl.pallas_call(
        body, grid=(t // tr,),
        in_specs=[pl.BlockSpec((tr, d), lambda i: (i, 0)), pl.BlockSpec((1, d), lambda i: (0, 0))],
        out_specs=pl.BlockSpec((tr, d), lambda i: (i, 0)),
        out_shape=_sds((t, d), out_dtype), compiler_params=_params("parallel"), name=name)(h, g)


def _rmsnorm_bwd(h, g, dy, resid, name):
    t, d = h.shape
    tr = _tile(t, 128)

    def body(h_ref, g_ref, dy_ref, res_ref, dh_ref, dhb_ref, dg_ref):
        x = h_ref[...]
        r = lax.rsqrt(jnp.mean(x * x, axis=-1, keepdims=True) + EPS)
        xn = x * r
        dyv = dy_ref[...]
        gdy = dyv * g_ref[...]
        dh = res_ref[...] + r * (gdy - xn * jnp.mean(gdy * xn, axis=-1, keepdims=True))
        dh_ref[...] = dh
        dhb_ref[...] = dh.astype(BF16)
        part = jnp.sum(dyv * xn, axis=0, keepdims=True)

        @pl.when(pl.program_id(0) == 0)
        def _():
            dg_ref[...] = part

        @pl.when(pl.program_id(0) > 0)
        def _():
            dg_ref[...] += part

    row = pl.BlockSpec((tr, d), lambda i: (i, 0))
    vec = pl.BlockSpec((1, d), lambda i: (0, 0))
    return pl.pallas_call(
        body, grid=(t // tr,), in_specs=[row, vec, row, row], out_specs=[row, row, vec],
        out_shape=[_sds((t, d), F32), _sds((t, d), BF16), _sds((1, d), F32)],
        compiler_params=_params("arbitrary"), name=name)(h, g, dy, resid)


def _final_bwd(h, g, tgt):
    t, d = h.shape
    tr = _tile(t, 128)

    def body(h_ref, g_ref, t_ref, loss_ref, dh_ref, dhb_ref, dg_ref):
        x = h_ref[...]
        r = lax.rsqrt(jnp.mean(x * x, axis=-1, keepdims=True) + EPS)
        xn = x * r
        gv = g_ref[...]
        e = xn * gv - t_ref[...]
        lpart = jnp.full((1, 128), 0.5 * jnp.sum(jnp.mean(e * e, axis=-1, keepdims=True)), F32)
        dyv = e * (1.0 / d)
        gdy = dyv * gv
        dh = r * (gdy - xn * jnp.mean(gdy * xn, axis=-1, keepdims=True))
        dh_ref[...] = dh
        dhb_ref[...] = dh.astype(BF16)
        part = jnp.sum(dyv * xn, axis=0, keepdims=True)

        @pl.when(pl.program_id(0) == 0)
        def _():
            dg_ref[...] = part
            loss_ref[...] = lpart

        @pl.when(pl.program_id(0) > 0)
        def _():
            dg_ref[...] += part
            loss_ref[...] += lpart

    row = pl.BlockSpec((tr, d), lambda i: (i, 0))
    vec = pl.BlockSpec((1, d), lambda i: (0, 0))
    return pl.pallas_call(
        body, grid=(t // tr,), in_specs=[row, vec, row],
        out_specs=[pl.BlockSpec((1, 128), lambda i: (0, 0)), row, row, vec],
        out_shape=[_sds((1, 128), F32), _sds((t, d), F32), _sds((t, d), BF16), _sds((1, d), F32)],
        compiler_params=_params("arbitrary"), name="final_loss_bwd")(h, g, tgt)


def _shift_rows(x, s, row):
    n = x.shape[0]
    y = pltpu.roll(x, (-s) % n, 0)
    return jnp.where((row + s >= 0) & (row + s < n), y, 0.0)


def _span_sum(x, start, length, row):
    if start >= 0:
        y, step = _shift_rows(x, start, row) if start else x, 1
    else:
        last = start + length - 1
        assert last <= 0
        y, step = _shift_rows(x, last, row) if last else x, -1
    n = 1
    while n < length:
        y = y + _shift_rows(y, step * n, row)
        n *= 2
    return y


def _pool_apply(x, seq, transpose, name):
    t, d = x.shape
    dg = d // N_GROUPS
    tc = _tile(dg, 256)
    nblk = dg // tc

    def body(x_ref, o_ref):
        grp = pl.program_id(1)
        row = lax.broadcasted_iota(jnp.int32, (seq, tc), 0)
        for gi, hw in enumerate(POOL_HALF):
            @pl.when(grp == gi)
            def _(hw=hw):
                xv = x_ref[...]
                cnt = (jnp.minimum(row + hw, seq) - jnp.maximum(row - hw, 0)).astype(F32)
                if not transpose:
                    w = _span_sum(xv, 0, hw, row) + _span_sum(xv, -hw, hw, row)
                    o_ref[...] = (w / cnt - xv).astype(o_ref.dtype)
                else:
                    u = xv / cnt
                    w = _span_sum(u, 1, hw, row) + _span_sum(u, -(hw - 1), hw, row)
                    o_ref[...] = (w - xv).astype(o_ref.dtype)

    spec = pl.BlockSpec((seq, tc), lambda b, g, j: (b, g * nblk + j))
    return pl.pallas_call(
        body, grid=(t // seq, N_GROUPS, nblk), in_specs=[spec], out_specs=spec,
        out_shape=_sds((t, d), F32 if transpose else BF16),
        compiler_params=_params("parallel", "parallel", "parallel"), name=name)(x)


def _pool_mm_fwd(dm, wp, x, scale):
    t, d = dm.shape
    dg = d // N_GROUPS
    rs = dg // N_SHARDS
    tm = _tile(t, 1024)
    o_spec = pl.BlockSpec((tm, dg), lambda i, j, kk: (i, j))
    return _matmul(
        dm, wp, dims=_NN, grid=(t // tm, N_GROUPS, N_SHARDS),
        a_spec=pl.BlockSpec((tm, rs), lambda i, j, kk: (i, j * N_SHARDS + kk)),
        b_spec=pl.BlockSpec((None, None, rs, dg), lambda i, j, kk: (kk, j, 0, 0)),
        acc_shape=(tm, dg), out_shapes=[_sds((t, d), F32), _sds((t, d), F32)], out_specs=[o_spec, o_spec],
        epilogue=lambda acc, xv, sc: (xv + acc * sc, acc),
        extras=(x, scale), extra_specs=[o_spec, pl.BlockSpec((1, dg), lambda i, j, kk: (0, j))], name="pool_mm_fwd")


def _pool_scale_bwd(dh, ypre, scale):
    t, d = dh.shape
    tr = _tile(t, 256)

    def body(dh_ref, y_ref, s_ref, o_ref, ds_ref):
        g = dh_ref[...]
        o_ref[...] = (g * s_ref[...]).astype(BF16)
        part = jnp.sum(g * y_ref[...], axis=0, keepdims=True)

        @pl.when(pl.program_id(0) == 0)
        def _():
            ds_ref[...] = part

        @pl.when(pl.program_id(0) > 0)
        def _():
            ds_ref[...] += part

    row = pl.BlockSpec((tr, d), lambda i: (i, 0))
    vec = pl.BlockSpec((1, d), lambda i: (0, 0))
    return pl.pallas_call(
        body, grid=(t // tr,), in_specs=[row, row, vec], out_specs=[row, vec],
        out_shape=[_sds((t, d), BF16), _sds((1, d), F32)], compiler_params=_params("arbitrary"),
        name="pool_scale_bwd")(dh, ypre, scale)


def _pool_mm_bwd_x(dys, wp, pin):
    t, d = dys.shape
    dg = d // N_GROUPS
    rs = dg // N_SHARDS
    tm = _tile(t, 1024)
    return _matmul(
        dys, wp, dims=_NT, grid=(t // tm, N_GROUPS * N_SHARDS, 1),
        a_spec=pl.BlockSpec((tm, dg), lambda i, j, kk: (i, j // N_SHARDS)),
        b_spec=pl.BlockSpec((None, None, rs, dg), lambda i, j, kk: (j % N_SHARDS, j // N_SHARDS, 0, 0)),
        acc_shape=(tm, rs), out_shapes=[_sds((t, d), F32)],
        out_specs=[pl.BlockSpec((tm, rs), lambda i, j, kk: (i, j))],
        epilogue=lambda acc: (acc,), name="pool_mm_bwd_x", pin=pin)[0]


def _pool_mm_bwd_w(dm, dys):
    t, d = dm.shape
    dg = d // N_GROUPS
    rs = dg // N_SHARDS
    tk = _tile(t, 4096)
    return _matmul(
        dm, dys, dims=_TN, grid=(N_GROUPS * N_SHARDS, 1, t // tk),
        a_spec=pl.BlockSpec((tk, rs), lambda i, j, kk: (kk, i)),
        b_spec=pl.BlockSpec((tk, dg), lambda i, j, kk: (kk, i // N_SHARDS)),
        acc_shape=(rs, dg), out_shapes=[_sds((N_SHARDS, N_GROUPS, rs, dg), BF16)],
        out_specs=[pl.BlockSpec((None, None, rs, dg), lambda i, j, kk: (i % N_SHARDS, i // N_SHARDS, 0, 0))],
        epilogue=lambda acc: (acc,), name="pool_mm_bwd_w")[0]


def _mlp_fwd(h, hn, w1g, w2_after, pin=None):
    t, d = hn.shape
    f4 = w1g.shape[2]
    dff = N_SHARDS * f4
    tm, tn, tk = _tile(t, 1024), _tile(f4, 1024), _tile(d, 4096)
    nb = f4 // tn
    o_spec = pl.BlockSpec((tm, tn), lambda i, j, kk: (i, j))

    def act(acc):
        r = jnp.maximum(acc, 0.0)
        return r, r * r

    r, u = _matmul(
        hn, w1g, dims=_NN, grid=(t // tm, dff // tn, d // tk),
        a_spec=pl.BlockSpec((tm, tk), lambda i, j, kk: (i, kk)),
        b_spec=pl.BlockSpec((None, tk, tn), lambda i, j, kk: (j // nb, kk, j % nb)),
        acc_shape=(tm, tn), out_shapes=[_sds((t, dff), BF16), _sds((t, dff), BF16)], out_specs=[o_spec, o_spec],
        epilogue=act, name="mlp_up", pin=pin)
    w2 = w2_after(u)
    (out,) = _mm(u, w2, "nn", out_dtypes=[F32], epilogue=lambda acc, hv: (hv + acc,), extras=(h,), name="mlp_down",
                 tk=2048)
    return out, r, u, w2


def _mlp_bwd(dhb, hn, r, u, w1g, w2):
    t, d = hn.shape
    f4 = w1g.shape[2]
    dff = N_SHARDS * f4
    (da,) = _mm(dhb, w2, "nt", out_dtypes=[BF16], epilogue=lambda acc, rv: (acc * (2.0 * rv.astype(F32)),),
                extras=(r,), name="mlp_bwd_da")
    (dw2,) = _mm(u, dhb, "tn", out_dtypes=[BF16], name="mlp_bwd_dw2")
    tok = yield dw2.reshape(N_SHARDS, f4, d)
    tm, tn, tk = _tile(d, 1024), _tile(f4, 1024), _tile(t, 4096)
    nb = f4 // tn
    (dw1,) = _matmul(
        hn, da, dims=_TN, grid=(d // tm, dff // tn, t // tk),
        a_spec=pl.BlockSpec((tk, tm), lambda i, j, kk: (kk, i)),
        b_spec=pl.BlockSpec((tk, tn), lambda i, j, kk: (kk, j)),
        acc_shape=(tm, tn), out_shapes=[_sds((N_SHARDS, d, f4), BF16)],
        out_specs=[pl.BlockSpec((None, tm, tn), lambda i, j, kk: (j // nb, i, j % nb))],
        epilogue=lambda acc: (acc,), name="mlp_bwd_dw1", pin=tok)
    tok = yield dw1
    tm, tn, tk = _tile(t, 1024), _tile(d, 1024), _tile(f4, 4096)
    nbk = f4 // tk
    (dhn,) = _matmul(
        da, w1g, dims=_NT, grid=(t // tm, d // tn, dff // tk),
        a_spec=pl.BlockSpec((tm, tk), lambda i, j, kk: (i, kk)),
        b_spec=pl.BlockSpec((None, tn, tk), lambda i, j, kk: (kk // nbk, j, kk % nbk)),
        acc_shape=(tm, tn), out_shapes=[_sds((t, d), F32)],
        out_specs=[pl.BlockSpec((tm, tn), lambda i, j, kk: (i, j))], epilogue=lambda acc: (acc,), name="mlp_bwd_dhn",
        pin=tok)
    yield dhn


def _split3(x):
    a = x.astype(BF16)
    r1 = x - a.astype(F32)
    b = r1.astype(BF16)
    c = (r1 - b.astype(F32)).astype(BF16)
    return a, b, c


def _dot(a, b, dims):
    return lax.dot_general(a.astype(BF16), b.astype(BF16), dims, preferred_element_type=F32)


def _chunk_terms(q, k, g, rev, scale):
    c = q.shape[0]
    ri = lax.broadcasted_iota(jnp.int32, (c, c), 0)
    ci = lax.broadcasted_iota(jnp.int32, (c, c), 1)
    seen = (ci >= ri) if rev else (ci <= ri)
    tri = seen.astype(BF16)
    g1, g2, g3 = _split3(g)
    b = (lax.dot_general(tri, g1, _NN, preferred_element_type=F32)
         + lax.dot_general(tri, g2, _NN, preferred_element_type=F32)
         + lax.dot_general(tri, g3, _NN, preferred_element_type=F32))
    mid = c // 2 if rev else c // 2 - 1
    last = 0 if rev else c - 1
    rows = lax.broadcasted_iota(jnp.int32, b.shape, 0)
    b_mid = jnp.sum(jnp.where(rows == mid, b, 0.0), axis=0, keepdims=True)
    b_last = jnp.sum(jnp.where(rows == last, b, 0.0), axis=0, keepdims=True)
    qs = q * scale
    e1 = jnp.exp(b - b_mid)
    e2 = jnp.exp(b_mid - b)
    eb = jnp.exp(b)
    el = jnp.exp(b_last - b)
    return dict(seen=seen, tri=tri, mid=mid, last=last, e1=e1, e2=e2, eb=eb, el=el, a=jnp.exp(b_last),
                qe=qs * e1, ke=k * e2, qi=qs * eb, ks=k * el)


def _step_terms(q_ref, k_ref, g_ref, rev, scale):
    lo, hi = slice(0, CHUNK), slice(CHUNK, STEP)
    rows = (hi, lo) if rev else (lo, hi)
    return rows, [_chunk_terms(q_ref[r, :], k_ref[r, :], g_ref[r, :], rev, scale) for r in rows]


def _gla_fwd(pm, la, seq):
    t = pm.shape[0]
    d = pm.shape[1] // 3
    dk, dv = d // 2 // N_HEADS, d // N_HEADS
    nb, nc = t // seq, seq // STEP
    scale = dk ** -0.5
    kq, kk_, kv = 0, N_HEADS, (d // dv)

    def body(qf, kf, vf, gf, qb, kb, vb, gb, of_ref, ob_ref, stf_ref, stb_ref, sf, sb):
        n = pl.program_id(1)

        @pl.when(n == 0)
        def _():
            sf[...] = jnp.zeros_like(sf)
            sb[...] = jnp.zeros_like(sb)

        for (q_ref, k_ref, v_ref, g_ref, o_ref, st_ref, s_ref, rev) in (
                (qf, kf, vf, gf, of_ref, stf_ref, sf, False), (qb, kb, vb, gb, ob_ref, stb_ref, sb, True)):
            rows, (t0, t1) = _step_terms(q_ref, k_ref, g_ref, rev, scale)
            v0, v1 = v_ref[rows[0], :], v_ref[rows[1], :]
            st = s_ref[...]
            stb = st.astype(BF16)
            st_ref[...] = stb
            sc0 = jnp.where(t0["seen"], _dot(t0["qe"], t0["ke"], _NT), 0.0)
            sc1 = jnp.where(t1["seen"], _dot(t1["qe"], t1["ke"], _NT), 0.0)
            cross = _dot(t1["qi"], t0["ks"], _NT)
            qi_all = jnp.concatenate([t0["qi"], t1["qi"] * t0["a"]], axis=0)
            o_inter = lax.dot_general(qi_all.astype(BF16), stb, _NT, preferred_element_type=F32)
            o_ref[rows[0], :] = _dot(sc0, v0, _NN) + o_inter[:CHUNK]
            o_ref[rows[1], :] = _dot(sc1, v1, _NN) + _dot(cross, v0, _NN) + o_inter[CHUNK:]
            ks_all = jnp.concatenate([t0["ks"] * t1["a"], t1["ks"]], axis=0)
            s_ref[...] = st * (t0["a"] * t1["a"]) + _dot(jnp.concatenate([v0, v1], axis=0), ks_all, _TN)

    def row(bh, n, rev):
        return (bh // N_HEADS) * nc + (nc - 1 - n if rev else n)

    def specs(rev):
        return [
            pl.BlockSpec((STEP,dk), lambda bh, n: (row(bh, n, rev), kq + bh % N_HEADS)),
            pl.BlockSpec((STEP,dk), lambda bh, n: (row(bh, n, rev), kk_ + bh % N_HEADS)),
            pl.BlockSpec((STEP,dv), lambda bh, n: (row(bh, n, rev), kv + bh % N_HEADS)),
            pl.BlockSpec((STEP,dk), lambda bh, n: (row(bh, n, rev), (N_HEADS if rev else 0) + bh % N_HEADS)),
        ]

    def o_spec(rev):
        return pl.BlockSpec((STEP,dv), lambda bh, n: (row(bh, n, rev), bh % N_HEADS))

    def st_spec(rev):
        return pl.BlockSpec((None, None, dv, dk), lambda bh, n: (bh, nc - 1 - n if rev else n, 0, 0))

    sf_, sb_ = specs(False), specs(True)
    return pl.pallas_call(
        body, grid=(nb * N_HEADS, nc),
        in_specs=[*sf_, *sb_],
        out_specs=[o_spec(False), o_spec(True), st_spec(False), st_spec(True)],
        out_shape=[_sds((t, d), F32), _sds((t, d), F32),
                   _sds((nb * N_HEADS, nc, dv, dk), BF16), _sds((nb * N_HEADS, nc, dv, dk), BF16)],
        scratch_shapes=[pltpu.VMEM((dv, dk), F32), pltpu.VMEM((dv, dk), F32)],
        compiler_params=_params("parallel", "arbitrary"), name="gla_scan_fwd",
    )(pm, pm, pm, la, pm, pm, pm, la)


def _gla_bwd(pm, la, do, st_f, st_b, seq):
    t = pm.shape[0]
    d = pm.shape[1] // 3
    dk, dv = d // 2 // N_HEADS, d // N_HEADS
    kd = dk * N_HEADS
    nb, nc = t // seq, seq // STEP
    scale = dk ** -0.5
    kq, kk_, kv = 0, N_HEADS, (d // dv)

    def body(qf, kf, vf, gf, dof, stf, qb, kb, vb, gb, dob, stb_,
             dqf, dkf, dvf, dgf, dqb, dkb, dvb, dgb, dsf, dsb):
        n = pl.program_id(1)

        @pl.when(n == 0)
        def _():
            dsf[...] = jnp.zeros_like(dsf)
            dsb[...] = jnp.zeros_like(dsb)

        for (q_ref, k_ref, v_ref, g_ref, do_ref, st_ref, dq_ref, dk_ref, dv_ref, dg_ref, ds_ref, rev) in (
                (qf, kf, vf, gf, dof, stf, dqf, dkf, dvf, dgf, dsf, False),
                (qb, kb, vb, gb, dob, stb_, dqb, dkb, dvb, dgb, dsb, True)):
            rows, (t0, t1) = _step_terms(q_ref, k_ref, g_ref, rev, scale)
            v0, v1 = v_ref[rows[0], :], v_ref[rows[1], :]
            do0, do1 = do_ref[rows[0], :], do_ref[rows[1], :]
            st = st_ref[...]
            ds = ds_ref[...]
            dsb16 = ds.astype(BF16)
            a0, a1 = t0["a"], t1["a"]
            sc0 = jnp.where(t0["seen"], _dot(t0["qe"], t0["ke"], _NT), 0.0)
            sc1 = jnp.where(t1["seen"], _dot(t1["qe"], t1["ke"], _NT), 0.0)
            cross = _dot(t1["qi"], t0["ks"], _NT)
            dsc0 = jnp.where(t0["seen"], _dot(do0, v0, _NT), 0.0)
            dsc1 = jnp.where(t1["seen"], _dot(do1, v1, _NT), 0.0)
            dcross = _dot(do1, v0, _NT)
            qi_all = jnp.concatenate([t0["qi"], t1["qi"] * a0], axis=0)
            ks_all = jnp.concatenate([t0["ks"] * a1, t1["ks"]], axis=0)
            v_all = jnp.concatenate([v0, v1], axis=0).astype(BF16)
            do_all = jnp.concatenate([do0, do1], axis=0).astype(BF16)
            dks_all = lax.dot_general(v_all, dsb16, _NN, preferred_element_type=F32)
            dv_all = lax.dot_general(ks_all.astype(BF16), dsb16, _NT, preferred_element_type=F32)
            dqi_all = lax.dot_general(do_all, st, _NN, preferred_element_type=F32)
            da01 = jnp.sum(ds * st.astype(F32), axis=0, keepdims=True)
            dv_ref[rows[0], :] = (_dot(sc0, do0, _TN) + _dot(cross, do1, _TN) + dv_all[:CHUNK]).astype(BF16)
            dv_ref[rows[1], :] = (_dot(sc1, do1, _TN) + dv_all[CHUNK:]).astype(BF16)
            dqi0 = dqi_all[:CHUNK]
            dqi1 = dqi_all[CHUNK:] * a0 + _dot(dcross, t0["ks"], _NN)
            dks0 = dks_all[:CHUNK] * a1 + _dot(dcross, t1["qi"], _TN)
            dks1 = dks_all[CHUNK:]
            da0 = da01 * a1 + jnp.sum(dqi_all[CHUNK:] * t1["qi"], axis=0, keepdims=True)
            da1 = da01 * a0 + jnp.sum(dks_all[:CHUNK] * t0["ks"], axis=0, keepdims=True)
            for r, tm, dsc, dqi, dks, da in ((rows[0], t0, dsc0, dqi0, dks0, da0), (rows[1], t1, dsc1, dqi1, dks1, da1)):
                dqe = _dot(dsc, tm["ke"], _NN)
                dke = _dot(dsc, tm["qe"], _TN)
                dq_ref[r, :] = ((dqe * tm["e1"] + dqi * tm["eb"]) * scale).astype(BF16)
                dk_ref[r, :] = (dke * tm["e2"] + dks * tm["el"]).astype(BF16)
                t_q, t_k, t_s = dqe * tm["qe"], dke * tm["ke"], dks * tm["ks"]
                db = t_q - t_k + dqi * tm["qi"] - t_s
                mid_row = jnp.sum(t_k - t_q, axis=0, keepdims=True)
                last_row = jnp.sum(t_s, axis=0, keepdims=True) + da * tm["a"]
                ridx = lax.broadcasted_iota(jnp.int32, db.shape, 0)
                db = db + jnp.where(ridx == tm["mid"], mid_row, 0.0) + jnp.where(ridx == tm["last"], last_row, 0.0)
                d1, d2, d3 = _split3(db)
                dg_ref[r, :] = (lax.dot_general(tm["tri"], d1, _TN, preferred_element_type=F32)
                                + lax.dot_general(tm["tri"], d2, _TN, preferred_element_type=F32)
                                + lax.dot_general(tm["tri"], d3, _TN, preferred_element_type=F32))
            ds_ref[...] = ds * (a0 * a1) + lax.dot_general(do_all, qi_all.astype(BF16), _TN,
                                                           preferred_element_type=F32)

    def row(bh, n, rev):
        return (bh // N_HEADS) * nc + (n if rev else nc - 1 - n)

    def specs(rev):
        return [
            pl.BlockSpec((STEP,dk), lambda bh, n: (row(bh, n, rev), kq + bh % N_HEADS)),
            pl.BlockSpec((STEP,dk), lambda bh, n: (row(bh, n, rev), kk_ + bh % N_HEADS)),
            pl.BlockSpec((STEP,dv), lambda bh, n: (row(bh, n, rev), kv + bh % N_HEADS)),
            pl.BlockSpec((STEP,dk), lambda bh, n: (row(bh, n, rev), (N_HEADS if rev else 0) + bh % N_HEADS)),
            pl.BlockSpec((STEP,dv), lambda bh, n: (row(bh, n, rev), bh % N_HEADS)),
            pl.BlockSpec((None, None, dv, dk), lambda bh, n: (bh, n if rev else nc - 1 - n, 0, 0)),
        ]

    def outs(rev):
        return [
            pl.BlockSpec((STEP,dk), lambda bh, n: (row(bh, n, rev), bh % N_HEADS)),
            pl.BlockSpec((STEP,dk), lambda bh, n: (row(bh, n, rev), bh % N_HEADS)),
            pl.BlockSpec((STEP,dv), lambda bh, n: (row(bh, n, rev), bh % N_HEADS)),
            pl.BlockSpec((STEP,dk), lambda bh, n: (row(bh, n, rev), bh % N_HEADS)),
        ]

    of_, ob_ = outs(False), outs(True)
    res = pl.pallas_call(
        body, grid=(nb * N_HEADS, nc),
        in_specs=[*specs(False), *specs(True)],
        out_specs=[*of_, *ob_],
        out_shape=[_sds((t, kd), BF16), _sds((t, kd), BF16), _sds((t, d), BF16), _sds((t, kd), F32),
                   _sds((t, kd), BF16), _sds((t, kd), BF16), _sds((t, d), BF16), _sds((t, kd), F32)],
        scratch_shapes=[pltpu.VMEM((dv, dk), F32), pltpu.VMEM((dv, dk), F32)],
        compiler_params=_params("parallel", "arbitrary"), name="gla_scan_bwd",
    )(pm, pm, pm, la, do, st_f, pm, pm, pm, la, do, st_b)
    return res


def _sigmoid(x):
    return 1.0 / (1.0 + jnp.exp(-x))


def _gla_post_fwd(o_f, o_b, pm, gn):
    t, d = o_f.shape
    dv = d // N_HEADS
    tr = _tile(t, 512)
    gate_blk = 2 * d // dv

    def body(of_ref, ob_ref, gt_ref, gn_ref, out_ref):
        o = of_ref[...] + ob_ref[...]
        n = o * lax.rsqrt(jnp.mean(o * o, axis=-1, keepdims=True) + EPS) * gn_ref[...]
        gt = gt_ref[...]
        out_ref[...] = (n * (gt * _sigmoid(gt))).astype(BF16)

    blk = pl.BlockSpec((tr, dv), lambda i, h: (i, h))
    return pl.pallas_call(
        body, grid=(t // tr, N_HEADS),
        in_specs=[blk, blk, pl.BlockSpec((tr, dv), lambda i, h: (i, gate_blk + h)),
                  pl.BlockSpec((1, dv), lambda i, h: (0, 0))],
        out_specs=blk, out_shape=_sds((t, d), BF16), compiler_params=_params("parallel", "parallel"),
        name="gla_post_fwd")(o_f, o_b, pm, gn)


def _gla_post_bwd(dog, o_f, o_b, pm, gn):
    t, d = o_f.shape
    dv = d // N_HEADS
    tr = _tile(t, 256)
    gate_blk = 2 * d // dv

    def body(dog_ref, of_ref, ob_ref, gt_ref, gn_ref, do_ref, dgt_ref, dgn_ref):
        o = of_ref[...] + ob_ref[...]
        rr = lax.rsqrt(jnp.mean(o * o, axis=-1, keepdims=True) + EPS)
        on = o * rr
        gnv = gn_ref[...]
        gt = gt_ref[...]
        sg = _sigmoid(gt)
        sl = gt * sg
        dg_out = dog_ref[...]
        dn = dg_out * sl
        dgt_ref[...] = dg_out * (on * gnv) * (sg * (1.0 + gt * (1.0 - sg)))
        gdn = dn * gnv
        do_ref[...] = rr * (gdn - on * jnp.mean(gdn * on, axis=-1, keepdims=True))
        part = jnp.sum(dn * on, axis=0, keepdims=True)
        first = (pl.program_id(0) == 0) & (pl.program_id(1) == 0)

        @pl.when(first)
        def _():
            dgn_ref[...] = part

        @pl.when(jnp.logical_not(first))
        def _():
            dgn_ref[...] += part

    blk = pl.BlockSpec((tr, dv), lambda i, h: (i, h))
    vec = pl.BlockSpec((1, dv), lambda i, h: (0, 0))
    return pl.pallas_call(
        body, grid=(t // tr, N_HEADS),
        in_specs=[blk, blk, blk, pl.BlockSpec((tr, dv), lambda i, h: (i, gate_blk + h)), vec],
        out_specs=[blk, blk, vec], out_shape=[_sds((t, d), F32), _sds((t, d), F32), _sds((1, dv), F32)],
        compiler_params=_params("arbitrary", "arbitrary"), name="gla_post_bwd")(dog, o_f, o_b, pm, gn)


def _gla_dp(dq_f, dq_b, dk_f, dk_b, dv_f, dv_b, dgate):
    t, d = dv_f.shape
    kd = dq_f.shape[1]
    tr = _tile(t, 128)

    def body(a1, a2, b1, b2, c1, c2, g, o_ref):
        o_ref[:, 0:kd] = (a1[...].astype(F32) + a2[...].astype(F32)).astype(BF16)
        o_ref[:, kd:2 * kd] = (b1[...].astype(F32) + b2[...].astype(F32)).astype(BF16)
        o_ref[:, 2 * kd:2 * kd + d] = (c1[...].astype(F32) + c2[...].astype(F32)).astype(BF16)
        o_ref[:, 2 * kd + d:] = g[...].astype(BF16)

    sk = pl.BlockSpec((tr, kd), lambda i: (i, 0))
    sd = pl.BlockSpec((tr, d), lambda i: (i, 0))
    return pl.pallas_call(
        body, grid=(t // tr,), in_specs=[sk, sk, sk, sk, sd, sd, sd],
        out_specs=pl.BlockSpec((tr, 3 * d), lambda i: (i, 0)), out_shape=_sds((t, 3 * d), BF16),
        compiler_params=_params("parallel"), name="gla_dp")(dq_f, dq_b, dk_f, dk_b, dv_f, dv_b, dgate)


def _adamw(w, g, m, v, name):
    r, c = w.shape
    tr = _tile(r, 128)
    bc1 = 1.0 - ADAM_B1 ** ADAM_STEP
    bc2 = 1.0 - ADAM_B2 ** ADAM_STEP

    def body(w_ref, g_ref, m_ref, v_ref, d_ref, nm_ref, nv_ref):
        gv = g_ref[...]
        mn = ADAM_B1 * m_ref[...] + (1.0 - ADAM_B1) * gv
        vn = ADAM_B2 * v_ref[...] + (1.0 - ADAM_B2) * (gv * gv)
        m_hat = mn / bc1
        v_hat = vn / bc2
        d_ref[...] = -ADAM_LR * (m_hat / (jnp.sqrt(v_hat) + ADAM_EPS) + ADAM_WD * w_ref[...])
        nm_ref[...] = mn
        nv_ref[...] = vn

    blk = pl.BlockSpec((tr, c), lambda i: (i, 0))
    return pl.pallas_call(
        body, grid=(r // tr,), in_specs=[blk] * 4, out_specs=[blk] * 3, out_shape=[_sds((r, c), F32)] * 3,
        compiler_params=_params("parallel"), name=name)(w, g, m, v)


def _adamw_half(w, g, m, v, layer, half, prev, pin, name, transposed=False):
    bc1 = 1.0 - ADAM_B1 ** ADAM_STEP
    bc2 = 1.0 - ADAM_B2 ** ADAM_STEP
    n_skip = 1 + (0 if prev is None else 4)
    if transposed:
        nl, c, r = w.shape
        tr = _tile(r // 2, 128)
        nblk = r // 2 // tr
        lay = pl.BlockSpec((None, c, tr), lambda i, h_ref: (layer, 0, h_ref[0] * nblk + i))
        g_spec = pl.BlockSpec((c, tr), lambda i, h_ref: (0, i))
    else:
        nl, r, c = w.shape
        tr = _tile(r // 2, 128)
        nblk = r // 2 // tr
        lay = pl.BlockSpec((None, tr, c), lambda i, h_ref: (layer, h_ref[0] * nblk + i, 0))
        g_spec = pl.BlockSpec((tr, c), lambda i, h_ref: (i, 0))

    def body(h_ref, w_ref, g_ref, m_ref, v_ref, *rest):
        go_ref, d_ref, nm_ref, nv_ref = rest[n_skip:]
        gv = g_ref[...]
        mn = ADAM_B1 * m_ref[...] + (1.0 - ADAM_B1) * gv
        vn = ADAM_B2 * v_ref[...] + (1.0 - ADAM_B2) * (gv * gv)
        m_hat = mn / bc1
        v_hat = vn / bc2
        d_ref[...] = -ADAM_LR * (m_hat / (jnp.sqrt(v_hat) + ADAM_EPS) + ADAM_WD * w_ref[...])
        nm_ref[...] = mn
        nv_ref[...] = vn
        go_ref[...] = gv

    prev = tuple(prev or ())
    grid_spec = pltpu.PrefetchScalarGridSpec(
        num_scalar_prefetch=1, grid=(nblk,),
        in_specs=[lay, g_spec, lay, lay, pl.BlockSpec(TOKEN, lambda i, h_ref: (0, 0))] + [ANY] * len(prev),
        out_specs=[lay] * 4)
    return pl.pallas_call(
        body, grid_spec=grid_spec, out_shape=[_sds(w.shape, F32)] * 4,
        input_output_aliases={6 + k: k for k in range(len(prev))},
        compiler_params=_params("parallel"), name=name)(half, w, g, m, v, pin, *prev)


def _sum8(parts, own, me):
    _, n, _ = parts.shape

    def body(me_ref, p_ref, own_ref, o_ref):
        acc = None
        for i in range(8):
            term = jnp.where(me_ref[0] == i, own_ref[...], p_ref[i])
            acc = term if acc is None else acc + term
        o_ref[...] = acc

    grid_spec = pltpu.PrefetchScalarGridSpec(
        num_scalar_prefetch=1, grid=(1,),
        in_specs=[pl.BlockSpec((8, n, 128), lambda i, me_ref: (0, 0, 0)),
                  pl.BlockSpec((n, 128), lambda i, me_ref: (0, 0))],
        out_specs=pl.BlockSpec((n, 128), lambda i, me_ref: (0, 0)))
    return pl.pallas_call(body, grid_spec=grid_spec, out_shape=_sds((n, 128), F32), name="sum8")(me, parts, own)


def _pair_sum(ga, recv, c):
    _, _, rh, cols = ga.shape
    tr = _tile(rh, 256)

    def body(c_ref, a_ref, b_ref, o_ref):
        o_ref[...] = (a_ref[...].astype(F32) + b_ref[...].astype(F32)).astype(BF16)

    grid_spec = pltpu.PrefetchScalarGridSpec(
        num_scalar_prefetch=1, grid=(N_SHARDS, rh // tr),
        in_specs=[pl.BlockSpec((None, None, tr, cols), lambda s, i, c_ref: (s, c_ref[0], i, 0)),
                  pl.BlockSpec((None, tr, cols), lambda s, i, c_ref: (s, i, 0))],
        out_specs=pl.BlockSpec((None, tr, cols), lambda s, i, c_ref: (s, i, 0)))
    return pl.pallas_call(
        body, grid_spec=grid_spec, out_shape=_sds((N_SHARDS, rh, cols), BF16),
        compiler_params=_params("parallel", "parallel"), name="grad_pair_sum")(c, ga, recv)


def _quad_sum(pair, recv, s_me):
    _, rh, cols = pair.shape
    tr = _tile(rh, 256)

    def body(s_ref, p_ref, r1_ref, r2_ref, r3_ref, o_ref):
        o_ref[...] = ((p_ref[...].astype(F32) + r1_ref[...].astype(F32)) + r2_ref[...].astype(F32)) \
            + r3_ref[...].astype(F32)

    def blk(off):
        return pl.BlockSpec((None, tr, cols), lambda i, s_ref: ((s_ref[0] + off) % N_SHARDS, i, 0))

    grid_spec = pltpu.PrefetchScalarGridSpec(
        num_scalar_prefetch=1, grid=(rh // tr,), in_specs=[blk(0), blk(1), blk(2), blk(3)],
        out_specs=pl.BlockSpec((tr, cols), lambda i, s_ref: (i, 0)))
    return pl.pallas_call(
        body, grid_spec=grid_spec, out_shape=_sds((rh, cols), F32),
        compiler_params=_params("parallel"), name="grad_quad_sum")(s_me, pair, recv, recv, recv)


def _fill_own(w, layer, s_me, pin):
    _, rows, cols = w.shape
    rh = rows // 2
    tr = _tile(rh, 256)
    nblk = rh // tr
    pins = () if pin is None else (pin,)

    def body(s_ref, w_ref, *rest):
        rest[-1][...] = w_ref[...].astype(BF16)

    grid_spec = pltpu.PrefetchScalarGridSpec(
        num_scalar_prefetch=1, grid=(2, nblk),
        in_specs=[pl.BlockSpec((None, tr, cols), lambda h, i, s_ref: (layer, h * nblk + i, 0)),
                  *[pl.BlockSpec(TOKEN, lambda h, i, s_ref: (0, 0)) for _ in pins]],
        out_specs=pl.BlockSpec((None, None, tr, cols), lambda h, i, s_ref: (s_ref[0], h, i, 0)))
    return pl.pallas_call(
        body, grid_spec=grid_spec, out_shape=_sds((N_SHARDS, 2, rh, cols), BF16),
        compiler_params=_params("parallel", "parallel"), name="weight_fill_own")(s_me, w, *pins)


def _coords():
    return lax.axis_index("x"), lax.axis_index("y"), lax.axis_index("c")


def _other_chips(x, y):
    return [(x, 1 - y), (1 - x, y), (1 - x, 1 - y)]


def _bcast8(buf):
    n = buf.shape[0]

    def body(b_ref, o_ref, send_sems, recv_sems):
        x, y, c = _coords()
        me = 4 * x + 2 * y + c
        o_ref[me] = b_ref[...]
        copies = []
        for k in range(1, 8):
            peer = (x ^ (k >> 2), y ^ ((k >> 1) & 1), c ^ (k & 1))
            copies.append(pltpu.make_async_remote_copy(
                src_ref=b_ref, dst_ref=o_ref.at[me], send_sem=send_sems.at[k - 1], recv_sem=recv_sems.at[k - 1],
                device_id=peer, device_id_type=MESH))
        for cp in copies:
            cp.start()
        for k in range(1, 8):
            pltpu.make_async_remote_copy(
                src_ref=b_ref, dst_ref=o_ref.at[me ^ k], send_sem=send_sems.at[k - 1], recv_sem=recv_sems.at[k - 1],
                device_id=(x, y, c), device_id_type=MESH).wait_recv()
        for cp in copies:
            cp.wait_send()

    return pl.pallas_call(
        body, out_shape=_sds((8, n, 128), F32),
        in_specs=[pl.BlockSpec(memory_space=pltpu.VMEM)], out_specs=pl.BlockSpec(memory_space=pltpu.VMEM),
        scratch_shapes=[pltpu.SemaphoreType.DMA((7,)), pltpu.SemaphoreType.DMA((7,))],
        compiler_params=pltpu.CompilerParams(vmem_limit_bytes=VMEM_LIMIT), name="bcast8")(buf)


HBM = pl.BlockSpec(memory_space=pltpu.HBM)
SEM = pl.BlockSpec(memory_space=pltpu.SEMAPHORE)
SIDE = pltpu.SideEffectType.DATAFLOW_SIDE_EFFECTING
TOKEN = (8, 128)


def _in_hbm(a):
    return pltpu.with_memory_space_constraint(a, pltpu.HBM)


def _sibling_start(src, pieces, land_shape, after, name):
    n = len(pieces(None, None, None))

    def body(s_ref, land_ref, after_ref, send, recv, land_thru, token):
        x, y, c = _coords()
        for k, (a, b) in enumerate(pieces(s_ref, land_ref, c)):
            pltpu.make_async_remote_copy(
                src_ref=a, dst_ref=b, send_sem=send.at[k], recv_sem=recv.at[k], device_id=(x, y, 1 - c),
                device_id_type=MESH).start()
        token[...] = jnp.zeros_like(token)

    return pl.pallas_call(
        body, name=name,
        out_shape=(pltpu.SemaphoreType.DMA((n,)), pltpu.SemaphoreType.DMA((n,)), pltpu.HBM(land_shape, src.dtype),
                   _sds(TOKEN, F32)),
        in_specs=(HBM, HBM, ANY), out_specs=(SEM, SEM, HBM, pl.BlockSpec(memory_space=pltpu.VMEM)),
        input_output_aliases={1: 2}, compiler_params=pltpu.CompilerParams(has_side_effects=SIDE),
    )(_in_hbm(src), _in_hbm(lax.empty(land_shape, src.dtype)), after)


def _sibling_wait(send, recv, src, land, pieces, after, name):
    def body(s_ref, land_ref, send, recv, after_ref, land_out):
        x, y, c = _coords()
        for k, (a, b) in enumerate(pieces(s_ref, land_ref, c)):
            cp = pltpu.make_async_remote_copy(
                src_ref=a, dst_ref=b, send_sem=send.at[k], recv_sem=recv.at[k], device_id=(x, y, 1 - c),
                device_id_type=MESH)
            cp.wait_send()
            cp.wait_recv()

    return pl.pallas_call(
        body, name=name, out_shape=pltpu.HBM(land.shape, land.dtype), in_specs=(HBM, HBM, SEM, SEM, ANY),
        out_specs=HBM, input_output_aliases={1: 0}, compiler_params=pltpu.CompilerParams(has_side_effects=SIDE),
    )(src, land, send, recv, after)


def _swap_pieces(g_ref, land_ref, c):
    if g_ref is None:
        return [None] * N_SHARDS
    return [(g_ref.at[s, 1 - c], land_ref.at[s]) for s in range(N_SHARDS)]


def _whole_piece(r_ref, land_ref, c):
    return [(r_ref, land_ref)]


def _bcast_copies(b_ref, land_ref, send, recv):
    x, y, c = _coords()
    me = 4 * x + 2 * y + c
    return [pltpu.make_async_remote_copy(
        src_ref=b_ref, dst_ref=land_ref.at[me], send_sem=send.at[k - 1], recv_sem=recv.at[k - 1],
        device_id=(x ^ (k >> 2), y ^ ((k >> 1) & 1), c ^ (k & 1)), device_id_type=MESH) for k in range(1, 8)]


def _bcast_start(buf, after, name):
    def body(b_ref, land_ref, after_ref, send, recv, land_thru, token):
        for cp in _bcast_copies(b_ref, land_ref, send, recv):
            cp.start()
        token[...] = jnp.zeros_like(token)

    shape = (8,) + buf.shape
    return pl.pallas_call(
        body, name=name,
        out_shape=(pltpu.SemaphoreType.DMA((7,)), pltpu.SemaphoreType.DMA((7,)), pltpu.HBM(shape, buf.dtype),
                   _sds(TOKEN, F32)),
        in_specs=(HBM, HBM, ANY), out_specs=(SEM, SEM, HBM, pl.BlockSpec(memory_space=pltpu.VMEM)),
        input_output_aliases={1: 2}, compiler_params=pltpu.CompilerParams(has_side_effects=SIDE),
    )(_in_hbm(buf), _in_hbm(lax.empty(shape, buf.dtype)), after)


def _bcast_wait(send, recv, buf, land, after, name):
    def body(b_ref, land_ref, send, recv, after_ref, land_out):
        for cp in _bcast_copies(b_ref, land_ref, send, recv):
            cp.wait_send()
            cp.wait_recv()

    return pl.pallas_call(
        body, name=name, out_shape=pltpu.HBM(land.shape, land.dtype), in_specs=(HBM, HBM, SEM, SEM, ANY),
        out_specs=HBM, input_output_aliases={1: 0}, compiler_params=pltpu.CompilerParams(has_side_effects=SIDE),
    )(buf, land, send, recv, after)


def _scatter_start(part, name):
    def body(p_ref, land_ref, send, recv, p_thru, land_thru, token):
        x, y, c = _coords()
        s_me = 2 * x + y
        for j, (px, py) in enumerate(_other_chips(x, y)):
            pltpu.make_async_remote_copy(
                src_ref=p_ref.at[2 * px + py], dst_ref=land_ref.at[s_me], send_sem=send.at[j], recv_sem=recv.at[j],
                device_id=(px, py, c), device_id_type=MESH).start()
        token[...] = jnp.zeros_like(token)

    buf = pltpu.HBM(part.shape, part.dtype)
    return pl.pallas_call(
        body, name=name,
        out_shape=(pltpu.SemaphoreType.DMA((3,)), pltpu.SemaphoreType.DMA((3,)), buf, buf, _sds(TOKEN, F32)),
        in_specs=(HBM, HBM), out_specs=(SEM, SEM, HBM, HBM, pl.BlockSpec(memory_space=pltpu.VMEM)),
        input_output_aliases={0: 2, 1: 3}, compiler_params=pltpu.CompilerParams(has_side_effects=SIDE),
    )(_in_hbm(part), _in_hbm(lax.empty(part.shape, part.dtype)))


def _scatter_wait(send, recv, part, land, after, name):
    def body(p_ref, land_ref, send, recv, after_ref, p_out, land_out):
        x, y, c = _coords()
        for j, (px, py) in enumerate(_other_chips(x, y)):
            cp = pltpu.make_async_remote_copy(
                src_ref=p_ref.at[2 * px + py], dst_ref=land_ref.at[2 * px + py], send_sem=send.at[j],
                recv_sem=recv.at[j], device_id=(px, py, c), device_id_type=MESH)
            cp.wait_send()
            cp.wait_recv()

    buf = pltpu.HBM(part.shape, part.dtype)
    return pl.pallas_call(
        body, name=name, out_shape=(buf, buf), in_specs=(HBM, HBM, SEM, SEM, ANY), out_specs=(HBM, HBM),
        input_output_aliases={0: 0, 1: 1}, compiler_params=pltpu.CompilerParams(has_side_effects=SIDE),
    )(part, land, send, recv, after)


def _gather_start(bufs, after, name):
    n = len(bufs)

    def body(*refs):
        ins = refs[:n]
        sems = refs[n + 1:3 * n + 1]
        token = refs[4 * n + 1]
        x, y, c = _coords()
        s_me = 2 * x + y
        for i in range(n):
            for j, (px, py) in enumerate(_other_chips(x, y)):
                pltpu.make_async_remote_copy(
                    src_ref=ins[i].at[s_me, c], dst_ref=ins[i].at[s_me, c], send_sem=sems[2 * i].at[j],
                    recv_sem=sems[2 * i + 1].at[j], device_id=(px, py, c), device_id_type=MESH).start()
        token[...] = jnp.zeros_like(token)

    res = pl.pallas_call(
        body, name=name,
        out_shape=(*[pltpu.SemaphoreType.DMA((3,))] * (2 * n), *[pltpu.HBM(b.shape, b.dtype) for b in bufs],
                   _sds(TOKEN, F32)),
        in_specs=(*[HBM] * n, ANY), out_specs=(*[SEM] * (2 * n), *[HBM] * n, pl.BlockSpec(memory_space=pltpu.VMEM)),
        input_output_aliases={i: 2 * n + i for i in range(n)},
        compiler_params=pltpu.CompilerParams(has_side_effects=SIDE),
    )(*[_in_hbm(b) for b in bufs], after)
    return [(res[2 * i], res[2 * i + 1]) for i in range(n)], list(res[2 * n:3 * n]), res[3 * n]


def _gather_wait(send, recv, buf, after, name):
    def body(b_ref, send, recv, after_ref, b_out):
        x, y, c = _coords()
        s_me = 2 * x + y
        for j, (px, py) in enumerate(_other_chips(x, y)):
            cp = pltpu.make_async_remote_copy(
                src_ref=b_ref.at[s_me, c], dst_ref=b_ref.at[2 * px + py, c], send_sem=send.at[j], recv_sem=recv.at[j],
                device_id=(px, py, c), device_id_type=MESH)
            cp.wait_send()
            cp.wait_recv()

    return pl.pallas_call(
        body, name=name, out_shape=pltpu.HBM(buf.shape, buf.dtype), in_specs=(HBM, SEM, SEM, ANY), out_specs=HBM,
        input_output_aliases={0: 0}, compiler_params=pltpu.CompilerParams(has_side_effects=SIDE),
    )(buf, send, recv, after)


def _pass_copies(b_ref, send, recv):
    x, y, c = _coords()
    return [pltpu.make_async_remote_copy(
        src_ref=b_ref.at[2 * px + py, c], dst_ref=b_ref.at[2 * px + py, c], send_sem=send.at[j], recv_sem=recv.at[j],
        device_id=(x, y, 1 - c), device_id_type=MESH) for j, (px, py) in enumerate(_other_chips(x, y))]


def _pass_start(buf, after, name):
    def body(b_ref, after_ref, send, recv, b_thru, token):
        for cp in _pass_copies(b_ref, send, recv):
            cp.start()
        token[...] = jnp.zeros_like(token)

    return pl.pallas_call(
        body, name=name,
        out_shape=(pltpu.SemaphoreType.DMA((3,)), pltpu.SemaphoreType.DMA((3,)), pltpu.HBM(buf.shape, buf.dtype),
                   _sds(TOKEN, F32)),
        in_specs=(HBM, ANY), out_specs=(SEM, SEM, HBM, pl.BlockSpec(memory_space=pltpu.VMEM)),
        input_output_aliases={0: 2}, compiler_params=pltpu.CompilerParams(has_side_effects=SIDE),
    )(_in_hbm(buf), after)


def _pass_wait(send, recv, buf, after, name):
    def body(b_ref, send, recv, after_ref, b_out):
        x, y, c = _coords()
        for j, (px, py) in enumerate(_other_chips(x, y)):
            pltpu.make_async_remote_copy(
                src_ref=b_ref.at[2 * px + py, c], dst_ref=b_ref.at[2 * px + py, 1 - c], send_sem=send.at[j],
                recv_sem=recv.at[j], device_id=(x, y, 1 - c), device_id_type=MESH).wait()

    return pl.pallas_call(
        body, name=name, out_shape=pltpu.HBM(buf.shape, buf.dtype), in_specs=(HBM, SEM, SEM, ANY), out_specs=HBM,
        input_output_aliases={0: 0}, compiler_params=pltpu.CompilerParams(has_side_effects=SIDE),
    )(buf, send, recv, after)


def _pass_on_halves(bufs):
    n = len(bufs)

    def body(*refs):
        ins, outs = refs[:n], refs[n:2 * n]
        send, recv = refs[2 * n:]
        x, y, c = _coords()
        copies = []
        for i in range(n):
            for j, (px, py) in enumerate(_other_chips(x, y)):
                cp = pltpu.make_async_remote_copy(
                    src_ref=ins[i].at[2 * px + py, c], dst_ref=outs[i].at[2 * px + py, c], send_sem=send.at[i, j],
                    recv_sem=recv.at[i, j], device_id=(x, y, 1 - c), device_id_type=MESH)
                cp.start()
                copies.append(cp)
        for i in range(n):
            for j, (px, py) in enumerate(_other_chips(x, y)):
                other = outs[i].at[2 * px + py, 1 - c]
                pltpu.make_async_remote_copy(
                    src_ref=other, dst_ref=other, send_sem=send.at[i, j], recv_sem=recv.at[i, j],
                    device_id=(x, y, c), device_id_type=MESH).wait_recv()
        for cp in copies:
            cp.wait_send()

    return pl.pallas_call(
        body, out_shape=[_sds(b.shape, b.dtype) for b in bufs],
        in_specs=[ANY] * n, out_specs=[ANY] * n, input_output_aliases={i: i for i in range(n)},
        scratch_shapes=[pltpu.SemaphoreType.DMA((n, 3)), pltpu.SemaphoreType.DMA((n, 3))],
        name="gather_pass_on")(*bufs)


def _to_rows(vec):
    n = -(-vec.shape[0] // 1024) * 1024
    return jnp.pad(vec, (0, n - vec.shape[0])).reshape(-1, 128)


def kernel(x, norm_mix, norm_mlp, norm_final, pool_w, pool_scale, gla_w_in, gla_w_up_f, gla_b_up_f, gla_w_up_b, gla_b_up_b, gla_g_norm, gla_w_out, mlp_w_in, mlp_w_out, loss_target, m_norm_mix, m_norm_mlp, m_norm_final, m_pool_w, m_pool_scale, m_gla_w_in, m_gla_w_up_f, m_gla_b_up_f, m_gla_w_up_b, m_gla_b_up_b, m_gla_g_norm, m_gla_w_out, m_mlp_w_in, m_mlp_w_out, v_norm_mix, v_norm_mlp, v_norm_final, v_pool_w, v_pool_scale, v_gla_w_in, v_gla_w_up_f, v_gla_b_up_f, v_gla_w_up_b, v_gla_b_up_b, v_gla_g_norm, v_gla_w_out, v_mlp_w_in, v_mlp_w_out):
    nb, seq, d = x.shape
    t = nb * seq
    dg = d // N_GROUPS
    kd = d // 2
    dv = d // N_HEADS
    pw = gla_w_in.shape[2]
    f4 = mlp_w_in.shape[2]
    dff = N_SHARDS * f4
    cx, cy, cc = _coords()
    s_me = 2 * cx + cy
    c_arr = jnp.reshape(cc, (1,)).astype(jnp.int32)
    s_arr = jnp.reshape(s_me, (1,)).astype(jnp.int32)

    xf = x.reshape(t, d)
    tgt = loss_target.reshape(t, d)

    ks = kd // N_SHARDS
    small = jnp.concatenate([gla_w_up_f[0].reshape(-1), gla_w_up_b[0].reshape(-1), gla_b_up_f[0], gla_b_up_b[0],
                             gla_g_norm[0]])
    small_all = _bcast8(_to_rows(small))
    pool_rows = N_GROUPS * (dg // N_SHARDS)
    gsems, gbufs, gtok = _gather_start([
        _fill_own(pool_w.reshape(1, pool_rows, dg), 0, s_arr, None), _fill_own(mlp_w_in, 0, s_arr, None)],
        small_all, "gather_start_first")
    gsems2, gbufs2, gtok = _gather_start([
        _fill_own(mlp_w_out, 0, s_arr, gtok), _fill_own(gla_w_in, 0, s_arr, gtok),
        _fill_own(gla_w_out, 0, s_arr, gtok), _fill_own(mlp_w_in, 1, s_arr, gtok),
        _fill_own(mlp_w_out, 1, s_arr, gtok)], gtok, "gather_start_rest")
    gsems, gbufs = gsems + gsems2, gbufs + gbufs2
    small_all = small_all[::2].reshape(N_SHARDS, -1)

    def arrived(i, after, name):
        return _gather_wait(gsems[i][0], gsems[i][1], gbufs[i], after, "gather_wait_" + name)

    def weight(i, after, name):
        return _pass_on_halves([arrived(i, after, name)])[0]

    def pinned(vec, tok):
        return vec + tok[0:1, 0:1]

    o = 0
    wuf = jnp.transpose(small_all[:, o:o + GATE_RANK * ks].reshape(N_SHARDS, GATE_RANK, ks), (1, 0, 2)).reshape(GATE_RANK, kd)
    o += GATE_RANK * ks
    wub = jnp.transpose(small_all[:, o:o + GATE_RANK * ks].reshape(N_SHARDS, GATE_RANK, ks), (1, 0, 2)).reshape(GATE_RANK, kd)
    o += GATE_RANK * ks
    buf = small_all[:, o:o + ks].reshape(1, kd)
    o += ks
    bub = small_all[:, o:o + ks].reshape(1, kd)
    o += ks
    gn = small_all[:, o:o + dv // N_SHARDS].reshape(1, dv)
    w_up = jnp.zeros((R_PAD, 2 * kd), F32).at[:GATE_RANK, :kd].set(wuf).at[GATE_RANK:2 * GATE_RANK, kd:].set(wub)
    w_up = w_up.astype(BF16)
    b_up = jnp.concatenate([buf, bub], axis=1)

    hn0 = _rmsnorm_fwd(xf, pinned(norm_mix[0:1], gtok), F32, "norm_mix0")
    dm = _pool_apply(hn0, seq, False, "pool_diff")
    wp = weight(0, dm, "pool_w").reshape(N_SHARDS, N_GROUPS, dg // N_SHARDS, dg)
    h1, ypre = _pool_mm_fwd(dm, wp, xf, pool_scale)
    hn1 = _rmsnorm_fwd(h1, norm_mlp[0:1], BF16, "norm_mlp0")
    w1g = [weight(1, hn1, "mlp_w_in0").reshape(N_SHARDS, d, f4), None]
    w2g = [None, None]
    h2, r0, u0, w2g[0] = _mlp_fwd(h1, hn1, w1g[0], lambda u: weight(2, u, "mlp_w_out0").reshape(dff, d))
    hn2 = _rmsnorm_fwd(h2, norm_mix[1:2], BF16, "norm_mix1")
    win = jnp.transpose(weight(3, hn2, "gla_w_in").reshape(N_SHARDS, d, pw), (1, 0, 2)).reshape(d, N_SHARDS * pw)
    win_t = jnp.transpose(win)
    w_r = jnp.pad(win[:, 3 * d:], ((0, 0), (0, R_PAD - 2 * GATE_RANK)))
    ptm, ptn = _tile(t, 1024), _tile(3 * d, 1024)
    (pm,) = _matmul(
        hn2, win_t, dims=_NT, grid=(t // ptm, 3 * d // ptn, 1),
        a_spec=pl.BlockSpec((ptm, d), lambda i, j, kk: (i, 0)), b_spec=pl.BlockSpec((ptn, d), lambda i, j, kk: (j, 0)),
        acc_shape=(ptm, ptn), out_shapes=[_sds((t, 3 * d), F32)],
        out_specs=[pl.BlockSpec((ptm, ptn), lambda i, j, kk: (i, j))], epilogue=lambda acc: (acc,), name="gla_proj")
    (pr,) = _mm(hn2, w_r, "nn", out_dtypes=[BF16], name="gla_proj_r")
    pr = pr.at[:, ONES_COL].set(1.0)

    def log_decay(acc, bv):
        z = acc + bv
        return ((jnp.minimum(z, 0.0) - jnp.log(1.0 + jnp.exp(-jnp.abs(z)))) / GATE_TAU,)

    tm_, tn_ = _tile(t, 1024), _tile(kd, 1024)
    (la,) = _matmul(
        pr, w_up, dims=_NN, grid=(t // tm_, 2 * kd // tn_, 1),
        a_spec=pl.BlockSpec((tm_, R_PAD), lambda i, j, kk: (i, 0)),
        b_spec=pl.BlockSpec((R_PAD, tn_), lambda i, j, kk: (0, j)), acc_shape=(tm_, tn_),
        out_shapes=[_sds((t, 2 * kd), F32)], out_specs=[pl.BlockSpec((tm_, tn_), lambda i, j, kk: (i, j))],
        epilogue=log_decay, extras=(b_up,), extra_specs=[pl.BlockSpec((1, tn_), lambda i, j, kk: (0, j))],
        name="gla_gate_fwd")
    o_f, o_b, st_f, st_b = _gla_fwd(pm, la, seq)
    send4, recv4, buf4, tok = _pass_start(arrived(4, o_f, "gla_w_out"), o_f, "gather_pass_start_gla_w_out")
    og = _gla_post_fwd(o_f, o_b, pm, pinned(gn, tok))
    wout = _pass_wait(send4, recv4, buf4, og, "gather_pass_wait_gla_w_out").reshape(d, d)
    send5, recv5, buf5, tok = _pass_start(arrived(5, og, "mlp_w_in1"), og, "gather_pass_start_mlp_w_in1")
    (h3,) = _mm(og, wout, "nn", out_dtypes=[F32], epilogue=lambda acc, hv: (hv + acc,), extras=(h2,), name="gla_out",
                pin=tok)
    hn3 = _rmsnorm_fwd(h3, norm_mlp[1:2], BF16, "norm_mlp1")
    w1g[1] = _pass_wait(send5, recv5, buf5, hn3, "gather_pass_wait_mlp_w_in1").reshape(N_SHARDS, d, f4)
    send6, recv6, buf6, tok = _pass_start(arrived(6, hn3, "mlp_w_out1"), hn3, "gather_pass_start_mlp_w_out1")
    h4, r1, u1, w2g[1] = _mlp_fwd(
        h3, hn3, w1g[1], lambda u: _pass_wait(send6, recv6, buf6, u, "gather_pass_wait_mlp_w_out1").reshape(dff, d), tok)

    chains = []
    last_tok = [gtok]

    def swap_begin(g, name):
        g5 = g.reshape(N_SHARDS, 2, g.shape[1] // 2, g.shape[2])
        send, recv, land, tok = _sibling_start(g5, _swap_pieces, (N_SHARDS,) + g5.shape[2:], last_tok[0],
                                               "grad_swap_start_" + name)
        last_tok[0] = tok
        return (name, send, recv, g5, land), tok

    def swap_end(state, after):
        name, send, recv, g5, land = state
        land = _sibling_wait(send, recv, g5, land, _swap_pieces, after, "grad_swap_wait_" + name)
        pair = _pair_sum(g5, land, c_arr)
        send, recv, pair, land, tok = _scatter_start(pair, "grad_scatter_start_" + name)
        chains.append((name, send, recv, pair, land))
        last_tok[0] = tok
        return tok

    def mlp_backward(dhb, hn, r, u, w1, w2, name):
        steps = _mlp_bwd(dhb, hn, r, u, w1, w2)
        state, tok = swap_begin(next(steps), "mlp_w_out" + name)
        dw1 = steps.send(tok)
        swap_end(state, dw1)
        state, tok = swap_begin(dw1, "mlp_w_in" + name)
        dhn = steps.send(tok)
        return dhn, swap_end(state, dhn)

    loss_part, dh4, dh4b, dg_final = _final_bwd(h4, norm_final.reshape(1, d), tgt)
    dhn3, tok = mlp_backward(dh4b, hn3, r1, u1, w1g[1], w2g[1], "1")
    dh3, dh3b, dg_mlp1 = _rmsnorm_bwd(h3, pinned(norm_mlp[1:2], tok), dhn3, dh4, "norm_mlp1_bwd")

    (dog,) = _mm(dh3b, wout, "nt", out_dtypes=[F32], name="gla_out_bwd_x")
    (dwout,) = _mm(og, dh3b, "tn", out_dtypes=[BF16], name="gla_out_bwd_w")
    state, tok = swap_begin(dwout.reshape(N_SHARDS, d // N_SHARDS, d), "gla_w_out")
    do, dgate, dg_gn = _gla_post_bwd(dog, o_f, o_b, pm, pinned(gn, tok))
    dq_f, dk_f, dv_f, dla_f, dq_b, dk_b, dv_b, dla_b = _gla_bwd(pm, la, do, st_f, st_b, seq)
    b_up_p = pinned(b_up, swap_end(state, dq_f))
    nkb = kd // tn_

    def gate_bwd(acc, bv, dl_f, dl_b):
        z = acc + bv
        dl = jnp.where(pl.program_id(1) < nkb, dl_f, dl_b)
        return (dl * (1.0 / GATE_TAU) / (1.0 + jnp.exp(z)),)

    (dz,) = _matmul(
        pr, w_up, dims=_NN, grid=(t // tm_, 2 * kd // tn_, 1),
        a_spec=pl.BlockSpec((tm_, R_PAD), lambda i, j, kk: (i, 0)),
        b_spec=pl.BlockSpec((R_PAD, tn_), lambda i, j, kk: (0, j)), acc_shape=(tm_, tn_),
        out_shapes=[_sds((t, 2 * kd), BF16)], out_specs=[pl.BlockSpec((tm_, tn_), lambda i, j, kk: (i, j))],
        epilogue=gate_bwd, extras=(b_up_p, dla_f, dla_b),
        extra_specs=[pl.BlockSpec((1, tn_), lambda i, j, kk: (0, j)),
                     pl.BlockSpec((tm_, tn_), lambda i, j, kk: (i, jnp.minimum(j, nkb - 1))),
                     pl.BlockSpec((tm_, tn_), lambda i, j, kk: (i, jnp.maximum(j - nkb, 0)))],
        name="gla_gate_bwd")
    (dpr,) = _mm(dz, w_up, "nt", out_dtypes=[BF16], name="gla_gate_bwd_r")
    (dw_up,) = _mm(pr, dz, "tn", out_dtypes=[F32], name="gla_gate_bwd_w")
    dp = _gla_dp(dq_f, dq_b, dk_f, dk_b, dv_f, dv_b, dgate)
    (dw_main,) = _mm(hn2, dp, "tn", out_dtypes=[BF16], name="gla_proj_bwd_w")
    (dw_r,) = _mm(hn2, dpr, "tn", out_dtypes=[BF16], name="gla_proj_bwd_wr")
    dwin = jnp.concatenate([dw_main, dw_r[:, :2 * GATE_RANK]], axis=1)
    state, tok = swap_begin(jnp.transpose(dwin.reshape(d, N_SHARDS, pw), (1, 0, 2)), "gla_w_in")
    btm, btn, btk = _tile(t, 1024), _tile(d, 1024), _tile(3 * d, 2048)
    (dhn2,) = _matmul(
        dp, win_t, dims=_NN, grid=(t // btm, d // btn, 3 * d // btk),
        a_spec=pl.BlockSpec((btm, btk), lambda i, j, kk: (i, kk)),
        b_spec=pl.BlockSpec((btk, btn), lambda i, j, kk: (kk, j)), acc_shape=(btm, btn),
        out_shapes=[_sds((t, d), F32)], out_specs=[pl.BlockSpec((btm, btn), lambda i, j, kk: (i, j))],
        epilogue=lambda acc, dr, wr: (acc + lax.dot_general(dr, wr, _NT, preferred_element_type=F32),),
        extras=(dpr, w_r), extra_specs=[pl.BlockSpec((btm, R_PAD), lambda i, j, kk: (i, 0)),
                                        pl.BlockSpec((btn, R_PAD), lambda i, j, kk: (j, 0))],
        name="gla_proj_bwd_x", pin=tok)
    tok = swap_end(state, dhn2)
    dh2, dh2b, dg_mix1 = _rmsnorm_bwd(h2, pinned(norm_mix[1:2], tok), dhn2, dh3, "norm_mix1_bwd")

    dhn1, tok = mlp_backward(dh2b, hn1, r0, u0, w1g[0], w2g[0], "0")
    dh1, _, dg_mlp0 = _rmsnorm_bwd(h1, pinned(norm_mlp[0:1], tok), dhn1, dh2, "norm_mlp0_bwd")

    dys, dg_pscale = _pool_scale_bwd(dh1, ypre, pool_scale)
    dwp = _pool_mm_bwd_w(dm, dys)
    state, tok = swap_begin(dwp.reshape(N_SHARDS, pool_rows, dg), "pool_w")
    dd = _pool_mm_bwd_x(dys, wp, tok)
    tok = swap_end(state, dd)
    dhn0 = _pool_apply(dd, seq, True, "pool_diff_bwd")
    dx, _, dg_mix0 = _rmsnorm_bwd(xf, pinned(norm_mix[0:1], tok), dhn0, dh1, "norm_mix0_bwd")

    dwuf, dwub = dw_up[:GATE_RANK, :kd], dw_up[GATE_RANK:2 * GATE_RANK, kd:]
    dbuf, dbub = dw_up[ONES_COL, :kd], dw_up[ONES_COL, kd:]
    pieces = [jnp.concatenate([dg_mix0, dg_mix1], 0), jnp.concatenate([dg_mlp0, dg_mlp1], 0), dg_final, dg_pscale,
              dwuf, dwub, dbuf, dbub, dg_gn]
    sizes = [p.size for p in pieces]
    packed = _to_rows(jnp.concatenate([p.reshape(-1) for p in pieces]))
    bsend, brecv, bland, _ = _bcast_start(packed, dx, "small_grads_start")

    loss = lax.psum(loss_part[0, 0], ("x", "y", "c"))

    weights = [norm_mix, norm_mlp, norm_final, pool_w, pool_scale, gla_w_in, gla_w_up_f, gla_b_up_f, gla_w_up_b,
               gla_b_up_b, gla_g_norm, gla_w_out, mlp_w_in, mlp_w_out]
    moms = [m_norm_mix, m_norm_mlp, m_norm_final, m_pool_w, m_pool_scale, m_gla_w_in, m_gla_w_up_f, m_gla_b_up_f,
            m_gla_w_up_b, m_gla_b_up_b, m_gla_g_norm, m_gla_w_out, m_mlp_w_in, m_mlp_w_out]
    vels = [v_norm_mix, v_norm_mlp, v_norm_final, v_pool_w, v_pool_scale, v_gla_w_in, v_gla_w_up_f, v_gla_b_up_f,
            v_gla_w_up_b, v_gla_b_up_b, v_gla_g_norm, v_gla_w_out, v_mlp_w_in, v_mlp_w_out]
    names = ["norm_mix", "norm_mlp", "norm_final", "pool_w", "pool_scale", "gla_w_in", "gla_w_up_f", "gla_b_up_f",
             "gla_w_up_b", "gla_b_up_b", "gla_g_norm", "gla_w_out", "mlp_w_in", "mlp_w_out"]
    index = {nm: k for k, nm in enumerate(names)}
    results = {}

    stacked = {"mlp_w_out1": ("mlp_w_out", 1), "mlp_w_in1": ("mlp_w_in", 1), "gla_w_out": ("gla_w_out", 0),
               "gla_w_in": ("gla_w_in", 0), "mlp_w_out0": ("mlp_w_out", 0), "mlp_w_in0": ("mlp_w_in", 0),
               "pool_w": ("pool_w", 0)}
    oc_arr = 1 - c_arr
    after = dx
    for name, send, recv, pair, land in chains:
        pair, land = _scatter_wait(send, recv, pair, land, after, "grad_scatter_wait_" + name)
        mine = _quad_sum(pair, land, s_arr)
        jsend, jrecv, jland, jtok = _sibling_start(mine, _whole_piece, mine.shape, after, "grad_join_start_" + name)
        nm, layer = stacked[name]
        w, m, v = weights[index[nm]], moms[index[nm]], vels[index[nm]]
        rows, cols = 2 * mine.shape[0], mine.shape[1]
        shp = (w.size // (rows * cols), rows, cols)
        w, m, v = w.reshape(shp), m.reshape(shp), v.reshape(shp)
        tr_ = nm == "gla_w_in"
        flip = (lambda a: jnp.swapaxes(a, -1, -2)) if tr_ else (lambda a: a)
        w, m, v = flip(w), flip(m), flip(v)
        res = _adamw_half(w, flip(mine), m, v, layer, c_arr, results.get(nm), jtok, "adamw_mine_" + name, tr_)
        theirs = _sibling_wait(jsend, jrecv, mine, jland, _whole_piece, res[1], "grad_join_wait_" + name)
        res = _adamw_half(w, flip(theirs), m, v, layer, oc_arr, res, jtok, "adamw_theirs_" + name, tr_)
        results[nm] = [flip(a) for a in res]
        after = res[1]

    bland = _bcast_wait(bsend, brecv, packed, bland, after, "small_grads_wait")
    me_arr = jnp.reshape(4 * cx + 2 * cy + cc, (1,)).astype(jnp.int32)
    summed = _sum8(bland, packed, me_arr).reshape(-1)
    outs_small, o = [], 0
    for p, n in zip(pieces, sizes):
        outs_small.append(summed[o:o + n].reshape(p.shape))
        o += n
    g_nmix, g_nmlp, g_nfinal, g_pscale, g_wuf, g_wub, g_buf, g_bub, g_gn = outs_small
    g_wuf = lax.dynamic_slice_in_dim(g_wuf, s_me * ks, ks, axis=1)
    g_wub = lax.dynamic_slice_in_dim(g_wub, s_me * ks, ks, axis=1)
    g_buf = lax.dynamic_slice_in_dim(g_buf, s_me * ks, ks, axis=0)
    g_bub = lax.dynamic_slice_in_dim(g_bub, s_me * ks, ks, axis=0)
    g_gn = lax.dynamic_slice_in_dim(g_gn.reshape(dv), s_me * (dv // N_SHARDS), dv // N_SHARDS, axis=0)
    small_grads = {"norm_mix": g_nmix, "norm_mlp": g_nmlp, "norm_final": g_nfinal, "pool_scale": g_pscale,
                   "gla_w_up_f": g_wuf, "gla_b_up_f": g_buf, "gla_w_up_b": g_wub, "gla_b_up_b": g_bub,
                   "gla_g_norm": g_gn}
    for nm, g in small_grads.items():
        w, m, v = weights[index[nm]], moms[index[nm]], vels[index[nm]]
        cols = w.shape[-1]
        shp = (w.size // cols, cols)
        dl, mn, vn = _adamw(w.reshape(shp), g.reshape(shp), m.reshape(shp), v.reshape(shp), "adamw_" + nm)
        results[nm] = (g, dl, mn, vn)

    outs = [[results[nm][k].reshape(weights[index[nm]].shape) for nm in names] for k in range(4)]
    return (loss, dx.reshape(x.shape), *outs[0], *outs[1], *outs[2], *outs[3])
```

```python
import jax
import jax.numpy as jnp
from jax import lax
from jax.experimental import pallas as pl
from jax.experimental.pallas import tpu as pltpu

F32 = jnp.float32
BF16 = jnp.bfloat16

N_HEADS = 4
N_GROUPS = 4
POOL_HALF = (1, 2, 4, 8)
GATE_RANK = 16
GATE_TAU = 16.0
CHUNK = 64
STEP = 2 * CHUNK
EPS = 1e-6
N_SHARDS = 4
R_PAD = 128
ONES_COL = 2 * GATE_RANK

ADAM_LR = 0.001
ADAM_B1 = 0.9
ADAM_B2 = 0.999
ADAM_EPS = 1e-08
ADAM_WD = 0.01
ADAM_STEP = 10

_NN = (((1,), (0,)), ((), ()))
_NT = (((1,), (1,)), ((), ()))
_TN = (((0,), (0,)), ((), ()))

VMEM_LIMIT = 56 * 1024 * 1024
MESH = pl.DeviceIdType.MESH
ANY = pl.BlockSpec(memory_space=pl.ANY)


def _tile(dim, pref):
    return pref if dim % pref == 0 else dim


def _params(*sem):
    return pltpu.CompilerParams(dimension_semantics=sem, vmem_limit_bytes=VMEM_LIMIT)


def _sds(shape, dtype):
    return jax.ShapeDtypeStruct(shape, dtype)


def _matmul(a, b, *, dims, grid, a_spec, b_spec, acc_shape, out_shapes, out_specs, epilogue,
            extras=(), extra_specs=(), name, pin=None):
    nk = grid[2]
    n_extra = len(extras)
    n_out = len(out_shapes)
    pins = () if pin is None else (pin,)

    def body(a_ref, b_ref, *rest):
        extra_refs = rest[:n_extra]
        rest = rest[n_extra + len(pins):]
        out_refs = rest[:n_out]
        acc_ref = rest[n_out]
        kk = pl.program_id(2)

        def part():
            return lax.dot_general(a_ref[...], b_ref[...], dims, preferred_element_type=F32)

        def finish(acc):
            outs = epilogue(acc, *[r[...] for r in extra_refs])
            for o_ref, o in zip(out_refs, outs):
                o_ref[...] = o.astype(o_ref.dtype)

        if nk == 1:
            finish(part())
        else:
            @pl.when(kk == 0)
            def _():
                acc_ref[...] = part()

            @pl.when((kk > 0) & (kk < nk - 1))
            def _():
                acc_ref[...] += part()

            @pl.when(kk == nk - 1)
            def _():
                finish(acc_ref[...] + part())

    return pl.pallas_call(
        body,
        grid=grid,
        in_specs=[a_spec, b_spec, *extra_specs, *[pl.BlockSpec((8, 128), lambda i, j, kk: (0, 0)) for _ in pins]],
        out_specs=list(out_specs),
        out_shape=list(out_shapes),
        scratch_shapes=[pltpu.VMEM(acc_shape if nk > 1 else (8, 128), F32)],
        compiler_params=_params("parallel", "parallel", "arbitrary"),
        name=name,
    )(a, b, *extras, *pins)


def _mm(a, b, kind, *, out_dtypes, epilogue=None, extras=(), name, tm=1024, tn=1024, tk=4096, pin=None):
    if kind == "nn":
        (m, k), n = a.shape, b.shape[1]
    elif kind == "nt":
        (m, k), n = a.shape, b.shape[0]
    else:
        (k, m), n = a.shape, b.shape[1]
    tm, tn, tk = _tile(m, tm), _tile(n, tn), _tile(k, tk)
    if kind == "nn":
        a_spec = pl.BlockSpec((tm, tk), lambda i, j, kk: (i, kk))
        b_spec = pl.BlockSpec((tk, tn), lambda i, j, kk: (kk, j))
        dims = _NN
    elif kind == "nt":
        a_spec = pl.BlockSpec((tm, tk), lambda i, j, kk: (i, kk))
        b_spec = pl.BlockSpec((tn, tk), lambda i, j, kk: (j, kk))
        dims = _NT
    else:
        a_spec = pl.BlockSpec((tk, tm), lambda i, j, kk: (kk, i))
        b_spec = pl.BlockSpec((tk, tn), lambda i, j, kk: (kk, j))
        dims = _TN
    o_spec = pl.BlockSpec((tm, tn), lambda i, j, kk: (i, j))
    if epilogue is None:
        epilogue = lambda acc, *e: tuple(acc for _ in out_dtypes)
    return _matmul(
        a, b, dims=dims, grid=(m // tm, n // tn, k // tk), a_spec=a_spec, b_spec=b_spec, acc_shape=(tm, tn),
        out_shapes=[_sds((m, n), d) for d in out_dtypes], out_specs=[o_spec for _ in out_dtypes],
        epilogue=epilogue, extras=extras, extra_specs=[o_spec for _ in extras], name=name, pin=pin)


def _rmsnorm_fwd(h, g, out_dtype, name):
    t, d = h.shape
    tr = _tile(t, 256)

    def body(h_ref, g_ref, o_ref):
        x = h_ref[...]
        r = lax.rsqrt(jnp.mean(x * x, axis=-1, keepdims=True) + EPS)
        o_ref[...] = (x * r * g_ref[...]).astype(o_ref.dtype)

    return pl.pallas_call(
        body, grid=(t // tr,),
        in_specs=[pl.BlockSpec((tr, d), lambda i: (i, 0)), pl.BlockSpec((1, d), lambda i: (0, 0))],
        out_specs=pl.BlockSpec((tr, d), lambda i: (i, 0)),
        out_shape=_sds((t, d), out_dtype), compiler_params=_params("parallel"), name=name)(h, g)


def _rmsnorm_bwd(h, g, dy, resid, name):
    t, d = h.shape
    tr = _tile(t, 128)

    def body(h_ref, g_ref, dy_ref, res_ref, dh_ref, dhb_ref, dg_ref):
        x = h_ref[...]
        r = lax.rsqrt(jnp.mean(x * x, axis=-1, keepdims=True) + EPS)
        xn = x * r
        dyv = dy_ref[...]
        gdy = dyv * g_ref[...]
        dh = res_ref[...] + r * (gdy - xn * jnp.mean(gdy * xn, axis=-1, keepdims=True))
        dh_ref[...] = dh
        dhb_ref[...] = dh.astype(BF16)
        part = jnp.sum(dyv * xn, axis=0, keepdims=True)

        @pl.when(pl.program_id(0) == 0)
        def _():
            dg_ref[...] = part

        @pl.when(pl.program_id(0) > 0)
        def _():
            dg_ref[...] += part

    row = pl.BlockSpec((tr, d), lambda i: (i, 0))
    vec = pl.BlockSpec((1, d), lambda i: (0, 0))
    return pl.pallas_call(
        body, grid=(t // tr,), in_specs=[row, vec, row, row], out_specs=[row, row, vec],
        out_shape=[_sds((t, d), F32), _sds((t, d), BF16), _sds((1, d), F32)],
        compiler_params=_params("arbitrary"), name=name)(h, g, dy, resid)


def _final_bwd(h, g, tgt):
    t, d = h.shape
    tr = _tile(t, 128)

    def body(h_ref, g_ref, t_ref, loss_ref, dh_ref, dhb_ref, dg_ref):
        x = h_ref[...]
        r = lax.rsqrt(jnp.mean(x * x, axis=-1, keepdims=True) + EPS)
        xn = x * r
        gv = g_ref[...]
        e = xn * gv - t_ref[...]
        lpart = jnp.full((1, 128), 0.5 * jnp.sum(jnp.mean(e * e, axis=-1, keepdims=True)), F32)
        dyv = e * (1.0 / d)
        gdy = dyv * gv
        dh = r * (gdy - xn * jnp.mean(gdy * xn, axis=-1, keepdims=True))
        dh_ref[...] = dh
        dhb_ref[...] = dh.astype(BF16)
        part = jnp.sum(dyv * xn, axis=0, keepdims=True)

        @pl.when(pl.program_id(0) == 0)
        def _():
            dg_ref[...] = part
            loss_ref[...] = lpart

        @pl.when(pl.program_id(0) > 0)
        def _():
            dg_ref[...] += part
            loss_ref[...] += lpart

    row = pl.BlockSpec((tr, d), lambda i: (i, 0))
    vec = pl.BlockSpec((1, d), lambda i: (0, 0))
    return pl.pallas_call(
        body, grid=(t // tr,), in_specs=[row, vec, row],
        out_specs=[pl.BlockSpec((1, 128), lambda i: (0, 0)), row, row, vec],
        out_shape=[_sds((1, 128), F32), _sds((t, d), F32), _sds((t, d), BF16), _sds((1, d), F32)],
        compiler_params=_params("arbitrary"), name="final_loss_bwd")(h, g, tgt)


def _shift_rows(x, s, row):
    n = x.shape[0]
    y = pltpu.roll(x, (-s) % n, 0)
    return jnp.where((row + s >= 0) & (row + s < n), y, 0.0)


def _span_sum(x, start, length, row):
    if start >= 0:
        y, step = _shift_rows(x, start, row) if start else x, 1
    else:
        last = start + length - 1
        assert last <= 0
        y, step = _shift_rows(x, last, row) if last else x, -1
    n = 1
    while n < length:
        y = y + _shift_rows(y, step * n, row)
        n *= 2
    return y


def _pool_apply(x, seq, transpose, name):
    t, d = x.shape
    dg = d // N_GROUPS
    tc = _tile(dg, 256)
    nblk = dg // tc

    def body(x_ref, o_ref):
        grp = pl.program_id(1)
        row = lax.broadcasted_iota(jnp.int32, (seq, tc), 0)
        for gi, hw in enumerate(POOL_HALF):
            @pl.when(grp == gi)
            def _(hw=hw):
                xv = x_ref[...]
                cnt = (jnp.minimum(row + hw, seq) - jnp.maximum(row - hw, 0)).astype(F32)
                if not transpose:
                    w = _span_sum(xv, 0, hw, row) + _span_sum(xv, -hw, hw, row)
                    o_ref[...] = (w / cnt - xv).astype(o_ref.dtype)
                else:
                    u = xv / cnt
                    w = _span_sum(u, 1, hw, row) + _span_sum(u, -(hw - 1), hw, row)
                    o_ref[...] = (w - xv).astype(o_ref.dtype)

    spec = pl.BlockSpec((seq, tc), lambda b, g, j: (b, g * nblk + j))
    return pl.pallas_call(
        body, grid=(t // seq, N_GROUPS, nblk), in_specs=[spec], out_specs=spec,
        out_shape=_sds((t, d), F32 if transpose else BF16),
        compiler_params=_params("parallel", "parallel", "parallel"), name=name)(x)


def _pool_mm_fwd(dm, wp, x, scale):
    t, d = dm.shape
    dg = d // N_GROUPS
    rs = dg // N_SHARDS
    tm = _tile(t, 1024)
    o_spec = pl.BlockSpec((tm, dg), lambda i, j, kk: (i, j))
    return _matmul(
        dm, wp, dims=_NN, grid=(t // tm, N_GROUPS, N_SHARDS),
        a_spec=pl.BlockSpec((tm, rs), lambda i, j, kk: (i, j * N_SHARDS + kk)),
        b_spec=pl.BlockSpec((None, None, rs, dg), lambda i, j, kk: (kk, j, 0, 0)),
        acc_shape=(tm, dg), out_shapes=[_sds((t, d), F32), _sds((t, d), F32)], out_specs=[o_spec, o_spec],
        epilogue=lambda acc, xv, sc: (xv + acc * sc, acc),
        extras=(x, scale), extra_specs=[o_spec, pl.BlockSpec((1, dg), lambda i, j, kk: (0, j))], name="pool_mm_fwd")


def _pool_scale_bwd(dh, ypre, scale):
    t, d = dh.shape
    tr = _tile(t, 256)

    def body(dh_ref, y_ref, s_ref, o_ref, ds_ref):
        g = dh_ref[...]
        o_ref[...] = (g * s_ref[...]).astype(BF16)
        part = jnp.sum(g * y_ref[...], axis=0, keepdims=True)

        @pl.when(pl.program_id(0) == 0)
        def _():
            ds_ref[...] = part

        @pl.when(pl.program_id(0) > 0)
        def _():
            ds_ref[...] += part

    row = pl.BlockSpec((tr, d), lambda i: (i, 0))
    vec = pl.BlockSpec((1, d), lambda i: (0, 0))
    return pl.pallas_call(
        body, grid=(t // tr,), in_specs=[row, row, vec], out_specs=[row, vec],
        out_shape=[_sds((t, d), BF16), _sds((1, d), F32)], compiler_params=_params("arbitrary"),
        name="pool_scale_bwd")(dh, ypre, scale)


def _pool_mm_bwd_x(dys, wp, pin):
    t, d = dys.shape
    dg = d // N_GROUPS
    rs = dg // N_SHARDS
    tm = _tile(t, 1024)
    return _matmul(
        dys, wp, dims=_NT, grid=(t // tm, N_GROUPS * N_SHARDS, 1),
        a_spec=pl.BlockSpec((tm, dg), lambda i, j, kk: (i, j // N_SHARDS)),
        b_spec=pl.BlockSpec((None, None, rs, dg), lambda i, j, kk: (j % N_SHARDS, j // N_SHARDS, 0, 0)),
        acc_shape=(tm, rs), out_shapes=[_sds((t, d), F32)],
        out_specs=[pl.BlockSpec((tm, rs), lambda i, j, kk: (i, j))],
        epilogue=lambda acc: (acc,), name="pool_mm_bwd_x", pin=pin)[0]


def _pool_mm_bwd_w(dm, dys):
    t, d = dm.shape
    dg = d // N_GROUPS
    rs = dg // N_SHARDS
    tk = _tile(t, 4096)
    return _matmul(
        dm, dys, dims=_TN, grid=(N_GROUPS * N_SHARDS, 1, t // tk),
        a_spec=pl.BlockSpec((tk, rs), lambda i, j, kk: (kk, i)),
        b_spec=pl.BlockSpec((tk, dg), lambda i, j, kk: (kk, i // N_SHARDS)),
        acc_shape=(rs, dg), out_shapes=[_sds((N_SHARDS, N_GROUPS, rs, dg), BF16)],
        out_specs=[pl.BlockSpec((None, None, rs, dg), lambda i, j, kk: (i % N_SHARDS, i // N_SHARDS, 0, 0))],
        epilogue=lambda acc: (acc,), name="pool_mm_bwd_w")[0]


def _mlp_fwd(h, hn, w1g, w2_after, pin=None):
    t, d = hn.shape
    f4 = w1g.shape[2]
    dff = N_SHARDS * f4
    tm, tn, tk = _tile(t, 1024), _tile(f4, 1024), _tile(d, 4096)
    nb = f4 // tn
    o_spec = pl.BlockSpec((tm, tn), lambda i, j, kk: (i, j))

    def act(acc):
        r = jnp.maximum(acc, 0.0)
        return r, r * r

    r, u = _matmul(
        hn, w1g, dims=_NN, grid=(t // tm, dff // tn, d // tk),
        a_spec=pl.BlockSpec((tm, tk), lambda i, j, kk: (i, kk)),
        b_spec=pl.BlockSpec((None, tk, tn), lambda i, j, kk: (j // nb, kk, j % nb)),
        acc_shape=(tm, tn), out_shapes=[_sds((t, dff), BF16), _sds((t, dff), BF16)], out_specs=[o_spec, o_spec],
        epilogue=act, name="mlp_up", pin=pin)
    w2 = w2_after(u)
    (out,) = _mm(u, w2, "nn", out_dtypes=[F32], epilogue=lambda acc, hv: (hv + acc,), extras=(h,), name="mlp_down",
                 tk=2048)
    return out, r, u, w2


def _mlp_bwd(dhb, hn, r, u, w1g, w2):
    t, d = hn.shape
    f4 = w1g.shape[2]
    dff = N_SHARDS * f4
    (da,) = _mm(dhb, w2, "nt", out_dtypes=[BF16], epilogue=lambda acc, rv: (acc * (2.0 * rv.astype(F32)),),
                extras=(r,), name="mlp_bwd_da")
    (dw2,) = _mm(u, dhb, "tn", out_dtypes=[BF16], name="mlp_bwd_dw2")
    tok = yield dw2.reshape(N_SHARDS, f4, d)
    tm, tn, tk = _tile(d, 1024), _tile(f4, 1024), _tile(t, 4096)
    nb = f4 // tn
    (dw1,) = _matmul(
        hn, da, dims=_TN, grid=(d // tm, dff // tn, t // tk),
        a_spec=pl.BlockSpec((tk, tm), lambda i, j, kk: (kk, i)),
        b_spec=pl.BlockSpec((tk, tn), lambda i, j, kk: (kk, j)),
        acc_shape=(tm, tn), out_shapes=[_sds((N_SHARDS, d, f4), BF16)],
        out_specs=[pl.BlockSpec((None, tm, tn), lambda i, j, kk: (j // nb, i, j % nb))],
        epilogue=lambda acc: (acc,), name="mlp_bwd_dw1", pin=tok)
    tok = yield dw1
    tm, tn, tk = _tile(t, 1024), _tile(d, 1024), _tile(f4, 4096)
    nbk = f4 // tk
    (dhn,) = _matmul(
        da, w1g, dims=_NT, grid=(t // tm, d // tn, dff // tk),
        a_spec=pl.BlockSpec((tm, tk), lambda i, j, kk: (i, kk)),
        b_spec=pl.BlockSpec((None, tn, tk), lambda i, j, kk: (kk // nbk, j, kk % nbk)),
        acc_shape=(tm, tn), out_shapes=[_sds((t, d), F32)],
        out_specs=[pl.BlockSpec((tm, tn), lambda i, j, kk: (i, j))], epilogue=lambda acc: (acc,), name="mlp_bwd_dhn",
        pin=tok)
    yield dhn


def _split3(x):
    a = x.astype(BF16)
    r1 = x - a.astype(F32)
    b = r1.astype(BF16)
    c = (r1 - b.astype(F32)).astype(BF16)
    return a, b, c


def _dot(a, b, dims):
    return lax.dot_general(a.astype(BF16), b.astype(BF16), dims, preferred_element_type=F32)


def _chunk_terms(q, k, g, rev, scale):
    c = q.shape[0]
    ri = lax.broadcasted_iota(jnp.int32, (c, c), 0)
    ci = lax.broadcasted_iota(jnp.int32, (c, c), 1)
    seen = (ci >= ri) if rev else (ci <= ri)
    tri = seen.astype(BF16)
    g1, g2, g3 = _split3(g)
    b = (lax.dot_general(tri, g1, _NN, preferred_element_type=F32)
         + lax.dot_general(tri, g2, _NN, preferred_element_type=F32)
         + lax.dot_general(tri, g3, _NN, preferred_element_type=F32))
    mid = c // 2 if rev else c // 2 - 1
    last = 0 if rev else c - 1
    rows = lax.broadcasted_iota(jnp.int32, b.shape, 0)
    b_mid = jnp.sum(jnp.where(rows == mid, b, 0.0), axis=0, keepdims=True)
    b_last = jnp.sum(jnp.where(rows == last, b, 0.0), axis=0, keepdims=True)
    qs = q * scale
    e1 = jnp.exp(b - b_mid)
    e2 = jnp.exp(b_mid - b)
    eb = jnp.exp(b)
    el = jnp.exp(b_last - b)
    return dict(seen=seen, tri=tri, mid=mid, last=last, e1=e1, e2=e2, eb=eb, el=el, a=jnp.exp(b_last),
                qe=qs * e1, ke=k * e2, qi=qs * eb, ks=k * el)


def _step_terms(q_ref, k_ref, g_ref, rev, scale):
    lo, hi = slice(0, CHUNK), slice(CHUNK, STEP)
    rows = (hi, lo) if rev else (lo, hi)
    return rows, [_chunk_terms(q_ref[r, :], k_ref[r, :], g_ref[r, :], rev, scale) for r in rows]


def _gla_fwd(pm, la, seq):
    t = pm.shape[0]
    d = pm.shape[1] // 3
    dk, dv = d // 2 // N_HEADS, d // N_HEADS
    nb, nc = t // seq, seq // STEP
    scale = dk ** -0.5
    kq, kk_, kv = 0, N_HEADS, (d // dv)

    def body(qf, kf, vf, gf, qb, kb, vb, gb, of_ref, ob_ref, stf_ref, stb_ref, sf, sb):
        n = pl.program_id(1)

        @pl.when(n == 0)
        def _():
            sf[...] = jnp.zeros_like(sf)
            sb[...] = jnp.zeros_like(sb)

        for (q_ref, k_ref, v_ref, g_ref, o_ref, st_ref, s_ref, rev) in (
                (qf, kf, vf, gf, of_ref, stf_ref, sf, False), (qb, kb, vb, gb, ob_ref, stb_ref, sb, True)):
            rows, (t0, t1) = _step_terms(q_ref, k_ref, g_ref, rev, scale)
            v0, v1 = v_ref[rows[0], :], v_ref[rows[1], :]
            st = s_ref[...]
            stb = st.astype(BF16)
            st_ref[...] = stb
            sc0 = jnp.where(t0["seen"], _dot(t0["qe"], t0["ke"], _NT), 0.0)
            sc1 = jnp.where(t1["seen"], _dot(t1["qe"], t1["ke"], _NT), 0.0)
            cross = _dot(t1["qi"], t0["ks"], _NT)
            qi_all = jnp.concatenate([t0["qi"], t1["qi"] * t0["a"]], axis=0)
            o_inter = lax.dot_general(qi_all.astype(BF16), stb, _NT, preferred_element_type=F32)
            o_ref[rows[0], :] = _dot(sc0, v0, _NN) + o_inter[:CHUNK]
            o_ref[rows[1], :] = _dot(sc1, v1, _NN) + _dot(cross, v0, _NN) + o_inter[CHUNK:]
            ks_all = jnp.concatenate([t0["ks"] * t1["a"], t1["ks"]], axis=0)
            s_ref[...] = st * (t0["a"] * t1["a"]) + _dot(jnp.concatenate([v0, v1], axis=0), ks_all, _TN)

    def row(bh, n, rev):
        return (bh // N_HEADS) * nc + (nc - 1 - n if rev else n)

    def specs(rev):
        return [
            pl.BlockSpec((STEP,dk), lambda bh, n: (row(bh, n, rev), kq + bh % N_HEADS)),
            pl.BlockSpec((STEP,dk), lambda bh, n: (row(bh, n, rev), kk_ + bh % N_HEADS)),
            pl.BlockSpec((STEP,dv), lambda bh, n: (row(bh, n, rev), kv + bh % N_HEADS)),
            pl.BlockSpec((STEP,dk), lambda bh, n: (row(bh, n, rev), (N_HEADS if rev else 0) + bh % N_HEADS)),
        ]

    def o_spec(rev):
        return pl.BlockSpec((STEP,dv), lambda bh, n: (row(bh, n, rev), bh % N_HEADS))

    def st_spec(rev):
        return pl.BlockSpec((None, None, dv, dk), lambda bh, n: (bh, nc - 1 - n if rev else n, 0, 0))

    sf_, sb_ = specs(False), specs(True)
    return pl.pallas_call(
        body, grid=(nb * N_HEADS, nc),
        in_specs=[*sf_, *sb_],
        out_specs=[o_spec(False), o_spec(True), st_spec(False), st_spec(True)],
        out_shape=[_sds((t, d), F32), _sds((t, d), F32),
                   _sds((nb * N_HEADS, nc, dv, dk), BF16), _sds((nb * N_HEADS, nc, dv, dk), BF16)],
        scratch_shapes=[pltpu.VMEM((dv, dk), F32), pltpu.VMEM((dv, dk), F32)],
        compiler_params=_params("parallel", "arbitrary"), name="gla_scan_fwd",
    )(pm, pm, pm, la, pm, pm, pm, la)


def _gla_bwd(pm, la, do, st_f, st_b, seq):
    t = pm.shape[0]
    d = pm.shape[1] // 3
    dk, dv = d // 2 // N_HEADS, d // N_HEADS
    kd = dk * N_HEADS
    nb, nc = t // seq, seq // STEP
    scale = dk ** -0.5
    kq, kk_, kv = 0, N_HEADS, (d // dv)

    def body(qf, kf, vf, gf, dof, stf, qb, kb, vb, gb, dob, stb_,
             dqf, dkf, dvf, dgf, dqb, dkb, dvb, dgb, dsf, dsb):
        n = pl.program_id(1)

        @pl.when(n == 0)
        def _():
            dsf[...] = jnp.zeros_like(dsf)
            dsb[...] = jnp.zeros_like(dsb)

        for (q_ref, k_ref, v_ref, g_ref, do_ref, st_ref, dq_ref, dk_ref, dv_ref, dg_ref, ds_ref, rev) in (
                (qf, kf, vf, gf, dof, stf, dqf, dkf, dvf, dgf, dsf, False),
                (qb, kb, vb, gb, dob, stb_, dqb, dkb, dvb, dgb, dsb, True)):
            rows, (t0, t1) = _step_terms(q_ref, k_ref, g_ref, rev, scale)
            v0, v1 = v_ref[rows[0], :], v_ref[rows[1], :]
            do0, do1 = do_ref[rows[0], :], do_ref[rows[1], :]
            st = st_ref[...]
            ds = ds_ref[...]
            dsb16 = ds.astype(BF16)
            a0, a1 = t0["a"], t1["a"]
            sc0 = jnp.where(t0["seen"], _dot(t0["qe"], t0["ke"], _NT), 0.0)
            sc1 = jnp.where(t1["seen"], _dot(t1["qe"], t1["ke"], _NT), 0.0)
            cross = _dot(t1["qi"], t0["ks"], _NT)
            dsc0 = jnp.where(t0["seen"], _dot(do0, v0, _NT), 0.0)
            dsc1 = jnp.where(t1["seen"], _dot(do1, v1, _NT), 0.0)
            dcross = _dot(do1, v0, _NT)
            qi_all = jnp.concatenate([t0["qi"], t1["qi"] * a0], axis=0)
            ks_all = jnp.concatenate([t0["ks"] * a1, t1["ks"]], axis=0)
            v_all = jnp.concatenate([v0, v1], axis=0).astype(BF16)
            do_all = jnp.concatenate([do0, do1], axis=0).astype(BF16)
            dks_all = lax.dot_general(v_all, dsb16, _NN, preferred_element_type=F32)
            dv_all = lax.dot_general(ks_all.astype(BF16), dsb16, _NT, preferred_element_type=F32)
            dqi_all = lax.dot_general(do_all, st, _NN, preferred_element_type=F32)
            da01 = jnp.sum(ds * st.astype(F32), axis=0, keepdims=True)
            dv_ref[rows[0], :] = (_dot(sc0, do0, _TN) + _dot(cross, do1, _TN) + dv_all[:CHUNK]).astype(BF16)
            dv_ref[rows[1], :] = (_dot(sc1, do1, _TN) + dv_all[CHUNK:]).astype(BF16)
            dqi0 = dqi_all[:CHUNK]
            dqi1 = dqi_all[CHUNK:] * a0 + _dot(dcross, t0["ks"], _NN)
            dks0 = dks_all[:CHUNK] * a1 + _dot(dcross, t1["qi"], _TN)
            dks1 = dks_all[CHUNK:]
            da0 = da01 * a1 + jnp.sum(dqi_all[CHUNK:] * t1["qi"], axis=0, keepdims=True)
            da1 = da01 * a0 + jnp.sum(dks_all[:CHUNK] * t0["ks"], axis=0, keepdims=True)
            for r, tm, dsc, dqi, dks, da in ((rows[0], t0, dsc0, dqi0, dks0, da0), (rows[1], t1, dsc1, dqi1, dks1, da1)):
                dqe = _dot(dsc, tm["ke"], _NN)
                dke = _dot(dsc, tm["qe"], _TN)
                dq_ref[r, :] = ((dqe * tm["e1"] + dqi * tm["eb"]) * scale).astype(BF16)
                dk_ref[r, :] = (dke * tm["e2"] + dks * tm["el"]).astype(BF16)
                t_q, t_k, t_s = dqe * tm["qe"], dke * tm["ke"], dks * tm["ks"]
                db = t_q - t_k + dqi * tm["qi"] - t_s
                mid_row = jnp.sum(t_k - t_q, axis=0, keepdims=True)
                last_row = jnp.sum(t_s, axis=0, keepdims=True) + da * tm["a"]
                ridx = lax.broadcasted_iota(jnp.int32, db.shape, 0)
                db = db + jnp.where(ridx == tm["mid"], mid_row, 0.0) + jnp.where(ridx == tm["last"], last_row, 0.0)
                d1, d2, d3 = _split3(db)
                dg_ref[r, :] = (lax.dot_general(tm["tri"], d1, _TN, preferred_element_type=F32)
                                + lax.dot_general(tm["tri"], d2, _TN, preferred_element_type=F32)
                                + lax.dot_general(tm["tri"], d3, _TN, preferred_element_type=F32))
            ds_ref[...] = ds * (a0 * a1) + lax.dot_general(do_all, qi_all.astype(BF16), _TN,
                                                           preferred_element_type=F32)

    def row(bh, n, rev):
        return (bh // N_HEADS) * nc + (n if rev else nc - 1 - n)

    def specs(rev):
        return [
            pl.BlockSpec((STEP,dk), lambda bh, n: (row(bh, n, rev), kq + bh % N_HEADS)),
            pl.BlockSpec((STEP,dk), lambda bh, n: (row(bh, n, rev), kk_ + bh % N_HEADS)),
            pl.BlockSpec((STEP,dv), lambda bh, n: (row(bh, n, rev), kv + bh % N_HEADS)),
            pl.BlockSpec((STEP,dk), lambda bh, n: (row(bh, n, rev), (N_HEADS if rev else 0) + bh % N_HEADS)),
            pl.BlockSpec((STEP,dv), lambda bh, n: (row(bh, n, rev), bh % N_HEADS)),
            pl.BlockSpec((None, None, dv, dk), lambda bh, n: (bh, n if rev else nc - 1 - n, 0, 0)),
        ]

    def outs(rev):
        return [
            pl.BlockSpec((STEP,dk), lambda bh, n: (row(bh, n, rev), bh % N_HEADS)),
            pl.BlockSpec((STEP,dk), lambda bh, n: (row(bh, n, rev), bh % N_HEADS)),
            pl.BlockSpec((STEP,dv), lambda bh, n: (row(bh, n, rev), bh % N_HEADS)),
            pl.BlockSpec((STEP,dk), lambda bh, n: (row(bh, n, rev), bh % N_HEADS)),
        ]

    of_, ob_ = outs(False), outs(True)
    res = pl.pallas_call(
        body, grid=(nb * N_HEADS, nc),
        in_specs=[*specs(False), *specs(True)],
        out_specs=[*of_, *ob_],
        out_shape=[_sds((t, kd), BF16), _sds((t, kd), BF16), _sds((t, d), BF16), _sds((t, kd), F32),
                   _sds((t, kd), BF16), _sds((t, kd), BF16), _sds((t, d), BF16), _sds((t, kd), F32)],
        scratch_shapes=[pltpu.VMEM((dv, dk), F32), pltpu.VMEM((dv, dk), F32)],
        compiler_params=_params("parallel", "arbitrary"), name="gla_scan_bwd",
    )(pm, pm, pm, la, do, st_f, pm, pm, pm, la, do, st_b)
    return res


def _sigmoid(x):
    return 1.0 / (1.0 + jnp.exp(-x))


def _gla_post_fwd(o_f, o_b, pm, gn):
    t, d = o_f.shape
    dv = d // N_HEADS
    tr = _tile(t, 512)
    gate_blk = 2 * d // dv

    def body(of_ref, ob_ref, gt_ref, gn_ref, out_ref):
        o = of_ref[...] + ob_ref[...]
        n = o * lax.rsqrt(jnp.mean(o * o, axis=-1, keepdims=True) + EPS) * gn_ref[...]
        gt = gt_ref[...]
        out_ref[...] = (n * (gt * _sigmoid(gt))).astype(BF16)

    blk = pl.BlockSpec((tr, dv), lambda i, h: (i, h))
    return pl.pallas_call(
        body, grid=(t // tr, N_HEADS),
        in_specs=[blk, blk, pl.BlockSpec((tr, dv), lambda i, h: (i, gate_blk + h)),
                  pl.BlockSpec((1, dv), lambda i, h: (0, 0))],
        out_specs=blk, out_shape=_sds((t, d), BF16), compiler_params=_params("parallel", "parallel"),
        name="gla_post_fwd")(o_f, o_b, pm, gn)


def _gla_post_bwd(dog, o_f, o_b, pm, gn):
    t, d = o_f.shape
    dv = d // N_HEADS
    tr = _tile(t, 256)
    gate_blk = 2 * d // dv

    def body(dog_ref, of_ref, ob_ref, gt_ref, gn_ref, do_ref, dgt_ref, dgn_ref):
        o = of_ref[...] + ob_ref[...]
        rr = lax.rsqrt(jnp.mean(o * o, axis=-1, keepdims=True) + EPS)
        on = o * rr
        gnv = gn_ref[...]
        gt = gt_ref[...]
        sg = _sigmoid(gt)
        sl = gt * sg
        dg_out = dog_ref[...]
        dn = dg_out * sl
        dgt_ref[...] = dg_out * (on * gnv) * (sg * (1.0 + gt * (1.0 - sg)))
        gdn = dn * gnv
        do_ref[...] = rr * (gdn - on * jnp.mean(gdn * on, axis=-1, keepdims=True))
        part = jnp.sum(dn * on, axis=0, keepdims=True)
        first = (pl.program_id(0) == 0) & (pl.program_id(1) == 0)

        @pl.when(first)
        def _():
            dgn_ref[...] = part

        @pl.when(jnp.logical_not(first))
        def _():
            dgn_ref[...] += part

    blk = pl.BlockSpec((tr, dv), lambda i, h: (i, h))
    vec = pl.BlockSpec((1, dv), lambda i, h: (0, 0))
    return pl.pallas_call(
        body, grid=(t // tr, N_HEADS),
        in_specs=[blk, blk, blk, pl.BlockSpec((tr, dv), lambda i, h: (i, gate_blk + h)), vec],
        out_specs=[blk, blk, vec], out_shape=[_sds((t, d), F32), _sds((t, d), F32), _sds((1, dv), F32)],
        compiler_params=_params("arbitrary", "arbitrary"), name="gla_post_bwd")(dog, o_f, o_b, pm, gn)


def _gla_dp(dq_f, dq_b, dk_f, dk_b, dv_f, dv_b, dgate):
    t, d = dv_f.shape
    kd = dq_f.shape[1]
    tr = _tile(t, 128)

    def body(a1, a2, b1, b2, c1, c2, g, o_ref):
        o_ref[:, 0:kd] = (a1[...].astype(F32) + a2[...].astype(F32)).astype(BF16)
        o_ref[:, kd:2 * kd] = (b1[...].astype(F32) + b2[...].astype(F32)).astype(BF16)
        o_ref[:, 2 * kd:2 * kd + d] = (c1[...].astype(F32) + c2[...].astype(F32)).astype(BF16)
        o_ref[:, 2 * kd + d:] = g[...].astype(BF16)

    sk = pl.BlockSpec((tr, kd), lambda i: (i, 0))
    sd = pl.BlockSpec((tr, d), lambda i: (i, 0))
    return pl.pallas_call(
        body, grid=(t // tr,), in_specs=[sk, sk, sk, sk, sd, sd, sd],
        out_specs=pl.BlockSpec((tr, 3 * d), lambda i: (i, 0)), out_shape=_sds((t, 3 * d), BF16),
        compiler_params=_params("parallel"), name="gla_dp")(dq_f, dq_b, dk_f, dk_b, dv_f, dv_b, dgate)


def _adamw(w, g, m, v, name):
    r, c = w.shape
    tr = _tile(r, 128)
    bc1 = 1.0 - ADAM_B1 ** ADAM_STEP
    bc2 = 1.0 - ADAM_B2 ** ADAM_STEP

    def body(w_ref, g_ref, m_ref, v_ref, d_ref, nm_ref, nv_ref):
        gv = g_ref[...]
        mn = ADAM_B1 * m_ref[...] + (1.0 - ADAM_B1) * gv
        vn = ADAM_B2 * v_ref[...] + (1.0 - ADAM_B2) * (gv * gv)
        m_hat = mn / bc1
        v_hat = vn / bc2
        d_ref[...] = -ADAM_LR * (m_hat / (jnp.sqrt(v_hat) + ADAM_EPS) + ADAM_WD * w_ref[...])
        nm_ref[...] = mn
        nv_ref[...] = vn

    blk = pl.BlockSpec((tr, c), lambda i: (i, 0))
    return pl.pallas_call(
        body, grid=(r // tr,), in_specs=[blk] * 4, out_specs=[blk] * 3, out_shape=[_sds((r, c), F32)] * 3,
        compiler_params=_params("parallel"), name=name)(w, g, m, v)


def _adamw_half(w, g, m, v, layer, half, prev, pin, name, transposed=False):
    bc1 = 1.0 - ADAM_B1 ** ADAM_STEP
    bc2 = 1.0 - ADAM_B2 ** ADAM_STEP
    n_skip = 1 + (0 if prev is None else 4)
    if transposed:
        nl, c, r = w.shape
        tr = _tile(r // 2, 128)
        nblk = r // 2 // tr
        lay = pl.BlockSpec((None, c, tr), lambda i, h_ref: (layer, 0, h_ref[0] * nblk + i))
        g_spec = pl.BlockSpec((c, tr), lambda i, h_ref: (0, i))
    else:
        nl, r, c = w.shape
        tr = _tile(r // 2, 128)
        nblk = r // 2 // tr
        lay = pl.BlockSpec((None, tr, c), lambda i, h_ref: (layer, h_ref[0] * nblk + i, 0))
        g_spec = pl.BlockSpec((tr, c), lambda i, h_ref: (i, 0))

    def body(h_ref, w_ref, g_ref, m_ref, v_ref, *rest):
        go_ref, d_ref, nm_ref, nv_ref = rest[n_skip:]
        gv = g_ref[...]
        mn = ADAM_B1 * m_ref[...] + (1.0 - ADAM_B1) * gv
        vn = ADAM_B2 * v_ref[...] + (1.0 - ADAM_B2) * (gv * gv)
        m_hat = mn / bc1
        v_hat = vn / bc2
        d_ref[...] = -ADAM_LR * (m_hat / (jnp.sqrt(v_hat) + ADAM_EPS) + ADAM_WD * w_ref[...])
        nm_ref[...] = mn
        nv_ref[...] = vn
        go_ref[...] = gv

    prev = tuple(prev or ())
    grid_spec = pltpu.PrefetchScalarGridSpec(
        num_scalar_prefetch=1, grid=(nblk,),
        in_specs=[lay, g_spec, lay, lay, pl.BlockSpec(TOKEN, lambda i, h_ref: (0, 0))] + [ANY] * len(prev),
        out_specs=[lay] * 4)
    return pl.pallas_call(
        body, grid_spec=grid_spec, out_shape=[_sds(w.shape, F32)] * 4,
        input_output_aliases={6 + k: k for k in range(len(prev))},
        compiler_params=_params("parallel"), name=name)(half, w, g, m, v, pin, *prev)


def _sum8(parts, own, me):
    _, n, _ = parts.shape

    def body(me_ref, p_ref, own_ref, o_ref):
        acc = None
        for i in range(8):
            term = jnp.where(me_ref[0] == i, own_ref[...], p_ref[i])
            acc = term if acc is None else acc + term
        o_ref[...] = acc

    grid_spec = pltpu.PrefetchScalarGridSpec(
        num_scalar_prefetch=1, grid=(1,),
        in_specs=[pl.BlockSpec((8, n, 128), lambda i, me_ref: (0, 0, 0)),
                  pl.BlockSpec((n, 128), lambda i, me_ref: (0, 0))],
        out_specs=pl.BlockSpec((n, 128), lambda i, me_ref: (0, 0)))
    return pl.pallas_call(body, grid_spec=grid_spec, out_shape=_sds((n, 128), F32), name="sum8")(me, parts, own)


def _pair_sum(ga, recv, c):
    _, _, rh, cols = ga.shape
    tr = _tile(rh, 256)

    def body(c_ref, a_ref, b_ref, o_ref):
        o_ref[...] = (a_ref[...].astype(F32) + b_ref[...].astype(F32)).astype(BF16)

    grid_spec = pltpu.PrefetchScalarGridSpec(
        num_scalar_prefetch=1, grid=(N_SHARDS, rh // tr),
        in_specs=[pl.BlockSpec((None, None, tr, cols), lambda s, i, c_ref: (s, c_ref[0], i, 0)),
                  pl.BlockSpec((None, tr, cols), lambda s, i, c_ref: (s, i, 0))],
        out_specs=pl.BlockSpec((None, tr, cols), lambda s, i, c_ref: (s, i, 0)))
    return pl.pallas_call(
        body, grid_spec=grid_spec, out_shape=_sds((N_SHARDS, rh, cols), BF16),
        compiler_params=_params("parallel", "parallel"), name="grad_pair_sum")(c, ga, recv)


def _quad_sum(pair, recv, s_me):
    _, rh, cols = pair.shape
    tr = _tile(rh, 256)

    def body(s_ref, p_ref, r1_ref, r2_ref, r3_ref, o_ref):
        o_ref[...] = ((p_ref[...].astype(F32) + r1_ref[...].astype(F32)) + r2_ref[...].astype(F32)) \
            + r3_ref[...].astype(F32)

    def blk(off):
        return pl.BlockSpec((None, tr, cols), lambda i, s_ref: ((s_ref[0] + off) % N_SHARDS, i, 0))

    grid_spec = pltpu.PrefetchScalarGridSpec(
        num_scalar_prefetch=1, grid=(rh // tr,), in_specs=[blk(0), blk(1), blk(2), blk(3)],
        out_specs=pl.BlockSpec((tr, cols), lambda i, s_ref: (i, 0)))
    return pl.pallas_call(
        body, grid_spec=grid_spec, out_shape=_sds((rh, cols), F32),
        compiler_params=_params("parallel"), name="grad_quad_sum")(s_me, pair, recv, recv, recv)


def _fill_own(w, layer, s_me, pin):
    _, rows, cols = w.shape
    rh = rows // 2
    tr = _tile(rh, 256)
    nblk = rh // tr
    pins = () if pin is None else (pin,)

    def body(s_ref, w_ref, *rest):
        rest[-1][...] = w_ref[...].astype(BF16)

    grid_spec = pltpu.PrefetchScalarGridSpec(
        num_scalar_prefetch=1, grid=(2, nblk),
        in_specs=[pl.BlockSpec((None, tr, cols), lambda h, i, s_ref: (layer, h * nblk + i, 0)),
                  *[pl.BlockSpec(TOKEN, lambda h, i, s_ref: (0, 0)) for _ in pins]],
        out_specs=pl.BlockSpec((None, None, tr, cols), lambda h, i, s_ref: (s_ref[0], h, i, 0)))
    return pl.pallas_call(
        body, grid_spec=grid_spec, out_shape=_sds((N_SHARDS, 2, rh, cols), BF16),
        compiler_params=_params("parallel", "parallel"), name="weight_fill_own")(s_me, w, *pins)


def _coords():
    return lax.axis_index("x"), lax.axis_index("y"), lax.axis_index("c")


def _other_chips(x, y):
    return [(x, 1 - y), (1 - x, y), (1 - x, 1 - y)]


def _bcast8(buf):
    n = buf.shape[0]

    def body(b_ref, o_ref, send_sems, recv_sems):
        x, y, c = _coords()
        me = 4 * x + 2 * y + c
        o_ref[me] = b_ref[...]
        copies = []
        for k in range(1, 8):
            peer = (x ^ (k >> 2), y ^ ((k >> 1) & 1), c ^ (k & 1))
            copies.append(pltpu.make_async_remote_copy(
                src_ref=b_ref, dst_ref=o_ref.at[me], send_sem=send_sems.at[k - 1], recv_sem=recv_sems.at[k - 1],
                device_id=peer, device_id_type=MESH))
        for cp in copies:
            cp.start()
        for k in range(1, 8):
            pltpu.make_async_remote_copy(
                src_ref=b_ref, dst_ref=o_ref.at[me ^ k], send_sem=send_sems.at[k - 1], recv_sem=recv_sems.at[k - 1],
                device_id=(x, y, c), device_id_type=MESH).wait_recv()
        for cp in copies:
            cp.wait_send()

    return pl.pallas_call(
        body, out_shape=_sds((8, n, 128), F32),
        in_specs=[pl.BlockSpec(memory_space=pltpu.VMEM)], out_specs=pl.BlockSpec(memory_space=pltpu.VMEM),
        scratch_shapes=[pltpu.SemaphoreType.DMA((7,)), pltpu.SemaphoreType.DMA((7,))],
        compiler_params=pltpu.CompilerParams(vmem_limit_bytes=VMEM_LIMIT), name="bcast8")(buf)


HBM = pl.BlockSpec(memory_space=pltpu.HBM)
SEM = pl.BlockSpec(memory_space=pltpu.SEMAPHORE)
SIDE = pltpu.SideEffectType.DATAFLOW_SIDE_EFFECTING
TOKEN = (8, 128)


def _in_hbm(a):
    return pltpu.with_memory_space_constraint(a, pltpu.HBM)


def _sibling_start(src, pieces, land_shape, after, name):
    n = len(pieces(None, None, None))

    def body(s_ref, land_ref, after_ref, send, recv, land_thru, token):
        x, y, c = _coords()
        for k, (a, b) in enumerate(pieces(s_ref, land_ref, c)):
            pltpu.make_async_remote_copy(
                src_ref=a, dst_ref=b, send_sem=send.at[k], recv_sem=recv.at[k], device_id=(x, y, 1 - c),
                device_id_type=MESH).start()
        token[...] = jnp.zeros_like(token)

    return pl.pallas_call(
        body, name=name,
        out_shape=(pltpu.SemaphoreType.DMA((n,)), pltpu.SemaphoreType.DMA((n,)), pltpu.HBM(land_shape, src.dtype),
                   _sds(TOKEN, F32)),
        in_specs=(HBM, HBM, ANY), out_specs=(SEM, SEM, HBM, pl.BlockSpec(memory_space=pltpu.VMEM)),
        input_output_aliases={1: 2}, compiler_params=pltpu.CompilerParams(has_side_effects=SIDE),
    )(_in_hbm(src), _in_hbm(lax.empty(land_shape, src.dtype)), after)


def _sibling_wait(send, recv, src, land, pieces, after, name):
    def body(s_ref, land_ref, send, recv, after_ref, land_out):
        x, y, c = _coords()
        for k, (a, b) in enumerate(pieces(s_ref, land_ref, c)):
            cp = pltpu.make_async_remote_copy(
                src_ref=a, dst_ref=b, send_sem=send.at[k], recv_sem=recv.at[k], device_id=(x, y, 1 - c),
                device_id_type=MESH)
            cp.wait_send()
            cp.wait_recv()

    return pl.pallas_call(
        body, name=name, out_shape=pltpu.HBM(land.shape, land.dtype), in_specs=(HBM, HBM, SEM, SEM, ANY),
        out_specs=HBM, input_output_aliases={1: 0}, compiler_params=pltpu.CompilerParams(has_side_effects=SIDE),
    )(src, land, send, recv, after)


def _swap_pieces(g_ref, land_ref, c):
    if g_ref is None:
        return [None] * N_SHARDS
    return [(g_ref.at[s, 1 - c], land_ref.at[s]) for s in range(N_SHARDS)]


def _whole_piece(r_ref, land_ref, c):
    return [(r_ref, land_ref)]


def _bcast_copies(b_ref, land_ref, send, recv):
    x, y, c = _coords()
    me = 4 * x + 2 * y + c
    return [pltpu.make_async_remote_copy(
        src_ref=b_ref, dst_ref=land_ref.at[me], send_sem=send.at[k - 1], recv_sem=recv.at[k - 1],
        device_id=(x ^ (k >> 2), y ^ ((k >> 1) & 1), c ^ (k & 1)), device_id_type=MESH) for k in range(1, 8)]


def _bcast_start(buf, after, name):
    def body(b_ref, land_ref, after_ref, send, recv, land_thru, token):
        for cp in _bcast_copies(b_ref, land_ref, send, recv):
            cp.start()
        token[...] = jnp.zeros_like(token)

    shape = (8,) + buf.shape
    return pl.pallas_call(
        body, name=name,
        out_shape=(pltpu.SemaphoreType.DMA((7,)), pltpu.SemaphoreType.DMA((7,)), pltpu.HBM(shape, buf.dtype),
                   _sds(TOKEN, F32)),
        in_specs=(HBM, HBM, ANY), out_specs=(SEM, SEM, HBM, pl.BlockSpec(memory_space=pltpu.VMEM)),
        input_output_aliases={1: 2}, compiler_params=pltpu.CompilerParams(has_side_effects=SIDE),
    )(_in_hbm(buf), _in_hbm(lax.empty(shape, buf.dtype)), after)


def _bcast_wait(send, recv, buf, land, after, name):
    def body(b_ref, land_ref, send, recv, after_ref, land_out):
        for cp in _bcast_copies(b_ref, land_ref, send, recv):
            cp.wait_send()
            cp.wait_recv()

    return pl.pallas_call(
        body, name=name, out_shape=pltpu.HBM(land.shape, land.dtype), in_specs=(HBM, HBM, SEM, SEM, ANY),
        out_specs=HBM, input_output_aliases={1: 0}, compiler_params=pltpu.CompilerParams(has_side_effects=SIDE),
    )(buf, land, send, recv, after)


def _scatter_start(part, name):
    def body(p_ref, land_ref, send, recv, p_thru, land_thru, token):
        x, y, c = _coords()
        s_me = 2 * x + y
        for j, (px, py) in enumerate(_other_chips(x, y)):
            pltpu.make_async_remote_copy(
                src_ref=p_ref.at[2 * px + py], dst_ref=land_ref.at[s_me], send_sem=send.at[j], recv_sem=recv.at[j],
                device_id=(px, py, c), device_id_type=MESH).start()
        token[...] = jnp.zeros_like(token)

    buf = pltpu.HBM(part.shape, part.dtype)
    return pl.pallas_call(
        body, name=name,
        out_shape=(pltpu.SemaphoreType.DMA((3,)), pltpu.SemaphoreType.DMA((3,)), buf, buf, _sds(TOKEN, F32)),
        in_specs=(HBM, HBM), out_specs=(SEM, SEM, HBM, HBM, pl.BlockSpec(memory_space=pltpu.VMEM)),
        input_output_aliases={0: 2, 1: 3}, compiler_params=pltpu.CompilerParams(has_side_effects=SIDE),
    )(_in_hbm(part), _in_hbm(lax.empty(part.shape, part.dtype)))


def _scatter_wait(send, recv, part, land, after, name):
    def body(p_ref, land_ref, send, recv, after_ref, p_out, land_out):
        x, y, c = _coords()
        for j, (px, py) in enumerate(_other_chips(x, y)):
            cp = pltpu.make_async_remote_copy(
                src_ref=p_ref.at[2 * px + py], dst_ref=land_ref.at[2 * px + py], send_sem=send.at[j],
                recv_sem=recv.at[j], device_id=(px, py, c), device_id_type=MESH)
            cp.wait_send()
            cp.wait_recv()

    buf = pltpu.HBM(part.shape, part.dtype)
    return pl.pallas_call(
        body, name=name, out_shape=(buf, buf), in_specs=(HBM, HBM, SEM, SEM, ANY), out_specs=(HBM, HBM),
        input_output_aliases={0: 0, 1: 1}, compiler_params=pltpu.CompilerParams(has_side_effects=SIDE),
    )(part, land, send, recv, after)


def _gather_start(bufs, after, name):
    n = len(bufs)

    def body(*refs):
        ins = refs[:n]
        sems = refs[n + 1:3 * n + 1]
        token = refs[4 * n + 1]
        x, y, c = _coords()
        s_me = 2 * x + y
        for i in range(n):
            for j, (px, py) in enumerate(_other_chips(x, y)):
                pltpu.make_async_remote_copy(
                    src_ref=ins[i].at[s_me, c], dst_ref=ins[i].at[s_me, c], send_sem=sems[2 * i].at[j],
                    recv_sem=sems[2 * i + 1].at[j], device_id=(px, py, c), device_id_type=MESH).start()
        token[...] = jnp.zeros_like(token)

    res = pl.pallas_call(
        body, name=name,
        out_shape=(*[pltpu.SemaphoreType.DMA((3,))] * (2 * n), *[pltpu.HBM(b.shape, b.dtype) for b in bufs],
                   _sds(TOKEN, F32)),
        in_specs=(*[HBM] * n, ANY), out_specs=(*[SEM] * (2 * n), *[HBM] * n, pl.BlockSpec(memory_space=pltpu.VMEM)),
        input_output_aliases={i: 2 * n + i for i in range(n)},
        compiler_params=pltpu.CompilerParams(has_side_effects=SIDE),
    )(*[_in_hbm(b) for b in bufs], after)
    return [(res[2 * i], res[2 * i + 1]) for i in range(n)], list(res[2 * n:3 * n]), res[3 * n]


def _gather_wait(send, recv, buf, after, name):
    def body(b_ref, send, recv, after_ref, b_out):
        x, y, c = _coords()
        s_me = 2 * x + y
        for j, (px, py) in enumerate(_other_chips(x, y)):
            cp = pltpu.make_async_remote_copy(
                src_ref=b_ref.at[s_me, c], dst_ref=b_ref.at[2 * px + py, c], send_sem=send.at[j], recv_sem=recv.at[j],
                device_id=(px, py, c), device_id_type=MESH)
            cp.wait_send()
            cp.wait_recv()

    return pl.pallas_call(
        body, name=name, out_shape=pltpu.HBM(buf.shape, buf.dtype), in_specs=(HBM, SEM, SEM, ANY), out_specs=HBM,
        input_output_aliases={0: 0}, compiler_params=pltpu.CompilerParams(has_side_effects=SIDE),
    )(buf, send, recv, after)


def _pass_copies(b_ref, send, recv):
    x, y, c = _coords()
    return [pltpu.make_async_remote_copy(
        src_ref=b_ref.at[2 * px + py, c], dst_ref=b_ref.at[2 * px + py, c], send_sem=send.at[j], recv_sem=recv.at[j],
        device_id=(x, y, 1 - c), device_id_type=MESH) for j, (px, py) in enumerate(_other_chips(x, y))]


def _pass_start(buf, after, name):
    def body(b_ref, after_ref, send, recv, b_thru, token):
        for cp in _pass_copies(b_ref, send, recv):
            cp.start()
        token[...] = jnp.zeros_like(token)

    return pl.pallas_call(
        body, name=name,
        out_shape=(pltpu.SemaphoreType.DMA((3,)), pltpu.SemaphoreType.DMA((3,)), pltpu.HBM(buf.shape, buf.dtype),
                   _sds(TOKEN, F32)),
        in_specs=(HBM, ANY), out_specs=(SEM, SEM, HBM, pl.BlockSpec(memory_space=pltpu.VMEM)),
        input_output_aliases={0: 2}, compiler_params=pltpu.CompilerParams(has_side_effects=SIDE),
    )(_in_hbm(buf), after)


def _pass_wait(send, recv, buf, after, name):
    def body(b_ref, send, recv, after_ref, b_out):
        x, y, c = _coords()
        for j, (px, py) in enumerate(_other_chips(x, y)):
            pltpu.make_async_remote_copy(
                src_ref=b_ref.at[2 * px + py, c], dst_ref=b_ref.at[2 * px + py, 1 - c], send_sem=send.at[j],
                recv_sem=recv.at[j], device_id=(x, y, 1 - c), device_id_type=MESH).wait()

    return pl.pallas_call(
        body, name=name, out_shape=pltpu.HBM(buf.shape, buf.dtype), in_specs=(HBM, SEM, SEM, ANY), out_specs=HBM,
        input_output_aliases={0: 0}, compiler_params=pltpu.CompilerParams(has_side_effects=SIDE),
    )(buf, send, recv, after)


def _pass_on_halves(bufs):
    n = len(bufs)

    def body(*refs):
        ins, outs = refs[:n], refs[n:2 * n]
        send, recv = refs[2 * n:]
        x, y, c = _coords()
        copies = []
        for i in range(n):
            for j, (px, py) in enumerate(_other_chips(x, y)):
                cp = pltpu.make_async_remote_copy(
                    src_ref=ins[i].at[2 * px + py, c], dst_ref=outs[i].at[2 * px + py, c], send_sem=send.at[i, j],
                    recv_sem=recv.at[i, j], device_id=(x, y, 1 - c), device_id_type=MESH)
                cp.start()
                copies.append(cp)
        for i in range(n):
            for j, (px, py) in enumerate(_other_chips(x, y)):
                other = outs[i].at[2 * px + py, 1 - c]
                pltpu.make_async_remote_copy(
                    src_ref=other, dst_ref=other, send_sem=send.at[i, j], recv_sem=recv.at[i, j],
                    device_id=(x, y, c), device_id_type=MESH).wait_recv()
        for cp in copies:
            cp.wait_send()

    return pl.pallas_call(
        body, out_shape=[_sds(b.shape, b.dtype) for b in bufs],
        in_specs=[ANY] * n, out_specs=[ANY] * n, input_output_aliases={i: i for i in range(n)},
        scratch_shapes=[pltpu.SemaphoreType.DMA((n, 3)), pltpu.SemaphoreType.DMA((n, 3))],
        name="gather_pass_on")(*bufs)


def _to_rows(vec):
    n = -(-vec.shape[0] // 1024) * 1024
    return jnp.pad(vec, (0, n - vec.shape[0])).reshape(-1, 128)


def kernel(x, norm_mix, norm_mlp, norm_final, pool_w, pool_scale, gla_w_in, gla_w_up_f, gla_b_up_f, gla_w_up_b, gla_b_up_b, gla_g_norm, gla_w_out, mlp_w_in, mlp_w_out, loss_target, m_norm_mix, m_norm_mlp, m_norm_final, m_pool_w, m_pool_scale, m_gla_w_in, m_gla_w_up_f, m_gla_b_up_f, m_gla_w_up_b, m_gla_b_up_b, m_gla_g_norm, m_gla_w_out, m_mlp_w_in, m_mlp_w_out, v_norm_mix, v_norm_mlp, v_norm_final, v_pool_w, v_pool_scale, v_gla_w_in, v_gla_w_up_f, v_gla_b_up_f, v_gla_w_up_b, v_gla_b_up_b, v_gla_g_norm, v_gla_w_out, v_mlp_w_in, v_mlp_w_out):
    nb, seq, d = x.shape
    t = nb * seq
    dg = d // N_GROUPS
    kd = d // 2
    dv = d // N_HEADS
    pw = gla_w_in.shape[2]
    f4 = mlp_w_in.shape[2]
    dff = N_SHARDS * f4
    cx, cy, cc = _coords()
    s_me = 2 * cx + cy
    c_arr = jnp.reshape(cc, (1,)).astype(jnp.int32)
    s_arr = jnp.reshape(s_me, (1,)).astype(jnp.int32)

    xf = x.reshape(t, d)
    tgt = loss_target.reshape(t, d)

    ks = kd // N_SHARDS
    small = jnp.concatenate([gla_w_up_f[0].reshape(-1), gla_w_up_b[0].reshape(-1), gla_b_up_f[0], gla_b_up_b[0],
                             gla_g_norm[0]])
    small_all = _bcast8(_to_rows(small))
    pool_rows = N_GROUPS * (dg // N_SHARDS)
    gsems, gbufs, gtok = _gather_start([
        _fill_own(pool_w.reshape(1, pool_rows, dg), 0, s_arr, None), _fill_own(mlp_w_in, 0, s_arr, None)],
        small_all, "gather_start_first")
    gsems2, gbufs2, gtok = _gather_start([
        _fill_own(mlp_w_out, 0, s_arr, gtok), _fill_own(gla_w_in, 0, s_arr, gtok),
        _fill_own(gla_w_out, 0, s_arr, gtok), _fill_own(mlp_w_in, 1, s_arr, gtok),
        _fill_own(mlp_w_out, 1, s_arr, gtok)], gtok, "gather_start_rest")
    gsems, gbufs = gsems + gsems2, gbufs + gbufs2
    small_all = small_all[::2].reshape(N_SHARDS, -1)

    def arrived(i, after, name):
        return _gather_wait(gsems[i][0], gsems[i][1], gbufs[i], after, "gather_wait_" + name)

    def weight(i, after, name):
        return _pass_on_halves([arrived(i, after, name)])[0]

    def pinned(vec, tok):
        return vec + tok[0:1, 0:1]

    o = 0
    wuf = jnp.transpose(small_all[:, o:o + GATE_RANK * ks].reshape(N_SHARDS, GATE_RANK, ks), (1, 0, 2)).reshape(GATE_RANK, kd)
    o += GATE_RANK * ks
    wub = jnp.transpose(small_all[:, o:o + GATE_RANK * ks].reshape(N_SHARDS, GATE_RANK, ks), (1, 0, 2)).reshape(GATE_RANK, kd)
    o += GATE_RANK * ks
    buf = small_all[:, o:o + ks].reshape(1, kd)
    o += ks
    bub = small_all[:, o:o + ks].reshape(1, kd)
    o += ks
    gn = small_all[:, o:o + dv // N_SHARDS].reshape(1, dv)
    w_up = jnp.zeros((R_PAD, 2 * kd), F32).at[:GATE_RANK, :kd].set(wuf).at[GATE_RANK:2 * GATE_RANK, kd:].set(wub)
    w_up = w_up.astype(BF16)
    b_up = jnp.concatenate([buf, bub], axis=1)

    hn0 = _rmsnorm_fwd(xf, pinned(norm_mix[0:1], gtok), F32, "norm_mix0")
    dm = _pool_apply(hn0, seq, False, "pool_diff")
    wp = weight(0, dm, "pool_w").reshape(N_SHARDS, N_GROUPS, dg // N_SHARDS, dg)
    h1, ypre = _pool_mm_fwd(dm, wp, xf, pool_scale)
    hn1 = _rmsnorm_fwd(h1, norm_mlp[0:1], BF16, "norm_mlp0")
    w1g = [weight(1, hn1, "mlp_w_in0").reshape(N_SHARDS, d, f4), None]
    w2g = [None, None]
    h2, r0, u0, w2g[0] = _mlp_fwd(h1, hn1, w1g[0], lambda u: weight(2, u, "mlp_w_out0").reshape(dff, d))
    hn2 = _rmsnorm_fwd(h2, norm_mix[1:2], BF16, "norm_mix1")
    win = jnp.transpose(weight(3, hn2, "gla_w_in").reshape(N_SHARDS, d, pw), (1, 0, 2)).reshape(d, N_SHARDS * pw)
    win_t = jnp.transpose(win)
    w_r = jnp.pad(win[:, 3 * d:], ((0, 0), (0, R_PAD - 2 * GATE_RANK)))
    ptm, ptn = _tile(t, 1024), _tile(3 * d, 1024)
    (pm,) = _matmul(
        hn2, win_t, dims=_NT, grid=(t // ptm, 3 * d // ptn, 1),
        a_spec=pl.BlockSpec((ptm, d), lambda i, j, kk: (i, 0)), b_spec=pl.BlockSpec((ptn, d), lambda i, j, kk: (j, 0)),
        acc_shape=(ptm, ptn), out_shapes=[_sds((t, 3 * d), F32)],
        out_specs=[pl.BlockSpec((ptm, ptn), lambda i, j, kk: (i, j))], epilogue=lambda acc: (acc,), name="gla_proj")
    (pr,) = _mm(hn2, w_r, "nn", out_dtypes=[BF16], name="gla_proj_r")
    pr = pr.at[:, ONES_COL].set(1.0)

    def log_decay(acc, bv):
        z = acc + bv
        return ((jnp.minimum(z, 0.0) - jnp.log(1.0 + jnp.exp(-jnp.abs(z)))) / GATE_TAU,)

    tm_, tn_ = _tile(t, 1024), _tile(kd, 1024)
    (la,) = _matmul(
        pr, w_up, dims=_NN, grid=(t // tm_, 2 * kd // tn_, 1),
        a_spec=pl.BlockSpec((tm_, R_PAD), lambda i, j, kk: (i, 0)),
        b_spec=pl.BlockSpec((R_PAD, tn_), lambda i, j, kk: (0, j)), acc_shape=(tm_, tn_),
        out_shapes=[_sds((t, 2 * kd), F32)], out_specs=[pl.BlockSpec((tm_, tn_), lambda i, j, kk: (i, j))],
        epilogue=log_decay, extras=(b_up,), extra_specs=[pl.BlockSpec((1, tn_), lambda i, j, kk: (0, j))],
        name="gla_gate_fwd")
    o_f, o_b, st_f, st_b = _gla_fwd(pm, la, seq)
    send4, recv4, buf4, tok = _pass_start(arrived(4, o_f, "gla_w_out"), o_f, "gather_pass_start_gla_w_out")
    og = _gla_post_fwd(o_f, o_b, pm, pinned(gn, tok))
    wout = _pass_wait(send4, recv4, buf4, og, "gather_pass_wait_gla_w_out").reshape(d, d)
    send5, recv5, buf5, tok = _pass_start(arrived(5, og, "mlp_w_in1"), og, "gather_pass_start_mlp_w_in1")
    (h3,) = _mm(og, wout, "nn", out_dtypes=[F32], epilogue=lambda acc, hv: (hv + acc,), extras=(h2,), name="gla_out",
                pin=tok)
    hn3 = _rmsnorm_fwd(h3, norm_mlp[1:2], BF16, "norm_mlp1")
    w1g[1] = _pass_wait(send5, recv5, buf5, hn3, "gather_pass_wait_mlp_w_in1").reshape(N_SHARDS, d, f4)
    send6, recv6, buf6, tok = _pass_start(arrived(6, hn3, "mlp_w_out1"), hn3, "gather_pass_start_mlp_w_out1")
    h4, r1, u1, w2g[1] = _mlp_fwd(
        h3, hn3, w1g[1], lambda u: _pass_wait(send6, recv6, buf6, u, "gather_pass_wait_mlp_w_out1").reshape(dff, d), tok)

    chains = []
    last_tok = [gtok]

    def swap_begin(g, name, after=None):
        g5 = g.reshape(N_SHARDS, 2, g.shape[1] // 2, g.shape[2])
        send, recv, land, tok = _sibling_start(g5, _swap_pieces, (N_SHARDS,) + g5.shape[2:],
                                               last_tok[0] if after is None else after, "grad_swap_start_" + name)
        last_tok[0] = tok
        return (name, send, recv, g5, land), tok

    def swap_end(state, after):
        name, send, recv, g5, land = state
        land = _sibling_wait(send, recv, g5, land, _swap_pieces, after, "grad_swap_wait_" + name)
        pair = _pair_sum(g5, land, c_arr)
        send, recv, pair, land, tok = _scatter_start(pair, "grad_scatter_start_" + name)
        chains.append((name, send, recv, pair, land))
        last_tok[0] = tok
        return tok

    def mlp_backward(dhb, hn, r, u, w1, w2, name):
        steps = _mlp_bwd(dhb, hn, r, u, w1, w2)
        state, tok = swap_begin(next(steps), "mlp_w_out" + name)
        dw1 = steps.send(tok)
        swap_end(state, dw1)
        state, tok = swap_begin(dw1, "mlp_w_in" + name)
        dhn = steps.send(tok)
        return dhn, swap_end(state, dhn)

    loss_part, dh4, dh4b, dg_final = _final_bwd(h4, norm_final.reshape(1, d), tgt)
    dhn3, tok = mlp_backward(dh4b, hn3, r1, u1, w1g[1], w2g[1], "1")
    dh3, dh3b, dg_mlp1 = _rmsnorm_bwd(h3, pinned(norm_mlp[1:2], tok), dhn3, dh4, "norm_mlp1_bwd")

    (dog,) = _mm(dh3b, wout, "nt", out_dtypes=[F32], name="gla_out_bwd_x")
    (dwout,) = _mm(og, dh3b, "tn", out_dtypes=[BF16], name="gla_out_bwd_w")
    state, tok = swap_begin(dwout.reshape(N_SHARDS, d // N_SHARDS, d), "gla_w_out")
    do, dgate, dg_gn = _gla_post_bwd(dog, o_f, o_b, pm, pinned(gn, tok))
    dq_f, dk_f, dv_f, dla_f, dq_b, dk_b, dv_b, dla_b = _gla_bwd(pm, la, do, st_f, st_b, seq)
    b_up_p = pinned(b_up, swap_end(state, dq_f))
    nkb = kd // tn_

    def gate_bwd(acc, bv, dl_f, dl_b):
        z = acc + bv
        dl = jnp.where(pl.program_id(1) < nkb, dl_f, dl_b)
        return (dl * (1.0 / GATE_TAU) / (1.0 + jnp.exp(z)),)

    (dz,) = _matmul(
        pr, w_up, dims=_NN, grid=(t // tm_, 2 * kd // tn_, 1),
        a_spec=pl.BlockSpec((tm_, R_PAD), lambda i, j, kk: (i, 0)),
        b_spec=pl.BlockSpec((R_PAD, tn_), lambda i, j, kk: (0, j)), acc_shape=(tm_, tn_),
        out_shapes=[_sds((t, 2 * kd), BF16)], out_specs=[pl.BlockSpec((tm_, tn_), lambda i, j, kk: (i, j))],
        epilogue=gate_bwd, extras=(b_up_p, dla_f, dla_b),
        extra_specs=[pl.BlockSpec((1, tn_), lambda i, j, kk: (0, j)),
                     pl.BlockSpec((tm_, tn_), lambda i, j, kk: (i, jnp.minimum(j, nkb - 1))),
                     pl.BlockSpec((tm_, tn_), lambda i, j, kk: (i, jnp.maximum(j - nkb, 0)))],
        name="gla_gate_bwd")
    (dpr,) = _mm(dz, w_up, "nt", out_dtypes=[BF16], name="gla_gate_bwd_r")
    (dw_up,) = _mm(pr, dz, "tn", out_dtypes=[F32], name="gla_gate_bwd_w")
    dp = _gla_dp(dq_f, dq_b, dk_f, dk_b, dv_f, dv_b, dgate)
    (dw_main,) = _mm(hn2, dp, "tn", out_dtypes=[BF16], name="gla_proj_bwd_w")
    (dw_r,) = _mm(hn2, dpr, "tn", out_dtypes=[BF16], name="gla_proj_bwd_wr")
    dwin = jnp.concatenate([dw_main, dw_r[:, :2 * GATE_RANK]], axis=1)
    dwin = jnp.transpose(dwin.reshape(d, N_SHARDS, pw), (1, 0, 2))
    btm, btn, btk = _tile(t, 1024), _tile(d, 1024), _tile(3 * d, 2048)
    (dhn2,) = _matmul(
        dp, win_t, dims=_NN, grid=(t // btm, d // btn, 3 * d // btk),
        a_spec=pl.BlockSpec((btm, btk), lambda i, j, kk: (i, kk)),
        b_spec=pl.BlockSpec((btk, btn), lambda i, j, kk: (kk, j)), acc_shape=(btm, btn),
        out_shapes=[_sds((t, d), F32)], out_specs=[pl.BlockSpec((btm, btn), lambda i, j, kk: (i, j))],
        epilogue=lambda acc, dr, wr: (acc + lax.dot_general(dr, wr, _NT, preferred_element_type=F32),),
        extras=(dpr, w_r), extra_specs=[pl.BlockSpec((btm, R_PAD), lambda i, j, kk: (i, 0)),
                                        pl.BlockSpec((btn, R_PAD), lambda i, j, kk: (j, 0))],
        name="gla_proj_bwd_x")
    state, tok = swap_begin(dwin, "gla_w_in", dhn2)
    dh2, dh2b, dg_mix1 = _rmsnorm_bwd(h2, pinned(norm_mix[1:2], tok), dhn2, dh3, "norm_mix1_bwd")
    swap_end(state, dh2)

    dhn1, tok = mlp_backward(dh2b, hn1, r0, u0, w1g[0], w2g[0], "0")
    dh1, _, dg_mlp0 = _rmsnorm_bwd(h1, pinned(norm_mlp[0:1], tok), dhn1, dh2, "norm_mlp0_bwd")

    dys, dg_pscale = _pool_scale_bwd(dh1, ypre, pool_scale)
    dwp = _pool_mm_bwd_w(dm, dys)
    state, tok = swap_begin(dwp.reshape(N_SHARDS, pool_rows, dg), "pool_w")
    dd = _pool_mm_bwd_x(dys, wp, tok)
    tok = swap_end(state, dd)
    dhn0 = _pool_apply(dd, seq, True, "pool_diff_bwd")
    dx, _, dg_mix0 = _rmsnorm_bwd(xf, pinned(norm_mix[0:1], tok), dhn0, dh1, "norm_mix0_bwd")

    dwuf, dwub = dw_up[:GATE_RANK, :kd], dw_up[GATE_RANK:2 * GATE_RANK, kd:]
    dbuf, dbub = dw_up[ONES_COL, :kd], dw_up[ONES_COL, kd:]
    pieces = [jnp.concatenate([dg_mix0, dg_mix1], 0), jnp.concatenate([dg_mlp0, dg_mlp1], 0), dg_final, dg_pscale,
              dwuf, dwub, dbuf, dbub, dg_gn]
    sizes = [p.size for p in pieces]
    packed = _to_rows(jnp.concatenate([p.reshape(-1) for p in pieces]))
    bsend, brecv, bland, _ = _bcast_start(packed, dx, "small_grads_start")

    loss = lax.psum(loss_part[0, 0], ("x", "y", "c"))

    weights = [norm_mix, norm_mlp, norm_final, pool_w, pool_scale, gla_w_in, gla_w_up_f, gla_b_up_f, gla_w_up_b,
               gla_b_up_b, gla_g_norm, gla_w_out, mlp_w_in, mlp_w_out]
    moms = [m_norm_mix, m_norm_mlp, m_norm_final, m_pool_w, m_pool_scale, m_gla_w_in, m_gla_w_up_f, m_gla_b_up_f,
            m_gla_w_up_b, m_gla_b_up_b, m_gla_g_norm, m_gla_w_out, m_mlp_w_in, m_mlp_w_out]
    vels = [v_norm_mix, v_norm_mlp, v_norm_final, v_pool_w, v_pool_scale, v_gla_w_in, v_gla_w_up_f, v_gla_b_up_f,
            v_gla_w_up_b, v_gla_b_up_b, v_gla_g_norm, v_gla_w_out, v_mlp_w_in, v_mlp_w_out]
    names = ["norm_mix", "norm_mlp", "norm_final", "pool_w", "pool_scale", "gla_w_in", "gla_w_up_f", "gla_b_up_f",
             "gla_w_up_b", "gla_b_up_b", "gla_g_norm", "gla_w_out", "mlp_w_in", "mlp_w_out"]
    index = {nm: k for k, nm in enumerate(names)}
    results = {}

    stacked = {"mlp_w_out1": ("mlp_w_out", 1), "mlp_w_in1": ("mlp_w_in", 1), "gla_w_out": ("gla_w_out", 0),
               "gla_w_in": ("gla_w_in", 0), "mlp_w_out0": ("mlp_w_out", 0), "mlp_w_in0": ("mlp_w_in", 0),
               "pool_w": ("pool_w", 0)}
    oc_arr = 1 - c_arr
    after = dx
    for name, send, recv, pair, land in chains:
        pair, land = _scatter_wait(send, recv, pair, land, after, "grad_scatter_wait_" + name)
        mine = _quad_sum(pair, land, s_arr)
        jsend, jrecv, jland, jtok = _sibling_start(mine, _whole_piece, mine.shape, after, "grad_join_start_" + name)
        nm, layer = stacked[name]
        w, m, v = weights[index[nm]], moms[index[nm]], vels[index[nm]]
        rows, cols = 2 * mine.shape[0], mine.shape[1]
        shp = (w.size // (rows * cols), rows, cols)
        w, m, v = w.reshape(shp), m.reshape(shp), v.reshape(shp)
        tr_ = nm == "gla_w_in"
        flip = (lambda a: jnp.swapaxes(a, -1, -2)) if tr_ else (lambda a: a)
        w, m, v = flip(w), flip(m), flip(v)
        res = _adamw_half(w, flip(mine), m, v, layer, c_arr, results.get(nm), jtok, "adamw_mine_" + name, tr_)
        theirs = _sibling_wait(jsend, jrecv, mine, jland, _whole_piece, res[1], "grad_join_wait_" + name)
        res = _adamw_half(w, flip(theirs), m, v, layer, oc_arr, res, jtok, "adamw_theirs_" + name, tr_)
        results[nm] = [flip(a) for a in res]
        after = res[1]

    bland = _bcast_wait(bsend, brecv, packed, bland, after, "small_grads_wait")
    me_arr = jnp.reshape(4 * cx + 2 * cy + cc, (1,)).astype(jnp.int32)
    summed = _sum8(bland, packed, me_arr).reshape(-1)
    outs_small, o = [], 0
    for p, n in zip(pieces, sizes):
        outs_small.append(summed[o:o + n].reshape(p.shape))
        o += n
    g_nmix, g_nmlp, g_nfinal, g_pscale, g_wuf, g_wub, g_buf, g_bub, g_gn = outs_small
    g_wuf = lax.dynamic_slice_in_dim(g_wuf, s_me * ks, ks, axis=1)
    g_wub = lax.dynamic_slice_in_dim(g_wub, s_me * ks, ks, axis=1)
    g_buf = lax.dynamic_slice_in_dim(g_buf, s_me * ks, ks, axis=0)
    g_bub = lax.dynamic_slice_in_dim(g_bub, s_me * ks, ks, axis=0)
    g_gn = lax.dynamic_slice_in_dim(g_gn.reshape(dv), s_me * (dv // N_SHARDS), dv // N_SHARDS, axis=0)
    small_grads = {"norm_mix": g_nmix, "norm_mlp": g_nmlp, "norm_final": g_nfinal, "pool_scale": g_pscale,
                   "gla_w_up_f": g_wuf, "gla_b_up_f": g_buf, "gla_w_up_b": g_wub, "gla_b_up_b": g_bub,
                   "gla_g_norm": g_gn}
    for nm, g in small_grads.items():
        w, m, v = weights[index[nm]], moms[index[nm]], vels[index[nm]]
        cols = w.shape[-1]
        shp = (w.size // cols, cols)
        dl, mn, vn = _adamw(w.reshape(shp), g.reshape(shp), m.reshape(shp), v.reshape(shp), "adamw_" + nm)
        results[nm] = (g, dl, mn, vn)

    outs = [[results[nm][k].reshape(weights[index[nm]].shape) for nm in names] for k in range(4)]
    return (loss, dx.reshape(x.shape), *outs[0], *outs[1], *outs[2], *outs[3])
```
